```python
import jax, jax.numpy as jnp
from jax import lax
import numpy as np

D_MODEL = 2048
BATCH = 4
SEQ = 2048
DEPTH = 2
DEC_BATCH = 128
DEC_SEQ = 4
PAST_LEN = 16384
PAGE_SIZE = 128

N_MIXERS = 2
N_A_LAYERS = (DEPTH + 1) // 2
N_B_LAYERS = DEPTH // 2
N_HEADS_A = 8
DK_A = D_MODEL // (2 * N_HEADS_A)
DV_A = D_MODEL // N_HEADS_A
CHUNK_A = 128
D_INNER_B = 2 * D_MODEL
N_GROUPS_B = 8
GROUP_DIM_B = D_INNER_B // N_GROUPS_B
CHUNK_B = 128
N_EXPERTS = 16
N_EXPERT_GROUPS = 4
EXPERTS_PER_GROUP = N_EXPERTS // N_EXPERT_GROUPS
TOP_K = 2
D_FF_EXPERT = D_MODEL // 2
ALPHA = float((2 * DEPTH) ** 0.25)
BETA = float((8 * DEPTH) ** -0.25)
LN_EPS = 1e-5

kernel_name = "hybrid_mlstm_chunkmlp_grouped_moe_step"


def layer_norm(x, g, b):
    xf = x.astype(jnp.float32)
    mu = xf.mean(-1, keepdims=True)
    var = jnp.square(xf - mu).mean(-1, keepdims=True)
    return ((xf - mu) * lax.rsqrt(var + LN_EPS) * g + b).astype(x.dtype)


def mlstm_chunkwise(q, k, v, ig, lf, C0, n0, m0, chunk):
    B, S, H, _ = q.shape
    nc = S // chunk

    def to_chunks(a):
        return jnp.moveaxis(a.reshape((B, nc, chunk) + a.shape[2:]), 1, 0)

    causal = jnp.tril(jnp.ones((chunk, chunk), bool))[None, :, :, None]

    def step(carry, xs):
        C, n, m = carry
        qc, kc, vc, igc, lfc = xs
        b = jnp.cumsum(lfc, axis=1)
        dmat = b[:, :, None, :] - b[:, None, :, :] + igc[:, None, :, :]
        dmat = jnp.where(causal, dmat, -jnp.inf)
        m_inter = b + m[:, None, :]
        m_t = jnp.maximum(m_inter, dmat.max(axis=2))
        scores = jnp.einsum('bthd,bshd->btsh', qc, kc) * jnp.exp(dmat - m_t[:, :, None, :])
        inter = jnp.exp(m_inter - m_t)
        num = (jnp.einsum('btsh,bshv->bthv', scores, vc)
               + inter[..., None] * jnp.einsum('bthd,bhdv->bthv', qc, C))
        den = scores.sum(axis=2) + inter * jnp.einsum('bthd,bhd->bth', qc, n)
        h = num / jnp.maximum(jnp.abs(den), jnp.exp(-m_t))[..., None]
        m_new = m_t[:, -1]
        w = jnp.exp(b[:, -1:, :] - b + igc - m_new[:, None, :])
        carry_decay = jnp.exp(b[:, -1] + m - m_new)
        C_new = carry_decay[..., None, None] * C + jnp.einsum('bsh,bshd,bshv->bhdv', w, kc, vc)
        n_new = carry_decay[..., None] * n + jnp.einsum('bsh,bshd->bhd', w, kc)
        return (C_new, n_new, m_new), h

    xs = (to_chunks(q), to_chunks(k), to_chunks(v), to_chunks(ig), to_chunks(lf))
    (C, n, m), h = lax.scan(step, (C0, n0, m0), xs)
    h = jnp.moveaxis(h, 0, 1).reshape(B, S, H, -1)
    return h, C, n, m


def mlstm_mixer(h, w_in, b_gates, norm_w, w_out, C0, n0, m0):
    B, S, _ = h.shape
    qk = N_HEADS_A * DK_A
    vd = N_HEADS_A * DV_A
    proj = (h @ w_in).astype(jnp.float32)
    q, k, v, o, gates = jnp.split(proj, [qk, 2 * qk, 2 * qk + vd, 2 * qk + 2 * vd], axis=-1)
    q = q.reshape(B, S, N_HEADS_A, DK_A) * (DK_A ** -0.5)
    k = k.reshape(B, S, N_HEADS_A, DK_A)
    v = v.reshape(B, S, N_HEADS_A, DV_A)
    gates = gates + b_gates.astype(jnp.float32)
    ig, fg = jnp.split(gates, 2, axis=-1)
    lf = jax.nn.log_sigmoid(fg)
    chunk = CHUNK_A if S % CHUNK_A == 0 else S
    hh, C, n, m = mlstm_chunkwise(q, k, v, ig, lf, C0.astype(jnp.float32),
                                  n0.astype(jnp.float32), m0.astype(jnp.float32), chunk)
    mu = hh.mean(-1, keepdims=True)
    var = jnp.square(hh - mu).mean(-1, keepdims=True)
    hn = ((hh - mu) * lax.rsqrt(var + LN_EPS)).reshape(B, S, vd) * norm_w
    y = (jax.nn.sigmoid(o) * hn).astype(h.dtype) @ w_out
    return y, C, n, m


def chunk_mlp_mixer(h, w_in, b_in, nv_g, nv_b, w_s, b_s, w_out):
    B, S, _ = h.shape
    L = min(CHUNK_B, S)
    proj = jax.nn.gelu(h @ w_in + b_in)
    u, v = jnp.split(proj, 2, axis=-1)
    v = layer_norm(v, nv_g, nv_b)
    mask = jnp.tril(jnp.ones((L, L), bool))
    ws = jnp.where(mask, w_s[:, :L, :L], jnp.zeros((), w_s.dtype))
    vg = v.reshape(B, S // L, L, N_GROUPS_B, GROUP_DIM_B)
    mixed = jnp.einsum('gts,bcsgd->bctgd', ws, vg) + b_s[:, :L].T[None, None, :, :, None]
    y = (u * mixed.reshape(B, S, D_INNER_B)) @ w_out
    return y, v


def grouped_moe(h, w_router, b_router, w1, w2):
    B, S, D = h.shape
    t = h.reshape(B * S, D)
    s = jax.nn.sigmoid((t @ w_router).astype(jnp.float32))
    sel = (s + b_router.astype(jnp.float32)).reshape(-1, N_EXPERT_GROUPS, EXPERTS_PER_GROUP)
    group_score = lax.top_k(sel, TOP_K)[0].sum(-1)
    g_idx = jnp.argmax(group_score, axis=-1)
    in_group = jnp.take_along_axis(sel, g_idx[:, None, None], axis=1)[:, 0]
    _, local = lax.top_k(in_group, TOP_K)
    e_idx = g_idx[:, None] * EXPERTS_PER_GROUP + local
    w = jnp.take_along_axis(s, e_idx, axis=-1)
    w = w / w.sum(-1, keepdims=True)
    gates = jnp.einsum('tk,tke->te', w, jax.nn.one_hot(e_idx, N_EXPERTS, dtype=jnp.float32))
    hid = jnp.einsum('td,edf->tef', t, w1)
    a, g = jnp.split(hid, 2, axis=-1)
    z = jax.nn.silu(g) * a * gates[..., None].astype(hid.dtype)
    return jnp.einsum('tef,efd->td', z, w2).reshape(B, S, D)


def run_trunk(x, c, C0, n0, m0, w_ada, b_ada, ln_g, ln_b, a_w_in, a_b_gates, a_norm_w, a_w_out,
              b_w_in, b_b_in, b_norm_g, b_norm_b, b_w_s, b_b_s, b_w_out,
              w_router, b_router, w_expert_in, w_expert_out):
    Cs, ns, ms, vs = [], [], [], []
    for i in range(DEPTH):
        mod = jax.nn.silu(c) @ w_ada[i] + b_ada[i]
        sh_m, sc_m, g_m, sh_f, sc_f, g_f = jnp.split(mod[:, None, :], 6, axis=-1)
        hin = x * (1 + sc_m) + sh_m
        j = i // N_MIXERS
        if i % N_MIXERS == 0:
            y, C, n, m = mlstm_mixer(hin, a_w_in[j], a_b_gates[j], a_norm_w[j], a_w_out[j],
                                     C0[j], n0[j], m0[j])
            Cs.append(C)
            ns.append(n)
            ms.append(m)
        else:
            y, v = chunk_mlp_mixer(hin, b_w_in[j], b_b_in[j], b_norm_g[j], b_norm_b[j],
                                   b_w_s[j], b_b_s[j], b_w_out[j])
            vs.append(v)
        x = layer_norm(ALPHA * x + g_m * y, ln_g[i, 0], ln_b[i, 0])
        hin = x * (1 + sc_f) + sh_f
        f = grouped_moe(hin, w_router, b_router, w_expert_in[i], w_expert_out[i])
        x = layer_norm(ALPHA * x + g_f * f, ln_g[i, 1], ln_b[i, 1])
    return x, jnp.stack(Cs), jnp.stack(ns), jnp.stack(ms), jnp.stack(vs)


def setup_inputs(seed: int = 0) -> dict:
    key = jax.random.key(seed)
    ks = jax.random.split(key, 32)
    f32 = jnp.float32

    def nrm(k, shape, scale):
        return jax.random.normal(k, shape, f32) * scale

    D = D_MODEL
    qk = N_HEADS_A * DK_A
    vd = N_HEADS_A * DV_A
    in_a_cols = 2 * qk + 2 * vd + 2 * N_HEADS_A
    gate_offset = jnp.repeat(jnp.array([0., 0., 1., 0., 0., 1.], f32), D)
    b_gates = jnp.concatenate([nrm(ks[9], (N_A_LAYERS, N_HEADS_A), 0.5),
                               3.0 + nrm(ks[10], (N_A_LAYERS, N_HEADS_A), 0.5)], axis=-1)
    return {
        "x_prompt": nrm(ks[0], (BATCH, SEQ, D), 1.0),
        "x_sample": nrm(ks[1], (DEC_BATCH, DEC_SEQ, D), 1.0),
        "state_mlstm_C": nrm(ks[2], (N_A_LAYERS, DEC_BATCH, N_HEADS_A, DK_A, DV_A), 0.5),
        "state_mlstm_n": nrm(ks[3], (N_A_LAYERS, DEC_BATCH, N_HEADS_A, DK_A), 1.0),
        "state_mlstm_m": nrm(ks[4], (N_A_LAYERS, DEC_BATCH, N_HEADS_A), 1.0),
        "c_prompt": nrm(ks[5], (BATCH, D), 1.0),
        "c_sample": nrm(ks[6], (DEC_BATCH, D), 1.0),
        "w_ada": nrm(ks[7], (DEPTH, D, 6 * D), 0.1 * D ** -0.5),
        "b_ada": nrm(ks[8], (DEPTH, 6 * D), 0.02) + gate_offset,
        "ln_g": 1.0 + nrm(ks[11], (DEPTH, 2, D), 0.02),
        "ln_b": nrm(ks[12], (DEPTH, 2, D), 0.02),
        "a_w_in": nrm(ks[13], (N_A_LAYERS, D, in_a_cols), D ** -0.5),
        "a_b_gates": b_gates,
        "a_norm_w": 1.0 + nrm(ks[14], (N_A_LAYERS, vd), 0.02),
        "a_w_out": nrm(ks[15], (N_A_LAYERS, vd, D), BETA * vd ** -0.5),
        "b_w_in": nrm(ks[16], (N_B_LAYERS, D, 2 * D_INNER_B), D ** -0.5),
        "b_b_in": nrm(ks[17], (N_B_LAYERS, 2 * D_INNER_B), 0.02),
        "b_norm_g": 1.0 + nrm(ks[18], (N_B_LAYERS, D_INNER_B), 0.02),
        "b_norm_b": nrm(ks[19], (N_B_LAYERS, D_INNER_B), 0.02),
        "b_w_s": nrm(ks[20], (N_B_LAYERS, N_GROUPS_B, CHUNK_B, CHUNK_B), 0.5 * CHUNK_B ** -0.5),
        "b_b_s": 1.0 + nrm(ks[21], (N_B_LAYERS, N_GROUPS_B, CHUNK_B), 0.02),
        "b_w_out": nrm(ks[22], (N_B_LAYERS, D_INNER_B, D), BETA * D_INNER_B ** -0.5),
        "w_router": nrm(ks[23], (D, N_EXPERTS), D ** -0.5),
        "b_router": nrm(ks[24], (N_EXPERTS,), 0.01),
        "w_expert_in": nrm(ks[25], (DEPTH, N_EXPERTS, D, 2 * D_FF_EXPERT), D ** -0.5),
        "w_expert_out": nrm(ks[26], (DEPTH, N_EXPERTS, D_FF_EXPERT, D), BETA * D_FF_EXPERT ** -0.5),
    }


def reference(x_prompt, x_sample, state_mlstm_C, state_mlstm_n, state_mlstm_m, c_prompt, c_sample,
              w_ada, b_ada, ln_g, ln_b, a_w_in, a_b_gates, a_norm_w, a_w_out,
              b_w_in, b_b_in, b_norm_g, b_norm_b, b_w_s, b_b_s, b_w_out,
              w_router, b_router, w_expert_in, w_expert_out):
    bp = x_prompt.shape[0]
    C_zero = jnp.zeros((N_A_LAYERS, bp, N_HEADS_A, DK_A, DV_A), jnp.float32)
    n_zero = jnp.zeros((N_A_LAYERS, bp, N_HEADS_A, DK_A), jnp.float32)
    m_zero = jnp.zeros((N_A_LAYERS, bp, N_HEADS_A), jnp.float32)
    y_prompt, C_prompt, n_prompt, m_prompt, _ = run_trunk(
        x_prompt, c_prompt, C_zero, n_zero, m_zero, w_ada, b_ada, ln_g, ln_b,
        a_w_in, a_b_gates, a_norm_w, a_w_out, b_w_in, b_b_in, b_norm_g, b_norm_b, b_w_s, b_b_s, b_w_out,
        w_router, b_router, w_expert_in, w_expert_out)
    y_sample, C_sample, n_sample, m_sample, v_rows_sample = run_trunk(
        x_sample, c_sample, state_mlstm_C, state_mlstm_n, state_mlstm_m, w_ada, b_ada, ln_g, ln_b,
        a_w_in, a_b_gates, a_norm_w, a_w_out, b_w_in, b_b_in, b_norm_g, b_norm_b, b_w_s, b_b_s, b_w_out,
        w_router, b_router, w_expert_in, w_expert_out)
    return (y_prompt, y_sample, C_prompt, n_prompt, m_prompt, C_sample, n_sample, m_sample, v_rows_sample)
```

```python
import functools

import jax
import jax.numpy as jnp
from jax import lax
from jax.experimental import pallas as pl
from jax.experimental.pallas import tpu as pltpu

F32 = jnp.float32
BF16 = jnp.bfloat16
I32 = jnp.int32

DEPTH = 2
N_HEADS = 8
DK = 128
DV = 256
CHUNK = 128
N_GROUPS_B = 8
N_EXPERTS = 16
N_EXPERT_GROUPS = 4
EXPERTS_PER_GROUP = 4
ALPHA = float((2 * DEPTH) ** 0.25)
LN_EPS = 1e-5

LANES = 128
SUBLANES = 8
MIB = 1024 * 1024
MOE_TILE = 256


def _cparams(n_axes, vmem_mib):
    return pltpu.CompilerParams(
        dimension_semantics=("arbitrary",) * n_axes,
        vmem_limit_bytes=int(vmem_mib * MIB))


def _split3(x):
    hi = x.astype(BF16)
    r1 = x - hi.astype(F32)
    mid = r1.astype(BF16)
    lo = (r1 - mid.astype(F32)).astype(BF16)
    return hi, mid, lo


def _log_sigmoid(x):
    return jnp.minimum(x, 0.0) - jnp.log1p(jnp.exp(-jnp.abs(x)))


def _adaln_kernel(c_ref, w_ref, b_ref, o_ref):
    c = c_ref[...]
    a = (c * jax.nn.sigmoid(c)).astype(BF16)
    o_ref[0] = jnp.dot(a, w_ref[0].astype(BF16), preferred_element_type=F32) + b_ref[0]


def _adaln(c_all, w_ada, b_ada):
    depth, d, n = w_ada.shape
    r = c_all.shape[0]
    tn = 1024
    return pl.pallas_call(
        _adaln_kernel,
        grid=(depth, n // tn),
        in_specs=[pl.BlockSpec((r, d), lambda l, j: (0, 0)),
                  pl.BlockSpec((1, d, tn), lambda l, j: (l, 0, j)),
                  pl.BlockSpec((1, 1, tn), lambda l, j: (l, 0, j))],
        out_specs=pl.BlockSpec((1, r, tn), lambda l, j: (l, 0, j)),
        out_shape=jax.ShapeDtypeStruct((depth, r, n), F32),
        compiler_params=_cparams(2, 32),
        name="adaln",
    )(c_all, w_ada, b_ada.reshape(depth, 1, n))


def _inproj_kernel(x_ref, sh_ref, sc_ref, w_ref, wg_ref, proj_ref, gates_ref, hin_ref):
    @pl.when(pl.program_id(1) == 0)
    def _():
        h = x_ref[...] * (1.0 + sc_ref[0]) + sh_ref[0]
        hb = h.astype(BF16)
        hin_ref[...] = hb
        h_lo = (h - hb.astype(F32)).astype(BF16)
        wg = wg_ref[...]
        wg_hi = wg.astype(BF16)
        wg_lo = (wg - wg_hi.astype(F32)).astype(BF16)
        gates_ref[...] = (jnp.dot(hb, wg_hi, preferred_element_type=F32)
                          + jnp.dot(h_lo, wg_hi, preferred_element_type=F32)
                          + jnp.dot(hb, wg_lo, preferred_element_type=F32))

    proj_ref[...] = jnp.dot(hin_ref[...], w_ref[...].astype(BF16), preferred_element_type=F32)


def _inproj(x, sh, sc, mod_idx, w_in, wg, n_main, tm):
    t, d = x.shape
    tn = 512
    r = sh.shape[1]
    mspec = pl.BlockSpec((1, r, d), lambda i, j: (mod_idx(i), 0, 0))
    return pl.pallas_call(
        _inproj_kernel,
        grid=(t // tm, n_main // tn),
        in_specs=[pl.BlockSpec((tm, d), lambda i, j: (i, 0)), mspec, mspec,
                  pl.BlockSpec((d, tn), lambda i, j: (0, j)),
                  pl.BlockSpec(wg.shape, lambda i, j: (0, 0))],
        out_specs=[pl.BlockSpec((tm, tn), lambda i, j: (i, j)),
                   pl.BlockSpec((tm, wg.shape[1]), lambda i, j: (i, 0))],
        out_shape=[jax.ShapeDtypeStruct((t, n_main), F32),
                   jax.ShapeDtypeStruct((t, wg.shape[1]), F32)],
        scratch_shapes=[pltpu.VMEM((tm, d), BF16)],
        compiler_params=_cparams(2, 48),
        name="mlstm_inproj",
    )(x, sh, sc, w_in, wg)


def _head_norm_gate(hh, nw, o):
    mu = jnp.mean(hh, axis=1, keepdims=True)
    xc = hh - mu
    var = jnp.mean(xc * xc, axis=1, keepdims=True)
    return jax.nn.sigmoid(o) * (xc * lax.rsqrt(var + LN_EPS) * nw)


def _mlstm_chunk_kernel(q_ref, k_ref, v_ref, o_ref, g_ref, gb_ref, nw_ref, c0_ref, n0_ref, m0_ref,
                        y_ref, cout_ref, nout_ref, mout_ref, c_scr, n_scr, m_scr):
    ci = pl.program_id(1)
    L = q_ref.shape[0]

    @pl.when(ci == 0)
    def _():
        c_scr[...] = c0_ref[0]
        n_scr[...] = n0_ref[0]
        m_scr[...] = m0_ref[0]

    g = g_ref[...] + gb_ref[...]
    gi = g[:, :LANES]
    lf = _log_sigmoid(g[:, LANES:])
    row = lax.broadcasted_iota(I32, (L, L), 0)
    col = lax.broadcasted_iota(I32, (L, L), 1)
    causal = col <= row
    ltri = jnp.where(causal, 1.0, 0.0).astype(BF16)
    hi, mid, lo = _split3(lf)
    bcum = (jnp.dot(ltri, hi, preferred_element_type=F32)
            + jnp.dot(ltri, mid, preferred_element_type=F32)
            + jnp.dot(ltri, lo, preferred_element_type=F32))
    a = gi - bcum
    a_t = a.T
    scale = DK ** -0.5
    nt = (((1,), (1,)), ((), ()))
    tn_dims = (((0,), (0,)), ((), ()))
    for h in range(N_HEADS):
        ks = slice(h * DK, (h + 1) * DK)
        vs = slice(h * DV, (h + 1) * DV)
        qf = q_ref[:, ks] * scale
        qb = qf.astype(BF16)
        kf = k_ref[:, ks]
        kb = kf.astype(BF16)
        vb = v_ref[:, vs].astype(BF16)
        a_row = a_t[h:h + 1, :]
        a_col = a[:, h:h + 1]
        b_col = bcum[:, h:h + 1]
        m_prev = m_scr[h:h + 1, 0:1]
        amat = jnp.where(causal, a_row, -jnp.inf)
        mx = jnp.max(amat, axis=1, keepdims=True)
        m_inter = b_col + m_prev
        m_t = jnp.maximum(m_inter, b_col + mx)
        dm = jnp.exp(amat + (b_col - m_t))
        s = lax.dot_general(qb, kb, nt, preferred_element_type=F32)
        scores = s * dm
        inter = jnp.exp(m_inter - m_t)
        c_old = c_scr[h]
        n_old = n_scr[h:h + 1, :]
        qc = jnp.dot(qb, c_old.astype(BF16), preferred_element_type=F32)
        num = jnp.dot(scores.astype(BF16), vb, preferred_element_type=F32) + inter * qc
        qn = jnp.sum(qf * n_old, axis=1, keepdims=True)
        den = jnp.sum(scores, axis=1, keepdims=True) + inter * qn
        hh = num / jnp.maximum(jnp.abs(den), jnp.exp(-m_t))
        m_new = m_t[L - 1:L, :]
        b_last = b_col[L - 1:L, :]
        w_col = jnp.exp(b_last + a_col - m_new)
        decay = jnp.exp(b_last + m_prev - m_new)
        kw = kf * w_col
        c_scr[h] = decay * c_old + lax.dot_general(kw.astype(BF16), vb, tn_dims,
                                                   preferred_element_type=F32)
        n_scr[h:h + 1, :] = decay * n_old + jnp.sum(kw, axis=0, keepdims=True)
        m_scr[h:h + 1, :] = jnp.broadcast_to(m_new, (1, LANES))
        y_ref[:, vs] = _head_norm_gate(hh, nw_ref[:, vs], o_ref[:, vs]).astype(y_ref.dtype)

    @pl.when(ci == pl.num_programs(1) - 1)
    def _():
        cout_ref[0] = c_scr[...]
        nout_ref[0] = n_scr[...]
        mout_ref[0] = m_scr[...]


def _mlstm_chunks(proj, gates, gb, nw, c0, n0, m0, batch, seq):
    nc = seq // CHUNK
    qk = N_HEADS * DK
    vd = N_HEADS * DV
    row = lambda b, c: b * nc + c
    st4 = pl.BlockSpec((1, N_HEADS, DK, DV), lambda b, c: (b, 0, 0, 0))
    st3 = pl.BlockSpec((1, N_HEADS, LANES), lambda b, c: (b, 0, 0))
    return pl.pallas_call(
        _mlstm_chunk_kernel,
        grid=(batch, nc),
        in_specs=[pl.BlockSpec((CHUNK, qk), lambda b, c: (row(b, c), 0)),
                  pl.BlockSpec((CHUNK, qk), lambda b, c: (row(b, c), 1)),
                  pl.BlockSpec((CHUNK, vd), lambda b, c: (row(b, c), 1)),
                  pl.BlockSpec((CHUNK, vd), lambda b, c: (row(b, c), 2)),
                  pl.BlockSpec((CHUNK, 2 * LANES), lambda b, c: (row(b, c), 0)),
                  pl.BlockSpec((1, 2 * LANES), lambda b, c: (0, 0)),
                  pl.BlockSpec((1, vd), lambda b, c: (0, 0)),
                  st4, st3, st3],
        out_specs=[pl.BlockSpec((CHUNK, vd), lambda b, c: (row(b, c), 0)), st4, st3, st3],
        out_shape=[jax.ShapeDtypeStruct((batch * seq, vd), BF16),
                   jax.ShapeDtypeStruct((batch, N_HEADS, DK, DV), F32),
                   jax.ShapeDtypeStruct((batch, N_HEADS, LANES), F32),
                   jax.ShapeDtypeStruct((batch, N_HEADS, LANES), F32)],
        scratch_shapes=[pltpu.VMEM((N_HEADS, DK, DV), F32),
                        pltpu.VMEM((N_HEADS, LANES), F32),
                        pltpu.VMEM((N_HEADS, LANES), F32)],
        compiler_params=_cparams(2, 40),
        name="mlstm_chunks",
    )(proj, proj, proj, proj, gates, gb, nw, c0, n0, m0)


def _mlstm_short_kernel(seq, q_ref, k_ref, v_ref, o_ref, g_ref, gb_ref, nw_ref, c0_ref, n0_ref, m0_ref,
                        y_ref, cout_ref, nout_ref, mout_ref):
    bt = q_ref.shape[0]
    rows = q_ref.shape[1]
    scale = DK ** -0.5
    tn_dims = (((0,), (0,)), ((), ()))
    row = lax.broadcasted_iota(I32, (rows, LANES), 0)
    for b in range(bt):
        g = g_ref[b] + gb_ref[...]
        gi = g[:, :LANES]
        lf = _log_sigmoid(g[:, LANES:])
        bcum = lf
        mx = None
        sh = 1
        while sh < seq:
            bcum = bcum + jnp.where(row >= sh, pltpu.roll(bcum, sh, 0), 0.0)
            sh *= 2
        a = gi - bcum
        mx = a
        sh = 1
        while sh < seq:
            mx = jnp.maximum(mx, jnp.where(row >= sh, pltpu.roll(mx, sh, 0), -jnp.inf))
            sh *= 2
        m_prev = m0_ref[b]
        m_inter = bcum + m_prev
        m_t = jnp.maximum(m_inter, bcum + mx)
        cmt = bcum - m_t
        inter = jnp.exp(m_inter - m_t)
        einv = jnp.exp(-m_t)
        m_new = m_t[seq - 1:seq, :]
        b_last = bcum[seq - 1:seq, :]
        w = jnp.where(row < seq, jnp.exp(b_last + a - m_new), 0.0)
        decay = jnp.exp(b_last + m_prev - m_new)
        dstack = [jnp.where(row >= s, jnp.exp(cmt + a[s:s + 1, :]), 0.0) for s in range(seq)]
        mout_ref[b] = m_new
        for h in range(N_HEADS):
            ks = slice(h * DK, (h + 1) * DK)
            vs = slice(h * DV, (h + 1) * DV)
            q8 = q_ref[b, :, ks] * scale
            k8 = k_ref[b, :, ks]
            v8 = v_ref[b, :, vs]
            c_old = c0_ref[b, h]
            n_old = n0_ref[b, h:h + 1, :]
            inter_c = inter[:, h:h + 1]
            num = inter_c * jnp.dot(q8, c_old, preferred_element_type=F32)
            den = inter_c * jnp.sum(q8 * n_old, axis=1, keepdims=True)
            for s in range(seq):
                p = jnp.sum(q8 * k8[s:s + 1, :], axis=1, keepdims=True) * dstack[s][:, h:h + 1]
                num = num + p * v8[s:s + 1, :]
                den = den + p
            hh = num / jnp.maximum(jnp.abs(den), einv[:, h:h + 1])
            y_ref[b, :, vs] = _head_norm_gate(hh, nw_ref[:, vs], o_ref[b, :, vs])
            kw = k8 * w[:, h:h + 1]
            dec = decay[:, h:h + 1]
            cout_ref[b, h] = dec * c_old + lax.dot_general(kw, v8, tn_dims, preferred_element_type=F32)
            nout_ref[b, h:h + 1, :] = dec * n_old + jnp.sum(kw, axis=0, keepdims=True)


def _mlstm_short(proj, gates, gb, nw, c0, n0, m0, seq):
    batch, rows, _ = proj.shape
    qk = N_HEADS * DK
    vd = N_HEADS * DV
    bt = 4
    st4 = pl.BlockSpec((bt, N_HEADS, DK, DV), lambda i: (i, 0, 0, 0))
    st3 = pl.BlockSpec((bt, N_HEADS, LANES), lambda i: (i, 0, 0))
    stm = pl.BlockSpec((bt, 1, LANES), lambda i: (i, 0, 0))
    return pl.pallas_call(
        functools.partial(_mlstm_short_kernel, seq),
        grid=(batch // bt,),
        in_specs=[pl.BlockSpec((bt, rows, qk), lambda i: (i, 0, 0)),
                  pl.BlockSpec((bt, rows, qk), lambda i: (i, 0, 1)),
                  pl.BlockSpec((bt, rows, vd), lambda i: (i, 0, 1)),
                  pl.BlockSpec((bt, rows, vd), lambda i: (i, 0, 2)),
                  pl.BlockSpec((bt, rows, 2 * LANES), lambda i: (i, 0, 0)),
                  pl.BlockSpec((1, 2 * LANES), lambda i: (0, 0)),
                  pl.BlockSpec((1, vd), lambda i: (0, 0)),
                  st4, st3, stm],
        out_specs=[pl.BlockSpec((bt, rows, vd), lambda i: (i, 0, 0)), st4, st3, stm],
        out_shape=[jax.ShapeDtypeStruct((batch, rows, vd), F32),
                   jax.ShapeDtypeStruct((batch, N_HEADS, DK, DV), F32),
                   jax.ShapeDtypeStruct((batch, N_HEADS, LANES), F32),
                   jax.ShapeDtypeStruct((batch, 1, LANES), F32)],
        compiler_params=_cparams(1, 40),
        name="mlstm_short",
    )(proj, proj, proj, proj, gates, gb, nw, c0, n0, m0)


def _residual_ln(zbuf, xres_ref, gate_ref, lng_ref, lnb_ref, xo_ref, nxt):
    nc, tm, tn = zbuf.shape
    inv_d = 1.0 / (nc * tn)
    ssum = jnp.zeros((tm, 1), F32)
    for c in range(nc):
        sl = pl.ds(c * tn, tn)
        z = ALPHA * xres_ref[:, sl] + gate_ref[0, :, sl] * zbuf[c]
        zbuf[c] = z
        ssum = ssum + jnp.sum(z, axis=1, keepdims=True)
    mu = ssum * inv_d
    vsum = jnp.zeros((tm, 1), F32)
    for c in range(nc):
        zc = zbuf[c] - mu
        vsum = vsum + jnp.sum(zc * zc, axis=1, keepdims=True)
    rstd = lax.rsqrt(vsum * inv_d + LN_EPS)
    for c in range(nc):
        sl = pl.ds(c * tn, tn)
        xn = (zbuf[c] - mu) * rstd * lng_ref[:, sl] + lnb_ref[:, sl]
        xo_ref[:, sl] = xn
        if nxt is not None:
            sh_ref, sc_ref, ho_ref = nxt
            ho_ref[:, sl] = (xn * (1.0 + sc_ref[0, :, sl]) + sh_ref[0, :, sl]).astype(ho_ref.dtype)


def _proj_ln_kernel(lhs_ref, w_ref, xres_ref, gate_ref, lng_ref, lnb_ref, shn_ref, scn_ref,
                    xo_ref, ho_ref, ybuf):
    j = pl.program_id(1)
    ybuf[j] = jnp.dot(lhs_ref[...].astype(BF16), w_ref[...].astype(BF16), preferred_element_type=F32)

    @pl.when(j == pl.num_programs(1) - 1)
    def _():
        _residual_ln(ybuf, xres_ref, gate_ref, lng_ref, lnb_ref, xo_ref, (shn_ref, scn_ref, ho_ref))


def _proj_ln(lhs, w, xres, gate, lng, lnb, shn, scn, mod_idx, tm, tn):
    t, k = lhs.shape
    d = w.shape[1]
    r = gate.shape[1]
    mspec = pl.BlockSpec((1, r, d), lambda i, j: (mod_idx(i), 0, 0))
    vspec = pl.BlockSpec((1, d), lambda i, j: (0, 0))
    ospec = pl.BlockSpec((tm, d), lambda i, j: (i, 0))
    return pl.pallas_call(
        _proj_ln_kernel,
        grid=(t // tm, d // tn),
        in_specs=[pl.BlockSpec((tm, k), lambda i, j: (i, 0)),
                  pl.BlockSpec((k, tn), lambda i, j: (0, j)),
                  ospec, mspec, vspec, vspec, mspec, mspec],
        out_specs=[ospec, ospec],
        out_shape=[jax.ShapeDtypeStruct((t, d), F32), jax.ShapeDtypeStruct((t, d), F32)],
        scratch_shapes=[pltpu.VMEM((d // tn, tm, tn), F32)],
        compiler_params=_cparams(2, 56),
        name="proj_ln",
    )(lhs, w, xres, gate, lng, lnb, shn, scn)


def _gelu_tanh(x):
    return x * (0.5 * (1.0 + jnp.tanh(0.7978845608028654 * (x + 0.044715 * (x * x * x)))))


def _gmlp_in_kernel(h_ref, w_ref, b_ref, o_ref):
    acc = jnp.dot(h_ref[...].astype(BF16), w_ref[...].astype(BF16), preferred_element_type=F32)
    o_ref[...] = _gelu_tanh(acc + b_ref[...])


def _gmlp_in(h, w, b, tm):
    t, d = h.shape
    n = w.shape[1]
    tn = 1024
    return pl.pallas_call(
        _gmlp_in_kernel,
        grid=(t // tm, n // tn),
        in_specs=[pl.BlockSpec((tm, d), lambda i, j: (i, 0)),
                  pl.BlockSpec((d, tn), lambda i, j: (0, j)),
                  pl.BlockSpec((1, tn), lambda i, j: (0, j))],
        out_specs=pl.BlockSpec((tm, tn), lambda i, j: (i, j)),
        out_shape=jax.ShapeDtypeStruct((t, n), F32),
        compiler_params=_cparams(2, 48),
        name="gmlp_in",
    )(h, w, b.reshape(1, n))


def _gmlp_mix_kernel(u_ref, v_ref, g_ref, b_ref, mix_ref, bias_ref, o_ref, vn_ref):
    v = v_ref[...]
    mu = jnp.mean(v, axis=1, keepdims=True)
    xc = v - mu
    var = jnp.mean(xc * xc, axis=1, keepdims=True)
    vn = xc * lax.rsqrt(var + LN_EPS) * g_ref[...] + b_ref[...]
    vn_ref[...] = vn
    gd = v.shape[1] // N_GROUPS_B
    for g in range(N_GROUPS_B):
        sl = slice(g * gd, (g + 1) * gd)
        mixed = jnp.dot(mix_ref[g], vn[:, sl].astype(BF16), preferred_element_type=F32) + bias_ref[g]
        o_ref[:, sl] = (u_ref[:, sl] * mixed).astype(o_ref.dtype)


def _gmlp_mix(uv, nv_g, nv_b, mix, bias, tm):
    t, n2 = uv.shape
    di = n2 // 2
    return pl.pallas_call(
        _gmlp_mix_kernel,
        grid=(t // tm,),
        in_specs=[pl.BlockSpec((tm, di), lambda i: (i, 0)),
                  pl.BlockSpec((tm, di), lambda i: (i, 1)),
                  pl.BlockSpec((1, di), lambda i: (0, 0)),
                  pl.BlockSpec((1, di), lambda i: (0, 0)),
                  pl.BlockSpec(mix.shape, lambda i: (0, 0, 0)),
                  pl.BlockSpec(bias.shape, lambda i: (0, 0, 0))],
        out_specs=[pl.BlockSpec((tm, di), lambda i: (i, 0)),
                   pl.BlockSpec((tm, di), lambda i: (i, 0))],
        out_shape=[jax.ShapeDtypeStruct((t, di), BF16), jax.ShapeDtypeStruct((t, di), F32)],
        compiler_params=_cparams(1, 48),
        name="gmlp_mix",
    )(uv, uv, nv_g.reshape(1, di), nv_b.reshape(1, di), mix, bias)


def _router_kernel(h_ref, wr_ref, br_ref, cin_ref, idx_ref, wts_ref, cnt_ref, carry):
    i = pl.program_id(0)
    tr = h_ref.shape[0]

    @pl.when(i == 0)
    def _():
        carry[...] = cin_ref[...]

    nt = (((1,), (1,)), ((), ()))
    h = h_ref[...]
    hb = h.astype(BF16)
    hl = (h - hb.astype(F32)).astype(BF16)
    wr = wr_ref[...]
    wb = wr.astype(BF16)
    wl = (wr - wb.astype(F32)).astype(BF16)
    logits = (lax.dot_general(wb, hb, nt, preferred_element_type=F32)
              + lax.dot_general(wb, hl, nt, preferred_element_type=F32)
              + lax.dot_general(wl, hb, nt, preferred_element_type=F32))
    s = jax.nn.sigmoid(logits)
    sel = s + br_ref[...]
    epg = EXPERTS_PER_GROUP
    r = [sel[e:e + 1, :] for e in range(N_EXPERTS)]
    su = [s[e:e + 1, :] for e in range(N_EXPERTS)]

    def top2sum(v):
        best = v[0] + v[1]
        for x in range(epg):
            for y in range(x + 1, epg):
                if (x, y) != (0, 1):
                    best = jnp.maximum(best, v[x] + v[y])
        return best

    gs = [top2sum(r[g * epg:(g + 1) * epg]) for g in range(N_EXPERT_GROUPS)]
    gbest = gs[0]
    gidx = jnp.zeros((1, tr), I32)
    for g in range(1, N_EXPERT_GROUPS):
        better = gs[g] > gbest
        gidx = jnp.where(better, g, gidx)
        gbest = jnp.where(better, gs[g], gbest)
    v = list(r[:epg])
    sv = list(su[:epg])
    for g in range(1, N_EXPERT_GROUPS):
        pick = gidx == g
        for x in range(epg):
            v[x] = jnp.where(pick, r[g * epg + x], v[x])
            sv[x] = jnp.where(pick, su[g * epg + x], sv[x])
    i1 = jnp.zeros((1, tr), I32)
    b1 = v[0]
    w1 = sv[0]
    for x in range(1, epg):
        better = v[x] > b1
        i1 = jnp.where(better, x, i1)
        b1 = jnp.where(better, v[x], b1)
        w1 = jnp.where(better, sv[x], w1)
    i2 = jnp.zeros((1, tr), I32)
    b2 = jnp.full((1, tr), -jnp.inf, F32)
    w2 = jnp.zeros((1, tr), F32)
    for x in range(epg):
        take = (i1 != x) & (v[x] > b2)
        i2 = jnp.where(take, x, i2)
        b2 = jnp.where(take, v[x], b2)
        w2 = jnp.where(take, sv[x], w2)
    wsum = w1 + w2
    e1 = gidx * epg + i1
    e2 = gidx * epg + i2
    eio = lax.broadcasted_iota(I32, (N_EXPERTS, tr), 0)
    hit1 = eio == e1
    hit2 = eio == e2
    oh = jnp.where(hit1 | hit2, 1.0, 0.0)
    ri = lax.broadcasted_iota(I32, (tr, tr), 0)
    cj = lax.broadcasted_iota(I32, (tr, tr), 1)
    before = jnp.where(ri < cj, 1.0, 0.0).astype(BF16)
    rank = jnp.dot(oh.astype(BF16), before, preferred_element_type=F32) + carry[:, 0:1]
    rank1 = jnp.sum(jnp.where(hit1, rank, 0.0), axis=0, keepdims=True)
    rank2 = jnp.sum(jnp.where(hit2, rank, 0.0), axis=0, keepdims=True)
    idx_ref[0:1, :] = e1
    idx_ref[1:2, :] = e2
    idx_ref[2:3, :] = rank1.astype(I32)
    idx_ref[3:4, :] = rank2.astype(I32)
    idx_ref[4:8, :] = jnp.zeros((4, tr), I32)
    wts_ref[0:1, :] = w1 / wsum
    wts_ref[1:2, :] = w2 / wsum
    wts_ref[2:8, :] = jnp.zeros((6, tr), F32)
    carry[...] = carry[...] + jnp.sum(oh, axis=1, keepdims=True)
    cnt_ref[...] = carry[...]


def _router(h, w_router, b_router, tr):
    t, d = h.shape
    cin = jnp.zeros((N_EXPERTS, LANES), F32)
    return pl.pallas_call(
        _router_kernel,
        grid=(t // tr,),
        in_specs=[pl.BlockSpec((tr, d), lambda i: (i, 0)),
                  pl.BlockSpec((N_EXPERTS, d), lambda i: (0, 0)),
                  pl.BlockSpec((N_EXPERTS, 1), lambda i: (0, 0)),
                  pl.BlockSpec((N_EXPERTS, LANES), lambda i: (0, 0))],
        out_specs=[pl.BlockSpec((SUBLANES, tr), lambda i: (0, i)),
                   pl.BlockSpec((SUBLANES, tr), lambda i: (0, i)),
                   pl.BlockSpec((N_EXPERTS, LANES), lambda i: (0, 0))],
        out_shape=[jax.ShapeDtypeStruct((SUBLANES, t), I32),
                   jax.ShapeDtypeStruct((SUBLANES, t), F32),
                   jax.ShapeDtypeStruct((N_EXPERTS, LANES), F32)],
        scratch_shapes=[pltpu.VMEM((N_EXPERTS, LANES), F32)],
        compiler_params=_cparams(1, 32),
        name="moe_router",
    )(h, w_router.T, b_router.reshape(N_EXPERTS, 1), cin)


def _scatter_kernel(n_tok, pos_ref, cnt_ref, off_ref, h_ref, o_hbm, zero_scr, sem):
    i = pl.program_id(0)
    ts = h_ref.shape[0]
    base = i * ts

    def row_copy(src, r, p):
        return pltpu.make_async_copy(src.at[pl.ds(r, 1)], o_hbm.at[pl.ds(p, 1)], sem)

    def start_rows(r, c):
        row_copy(h_ref, r, pos_ref[base + r]).start()
        row_copy(h_ref, r, pos_ref[n_tok + base + r]).start()
        return c

    lax.fori_loop(0, ts, start_rows, 0)

    def wait_rows(r, c):
        row_copy(h_ref, 0, 0).wait()
        row_copy(h_ref, 0, 0).wait()
        return c

    lax.fori_loop(0, ts, wait_rows, 0)

    @pl.when(i == 0)
    def _():
        zero_scr[...] = jnp.zeros(zero_scr.shape, zero_scr.dtype)
        for e in range(N_EXPERTS + 1):
            lo = off_ref[e] + cnt_ref[e]
            hi = off_ref[e + 1]

            def start_zero(p, c):
                row_copy(zero_scr, 0, p).start()
                return c

            def wait_zero(p, c):
                row_copy(zero_scr, 0, 0).wait()
                return c

            lax.fori_loop(lo, hi, start_zero, 0)
            lax.fori_loop(lo, hi, wait_zero, 0)


def _scatter_rows(h, pos, cnt, off, n_rows, ts):
    t, d = h.shape
    return pl.pallas_call(
        functools.partial(_scatter_kernel, t),
        grid_spec=pltpu.PrefetchScalarGridSpec(
            num_scalar_prefetch=3,
            grid=(t // ts,),
            in_specs=[pl.BlockSpec((ts, d), lambda i, *_: (i, 0))],
            out_specs=pl.BlockSpec(memory_space=pl.ANY),
            scratch_shapes=[pltpu.VMEM((SUBLANES, d), h.dtype), pltpu.SemaphoreType.DMA]),
        out_shape=jax.ShapeDtypeStruct((n_rows, d), h.dtype),
        compiler_params=_cparams(1, 32),
        name="moe_scatter",
    )(pos, cnt, off, h)


def _expert_kernel(te_ref, nu_ref, x_ref, w1_ref, w2_ref, y_ref):
    @pl.when(pl.program_id(0) < nu_ref[0])
    def _():
        hid = jnp.dot(x_ref[...].astype(BF16), w1_ref[0], preferred_element_type=F32)
        f = hid.shape[1] // 2
        a = hid[:, :f]
        g = hid[:, f:]
        z = (g * jax.nn.sigmoid(g) * a).astype(BF16)
        y_ref[...] = jnp.dot(z, w2_ref[0], preferred_element_type=F32)

    @pl.when(pl.program_id(0) >= nu_ref[0])
    def _():
        y_ref[...] = jnp.zeros(y_ref.shape, y_ref.dtype)


def _experts(xs, w1, w2, tile_expert, n_used):
    p, d = xs.shape
    tm = MOE_TILE
    ff2 = w1.shape[2]
    row = lambda i, te, nu: (jnp.minimum(i, nu[0] - 1), 0)
    return pl.pallas_call(
        _expert_kernel,
        grid_spec=pltpu.PrefetchScalarGridSpec(
            num_scalar_prefetch=2,
            grid=(p // tm,),
            in_specs=[pl.BlockSpec((tm, d), row),
                      pl.BlockSpec((1, d, ff2), lambda i, te, nu: (te[i], 0, 0)),
                      pl.BlockSpec((1, ff2 // 2, d), lambda i, te, nu: (te[i], 0, 0))],
            out_specs=pl.BlockSpec((tm, d), lambda i, te, nu: (i, 0))),
        out_shape=jax.ShapeDtypeStruct((p, d), F32),
        compiler_params=_cparams(1, 48),
        name="moe_experts",
    )(tile_expert, n_used, xs, w1, w2)


def _combine_kernel(n_tok, tok_base, has_next, pos_ref, w_ref, y_hbm, xres_ref, gate_ref, lng_ref, lnb_ref,
                    *rest):
    if has_next:
        shn_ref, scn_ref, xo_ref, ho_ref, gbuf, fbuf, sem = rest
        nxt = (shn_ref, scn_ref, ho_ref)
    else:
        xo_ref, gbuf, fbuf, sem = rest
        nxt = None
    tc = xres_ref.shape[0]
    base = tok_base + pl.program_id(0) * tc

    def row_copy(slot, r, p):
        return pltpu.make_async_copy(y_hbm.at[pl.ds(p, 1)], gbuf.at[slot, pl.ds(r, 1)], sem)

    def start_rows(r, c):
        row_copy(0, r, pos_ref[base + r]).start()
        row_copy(1, r, pos_ref[n_tok + base + r]).start()
        return c

    lax.fori_loop(0, tc, start_rows, 0)

    def wait_rows(r, c):
        row_copy(0, 0, 0).wait()
        row_copy(1, 0, 0).wait()
        return c

    lax.fori_loop(0, tc, wait_rows, 0)
    fbuf[0] = w_ref[:, 0:1] * gbuf[0] + w_ref[:, 1:2] * gbuf[1]
    _residual_ln(fbuf, xres_ref, gate_ref, lng_ref, lnb_ref, xo_ref, nxt)


def _combine_ln(ys, pos, wts, n_tok, tok_base, xres, gate, lng, lnb, nxt_mods, mod_idx, tc):
    t, d = xres.shape
    r = gate.shape[1]
    has_next = nxt_mods is not None
    mspec = pl.BlockSpec((1, r, d), lambda i, *_: (mod_idx(i), 0, 0))
    vspec = pl.BlockSpec((1, d), lambda i, *_: (0, 0))
    ospec = pl.BlockSpec((tc, d), lambda i, *_: (i, 0))
    wbase = tok_base // tc
    in_specs = [pl.BlockSpec((tc, 2), lambda i, *_: (wbase + i, 0)),
                pl.BlockSpec(memory_space=pl.ANY), ospec, mspec, vspec, vspec]
    args = [wts, ys, xres, gate, lng, lnb]
    out_specs = [ospec]
    out_shape = [jax.ShapeDtypeStruct((t, d), F32)]
    if has_next:
        in_specs += [mspec, mspec]
        args += list(nxt_mods)
        out_specs.append(ospec)
        out_shape.append(jax.ShapeDtypeStruct((t, d), F32))
    return pl.pallas_call(
        functools.partial(_combine_kernel, n_tok, tok_base, has_next),
        grid_spec=pltpu.PrefetchScalarGridSpec(
            num_scalar_prefetch=1,
            grid=(t // tc,),
            in_specs=in_specs,
            out_specs=out_specs,
            scratch_shapes=[pltpu.VMEM((2, tc, d), F32), pltpu.VMEM((1, tc, d), F32),
                            pltpu.SemaphoreType.DMA]),
        out_shape=out_shape,
        compiler_params=_cparams(1, 40),
        name="moe_combine",
    )(pos, *args)


def _moe_plan(idx, cnt_pad, n_tok):
    tm = MOE_TILE
    n_tiles = (2 * n_tok + N_EXPERTS * (tm - 1) + tm - 1) // tm
    cnt = cnt_pad[:, 0].astype(I32)
    padded = ((cnt + tm - 1) // tm) * tm
    ends = jnp.cumsum(padded)
    off = ends - padded
    pos = jnp.concatenate([jnp.take(off, idx[0]) + idx[2], jnp.take(off, idx[1]) + idx[3]])
    n_used = jnp.maximum(ends[-1] // tm, 1)
    tile_start = jnp.minimum(jnp.arange(n_tiles, dtype=I32), n_used - 1) * tm
    tile_expert = jnp.minimum(jnp.sum(tile_start[:, None] >= ends[None, :], axis=1), N_EXPERTS - 1).astype(I32)
    n_rows = jnp.full((1,), n_tiles * tm, I32)
    off_ext = jnp.concatenate([off, ends[-1:], n_rows]).astype(I32)
    cnt = jnp.concatenate([cnt, jnp.zeros((1,), I32)])
    return pos.astype(I32), cnt, off_ext, tile_expert, n_used.reshape(1).astype(I32), n_tiles * tm


class _Rows:
    def __init__(self, n_seq, seq_len, tile):
        self.n_seq, self.seq_len, self.tile = n_seq, seq_len, tile
        self.per_row = seq_len < tile

    def mods(self, m):
        if self.per_row:
            return jnp.repeat(m, self.seq_len, axis=0).reshape(-1, self.tile, m.shape[1])
        return m[:, None, :]

    def mod_idx(self, i):
        return i if self.per_row else (i * self.tile) // self.seq_len


def _split_mods(mod_l, n_prompt):
    d = mod_l.shape[1] // 6
    cols = [mod_l[:, k * d:(k + 1) * d] for k in range(6)]
    return [c[:n_prompt] for c in cols], [c[n_prompt:] for c in cols]


def kernel(x_prompt, x_sample, state_mlstm_C, state_mlstm_n, state_mlstm_m, c_prompt, c_sample, w_ada, b_ada, ln_g, ln_b, a_w_in, a_b_gates, a_norm_w, a_w_out, b_w_in, b_b_in, b_norm_g, b_norm_b, b_w_s, b_b_s, b_w_out, w_router, b_router, w_expert_in, w_expert_out):
    bp, sp, d = x_prompt.shape
    bs, ss, _ = x_sample.shape
    tp = bp * sp
    ts = bs * ss
    n_tok = tp + ts
    qk = N_HEADS * DK
    vd = N_HEADS * DV
    n_main = 2 * qk + 2 * vd

    n_seq = bp + bs
    pad = (-n_seq) % SUBLANES
    c_all = jnp.concatenate([c_prompt, c_sample, jnp.zeros((pad, d), F32)])
    mod = _adaln(c_all, w_ada, b_ada)[:, :n_seq]

    sets = {"p": _Rows(bp, sp, 512), "s": _Rows(bs, ss, 256)}
    base = {"p": 0, "s": tp}
    x = {"p": x_prompt.reshape(tp, d), "s": x_sample.reshape(ts, d)}
    hm = {}
    outs = {}
    for layer in range(DEPTH):
        j = layer // 2
        mp, ms = _split_mods(mod[layer], bp)
        md = {"p": mp, "s": ms}
        lng = ln_g[layer]
        lnb = ln_b[layer]
        lhs = {}
        if layer % 2 == 0:
            w_in = a_w_in[j]
            wg = jnp.zeros((d, 2 * LANES), F32)
            wg = wg.at[:, :N_HEADS].set(w_in[:, n_main:n_main + N_HEADS])
            wg = wg.at[:, LANES:LANES + N_HEADS].set(w_in[:, n_main + N_HEADS:])
            gb = jnp.zeros((1, 2 * LANES), F32)
            gb = gb.at[0, :N_HEADS].set(a_b_gates[j, :N_HEADS])
            gb = gb.at[0, LANES:LANES + N_HEADS].set(a_b_gates[j, N_HEADS:])
            nw = a_norm_w[j].reshape(1, vd)
            proj = {}
            gates = {}
            for k, rs in sets.items():
                proj[k], gates[k] = _inproj(x[k], rs.mods(md[k][0]), rs.mods(md[k][1]), rs.mod_idx,
                                            w_in, wg, n_main, rs.tile)
            lhs["p"], c_p, n_p, m_p = _mlstm_chunks(
                proj["p"], gates["p"], gb, nw,
                jnp.zeros((bp, N_HEADS, DK, DV), F32), jnp.zeros((bp, N_HEADS, LANES), F32),
                jnp.zeros((bp, N_HEADS, LANES), F32), bp, sp)
            rpad = SUBLANES - ss
            proj_s = jnp.pad(proj["s"].reshape(bs, ss, n_main), ((0, 0), (0, rpad), (0, 0)))
            gates_s = jnp.pad(gates["s"].reshape(bs, ss, 2 * LANES), ((0, 0), (0, rpad), (0, 0)))
            m0 = jnp.pad(state_mlstm_m[j], ((0, 0), (0, LANES - N_HEADS)))[:, None, :]
            ypre_s, c_s, n_s, m_s = _mlstm_short(proj_s, gates_s, gb, nw, state_mlstm_C[j],
                                                 state_mlstm_n[j], m0, ss)
            lhs["s"] = ypre_s[:, :ss].reshape(ts, vd)
            outs["C_p"], outs["n_p"], outs["m_p"] = c_p, n_p, m_p[:, :, 0]
            outs["C_s"], outs["n_s"], outs["m_s"] = c_s, n_s, m_s[:, 0, :N_HEADS]
            w_out = a_w_out[j]
        else:
            tmix = 256
            ws = b_w_s[j]
            bsv = b_b_s[j]

            def mixing(l):
                tri = jnp.tril(jnp.ones((l, l), bool))
                wsl = jnp.where(tri, ws[:, :l, :l], 0.0)
                eye = jnp.eye(tmix // l, dtype=F32)
                mats = jax.vmap(lambda m: jnp.kron(eye, m))(wsl).astype(BF16)
                bias = jnp.tile(bsv[:, :l], (1, tmix // l))[:, :, None]
                return mats, bias

            for k, rs in sets.items():
                uv = _gmlp_in(hm[k], b_w_in[j], b_b_in[j], min(1024, rs.n_seq * rs.seq_len))
                mats, bias = mixing(min(CHUNK, rs.seq_len))
                lhs[k], vn = _gmlp_mix(uv, b_norm_g[j], b_norm_b[j], mats, bias, tmix)
                if k == "s":
                    outs["v_s"] = vn
            w_out = b_w_out[j]
        x1 = {}
        hf = {}
        for k, rs in sets.items():
            x1[k], hf[k] = _proj_ln(lhs[k], w_out, x[k], rs.mods(md[k][2]), lng[0:1], lnb[0:1],
                                    rs.mods(md[k][3]), rs.mods(md[k][4]), rs.mod_idx, rs.tile, 256)

        hf_all = jnp.concatenate([hf["p"], hf["s"]])
        idx, wts, cnt = _router(hf_all, w_router, b_router, 512)
        pos, cnt_i, off_ext, tile_expert, n_used, n_rows = _moe_plan(idx, cnt, n_tok)
        xsorted = _scatter_rows(hf_all, pos, cnt_i, off_ext, n_rows, 256)
        ysorted = _experts(xsorted, w_expert_in[layer].astype(BF16), w_expert_out[layer].astype(BF16),
                           tile_expert, n_used)
        wts2 = wts[:2].T
        nxt = _split_mods(mod[layer + 1], bp) if layer + 1 < DEPTH else None
        for ki, (k, rs) in enumerate(sets.items()):
            rc = _Rows(rs.n_seq, rs.seq_len, 256)
            nxt_mods = None if nxt is None else (rc.mods(nxt[ki][0]), rc.mods(nxt[ki][1]))
            res = _combine_ln(ysorted, pos, wts2, n_tok, base[k], x1[k], rc.mods(md[k][5]),
                              lng[1:2], lnb[1:2], nxt_mods, rc.mod_idx, 256)
            x[k] = res[0]
            if nxt is not None:
                hm[k] = res[1]

    return (x["p"].reshape(bp, sp, d), x["s"].reshape(bs, ss, d),
            outs["C_p"][None], outs["n_p"][None], outs["m_p"][None],
            outs["C_s"][None], outs["n_s"][None], outs["m_s"][None],
            outs["v_s"].reshape(bs, ss, -1)[None])
```

```python
import functools

import jax
import jax.numpy as jnp
from jax import lax
from jax.experimental import pallas as pl
from jax.experimental.pallas import tpu as pltpu

F32 = jnp.float32
BF16 = jnp.bfloat16
I32 = jnp.int32

DEPTH = 2
N_HEADS = 8
DK = 128
DV = 256
CHUNK = 128
N_GROUPS_B = 8
N_EXPERTS = 16
N_EXPERT_GROUPS = 4
EXPERTS_PER_GROUP = 4
ALPHA = float((2 * DEPTH) ** 0.25)
LN_EPS = 1e-5

LANES = 128
SUBLANES = 8
MIB = 1024 * 1024
MOE_TILE = 256


def _cparams(n_axes, vmem_mib):
    return pltpu.CompilerParams(
        dimension_semantics=("arbitrary",) * n_axes,
        vmem_limit_bytes=int(vmem_mib * MIB))


def _split3(x):
    hi = x.astype(BF16)
    r1 = x - hi.astype(F32)
    mid = r1.astype(BF16)
    lo = (r1 - mid.astype(F32)).astype(BF16)
    return hi, mid, lo


def _log_sigmoid(x):
    return jnp.minimum(x, 0.0) - jnp.log1p(jnp.exp(-jnp.abs(x)))


def _adaln_kernel(c_ref, w_ref, b_ref, o_ref):
    c = c_ref[...]
    a = (c * jax.nn.sigmoid(c)).astype(BF16)
    o_ref[0] = jnp.dot(a, w_ref[0].astype(BF16), preferred_element_type=F32) + b_ref[0]


def _adaln(c_all, w_ada, b_ada):
    depth, d, n = w_ada.shape
    r = c_all.shape[0]
    tn = 1024
    return pl.pallas_call(
        _adaln_kernel,
        grid=(depth, n // tn),
        in_specs=[pl.BlockSpec((r, d), lambda l, j: (0, 0)),
                  pl.BlockSpec((1, d, tn), lambda l, j: (l, 0, j)),
                  pl.BlockSpec((1, 1, tn), lambda l, j: (l, 0, j))],
        out_specs=pl.BlockSpec((1, r, tn), lambda l, j: (l, 0, j)),
        out_shape=jax.ShapeDtypeStruct((depth, r, n), F32),
        compiler_params=_cparams(2, 32),
        name="adaln",
    )(c_all, w_ada, b_ada.reshape(depth, 1, n))


def _inproj_kernel(x_ref, sh_ref, sc_ref, w_ref, wg_ref, proj_ref, gates_ref, hin_ref):
    @pl.when(pl.program_id(1) == 0)
    def _():
        h = x_ref[...] * (1.0 + sc_ref[0]) + sh_ref[0]
        hb = h.astype(BF16)
        hin_ref[...] = hb
        h_lo = (h - hb.astype(F32)).astype(BF16)
        wg = wg_ref[...]
        wg_hi = wg.astype(BF16)
        wg_lo = (wg - wg_hi.astype(F32)).astype(BF16)
        gates_ref[...] = (jnp.dot(hb, wg_hi, preferred_element_type=F32)
                          + jnp.dot(h_lo, wg_hi, preferred_element_type=F32)
                          + jnp.dot(hb, wg_lo, preferred_element_type=F32))

    proj_ref[...] = jnp.dot(hin_ref[...], w_ref[...].astype(BF16), preferred_element_type=F32)


def _inproj(x, sh, sc, mod_idx, w_in, wg, n_main, tm):
    t, d = x.shape
    tn = 512
    r = sh.shape[1]
    mspec = pl.BlockSpec((1, r, d), lambda i, j: (mod_idx(i), 0, 0))
    return pl.pallas_call(
        _inproj_kernel,
        grid=(t // tm, n_main // tn),
        in_specs=[pl.BlockSpec((tm, d), lambda i, j: (i, 0)), mspec, mspec,
                  pl.BlockSpec((d, tn), lambda i, j: (0, j)),
                  pl.BlockSpec(wg.shape, lambda i, j: (0, 0))],
        out_specs=[pl.BlockSpec((tm, tn), lambda i, j: (i, j)),
                   pl.BlockSpec((tm, wg.shape[1]), lambda i, j: (i, 0))],
        out_shape=[jax.ShapeDtypeStruct((t, n_main), F32),
                   jax.ShapeDtypeStruct((t, wg.shape[1]), F32)],
        scratch_shapes=[pltpu.VMEM((tm, d), BF16)],
        compiler_params=_cparams(2, 48),
        name="mlstm_inproj",
    )(x, sh, sc, w_in, wg)


def _head_norm_gate(hh, nw, o):
    mu = jnp.mean(hh, axis=1, keepdims=True)
    xc = hh - mu
    var = jnp.mean(xc * xc, axis=1, keepdims=True)
    return jax.nn.sigmoid(o) * (xc * lax.rsqrt(var + LN_EPS) * nw)


def _mlstm_chunk_kernel(q_ref, k_ref, v_ref, o_ref, g_ref, gb_ref, nw_ref, c0_ref, n0_ref, m0_ref,
                        y_ref, cout_ref, nout_ref, mout_ref, c_scr, n_scr, m_scr):
    ci = pl.program_id(1)
    L = q_ref.shape[0]

    @pl.when(ci == 0)
    def _():
        c_scr[...] = c0_ref[0]
        n_scr[...] = n0_ref[0]
        m_scr[...] = m0_ref[0]

    g = g_ref[...] + gb_ref[...]
    gi = g[:, :LANES]
    lf = _log_sigmoid(g[:, LANES:])
    row = lax.broadcasted_iota(I32, (L, L), 0)
    col = lax.broadcasted_iota(I32, (L, L), 1)
    causal = col <= row
    ltri = jnp.where(causal, 1.0, 0.0).astype(BF16)
    hi, mid, lo = _split3(lf)
    bcum = (jnp.dot(ltri, hi, preferred_element_type=F32)
            + jnp.dot(ltri, mid, preferred_element_type=F32)
            + jnp.dot(ltri, lo, preferred_element_type=F32))
    a = gi - bcum
    a_t = a.T
    scale = DK ** -0.5
    nt = (((1,), (1,)), ((), ()))
    tn_dims = (((0,), (0,)), ((), ()))
    for h in range(N_HEADS):
        ks = slice(h * DK, (h + 1) * DK)
        vs = slice(h * DV, (h + 1) * DV)
        qf = q_ref[:, ks] * scale
        qb = qf.astype(BF16)
        kf = k_ref[:, ks]
        kb = kf.astype(BF16)
        vb = v_ref[:, vs].astype(BF16)
        a_row = a_t[h:h + 1, :]
        a_col = a[:, h:h + 1]
        b_col = bcum[:, h:h + 1]
        m_prev = m_scr[h:h + 1, 0:1]
        amat = jnp.where(causal, a_row, -jnp.inf)
        mx = jnp.max(amat, axis=1, keepdims=True)
        m_inter = b_col + m_prev
        m_t = jnp.maximum(m_inter, b_col + mx)
        dm = jnp.exp(amat + (b_col - m_t))
        s = lax.dot_general(qb, kb, nt, preferred_element_type=F32)
        scores = s * dm
        inter = jnp.exp(m_inter - m_t)
        c_old = c_scr[h]
        n_old = n_scr[h:h + 1, :]
        qc = jnp.dot(qb, c_old.astype(BF16), preferred_element_type=F32)
        num = jnp.dot(scores.astype(BF16), vb, preferred_element_type=F32) + inter * qc
        qn = jnp.sum(qf * n_old, axis=1, keepdims=True)
        den = jnp.sum(scores, axis=1, keepdims=True) + inter * qn
        hh = num / jnp.maximum(jnp.abs(den), jnp.exp(-m_t))
        m_new = m_t[L - 1:L, :]
        b_last = b_col[L - 1:L, :]
        w_col = jnp.exp(b_last + a_col - m_new)
        decay = jnp.exp(b_last + m_prev - m_new)
        kw = kf * w_col
        c_scr[h] = decay * c_old + lax.dot_general(kw.astype(BF16), vb, tn_dims,
                                                   preferred_element_type=F32)
        n_scr[h:h + 1, :] = decay * n_old + jnp.sum(kw, axis=0, keepdims=True)
        m_scr[h:h + 1, :] = jnp.broadcast_to(m_new, (1, LANES))
        y_ref[:, vs] = _head_norm_gate(hh, nw_ref[:, vs], o_ref[:, vs]).astype(y_ref.dtype)

    @pl.when(ci == pl.num_programs(1) - 1)
    def _():
        cout_ref[0] = c_scr[...]
        nout_ref[0] = n_scr[...]
        mout_ref[0] = m_scr[...]


def _mlstm_chunks(proj, gates, gb, nw, c0, n0, m0, batch, seq):
    nc = seq // CHUNK
    qk = N_HEADS * DK
    vd = N_HEADS * DV
    row = lambda b, c: b * nc + c
    st4 = pl.BlockSpec((1, N_HEADS, DK, DV), lambda b, c: (b, 0, 0, 0))
    st3 = pl.BlockSpec((1, N_HEADS, LANES), lambda b, c: (b, 0, 0))
    return pl.pallas_call(
        _mlstm_chunk_kernel,
        grid=(batch, nc),
        in_specs=[pl.BlockSpec((CHUNK, qk), lambda b, c: (row(b, c), 0)),
                  pl.BlockSpec((CHUNK, qk), lambda b, c: (row(b, c), 1)),
                  pl.BlockSpec((CHUNK, vd), lambda b, c: (row(b, c), 1)),
                  pl.BlockSpec((CHUNK, vd), lambda b, c: (row(b, c), 2)),
                  pl.BlockSpec((CHUNK, 2 * LANES), lambda b, c: (row(b, c), 0)),
                  pl.BlockSpec((1, 2 * LANES), lambda b, c: (0, 0)),
                  pl.BlockSpec((1, vd), lambda b, c: (0, 0)),
                  st4, st3, st3],
        out_specs=[pl.BlockSpec((CHUNK, vd), lambda b, c: (row(b, c), 0)), st4, st3, st3],
        out_shape=[jax.ShapeDtypeStruct((batch * seq, vd), BF16),
                   jax.ShapeDtypeStruct((batch, N_HEADS, DK, DV), F32),
                   jax.ShapeDtypeStruct((batch, N_HEADS, LANES), F32),
                   jax.ShapeDtypeStruct((batch, N_HEADS, LANES), F32)],
        scratch_shapes=[pltpu.VMEM((N_HEADS, DK, DV), F32),
                        pltpu.VMEM((N_HEADS, LANES), F32),
                        pltpu.VMEM((N_HEADS, LANES), F32)],
        compiler_params=_cparams(2, 40),
        name="mlstm_chunks",
    )(proj, proj, proj, proj, gates, gb, nw, c0, n0, m0)


def _per_head(x, width):
    return jnp.concatenate(
        [jnp.broadcast_to(x[..., h:h + 1], x.shape[:-1] + (width,)) for h in range(N_HEADS)], axis=-1)


def _head_sums(x, width):
    lane = lax.broadcasted_iota(I32, x.shape[:-1] + (LANES,), x.ndim - 1)
    out = jnp.zeros(x.shape[:-1] + (LANES,), F32)
    for h in range(N_HEADS):
        s = jnp.sum(x[..., h * width:(h + 1) * width], axis=-1, keepdims=True)
        out = jnp.where(lane == h, s, out)
    return out


def _mlstm_short_kernel(seq, q_ref, k_ref, v_ref, o_ref, g_ref, gb_ref, nw_ref, c0_ref, n0_ref, m0_ref,
                        y_ref, cout_ref, nout_ref, mout_ref, qc_scr, kw_scr):
    bt, rows, _ = q_ref.shape
    scale = DK ** -0.5
    tn_dims = (((0,), (0,)), ((), ()))
    row = lax.broadcasted_iota(I32, (bt, rows, LANES), 1)
    g = g_ref[...] + gb_ref[...]
    gi = g[:, :, :LANES]
    lf = _log_sigmoid(g[:, :, LANES:])
    bcum = jnp.zeros_like(lf)
    for s in range(seq):
        bcum = bcum + jnp.where(row >= s, lf[:, s:s + 1, :], 0.0)
    a = gi - bcum
    mx = jnp.full_like(a, -jnp.inf)
    for s in range(seq):
        mx = jnp.maximum(mx, jnp.where(row >= s, a[:, s:s + 1, :], -jnp.inf))
    m_prev = m0_ref[...]
    m_inter = bcum + m_prev
    m_t = jnp.maximum(m_inter, bcum + mx)
    cmt = bcum - m_t
    inter = jnp.exp(m_inter - m_t)
    einv = jnp.exp(-m_t)
    m_new = m_t[:, seq - 1:seq, :]
    b_last = bcum[:, seq - 1:seq, :]
    w = jnp.where(row < seq, jnp.exp(b_last + a - m_new), 0.0)
    decay = jnp.exp(b_last + m_prev - m_new)
    mout_ref[...] = m_new

    q = q_ref[...] * scale
    k = k_ref[...]
    v = v_ref[...]
    n_old = n0_ref[...]
    for b in range(bt):
        for h in range(N_HEADS):
            qc_scr[b, :, h * DV:(h + 1) * DV] = jnp.dot(
                (q_ref[b, :, h * DK:(h + 1) * DK] * scale).astype(BF16), c0_ref[b, h].astype(BF16),
                preferred_element_type=F32)
    den = inter * _head_sums(q * n_old, DK)
    num = _per_head(inter, DV) * qc_scr[...]
    for s in range(seq):
        p = _head_sums(q * k[:, s:s + 1, :], DK) * jnp.where(row >= s, jnp.exp(cmt + a[:, s:s + 1, :]), 0.0)
        den = den + p
        num = num + _per_head(p, DV) * v[:, s:s + 1, :]
    hh = num * _per_head(1.0 / jnp.maximum(jnp.abs(den), einv), DV)
    mu = _head_sums(hh, DV) * (1.0 / DV)
    xc = hh - _per_head(mu, DV)
    var = _head_sums(xc * xc, DV) * (1.0 / DV)
    hn = xc * _per_head(lax.rsqrt(var + LN_EPS), DV) * nw_ref[...]
    y_ref[...] = jax.nn.sigmoid(o_ref[...]) * hn

    kw = k * _per_head(w, DK)
    kw_scr[...] = kw
    nout_ref[...] = _per_head(decay, DK) * n_old + jnp.sum(kw, axis=1, keepdims=True)
    for b in range(bt):
        for h in range(N_HEADS):
            cout_ref[b, h] = (decay[b, :, h:h + 1] * c0_ref[b, h]
                              + lax.dot_general(kw_scr[b, :, h * DK:(h + 1) * DK],
                                                v_ref[b, :, h * DV:(h + 1) * DV],
                                                tn_dims, preferred_element_type=F32))


def _mlstm_short(proj, gates, gb, nw, c0, n0, m0, seq):
    batch, rows, _ = proj.shape
    qk = N_HEADS * DK
    vd = N_HEADS * DV
    bt = 8
    st4 = pl.BlockSpec((bt, N_HEADS, DK, DV), lambda i: (i, 0, 0, 0))
    st3 = pl.BlockSpec((bt, 1, qk), lambda i: (i, 0, 0))
    stm = pl.BlockSpec((bt, 1, LANES), lambda i: (i, 0, 0))
    return pl.pallas_call(
        functools.partial(_mlstm_short_kernel, seq),
        grid=(batch // bt,),
        in_specs=[pl.BlockSpec((bt, rows, qk), lambda i: (i, 0, 0)),
                  pl.BlockSpec((bt, rows, qk), lambda i: (i, 0, 1)),
                  pl.BlockSpec((bt, rows, vd), lambda i: (i, 0, 1)),
                  pl.BlockSpec((bt, rows, vd), lambda i: (i, 0, 2)),
                  pl.BlockSpec((bt, rows, 2 * LANES), lambda i: (i, 0, 0)),
                  pl.BlockSpec((1, 2 * LANES), lambda i: (0, 0)),
                  pl.BlockSpec((1, vd), lambda i: (0, 0)),
                  st4, st3, stm],
        out_specs=[pl.BlockSpec((bt, rows, vd), lambda i: (i, 0, 0)), st4, st3, stm],
        out_shape=[jax.ShapeDtypeStruct((batch, rows, vd), F32),
                   jax.ShapeDtypeStruct((batch, N_HEADS, DK, DV), F32),
                   jax.ShapeDtypeStruct((batch, 1, qk), F32),
                   jax.ShapeDtypeStruct((batch, 1, LANES), F32)],
        scratch_shapes=[pltpu.VMEM((bt, rows, vd), F32), pltpu.VMEM((bt, rows, qk), F32)],
        compiler_params=_cparams(1, 48),
        name="mlstm_short",
    )(proj, proj, proj, proj, gates, gb, nw, c0, n0, m0)


def _residual_ln(zbuf, xres_ref, gate_ref, lng_ref, lnb_ref, xo_ref, nxt):
    nc, tm, tn = zbuf.shape
    inv_d = 1.0 / (nc * tn)
    ssum = jnp.zeros((tm, 1), F32)
    for c in range(nc):
        sl = pl.ds(c * tn, tn)
        z = ALPHA * xres_ref[:, sl] + gate_ref[0, :, sl] * zbuf[c]
        zbuf[c] = z
        ssum = ssum + jnp.sum(z, axis=1, keepdims=True)
    mu = ssum * inv_d
    vsum = jnp.zeros((tm, 1), F32)
    for c in range(nc):
        zc = zbuf[c] - mu
        vsum = vsum + jnp.sum(zc * zc, axis=1, keepdims=True)
    rstd = lax.rsqrt(vsum * inv_d + LN_EPS)
    for c in range(nc):
        sl = pl.ds(c * tn, tn)
        xn = (zbuf[c] - mu) * rstd * lng_ref[:, sl] + lnb_ref[:, sl]
        xo_ref[:, sl] = xn
        if nxt is not None:
            sh_ref, sc_ref, ho_ref = nxt
            ho_ref[:, sl] = (xn * (1.0 + sc_ref[0, :, sl]) + sh_ref[0, :, sl]).astype(ho_ref.dtype)


def _proj_ln_kernel(lhs_ref, w_ref, xres_ref, gate_ref, lng_ref, lnb_ref, shn_ref, scn_ref,
                    xo_ref, ho_ref, ybuf):
    j = pl.program_id(1)
    ybuf[j] = jnp.dot(lhs_ref[...].astype(BF16), w_ref[...].astype(BF16), preferred_element_type=F32)

    @pl.when(j == pl.num_programs(1) - 1)
    def _():
        _residual_ln(ybuf, xres_ref, gate_ref, lng_ref, lnb_ref, xo_ref, (shn_ref, scn_ref, ho_ref))


def _proj_ln(lhs, w, xres, gate, lng, lnb, shn, scn, mod_idx, tm, tn):
    t, k = lhs.shape
    d = w.shape[1]
    r = gate.shape[1]
    mspec = pl.BlockSpec((1, r, d), lambda i, j: (mod_idx(i), 0, 0))
    vspec = pl.BlockSpec((1, d), lambda i, j: (0, 0))
    ospec = pl.BlockSpec((tm, d), lambda i, j: (i, 0))
    return pl.pallas_call(
        _proj_ln_kernel,
        grid=(t // tm, d // tn),
        in_specs=[pl.BlockSpec((tm, k), lambda i, j: (i, 0)),
                  pl.BlockSpec((k, tn), lambda i, j: (0, j)),
                  ospec, mspec, vspec, vspec, mspec, mspec],
        out_specs=[ospec, ospec],
        out_shape=[jax.ShapeDtypeStruct((t, d), F32), jax.ShapeDtypeStruct((t, d), F32)],
        scratch_shapes=[pltpu.VMEM((d // tn, tm, tn), F32)],
        compiler_params=_cparams(2, 56),
        name="proj_ln",
    )(lhs, w, xres, gate, lng, lnb, shn, scn)


def _gelu_tanh(x):
    return x * (0.5 * (1.0 + jnp.tanh(0.7978845608028654 * (x + 0.044715 * (x * x * x)))))


def _gmlp_in_kernel(h_ref, w_ref, b_ref, o_ref):
    acc = jnp.dot(h_ref[...].astype(BF16), w_ref[...].astype(BF16), preferred_element_type=F32)
    o_ref[...] = _gelu_tanh(acc + b_ref[...])


def _gmlp_in(h, w, b, tm):
    t, d = h.shape
    n = w.shape[1]
    tn = 1024
    return pl.pallas_call(
        _gmlp_in_kernel,
        grid=(t // tm, n // tn),
        in_specs=[pl.BlockSpec((tm, d), lambda i, j: (i, 0)),
                  pl.BlockSpec((d, tn), lambda i, j: (0, j)),
                  pl.BlockSpec((1, tn), lambda i, j: (0, j))],
        out_specs=pl.BlockSpec((tm, tn), lambda i, j: (i, j)),
        out_shape=jax.ShapeDtypeStruct((t, n), F32),
        compiler_params=_cparams(2, 48),
        name="gmlp_in",
    )(h, w, b.reshape(1, n))


def _gmlp_mix_kernel(u_ref, v_ref, g_ref, b_ref, mix_ref, bias_ref, o_ref, vn_ref):
    v = v_ref[...]
    mu = jnp.mean(v, axis=1, keepdims=True)
    xc = v - mu
    var = jnp.mean(xc * xc, axis=1, keepdims=True)
    vn = xc * lax.rsqrt(var + LN_EPS) * g_ref[...] + b_ref[...]
    vn_ref[...] = vn
    gd = v.shape[1] // N_GROUPS_B
    for g in range(N_GROUPS_B):
        sl = slice(g * gd, (g + 1) * gd)
        mixed = jnp.dot(mix_ref[g], vn[:, sl].astype(BF16), preferred_element_type=F32) + bias_ref[g]
        o_ref[:, sl] = (u_ref[:, sl] * mixed).astype(o_ref.dtype)


def _gmlp_mix(uv, nv_g, nv_b, mix, bias, tm):
    t, n2 = uv.shape
    di = n2 // 2
    return pl.pallas_call(
        _gmlp_mix_kernel,
        grid=(t // tm,),
        in_specs=[pl.BlockSpec((tm, di), lambda i: (i, 0)),
                  pl.BlockSpec((tm, di), lambda i: (i, 1)),
                  pl.BlockSpec((1, di), lambda i: (0, 0)),
                  pl.BlockSpec((1, di), lambda i: (0, 0)),
                  pl.BlockSpec(mix.shape, lambda i: (0, 0, 0)),
                  pl.BlockSpec(bias.shape, lambda i: (0, 0, 0))],
        out_specs=[pl.BlockSpec((tm, di), lambda i: (i, 0)),
                   pl.BlockSpec((tm, di), lambda i: (i, 0))],
        out_shape=[jax.ShapeDtypeStruct((t, di), BF16), jax.ShapeDtypeStruct((t, di), F32)],
        compiler_params=_cparams(1, 48),
        name="gmlp_mix",
    )(uv, uv, nv_g.reshape(1, di), nv_b.reshape(1, di), mix, bias)


def _router_kernel(h_ref, wr_ref, br_ref, cin_ref, idx_ref, wts_ref, cnt_ref, carry):
    i = pl.program_id(0)
    tr = h_ref.shape[0]

    @pl.when(i == 0)
    def _():
        carry[...] = cin_ref[...]

    nt = (((1,), (1,)), ((), ()))
    h = h_ref[...]
    hb = h.astype(BF16)
    hl = (h - hb.astype(F32)).astype(BF16)
    wr = wr_ref[...]
    wb = wr.astype(BF16)
    wl = (wr - wb.astype(F32)).astype(BF16)
    logits = (lax.dot_general(wb, hb, nt, preferred_element_type=F32)
              + lax.dot_general(wb, hl, nt, preferred_element_type=F32)
              + lax.dot_general(wl, hb, nt, preferred_element_type=F32))
    s = jax.nn.sigmoid(logits)
    sel = s + br_ref[...]
    epg = EXPERTS_PER_GROUP
    r = [sel[e:e + 1, :] for e in range(N_EXPERTS)]
    su = [s[e:e + 1, :] for e in range(N_EXPERTS)]

    def top2sum(v):
        best = v[0] + v[1]
        for x in range(epg):
            for y in range(x + 1, epg):
                if (x, y) != (0, 1):
                    best = jnp.maximum(best, v[x] + v[y])
        return best

    gs = [top2sum(r[g * epg:(g + 1) * epg]) for g in range(N_EXPERT_GROUPS)]
    gbest = gs[0]
    gidx = jnp.zeros((1, tr), I32)
    for g in range(1, N_EXPERT_GROUPS):
        better = gs[g] > gbest
        gidx = jnp.where(better, g, gidx)
        gbest = jnp.where(better, gs[g], gbest)
    v = list(r[:epg])
    sv = list(su[:epg])
    for g in range(1, N_EXPERT_GROUPS):
        pick = gidx == g
        for x in range(epg):
            v[x] = jnp.where(pick, r[g * epg + x], v[x])
            sv[x] = jnp.where(pick, su[g * epg + x], sv[x])
    i1 = jnp.zeros((1, tr), I32)
    b1 = v[0]
    w1 = sv[0]
    for x in range(1, epg):
        better = v[x] > b1
        i1 = jnp.where(better, x, i1)
        b1 = jnp.where(better, v[x], b1)
        w1 = jnp.where(better, sv[x], w1)
    i2 = jnp.zeros((1, tr), I32)
    b2 = jnp.full((1, tr), -jnp.inf, F32)
    w2 = jnp.zeros((1, tr), F32)
    for x in range(epg):
        take = (i1 != x) & (v[x] > b2)
        i2 = jnp.where(take, x, i2)
        b2 = jnp.where(take, v[x], b2)
        w2 = jnp.where(take, sv[x], w2)
    wsum = w1 + w2
    e1 = gidx * epg + i1
    e2 = gidx * epg + i2
    eio = lax.broadcasted_iota(I32, (N_EXPERTS, tr), 0)
    hit1 = eio == e1
    hit2 = eio == e2
    oh = jnp.where(hit1 | hit2, 1.0, 0.0)
    ri = lax.broadcasted_iota(I32, (tr, tr), 0)
    cj = lax.broadcasted_iota(I32, (tr, tr), 1)
    before = jnp.where(ri < cj, 1.0, 0.0).astype(BF16)
    rank = jnp.dot(oh.astype(BF16), before, preferred_element_type=F32) + carry[:, 0:1]
    rank1 = jnp.sum(jnp.where(hit1, rank, 0.0), axis=0, keepdims=True)
    rank2 = jnp.sum(jnp.where(hit2, rank, 0.0), axis=0, keepdims=True)
    idx_ref[0:1, :] = e1
    idx_ref[1:2, :] = e2
    idx_ref[2:3, :] = rank1.astype(I32)
    idx_ref[3:4, :] = rank2.astype(I32)
    idx_ref[4:8, :] = jnp.zeros((4, tr), I32)
    wts_ref[0:1, :] = w1 / wsum
    wts_ref[1:2, :] = w2 / wsum
    wts_ref[2:8, :] = jnp.zeros((6, tr), F32)
    carry[...] = carry[...] + jnp.sum(oh, axis=1, keepdims=True)
    cnt_ref[...] = carry[...]


def _router(h, w_router, b_router, tr):
    t, d = h.shape
    cin = jnp.zeros((N_EXPERTS, LANES), F32)
    return pl.pallas_call(
        _router_kernel,
        grid=(t // tr,),
        in_specs=[pl.BlockSpec((tr, d), lambda i: (i, 0)),
                  pl.BlockSpec((N_EXPERTS, d), lambda i: (0, 0)),
                  pl.BlockSpec((N_EXPERTS, 1), lambda i: (0, 0)),
                  pl.BlockSpec((N_EXPERTS, LANES), lambda i: (0, 0))],
        out_specs=[pl.BlockSpec((SUBLANES, tr), lambda i: (0, i)),
                   pl.BlockSpec((SUBLANES, tr), lambda i: (0, i)),
                   pl.BlockSpec((N_EXPERTS, LANES), lambda i: (0, 0))],
        out_shape=[jax.ShapeDtypeStruct((SUBLANES, t), I32),
                   jax.ShapeDtypeStruct((SUBLANES, t), F32),
                   jax.ShapeDtypeStruct((N_EXPERTS, LANES), F32)],
        scratch_shapes=[pltpu.VMEM((N_EXPERTS, LANES), F32)],
        compiler_params=_cparams(1, 32),
        name="moe_router",
    )(h, w_router.T, b_router.reshape(N_EXPERTS, 1), cin)


def _scatter_kernel(n_tok, pos_ref, cnt_ref, off_ref, h_ref, o_hbm, zero_scr, sem):
    i = pl.program_id(0)
    ts = h_ref.shape[0]
    base = i * ts

    def row_copy(src, r, p):
        return pltpu.make_async_copy(src.at[pl.ds(r, 1)], o_hbm.at[pl.ds(p, 1)], sem)

    def start_rows(r, c):
        row_copy(h_ref, r, pos_ref[base + r]).start()
        row_copy(h_ref, r, pos_ref[n_tok + base + r]).start()
        return c

    lax.fori_loop(0, ts, start_rows, 0, unroll=8)
    for _ in range(2):
        pltpu.make_async_copy(h_ref, o_hbm.at[pl.ds(0, ts)], sem).wait()

    @pl.when(i == 0)
    def _():
        zero_scr[...] = jnp.zeros(zero_scr.shape, zero_scr.dtype)
        for e in range(N_EXPERTS + 1):
            lo = off_ref[e] + cnt_ref[e]
            hi = off_ref[e + 1]

            def start_zero(p, c):
                row_copy(zero_scr, 0, p).start()
                return c

            def wait_zero(p, c):
                row_copy(zero_scr, 0, 0).wait()
                return c

            lax.fori_loop(lo, hi, start_zero, 0)
            lax.fori_loop(lo, hi, wait_zero, 0)


def _scatter_rows(h, pos, cnt, off, n_rows, ts):
    t, d = h.shape
    return pl.pallas_call(
        functools.partial(_scatter_kernel, t),
        grid_spec=pltpu.PrefetchScalarGridSpec(
            num_scalar_prefetch=3,
            grid=(t // ts,),
            in_specs=[pl.BlockSpec((ts, d), lambda i, *_: (i, 0))],
            out_specs=pl.BlockSpec(memory_space=pl.ANY),
            scratch_shapes=[pltpu.VMEM((SUBLANES, d), h.dtype), pltpu.SemaphoreType.DMA]),
        out_shape=jax.ShapeDtypeStruct((n_rows, d), h.dtype),
        compiler_params=_cparams(1, 32),
        name="moe_scatter",
    )(pos, cnt, off, h)


def _expert_kernel(te_ref, nu_ref, x_ref, w1_ref, w2_ref, y_ref):
    @pl.when(pl.program_id(0) < nu_ref[0])
    def _():
        hid = jnp.dot(x_ref[...].astype(BF16), w1_ref[0], preferred_element_type=F32)
        f = hid.shape[1] // 2
        a = hid[:, :f]
        g = hid[:, f:]
        z = (g * jax.nn.sigmoid(g) * a).astype(BF16)
        y_ref[...] = jnp.dot(z, w2_ref[0], preferred_element_type=F32)

    @pl.when(pl.program_id(0) >= nu_ref[0])
    def _():
        y_ref[...] = jnp.zeros(y_ref.shape, y_ref.dtype)


def _experts(xs, w1, w2, tile_expert, n_used):
    p, d = xs.shape
    tm = MOE_TILE
    ff2 = w1.shape[2]
    row = lambda i, te, nu: (jnp.minimum(i, nu[0] - 1), 0)
    return pl.pallas_call(
        _expert_kernel,
        grid_spec=pltpu.PrefetchScalarGridSpec(
            num_scalar_prefetch=2,
            grid=(p // tm,),
            in_specs=[pl.BlockSpec((tm, d), row),
                      pl.BlockSpec((1, d, ff2), lambda i, te, nu: (te[i], 0, 0)),
                      pl.BlockSpec((1, ff2 // 2, d), lambda i, te, nu: (te[i], 0, 0))],
            out_specs=pl.BlockSpec((tm, d), lambda i, te, nu: (i, 0))),
        out_shape=jax.ShapeDtypeStruct((p, d), F32),
        compiler_params=_cparams(1, 48),
        name="moe_experts",
    )(tile_expert, n_used, xs, w1, w2)


def _combine_kernel(n_tok, tok_base, has_next, pos_ref, w_ref, y_hbm, xres_ref, gate_ref, lng_ref, lnb_ref,
                    *rest):
    if has_next:
        shn_ref, scn_ref, xo_ref, ho_ref, gbuf, fbuf, sem = rest
        nxt = (shn_ref, scn_ref, ho_ref)
    else:
        xo_ref, gbuf, fbuf, sem = rest
        nxt = None
    tc = xres_ref.shape[0]
    i = pl.program_id(0)
    n_steps = pl.num_programs(0)

    def gather_tile(step, slot):
        base = tok_base + step * tc

        def start_rows(r, c):
            for k in range(2):
                pltpu.make_async_copy(y_hbm.at[pl.ds(pos_ref[k * n_tok + base + r], 1)],
                                      gbuf.at[slot, k, pl.ds(r, 1)], sem.at[slot]).start()
            return c

        lax.fori_loop(0, tc, start_rows, 0, unroll=8)

    slot = i % 2

    @pl.when(i == 0)
    def _():
        gather_tile(0, 0)

    @pl.when(i + 1 < n_steps)
    def _():
        gather_tile(i + 1, 1 - slot)

    for k in range(2):
        pltpu.make_async_copy(y_hbm.at[pl.ds(0, tc)], gbuf.at[slot, k], sem.at[slot]).wait()
    fbuf[0] = w_ref[:, 0:1] * gbuf[slot, 0] + w_ref[:, 1:2] * gbuf[slot, 1]
    _residual_ln(fbuf, xres_ref, gate_ref, lng_ref, lnb_ref, xo_ref, nxt)


def _combine_ln(ys, pos, wts, n_tok, tok_base, xres, gate, lng, lnb, nxt_mods, mod_idx, tc):
    t, d = xres.shape
    r = gate.shape[1]
    has_next = nxt_mods is not None
    mspec = pl.BlockSpec((1, r, d), lambda i, *_: (mod_idx(i), 0, 0))
    vspec = pl.BlockSpec((1, d), lambda i, *_: (0, 0))
    ospec = pl.BlockSpec((tc, d), lambda i, *_: (i, 0))
    wbase = tok_base // tc
    in_specs = [pl.BlockSpec((tc, 2), lambda i, *_: (wbase + i, 0)),
                pl.BlockSpec(memory_space=pl.ANY), ospec, mspec, vspec, vspec]
    args = [wts, ys, xres, gate, lng, lnb]
    out_specs = [ospec]
    out_shape = [jax.ShapeDtypeStruct((t, d), F32)]
    if has_next:
        in_specs += [mspec, mspec]
        args += list(nxt_mods)
        out_specs.append(ospec)
        out_shape.append(jax.ShapeDtypeStruct((t, d), F32))
    return pl.pallas_call(
        functools.partial(_combine_kernel, n_tok, tok_base, has_next),
        grid_spec=pltpu.PrefetchScalarGridSpec(
            num_scalar_prefetch=1,
            grid=(t // tc,),
            in_specs=in_specs,
            out_specs=out_specs,
            scratch_shapes=[pltpu.VMEM((2, 2, tc, d), F32), pltpu.VMEM((1, tc, d), F32),
                            pltpu.SemaphoreType.DMA((2,))]),
        out_shape=out_shape,
        compiler_params=_cparams(1, 40),
        name="moe_combine",
    )(pos, *args)


def _moe_plan(idx, cnt_pad, n_tok):
    tm = MOE_TILE
    n_tiles = (2 * n_tok + N_EXPERTS * (tm - 1) + tm - 1) // tm
    cnt = cnt_pad[:, 0].astype(I32)
    padded = ((cnt + tm - 1) // tm) * tm
    ends = jnp.cumsum(padded)
    off = ends - padded
    pos = jnp.concatenate([jnp.take(off, idx[0]) + idx[2], jnp.take(off, idx[1]) + idx[3]])
    n_used = jnp.maximum(ends[-1] // tm, 1)
    tile_start = jnp.minimum(jnp.arange(n_tiles, dtype=I32), n_used - 1) * tm
    tile_expert = jnp.minimum(jnp.sum(tile_start[:, None] >= ends[None, :], axis=1), N_EXPERTS - 1).astype(I32)
    n_rows = jnp.full((1,), n_tiles * tm, I32)
    off_ext = jnp.concatenate([off, ends[-1:], n_rows]).astype(I32)
    cnt = jnp.concatenate([cnt, jnp.zeros((1,), I32)])
    return pos.astype(I32), cnt, off_ext, tile_expert, n_used.reshape(1).astype(I32), n_tiles * tm


class _Rows:
    def __init__(self, n_seq, seq_len, tile):
        self.n_seq, self.seq_len, self.tile = n_seq, seq_len, tile
        self.per_row = seq_len < tile

    def mods(self, m):
        if self.per_row:
            return jnp.repeat(m, self.seq_len, axis=0).reshape(-1, self.tile, m.shape[1])
        return m[:, None, :]

    def mod_idx(self, i):
        return i if self.per_row else (i * self.tile) // self.seq_len


def _split_mods(mod_l, n_prompt):
    d = mod_l.shape[1] // 6
    cols = [mod_l[:, k * d:(k + 1) * d] for k in range(6)]
    return [c[:n_prompt] for c in cols], [c[n_prompt:] for c in cols]


def kernel(x_prompt, x_sample, state_mlstm_C, state_mlstm_n, state_mlstm_m, c_prompt, c_sample, w_ada, b_ada, ln_g, ln_b, a_w_in, a_b_gates, a_norm_w, a_w_out, b_w_in, b_b_in, b_norm_g, b_norm_b, b_w_s, b_b_s, b_w_out, w_router, b_router, w_expert_in, w_expert_out):
    bp, sp, d = x_prompt.shape
    bs, ss, _ = x_sample.shape
    tp = bp * sp
    ts = bs * ss
    n_tok = tp + ts
    qk = N_HEADS * DK
    vd = N_HEADS * DV
    n_main = 2 * qk + 2 * vd

    n_seq = bp + bs
    pad = (-n_seq) % SUBLANES
    c_all = jnp.concatenate([c_prompt, c_sample, jnp.zeros((pad, d), F32)])
    mod = _adaln(c_all, w_ada, b_ada)[:, :n_seq]

    sets = {"p": _Rows(bp, sp, 512), "s": _Rows(bs, ss, 256)}
    base = {"p": 0, "s": tp}
    x = {"p": x_prompt.reshape(tp, d), "s": x_sample.reshape(ts, d)}
    hm = {}
    outs = {}
    for layer in range(DEPTH):
        j = layer // 2
        mp, ms = _split_mods(mod[layer], bp)
        md = {"p": mp, "s": ms}
        lng = ln_g[layer]
        lnb = ln_b[layer]
        lhs = {}
        if layer % 2 == 0:
            w_in = a_w_in[j]
            wg = jnp.zeros((d, 2 * LANES), F32)
            wg = wg.at[:, :N_HEADS].set(w_in[:, n_main:n_main + N_HEADS])
            wg = wg.at[:, LANES:LANES + N_HEADS].set(w_in[:, n_main + N_HEADS:])
            gb = jnp.zeros((1, 2 * LANES), F32)
            gb = gb.at[0, :N_HEADS].set(a_b_gates[j, :N_HEADS])
            gb = gb.at[0, LANES:LANES + N_HEADS].set(a_b_gates[j, N_HEADS:])
            nw = a_norm_w[j].reshape(1, vd)
            proj = {}
            gates = {}
            for k, rs in {"p": _Rows(bp, sp, 1024), "s": sets["s"]}.items():
                proj[k], gates[k] = _inproj(x[k], rs.mods(md[k][0]), rs.mods(md[k][1]), rs.mod_idx,
                                            w_in, wg, n_main, rs.tile)
            lhs["p"], c_p, n_p, m_p = _mlstm_chunks(
                proj["p"], gates["p"], gb, nw,
                jnp.zeros((bp, N_HEADS, DK, DV), F32), jnp.zeros((bp, N_HEADS, LANES), F32),
                jnp.zeros((bp, N_HEADS, LANES), F32), bp, sp)
            rpad = SUBLANES - ss
            proj_s = jnp.pad(proj["s"].reshape(bs, ss, n_main), ((0, 0), (0, rpad), (0, 0)))
            gates_s = jnp.pad(gates["s"].reshape(bs, ss, 2 * LANES), ((0, 0), (0, rpad), (0, 0)))
            m0 = jnp.pad(state_mlstm_m[j], ((0, 0), (0, LANES - N_HEADS)))[:, None, :]
            ypre_s, c_s, n_s, m_s = _mlstm_short(proj_s, gates_s, gb, nw, state_mlstm_C[j],
                                                 state_mlstm_n[j].reshape(bs, 1, qk), m0, ss)
            n_s = n_s.reshape(bs, N_HEADS, DK)
            lhs["s"] = ypre_s[:, :ss].reshape(ts, vd)
            outs["C_p"], outs["n_p"], outs["m_p"] = c_p, n_p, m_p[:, :, 0]
            outs["C_s"], outs["n_s"], outs["m_s"] = c_s, n_s, m_s[:, 0, :N_HEADS]
            w_out = a_w_out[j]
        else:
            tmix = 256
            ws = b_w_s[j]
            bsv = b_b_s[j]

            def mixing(l):
                tri = jnp.tril(jnp.ones((l, l), bool))
                wsl = jnp.where(tri, ws[:, :l, :l], 0.0)
                eye = jnp.eye(tmix // l, dtype=F32)
                mats = jax.vmap(lambda m: jnp.kron(eye, m))(wsl).astype(BF16)
                bias = jnp.tile(bsv[:, :l], (1, tmix // l))[:, :, None]
                return mats, bias

            for k, rs in sets.items():
                uv = _gmlp_in(hm[k], b_w_in[j], b_b_in[j], min(1024, rs.n_seq * rs.seq_len))
                mats, bias = mixing(min(CHUNK, rs.seq_len))
                lhs[k], vn = _gmlp_mix(uv, b_norm_g[j], b_norm_b[j], mats, bias, tmix)
                if k == "s":
                    outs["v_s"] = vn
            w_out = b_w_out[j]
        x1 = {}
        hf = {}
        for k, rs in sets.items():
            x1[k], hf[k] = _proj_ln(lhs[k], w_out, x[k], rs.mods(md[k][2]), lng[0:1], lnb[0:1],
                                    rs.mods(md[k][3]), rs.mods(md[k][4]), rs.mod_idx, rs.tile, 256)

        hf_all = jnp.concatenate([hf["p"], hf["s"]])
        idx, wts, cnt = _router(hf_all, w_router, b_router, 512)
        pos, cnt_i, off_ext, tile_expert, n_used, n_rows = _moe_plan(idx, cnt, n_tok)
        xsorted = _scatter_rows(hf_all, pos, cnt_i, off_ext, n_rows, 256)
        ysorted = _experts(xsorted, w_expert_in[layer].astype(BF16), w_expert_out[layer].astype(BF16),
                           tile_expert, n_used)
        wts2 = wts[:2].T
        nxt = _split_mods(mod[layer + 1], bp) if layer + 1 < DEPTH else None
        for ki, (k, rs) in enumerate(sets.items()):
            rc = _Rows(rs.n_seq, rs.seq_len, 256)
            nxt_mods = None if nxt is None else (rc.mods(nxt[ki][0]), rc.mods(nxt[ki][1]))
            res = _combine_ln(ysorted, pos, wts2, n_tok, base[k], x1[k], rc.mods(md[k][5]),
                              lng[1:2], lnb[1:2], nxt_mods, rc.mod_idx, 256)
            x[k] = res[0]
            if nxt is not None:
                hm[k] = res[1]

    return (x["p"].reshape(bp, sp, d), x["s"].reshape(bs, ss, d),
            outs["C_p"][None], outs["n_p"][None], outs["m_p"][None],
            outs["C_s"][None], outs["n_s"][None], outs["m_s"][None],
            outs["v_s"].reshape(bs, ss, -1)[None])
```

```python
import functools

import jax
import jax.numpy as jnp
from jax import lax
from jax.experimental import pallas as pl
from jax.experimental.pallas import tpu as pltpu

F32 = jnp.float32
BF16 = jnp.bfloat16
I32 = jnp.int32

DEPTH = 2
N_HEADS = 8
DK = 128
DV = 256
CHUNK = 128
N_GROUPS_B = 8
N_EXPERTS = 16
N_EXPERT_GROUPS = 4
EXPERTS_PER_GROUP = 4
ALPHA = float((2 * DEPTH) ** 0.25)
LN_EPS = 1e-5

LANES = 128
SUBLANES = 8
MIB = 1024 * 1024
MOE_TILE = 256
CAST_ROWS = 256


def _cparams(n_axes, vmem_mib):
    return pltpu.CompilerParams(
        dimension_semantics=("arbitrary",) * n_axes,
        vmem_limit_bytes=int(vmem_mib * MIB))


def _split3(x):
    hi = x.astype(BF16)
    r1 = x - hi.astype(F32)
    mid = r1.astype(BF16)
    lo = (r1 - mid.astype(F32)).astype(BF16)
    return hi, mid, lo


def _log_sigmoid(x):
    return jnp.minimum(x, 0.0) - jnp.log1p(jnp.exp(-jnp.abs(x)))


def _adaln_kernel(c_ref, w_ref, b_ref, o_ref):
    c = c_ref[...]
    a = (c * jax.nn.sigmoid(c)).astype(BF16)
    o_ref[0] = jnp.dot(a, w_ref[0].astype(BF16), preferred_element_type=F32) + b_ref[0]


def _adaln(c_all, w_ada, b_ada):
    depth, d, n = w_ada.shape
    r = c_all.shape[0]
    tn = 1024
    return pl.pallas_call(
        _adaln_kernel,
        grid=(depth, n // tn),
        in_specs=[pl.BlockSpec((r, d), lambda l, j: (0, 0)),
                  pl.BlockSpec((1, d, tn), lambda l, j: (l, 0, j)),
                  pl.BlockSpec((1, 1, tn), lambda l, j: (l, 0, j))],
        out_specs=pl.BlockSpec((1, r, tn), lambda l, j: (l, 0, j)),
        out_shape=jax.ShapeDtypeStruct((depth, r, n), F32),
        compiler_params=_cparams(2, 32),
        name="adaln",
    )(c_all, w_ada, b_ada.reshape(depth, 1, n))


def _inproj_kernel(x_ref, sh_ref, sc_ref, w_ref, wg_ref, proj_ref, gates_ref, hin_ref):
    @pl.when(pl.program_id(1) == 0)
    def _():
        h = x_ref[...] * (1.0 + sc_ref[0]) + sh_ref[0]
        hb = h.astype(BF16)
        hin_ref[...] = hb
        h_lo = (h - hb.astype(F32)).astype(BF16)
        wg = wg_ref[...]
        wg_hi = wg.astype(BF16)
        wg_lo = (wg - wg_hi.astype(F32)).astype(BF16)
        gates_ref[...] = (jnp.dot(hb, wg_hi, preferred_element_type=F32)
                          + jnp.dot(h_lo, wg_hi, preferred_element_type=F32)
                          + jnp.dot(hb, wg_lo, preferred_element_type=F32))

    proj_ref[...] = jnp.dot(hin_ref[...], w_ref[...].astype(BF16),
                            preferred_element_type=F32).astype(proj_ref.dtype)


def _inproj(x, sh, sc, mod_idx, w_in, wg, n_main, tm, out_dtype):
    t, d = x.shape
    tn = 512
    r = sh.shape[1]
    mspec = pl.BlockSpec((1, r, d), lambda i, j: (mod_idx(i), 0, 0))
    return pl.pallas_call(
        _inproj_kernel,
        grid=(t // tm, n_main // tn),
        in_specs=[pl.BlockSpec((tm, d), lambda i, j: (i, 0)), mspec, mspec,
                  pl.BlockSpec((d, tn), lambda i, j: (0, j)),
                  pl.BlockSpec(wg.shape, lambda i, j: (0, 0))],
        out_specs=[pl.BlockSpec((tm, tn), lambda i, j: (i, j)),
                   pl.BlockSpec((tm, wg.shape[1]), lambda i, j: (i, 0))],
        out_shape=[jax.ShapeDtypeStruct((t, n_main), out_dtype),
                   jax.ShapeDtypeStruct((t, wg.shape[1]), F32)],
        scratch_shapes=[pltpu.VMEM((tm, d), BF16)],
        compiler_params=_cparams(2, 48),
        name="mlstm_inproj",
    )(x, sh, sc, w_in, wg)


def _head_norm_gate(hh, nw, o):
    mu = jnp.mean(hh, axis=1, keepdims=True)
    xc = hh - mu
    var = jnp.mean(xc * xc, axis=1, keepdims=True)
    return jax.nn.sigmoid(o) * (xc * lax.rsqrt(var + LN_EPS) * nw)


def _mlstm_chunk_kernel(q_ref, k_ref, v_ref, o_ref, g_ref, gb_ref, nw_ref, c0_ref, n0_ref, m0_ref,
                        y_ref, cout_ref, nout_ref, mout_ref, c_scr, n_scr, m_scr):
    ci = pl.program_id(1)
    L = q_ref.shape[0]

    @pl.when(ci == 0)
    def _():
        c_scr[...] = c0_ref[0]
        n_scr[...] = n0_ref[0]
        m_scr[...] = m0_ref[0]

    g = g_ref[...] + gb_ref[...]
    gi = g[:, :LANES]
    lf = _log_sigmoid(g[:, LANES:])
    row = lax.broadcasted_iota(I32, (L, L), 0)
    col = lax.broadcasted_iota(I32, (L, L), 1)
    causal = col <= row
    ltri = jnp.where(causal, 1.0, 0.0).astype(BF16)
    hi, mid, lo = _split3(lf)
    bcum = (jnp.dot(ltri, hi, preferred_element_type=F32)
            + jnp.dot(ltri, mid, preferred_element_type=F32)
            + jnp.dot(ltri, lo, preferred_element_type=F32))
    a = gi - bcum
    a_t = a.T
    scale = DK ** -0.5
    nt = (((1,), (1,)), ((), ()))
    tn_dims = (((0,), (0,)), ((), ()))
    for h in range(N_HEADS):
        ks = slice(h * DK, (h + 1) * DK)
        vs = slice(h * DV, (h + 1) * DV)
        qf = q_ref[:, ks].astype(F32) * scale
        qb = qf.astype(BF16)
        kf = k_ref[:, ks].astype(F32)
        kb = kf.astype(BF16)
        vb = v_ref[:, vs].astype(BF16)
        a_row = a_t[h:h + 1, :]
        a_col = a[:, h:h + 1]
        b_col = bcum[:, h:h + 1]
        m_prev = m_scr[h:h + 1, 0:1]
        amat = jnp.where(causal, a_row, -jnp.inf)
        mx = jnp.max(amat, axis=1, keepdims=True)
        m_inter = b_col + m_prev
        m_t = jnp.maximum(m_inter, b_col + mx)
        dm = jnp.exp(amat + (b_col - m_t))
        s = lax.dot_general(qb, kb, nt, preferred_element_type=F32)
        scores = s * dm
        inter = jnp.exp(m_inter - m_t)
        c_old = c_scr[h]
        n_old = n_scr[h:h + 1, :]
        qc = jnp.dot(qb, c_old.astype(BF16), preferred_element_type=F32)
        num = jnp.dot(scores.astype(BF16), vb, preferred_element_type=F32) + inter * qc
        qn = jnp.sum(qf * n_old, axis=1, keepdims=True)
        den = jnp.sum(scores, axis=1, keepdims=True) + inter * qn
        hh = num / jnp.maximum(jnp.abs(den), jnp.exp(-m_t))
        m_new = m_t[L - 1:L, :]
        b_last = b_col[L - 1:L, :]
        w_col = jnp.exp(b_last + a_col - m_new)
        decay = jnp.exp(b_last + m_prev - m_new)
        kw = kf * w_col
        c_scr[h] = decay * c_old + lax.dot_general(kw.astype(BF16), vb, tn_dims,
                                                   preferred_element_type=F32)
        n_scr[h:h + 1, :] = decay * n_old + jnp.sum(kw, axis=0, keepdims=True)
        m_scr[h:h + 1, :] = jnp.broadcast_to(m_new, (1, LANES))
        y_ref[:, vs] = _head_norm_gate(hh, nw_ref[:, vs], o_ref[:, vs].astype(F32)).astype(y_ref.dtype)

    @pl.when(ci == pl.num_programs(1) - 1)
    def _():
        cout_ref[0] = c_scr[...]
        nout_ref[0] = n_scr[...]
        mout_ref[0] = m_scr[...]


def _mlstm_chunks(proj, gates, gb, nw, c0, n0, m0, batch, seq):
    nc = seq // CHUNK
    qk = N_HEADS * DK
    vd = N_HEADS * DV
    row = lambda b, c: b * nc + c
    st4 = pl.BlockSpec((1, N_HEADS, DK, DV), lambda b, c: (b, 0, 0, 0))
    st3 = pl.BlockSpec((1, N_HEADS, LANES), lambda b, c: (b, 0, 0))
    return pl.pallas_call(
        _mlstm_chunk_kernel,
        grid=(batch, nc),
        in_specs=[pl.BlockSpec((CHUNK, qk), lambda b, c: (row(b, c), 0)),
                  pl.BlockSpec((CHUNK, qk), lambda b, c: (row(b, c), 1)),
                  pl.BlockSpec((CHUNK, vd), lambda b, c: (row(b, c), 1)),
                  pl.BlockSpec((CHUNK, vd), lambda b, c: (row(b, c), 2)),
                  pl.BlockSpec((CHUNK, 2 * LANES), lambda b, c: (row(b, c), 0)),
                  pl.BlockSpec((1, 2 * LANES), lambda b, c: (0, 0)),
                  pl.BlockSpec((1, vd), lambda b, c: (0, 0)),
                  st4, st3, st3],
        out_specs=[pl.BlockSpec((CHUNK, vd), lambda b, c: (row(b, c), 0)), st4, st3, st3],
        out_shape=[jax.ShapeDtypeStruct((batch * seq, vd), BF16),
                   jax.ShapeDtypeStruct((batch, N_HEADS, DK, DV), F32),
                   jax.ShapeDtypeStruct((batch, N_HEADS, LANES), F32),
                   jax.ShapeDtypeStruct((batch, N_HEADS, LANES), F32)],
        scratch_shapes=[pltpu.VMEM((N_HEADS, DK, DV), F32),
                        pltpu.VMEM((N_HEADS, LANES), F32),
                        pltpu.VMEM((N_HEADS, LANES), F32)],
        compiler_params=_cparams(2, 40),
        name="mlstm_chunks",
    )(proj, proj, proj, proj, gates, gb, nw, c0, n0, m0)


def _per_head(x, width):
    return jnp.concatenate(
        [jnp.broadcast_to(x[..., h:h + 1], x.shape[:-1] + (width,)) for h in range(N_HEADS)], axis=-1)


def _head_sums(x, width):
    lane = lax.broadcasted_iota(I32, x.shape[:-1] + (LANES,), x.ndim - 1)
    out = jnp.zeros(x.shape[:-1] + (LANES,), F32)
    for h in range(N_HEADS):
        s = jnp.sum(x[..., h * width:(h + 1) * width], axis=-1, keepdims=True)
        out = jnp.where(lane == h, s, out)
    return out


def _mlstm_short_kernel(seq, q_ref, k_ref, v_ref, o_ref, g_ref, gb_ref, nw_ref, c0_ref, n0_ref, m0_ref,
                        y_ref, cout_ref, nout_ref, mout_ref, qc_scr, kw_scr):
    bt, rows, _ = q_ref.shape
    scale = DK ** -0.5
    tn_dims = (((0,), (0,)), ((), ()))
    row = lax.broadcasted_iota(I32, (bt, rows, LANES), 1)
    g = g_ref[...] + gb_ref[...]
    gi = g[:, :, :LANES]
    lf = _log_sigmoid(g[:, :, LANES:])
    bcum = jnp.zeros_like(lf)
    for s in range(seq):
        bcum = bcum + jnp.where(row >= s, lf[:, s:s + 1, :], 0.0)
    a = gi - bcum
    mx = jnp.full_like(a, -jnp.inf)
    for s in range(seq):
        mx = jnp.maximum(mx, jnp.where(row >= s, a[:, s:s + 1, :], -jnp.inf))
    m_prev = m0_ref[...]
    m_inter = bcum + m_prev
    m_t = jnp.maximum(m_inter, bcum + mx)
    cmt = bcum - m_t
    inter = jnp.exp(m_inter - m_t)
    einv = jnp.exp(-m_t)
    m_new = m_t[:, seq - 1:seq, :]
    b_last = bcum[:, seq - 1:seq, :]
    w = jnp.where(row < seq, jnp.exp(b_last + a - m_new), 0.0)
    decay = jnp.exp(b_last + m_prev - m_new)
    mout_ref[...] = m_new

    q = q_ref[...] * scale
    k = k_ref[...]
    v = v_ref[...]
    n_old = n0_ref[...]
    for b in range(bt):
        for h in range(N_HEADS):
            qc_scr[b, :, h * DV:(h + 1) * DV] = jnp.dot(
                (q_ref[b, :, h * DK:(h + 1) * DK] * scale).astype(BF16), c0_ref[b, h].astype(BF16),
                preferred_element_type=F32)
    den = inter * _head_sums(q * n_old, DK)
    num = _per_head(inter, DV) * qc_scr[...]
    for s in range(seq):
        p = _head_sums(q * k[:, s:s + 1, :], DK) * jnp.where(row >= s, jnp.exp(cmt + a[:, s:s + 1, :]), 0.0)
        den = den + p
        num = num + _per_head(p, DV) * v[:, s:s + 1, :]
    hh = num * _per_head(1.0 / jnp.maximum(jnp.abs(den), einv), DV)
    mu = _head_sums(hh, DV) * (1.0 / DV)
    xc = hh - _per_head(mu, DV)
    var = _head_sums(xc * xc, DV) * (1.0 / DV)
    hn = xc * _per_head(lax.rsqrt(var + LN_EPS), DV) * nw_ref[...]
    y_ref[...] = jax.nn.sigmoid(o_ref[...]) * hn

    kw = k * _per_head(w, DK)
    kw_scr[...] = kw
    nout_ref[...] = _per_head(decay, DK) * n_old + jnp.sum(kw, axis=1, keepdims=True)
    for b in range(bt):
        for h in range(N_HEADS):
            cout_ref[b, h] = (decay[b, :, h:h + 1] * c0_ref[b, h]
                              + lax.dot_general(kw_scr[b, :, h * DK:(h + 1) * DK],
                                                v_ref[b, :, h * DV:(h + 1) * DV],
                                                tn_dims, preferred_element_type=F32))


def _mlstm_short(proj, gates, gb, nw, c0, n0, m0, seq):
    batch, rows, _ = proj.shape
    qk = N_HEADS * DK
    vd = N_HEADS * DV
    bt = 8
    st4 = pl.BlockSpec((bt, N_HEADS, DK, DV), lambda i: (i, 0, 0, 0))
    st3 = pl.BlockSpec((bt, 1, qk), lambda i: (i, 0, 0))
    stm = pl.BlockSpec((bt, 1, LANES), lambda i: (i, 0, 0))
    return pl.pallas_call(
        functools.partial(_mlstm_short_kernel, seq),
        grid=(batch // bt,),
        in_specs=[pl.BlockSpec((bt, rows, qk), lambda i: (i, 0, 0)),
                  pl.BlockSpec((bt, rows, qk), lambda i: (i, 0, 1)),
                  pl.BlockSpec((bt, rows, vd), lambda i: (i, 0, 1)),
                  pl.BlockSpec((bt, rows, vd), lambda i: (i, 0, 2)),
                  pl.BlockSpec((bt, rows, 2 * LANES), lambda i: (i, 0, 0)),
                  pl.BlockSpec((1, 2 * LANES), lambda i: (0, 0)),
                  pl.BlockSpec((1, vd), lambda i: (0, 0)),
                  st4, st3, stm],
        out_specs=[pl.BlockSpec((bt, rows, vd), lambda i: (i, 0, 0)), st4, st3, stm],
        out_shape=[jax.ShapeDtypeStruct((batch, rows, vd), F32),
                   jax.ShapeDtypeStruct((batch, N_HEADS, DK, DV), F32),
                   jax.ShapeDtypeStruct((batch, 1, qk), F32),
                   jax.ShapeDtypeStruct((batch, 1, LANES), F32)],
        scratch_shapes=[pltpu.VMEM((bt, rows, vd), F32), pltpu.VMEM((bt, rows, qk), F32)],
        compiler_params=_cparams(1, 48),
        name="mlstm_short",
    )(proj, proj, proj, proj, gates, gb, nw, c0, n0, m0)


def _residual_ln(zbuf, xres_ref, gate_ref, lng_ref, lnb_ref, xo_ref, nxt):
    nc, tm, tn = zbuf.shape
    inv_d = 1.0 / (nc * tn)
    ssum = jnp.zeros((tm, 1), F32)
    for c in range(nc):
        sl = pl.ds(c * tn, tn)
        z = ALPHA * xres_ref[:, sl] + gate_ref[0, :, sl] * zbuf[c]
        zbuf[c] = z
        ssum = ssum + jnp.sum(z, axis=1, keepdims=True)
    mu = ssum * inv_d
    vsum = jnp.zeros((tm, 1), F32)
    for c in range(nc):
        zc = zbuf[c] - mu
        vsum = vsum + jnp.sum(zc * zc, axis=1, keepdims=True)
    rstd = lax.rsqrt(vsum * inv_d + LN_EPS)
    for c in range(nc):
        sl = pl.ds(c * tn, tn)
        xn = (zbuf[c] - mu) * rstd * lng_ref[:, sl] + lnb_ref[:, sl]
        xo_ref[:, sl] = xn
        if nxt is not None:
            sh_ref, sc_ref, ho_ref = nxt
            ho_ref[:, sl] = (xn * (1.0 + sc_ref[0, :, sl]) + sh_ref[0, :, sl]).astype(ho_ref.dtype)


def _proj_ln_kernel(lhs_ref, w_hbm, xres_ref, gate_ref, lng_ref, lnb_ref, shn_ref, scn_ref,
                    xo_ref, ho_ref, wb, stg, ybuf, sem):
    @pl.when(pl.program_id(0) == 0)
    def _():
        ch = stg.shape[1]
        nch = wb.shape[0] // ch

        def chunk(c):
            return pltpu.make_async_copy(w_hbm.at[pl.ds(c * ch, ch)], stg.at[c % 2], sem.at[c % 2])

        chunk(0).start()
        for c in range(nch):
            if c + 1 < nch:
                chunk(c + 1).start()
            chunk(c).wait()
            wb[c * ch:(c + 1) * ch, :] = stg[c % 2].astype(BF16)

    lhs = lhs_ref[...].astype(BF16)
    nc, _, tn = ybuf.shape
    for c in range(nc):
        ybuf[c] = jnp.dot(lhs, wb[:, c * tn:(c + 1) * tn], preferred_element_type=F32)
    _residual_ln(ybuf, xres_ref, gate_ref, lng_ref, lnb_ref, xo_ref, (shn_ref, scn_ref, ho_ref))


def _proj_ln(lhs, w, xres, gate, lng, lnb, shn, scn, mod_idx, tm, tn):
    t, k = lhs.shape
    d = w.shape[1]
    r = gate.shape[1]
    mspec = pl.BlockSpec((1, r, d), lambda i: (mod_idx(i), 0, 0))
    vspec = pl.BlockSpec((1, d), lambda i: (0, 0))
    ospec = pl.BlockSpec((tm, d), lambda i: (i, 0))
    return pl.pallas_call(
        _proj_ln_kernel,
        grid=(t // tm,),
        in_specs=[pl.BlockSpec((tm, k), lambda i: (i, 0)),
                  pl.BlockSpec(memory_space=pl.ANY),
                  ospec, mspec, vspec, vspec, mspec, mspec],
        out_specs=[ospec, ospec],
        out_shape=[jax.ShapeDtypeStruct((t, d), F32), jax.ShapeDtypeStruct((t, d), F32)],
        scratch_shapes=[pltpu.VMEM((k, d), BF16), pltpu.VMEM((2, CAST_ROWS, d), F32),
                        pltpu.VMEM((d // tn, tm, tn), F32), pltpu.SemaphoreType.DMA((2,))],
        compiler_params=_cparams(1, 58),
        name="proj_ln",
    )(lhs, w, xres, gate, lng, lnb, shn, scn)


def _gelu_tanh(x):
    return x * (0.5 * (1.0 + jnp.tanh(0.7978845608028654 * (x + 0.044715 * (x * x * x)))))


def _gmlp_in_kernel(h_ref, w_ref, b_ref, o_ref):
    acc = jnp.dot(h_ref[...].astype(BF16), w_ref[...].astype(BF16), preferred_element_type=F32)
    o_ref[...] = _gelu_tanh(acc + b_ref[...]).astype(o_ref.dtype)


def _gmlp_in(h, w, b, tm, out_dtype):
    t, d = h.shape
    n = w.shape[1]
    tn = 1024
    return pl.pallas_call(
        _gmlp_in_kernel,
        grid=(t // tm, n // tn),
        in_specs=[pl.BlockSpec((tm, d), lambda i, j: (i, 0)),
                  pl.BlockSpec((d, tn), lambda i, j: (0, j)),
                  pl.BlockSpec((1, tn), lambda i, j: (0, j))],
        out_specs=pl.BlockSpec((tm, tn), lambda i, j: (i, j)),
        out_shape=jax.ShapeDtypeStruct((t, n), out_dtype),
        compiler_params=_cparams(2, 48),
        name="gmlp_in",
    )(h, w, b.reshape(1, n))


def _gmlp_mix_kernel(u_ref, v_ref, g_ref, b_ref, mix_ref, bias_ref, o_ref, vn_ref=None):
    v = v_ref[...].astype(F32)
    mu = jnp.mean(v, axis=1, keepdims=True)
    xc = v - mu
    var = jnp.mean(xc * xc, axis=1, keepdims=True)
    vn = xc * lax.rsqrt(var + LN_EPS) * g_ref[...] + b_ref[...]
    if vn_ref is not None:
        vn_ref[...] = vn
    gd = v.shape[1] // N_GROUPS_B
    for g in range(N_GROUPS_B):
        sl = slice(g * gd, (g + 1) * gd)
        mixed = jnp.dot(mix_ref[g], vn[:, sl].astype(BF16), preferred_element_type=F32) + bias_ref[g]
        o_ref[:, sl] = (u_ref[:, sl].astype(F32) * mixed).astype(o_ref.dtype)


def _gmlp_mix(uv, nv_g, nv_b, mix, bias, tm, emit_v):
    t, n2 = uv.shape
    di = n2 // 2
    return pl.pallas_call(
        _gmlp_mix_kernel,
        grid=(t // tm,),
        in_specs=[pl.BlockSpec((tm, di), lambda i: (i, 0)),
                  pl.BlockSpec((tm, di), lambda i: (i, 1)),
                  pl.BlockSpec((1, di), lambda i: (0, 0)),
                  pl.BlockSpec((1, di), lambda i: (0, 0)),
                  pl.BlockSpec(mix.shape, lambda i: (0, 0, 0)),
                  pl.BlockSpec(bias.shape, lambda i: (0, 0, 0))],
        out_specs=[pl.BlockSpec((tm, di), lambda i: (i, 0))] * (2 if emit_v else 1),
        out_shape=[jax.ShapeDtypeStruct((t, di), BF16), jax.ShapeDtypeStruct((t, di), F32)][:2 if emit_v else 1],
        compiler_params=_cparams(1, 48),
        name="gmlp_mix",
    )(uv, uv, nv_g.reshape(1, di), nv_b.reshape(1, di), mix, bias)


def _router_kernel(n_first, ha_ref, hb_ref, wr_ref, br_ref, idx_ref, wts_ref, cnt_ref, carry):
    i = pl.program_id(0)
    tr = ha_ref.shape[0]

    @pl.when(i == 0)
    def _():
        carry[...] = jnp.zeros(carry.shape, carry.dtype)

    nt = (((1,), (1,)), ((), ()))
    h = jnp.where(i < n_first, ha_ref[...], hb_ref[...])
    hb = h.astype(BF16)
    hl = (h - hb.astype(F32)).astype(BF16)
    wr = wr_ref[...]
    wb = wr.astype(BF16)
    wl = (wr - wb.astype(F32)).astype(BF16)
    logits = (lax.dot_general(wb, hb, nt, preferred_element_type=F32)
              + lax.dot_general(wb, hl, nt, preferred_element_type=F32)
              + lax.dot_general(wl, hb, nt, preferred_element_type=F32))
    s = jax.nn.sigmoid(logits)
    sel = s + br_ref[...]
    epg = EXPERTS_PER_GROUP
    r = [sel[e:e + 1, :] for e in range(N_EXPERTS)]
    su = [s[e:e + 1, :] for e in range(N_EXPERTS)]

    def top2sum(v):
        best = v[0] + v[1]
        for x in range(epg):
            for y in range(x + 1, epg):
                if (x, y) != (0, 1):
                    best = jnp.maximum(best, v[x] + v[y])
        return best

    gs = [top2sum(r[g * epg:(g + 1) * epg]) for g in range(N_EXPERT_GROUPS)]
    gbest = gs[0]
    gidx = jnp.zeros((1, tr), I32)
    for g in range(1, N_EXPERT_GROUPS):
        better = gs[g] > gbest
        gidx = jnp.where(better, g, gidx)
        gbest = jnp.where(better, gs[g], gbest)
    v = list(r[:epg])
    sv = list(su[:epg])
    for g in range(1, N_EXPERT_GROUPS):
        pick = gidx == g
        for x in range(epg):
            v[x] = jnp.where(pick, r[g * epg + x], v[x])
            sv[x] = jnp.where(pick, su[g * epg + x], sv[x])
    i1 = jnp.zeros((1, tr), I32)
    b1 = v[0]
    w1 = sv[0]
    for x in range(1, epg):
        better = v[x] > b1
        i1 = jnp.where(better, x, i1)
        b1 = jnp.where(better, v[x], b1)
        w1 = jnp.where(better, sv[x], w1)
    i2 = jnp.zeros((1, tr), I32)
    b2 = jnp.full((1, tr), -jnp.inf, F32)
    w2 = jnp.zeros((1, tr), F32)
    for x in range(epg):
        take = (i1 != x) & (v[x] > b2)
        i2 = jnp.where(take, x, i2)
        b2 = jnp.where(take, v[x], b2)
        w2 = jnp.where(take, sv[x], w2)
    wsum = w1 + w2
    e1 = gidx * epg + i1
    e2 = gidx * epg + i2
    eio = lax.broadcasted_iota(I32, (N_EXPERTS, tr), 0)
    hit1 = eio == e1
    hit2 = eio == e2
    oh = jnp.where(hit1 | hit2, 1.0, 0.0)
    ri = lax.broadcasted_iota(I32, (tr, tr), 0)
    cj = lax.broadcasted_iota(I32, (tr, tr), 1)
    before = jnp.where(ri < cj, 1.0, 0.0).astype(BF16)
    rank = jnp.dot(oh.astype(BF16), before, preferred_element_type=F32) + carry[:, 0:1]
    rank1 = jnp.sum(jnp.where(hit1, rank, 0.0), axis=0, keepdims=True)
    rank2 = jnp.sum(jnp.where(hit2, rank, 0.0), axis=0, keepdims=True)
    idx_ref[0:1, :] = e1
    idx_ref[1:2, :] = e2
    idx_ref[2:3, :] = rank1.astype(I32)
    idx_ref[3:4, :] = rank2.astype(I32)
    idx_ref[4:8, :] = jnp.zeros((4, tr), I32)
    wts_ref[0:1, :] = w1 / wsum
    wts_ref[1:2, :] = w2 / wsum
    wts_ref[2:8, :] = jnp.zeros((6, tr), F32)
    carry[...] = carry[...] + jnp.sum(oh, axis=1, keepdims=True)
    cnt_ref[...] = carry[...]


def _two_set_specs(ha, hb, tile):
    d = ha.shape[1]
    na = ha.shape[0] // tile
    nb = hb.shape[0] // tile
    return na, nb, [pl.BlockSpec((tile, d), lambda i, *_: (jnp.minimum(i, na - 1), 0)),
                    pl.BlockSpec((tile, d), lambda i, *_: (jnp.maximum(i - na, 0), 0))]


def _router(ha, hb, w_router, b_router, tr):
    d = ha.shape[1]
    t = ha.shape[0] + hb.shape[0]
    na, nb, hspecs = _two_set_specs(ha, hb, tr)
    return pl.pallas_call(
        functools.partial(_router_kernel, na),
        grid=(na + nb,),
        in_specs=hspecs + [pl.BlockSpec((N_EXPERTS, d), lambda i: (0, 0)),
                           pl.BlockSpec((N_EXPERTS, 1), lambda i: (0, 0))],
        out_specs=[pl.BlockSpec((SUBLANES, tr), lambda i: (0, i)),
                   pl.BlockSpec((SUBLANES, tr), lambda i: (0, i)),
                   pl.BlockSpec((N_EXPERTS, LANES), lambda i: (0, 0))],
        out_shape=[jax.ShapeDtypeStruct((SUBLANES, t), I32),
                   jax.ShapeDtypeStruct((SUBLANES, t), F32),
                   jax.ShapeDtypeStruct((N_EXPERTS, LANES), F32)],
        scratch_shapes=[pltpu.VMEM((N_EXPERTS, LANES), F32)],
        compiler_params=_cparams(1, 32),
        name="moe_router",
    )(ha, hb, w_router.T, b_router.reshape(N_EXPERTS, 1))


def _scatter_kernel(n_tok, n_first, pos_ref, cnt_ref, off_ref, ha_ref, hb_ref, o_hbm, zero_scr, sem):
    i = pl.program_id(0)
    ts = ha_ref.shape[0]
    base = i * ts

    def row_copy(src, r, p):
        return pltpu.make_async_copy(src.at[pl.ds(r, 1)], o_hbm.at[pl.ds(p, 1)], sem)

    def scatter_tile(h_ref):
        def start_rows(r, c):
            row_copy(h_ref, r, pos_ref[base + r]).start()
            row_copy(h_ref, r, pos_ref[n_tok + base + r]).start()
            return c

        lax.fori_loop(0, ts, start_rows, 0, unroll=8)
        for _ in range(2):
            pltpu.make_async_copy(h_ref, o_hbm.at[pl.ds(0, ts)], sem).wait()

    @pl.when(i < n_first)
    def _():
        scatter_tile(ha_ref)

    @pl.when(i >= n_first)
    def _():
        scatter_tile(hb_ref)

    @pl.when(i == 0)
    def _():
        zero_scr[...] = jnp.zeros(zero_scr.shape, zero_scr.dtype)
        for e in range(N_EXPERTS + 1):
            lo = off_ref[e] + cnt_ref[e]
            hi = off_ref[e + 1]

            def start_zero(p, c):
                row_copy(zero_scr, 0, p).start()
                return c

            def wait_zero(p, c):
                row_copy(zero_scr, 0, 0).wait()
                return c

            lax.fori_loop(lo, hi, start_zero, 0)
            lax.fori_loop(lo, hi, wait_zero, 0)


def _scatter_rows(ha, hb, pos, cnt, off, n_rows, ts):
    d = ha.shape[1]
    t = ha.shape[0] + hb.shape[0]
    na, nb, hspecs = _two_set_specs(ha, hb, ts)
    return pl.pallas_call(
        functools.partial(_scatter_kernel, t, na),
        grid_spec=pltpu.PrefetchScalarGridSpec(
            num_scalar_prefetch=3,
            grid=(na + nb,),
            in_specs=hspecs,
            out_specs=pl.BlockSpec(memory_space=pl.ANY),
            scratch_shapes=[pltpu.VMEM((SUBLANES, d), ha.dtype), pltpu.SemaphoreType.DMA]),
        out_shape=jax.ShapeDtypeStruct((n_rows, d), ha.dtype),
        compiler_params=_cparams(1, 32),
        name="moe_scatter",
    )(pos, cnt, off, ha, hb)


def _cast_rows(src, dst):
    def body(c, carry):
        r = pl.multiple_of(c * CAST_ROWS, CAST_ROWS)
        dst[pl.ds(r, CAST_ROWS), :] = src[pl.ds(r, CAST_ROWS), :].astype(BF16)
        return carry

    lax.fori_loop(0, src.shape[0] // CAST_ROWS, body, 0)


def _expert_kernel(te_ref, nu_ref, first_ref, nxt_ref, x_ref, w1_hbm, w2_hbm, y_ref,
                   stg1, stg2, wb1, wb2, sem):
    i = pl.program_id(0)

    def fetch(e):
        return (pltpu.make_async_copy(w1_hbm.at[e], stg1, sem.at[0]),
                pltpu.make_async_copy(w2_hbm.at[e], stg2, sem.at[1]))

    @pl.when(i == 0)
    def _():
        for cp in fetch(te_ref[0]):
            cp.start()

    @pl.when(first_ref[i] == 1)
    def _():
        for cp in fetch(0):
            cp.wait()
        _cast_rows(stg1, wb1)
        _cast_rows(stg2, wb2)

        @pl.when(nxt_ref[i] >= 0)
        def _():
            for cp in fetch(nxt_ref[i]):
                cp.start()

    @pl.when(i < nu_ref[0])
    def _():
        hid = jnp.dot(x_ref[...].astype(BF16), wb1[...], preferred_element_type=F32)
        f = hid.shape[1] // 2
        a = hid[:, :f]
        g = hid[:, f:]
        z = (g * jax.nn.sigmoid(g) * a).astype(BF16)
        y_ref[...] = jnp.dot(z, wb2[...], preferred_element_type=F32)

    @pl.when(i >= nu_ref[0])
    def _():
        y_ref[...] = jnp.zeros(y_ref.shape, y_ref.dtype)


def _experts(xs, w1, w2, tile_expert, n_used, first, nxt):
    p, d = xs.shape
    tm = MOE_TILE
    _, _, ff2 = w1.shape
    ff = ff2 // 2
    row = lambda i, te, nu, *_: (jnp.minimum(i, nu[0] - 1), 0)
    return pl.pallas_call(
        _expert_kernel,
        grid_spec=pltpu.PrefetchScalarGridSpec(
            num_scalar_prefetch=4,
            grid=(p // tm,),
            in_specs=[pl.BlockSpec((tm, d), row),
                      pl.BlockSpec(memory_space=pl.ANY),
                      pl.BlockSpec(memory_space=pl.ANY)],
            out_specs=pl.BlockSpec((tm, d), lambda i, *_: (i, 0)),
            scratch_shapes=[pltpu.VMEM((d, ff2), F32), pltpu.VMEM((ff, d), F32),
                            pltpu.VMEM((d, ff2), BF16), pltpu.VMEM((ff, d), BF16),
                            pltpu.SemaphoreType.DMA((2,))]),
        out_shape=jax.ShapeDtypeStruct((p, d), F32),
        compiler_params=_cparams(1, 56),
        name="moe_experts",
    )(tile_expert, n_used, first, nxt, xs, w1, w2)


def _combine_kernel(n_tok, tok_base, has_next, pos_ref, w_ref, y_hbm, xres_ref, gate_ref, lng_ref, lnb_ref,
                    *rest):
    if has_next:
        shn_ref, scn_ref, xo_ref, ho_ref, gbuf, fbuf, sem = rest
        nxt = (shn_ref, scn_ref, ho_ref)
    else:
        xo_ref, gbuf, fbuf, sem = rest
        nxt = None
    tc = xres_ref.shape[0]
    i = pl.program_id(0)
    n_steps = pl.num_programs(0)

    def gather_tile(step, slot):
        base = tok_base + step * tc

        def start_rows(r, c):
            for k in range(2):
                pltpu.make_async_copy(y_hbm.at[pl.ds(pos_ref[k * n_tok + base + r], 1)],
                                      gbuf.at[slot, k, pl.ds(r, 1)], sem.at[slot]).start()
            return c

        lax.fori_loop(0, tc, start_rows, 0, unroll=8)

    slot = i % 2

    @pl.when(i == 0)
    def _():
        gather_tile(0, 0)

    @pl.when(i + 1 < n_steps)
    def _():
        gather_tile(i + 1, 1 - slot)

    for k in range(2):
        pltpu.make_async_copy(y_hbm.at[pl.ds(0, tc)], gbuf.at[slot, k], sem.at[slot]).wait()
    fbuf[0] = w_ref[:, 0:1] * gbuf[slot, 0] + w_ref[:, 1:2] * gbuf[slot, 1]
    _residual_ln(fbuf, xres_ref, gate_ref, lng_ref, lnb_ref, xo_ref, nxt)


def _combine_ln(ys, pos, wts, n_tok, tok_base, xres, gate, lng, lnb, nxt_mods, mod_idx, tc):
    t, d = xres.shape
    r = gate.shape[1]
    has_next = nxt_mods is not None
    mspec = pl.BlockSpec((1, r, d), lambda i, *_: (mod_idx(i), 0, 0))
    vspec = pl.BlockSpec((1, d), lambda i, *_: (0, 0))
    ospec = pl.BlockSpec((tc, d), lambda i, *_: (i, 0))
    wbase = tok_base // tc
    in_specs = [pl.BlockSpec((tc, 2), lambda i, *_: (wbase + i, 0)),
                pl.BlockSpec(memory_space=pl.ANY), ospec, mspec, vspec, vspec]
    args = [wts, ys, xres, gate, lng, lnb]
    out_specs = [ospec]
    out_shape = [jax.ShapeDtypeStruct((t, d), F32)]
    if has_next:
        in_specs += [mspec, mspec]
        args += list(nxt_mods)
        out_specs.append(ospec)
        out_shape.append(jax.ShapeDtypeStruct((t, d), BF16))
    return pl.pallas_call(
        functools.partial(_combine_kernel, n_tok, tok_base, has_next),
        grid_spec=pltpu.PrefetchScalarGridSpec(
            num_scalar_prefetch=1,
            grid=(t // tc,),
            in_specs=in_specs,
            out_specs=out_specs,
            scratch_shapes=[pltpu.VMEM((2, 2, tc, d), F32), pltpu.VMEM((1, tc, d), F32),
                            pltpu.SemaphoreType.DMA((2,))]),
        out_shape=out_shape,
        compiler_params=_cparams(1, 40),
        name="moe_combine",
    )(pos, *args)


def _moe_plan(idx, cnt_pad, n_tok):
    tm = MOE_TILE
    n_tiles = (2 * n_tok + N_EXPERTS * (tm - 1) + tm - 1) // tm
    cnt = cnt_pad[:, 0].astype(I32)
    padded = ((cnt + tm - 1) // tm) * tm
    ends = jnp.cumsum(padded)
    off = ends - padded
    pos = jnp.concatenate([jnp.take(off, idx[0]) + idx[2], jnp.take(off, idx[1]) + idx[3]])
    n_used = jnp.maximum(ends[-1] // tm, 1)
    tile_start = jnp.minimum(jnp.arange(n_tiles, dtype=I32), n_used - 1) * tm
    tile_expert = jnp.minimum(jnp.sum(tile_start[:, None] >= ends[None, :], axis=1), N_EXPERTS - 1).astype(I32)
    tiles = jnp.arange(n_tiles, dtype=I32)
    prev = jnp.concatenate([jnp.full((1,), -1, I32), tile_expert[:-1]])
    first = ((tiles < n_used) & (tile_expert != prev)).astype(I32)
    eid = jnp.arange(N_EXPERTS, dtype=I32)
    later = jnp.where((cnt[None, :] > 0) & (eid[None, :] > eid[:, None]), eid[None, :], N_EXPERTS)
    nxt_e = jnp.min(later, axis=1)
    nxt_e = jnp.where(nxt_e == N_EXPERTS, -1, nxt_e).astype(I32)
    nxt = jnp.take(nxt_e, tile_expert)
    n_rows = jnp.full((1,), n_tiles * tm, I32)
    off_ext = jnp.concatenate([off, ends[-1:], n_rows]).astype(I32)
    cnt = jnp.concatenate([cnt, jnp.zeros((1,), I32)])
    return (pos.astype(I32), cnt, off_ext, tile_expert, n_used.reshape(1).astype(I32), first, nxt,
            n_tiles * tm)


class _Rows:
    def __init__(self, n_seq, seq_len, tile):
        self.n_seq, self.seq_len, self.tile = n_seq, seq_len, tile
        self.per_row = seq_len < tile

    def mods(self, m):
        if self.per_row:
            return jnp.repeat(m, self.seq_len, axis=0).reshape(-1, self.tile, m.shape[1])
        return m[:, None, :]

    def mod_idx(self, i):
        return i if self.per_row else (i * self.tile) // self.seq_len


def _split_mods(mod_l, n_prompt):
    d = mod_l.shape[1] // 6
    cols = [mod_l[:, k * d:(k + 1) * d] for k in range(6)]
    return [c[:n_prompt] for c in cols], [c[n_prompt:] for c in cols]


def kernel(x_prompt, x_sample, state_mlstm_C, state_mlstm_n, state_mlstm_m, c_prompt, c_sample, w_ada, b_ada, ln_g, ln_b, a_w_in, a_b_gates, a_norm_w, a_w_out, b_w_in, b_b_in, b_norm_g, b_norm_b, b_w_s, b_b_s, b_w_out, w_router, b_router, w_expert_in, w_expert_out):
    bp, sp, d = x_prompt.shape
    bs, ss, _ = x_sample.shape
    tp = bp * sp
    ts = bs * ss
    n_tok = tp + ts
    qk = N_HEADS * DK
    vd = N_HEADS * DV
    n_main = 2 * qk + 2 * vd

    n_seq = bp + bs
    pad = (-n_seq) % SUBLANES
    c_all = jnp.concatenate([c_prompt, c_sample, jnp.zeros((pad, d), F32)])
    mod = _adaln(c_all, w_ada, b_ada)[:, :n_seq]

    sets = {"p": _Rows(bp, sp, 512), "s": _Rows(bs, ss, 256)}
    base = {"p": 0, "s": tp}
    x = {"p": x_prompt.reshape(tp, d), "s": x_sample.reshape(ts, d)}
    hm = {}
    outs = {}
    for layer in range(DEPTH):
        j = layer // 2
        mp, ms = _split_mods(mod[layer], bp)
        md = {"p": mp, "s": ms}
        lng = ln_g[layer]
        lnb = ln_b[layer]
        lhs = {}
        if layer % 2 == 0:
            w_in = a_w_in[j]
            wg = jnp.zeros((d, 2 * LANES), F32)
            wg = wg.at[:, :N_HEADS].set(w_in[:, n_main:n_main + N_HEADS])
            wg = wg.at[:, LANES:LANES + N_HEADS].set(w_in[:, n_main + N_HEADS:])
            gb = jnp.zeros((1, 2 * LANES), F32)
            gb = gb.at[0, :N_HEADS].set(a_b_gates[j, :N_HEADS])
            gb = gb.at[0, LANES:LANES + N_HEADS].set(a_b_gates[j, N_HEADS:])
            nw = a_norm_w[j].reshape(1, vd)
            proj = {}
            gates = {}
            for k, rs in {"p": _Rows(bp, sp, 1024), "s": sets["s"]}.items():
                proj[k], gates[k] = _inproj(x[k], rs.mods(md[k][0]), rs.mods(md[k][1]), rs.mod_idx,
                                            w_in, wg, n_main, rs.tile, BF16 if k == "p" else F32)
            lhs["p"], c_p, n_p, m_p = _mlstm_chunks(
                proj["p"], gates["p"], gb, nw,
                jnp.zeros((bp, N_HEADS, DK, DV), F32), jnp.zeros((bp, N_HEADS, LANES), F32),
                jnp.zeros((bp, N_HEADS, LANES), F32), bp, sp)
            rpad = SUBLANES - ss
            proj_s = jnp.pad(proj["s"].reshape(bs, ss, n_main), ((0, 0), (0, rpad), (0, 0)))
            gates_s = jnp.pad(gates["s"].reshape(bs, ss, 2 * LANES), ((0, 0), (0, rpad), (0, 0)))
            m0 = jnp.pad(state_mlstm_m[j], ((0, 0), (0, LANES - N_HEADS)))[:, None, :]
            ypre_s, c_s, n_s, m_s = _mlstm_short(proj_s, gates_s, gb, nw, state_mlstm_C[j],
                                                 state_mlstm_n[j].reshape(bs, 1, qk), m0, ss)
            n_s = n_s.reshape(bs, N_HEADS, DK)
            lhs["s"] = ypre_s[:, :ss].reshape(ts, vd)
            outs["C_p"], outs["n_p"], outs["m_p"] = c_p, n_p, m_p[:, :, 0]
            outs["C_s"], outs["n_s"], outs["m_s"] = c_s, n_s, m_s[:, 0, :N_HEADS]
            w_out = a_w_out[j]
        else:
            tmix = 256
            ws = b_w_s[j]
            bsv = b_b_s[j]

            def mixing(l):
                tri = jnp.tril(jnp.ones((l, l), bool))
                wsl = jnp.where(tri, ws[:, :l, :l], 0.0)
                eye = jnp.eye(tmix // l, dtype=F32)
                mats = jax.vmap(lambda m: jnp.kron(eye, m))(wsl).astype(BF16)
                bias = jnp.tile(bsv[:, :l], (1, tmix // l))[:, :, None]
                return mats, bias

            for k, rs in sets.items():
                uv = _gmlp_in(hm[k], b_w_in[j], b_b_in[j], min(1024, rs.n_seq * rs.seq_len),
                              BF16 if k == "p" else F32)
                mats, bias = mixing(min(CHUNK, rs.seq_len))
                res = _gmlp_mix(uv, b_norm_g[j], b_norm_b[j], mats, bias, tmix, k == "s")
                lhs[k] = res[0]
                if k == "s":
                    outs["v_s"] = res[1]
            w_out = b_w_out[j]
        x1 = {}
        hf = {}
        for k, rs in sets.items():
            rs = _Rows(rs.n_seq, rs.seq_len, min(rs.tile, 512 * 2048 // w_out.shape[0]))
            x1[k], hf[k] = _proj_ln(lhs[k], w_out, x[k], rs.mods(md[k][2]), lng[0:1], lnb[0:1],
                                    rs.mods(md[k][3]), rs.mods(md[k][4]), rs.mod_idx, rs.tile, 512)

        idx, wts, cnt = _router(hf["p"], hf["s"], w_router, b_router, 512)
        pos, cnt_i, off_ext, tile_expert, n_used, first, nxt, n_rows = _moe_plan(idx, cnt, n_tok)
        xsorted = _scatter_rows(hf["p"], hf["s"], pos, cnt_i, off_ext, n_rows, 256)
        ysorted = _experts(xsorted, w_expert_in[layer], w_expert_out[layer], tile_expert, n_used, first, nxt)
        wts2 = wts[:2].T
        nxt = _split_mods(mod[layer + 1], bp) if layer + 1 < DEPTH else None
        for ki, (k, rs) in enumerate(sets.items()):
            rc = _Rows(rs.n_seq, rs.seq_len, 256)
            nxt_mods = None if nxt is None else (rc.mods(nxt[ki][0]), rc.mods(nxt[ki][1]))
            res = _combine_ln(ysorted, pos, wts2, n_tok, base[k], x1[k], rc.mods(md[k][5]),
                              lng[1:2], lnb[1:2], nxt_mods, rc.mod_idx, 256)
            x[k] = res[0]
            if nxt is not None:
                hm[k] = res[1]

    return (x["p"].reshape(bp, sp, d), x["s"].reshape(bs, ss, d),
            outs["C_p"][None], outs["n_p"][None], outs["m_p"][None],
            outs["C_s"][None], outs["n_s"][None], outs["m_s"][None],
            outs["v_s"].reshape(bs, ss, -1)[None])
```

```python
import functools

import jax
import jax.numpy as jnp
from jax import lax
from jax.experimental import pallas as pl
from jax.experimental.pallas import tpu as pltpu

F32 = jnp.float32
BF16 = jnp.bfloat16
I32 = jnp.int32

DEPTH = 2
N_HEADS = 8
DK = 128
DV = 256
CHUNK = 128
N_GROUPS_B = 8
N_EXPERTS = 16
N_EXPERT_GROUPS = 4
EXPERTS_PER_GROUP = 4
ALPHA = float((2 * DEPTH) ** 0.25)
LN_EPS = 1e-5

LANES = 128
SUBLANES = 8
MIB = 1024 * 1024
MOE_TILE = 256
CAST_ROWS = 256
STAGE_ROWS = 128


def _cparams(n_axes, vmem_mib):
    return pltpu.CompilerParams(
        dimension_semantics=("arbitrary",) * n_axes,
        vmem_limit_bytes=int(vmem_mib * MIB))


def _split3(x):
    hi = x.astype(BF16)
    r1 = x - hi.astype(F32)
    mid = r1.astype(BF16)
    lo = (r1 - mid.astype(F32)).astype(BF16)
    return hi, mid, lo


def _log_sigmoid(x):
    return jnp.minimum(x, 0.0) - jnp.log1p(jnp.exp(-jnp.abs(x)))


def _load_weight_bf16(w_hbm, wb, stg, sem):
    nj, k, tn = wb.shape
    ch = stg.shape[1]
    nch = k // ch

    def chunk(c):
        return pltpu.make_async_copy(w_hbm.at[pl.ds(c * ch, ch), pl.ds(0, nj * tn)], stg.at[c % 2],
                                     sem.at[c % 2])

    chunk(0).start()
    for c in range(nch):
        if c + 1 < nch:
            chunk(c + 1).start()
        chunk(c).wait()
        for jj in range(nj):
            wb[jj, c * ch:(c + 1) * ch, :] = stg[c % 2, :, jj * tn:(jj + 1) * tn].astype(BF16)


def _adaln_kernel(c_ref, w_ref, b_ref, o_ref):
    c = c_ref[...]
    a = (c * jax.nn.sigmoid(c)).astype(BF16)
    o_ref[0] = jnp.dot(a, w_ref[0].astype(BF16), preferred_element_type=F32) + b_ref[0]


def _adaln(c_all, w_ada, b_ada):
    depth, d, n = w_ada.shape
    r = c_all.shape[0]
    tn = 1024
    return pl.pallas_call(
        _adaln_kernel,
        grid=(depth, n // tn),
        in_specs=[pl.BlockSpec((r, d), lambda l, j: (0, 0)),
                  pl.BlockSpec((1, d, tn), lambda l, j: (l, 0, j)),
                  pl.BlockSpec((1, 1, tn), lambda l, j: (l, 0, j))],
        out_specs=pl.BlockSpec((1, r, tn), lambda l, j: (l, 0, j)),
        out_shape=jax.ShapeDtypeStruct((depth, r, n), F32),
        compiler_params=_cparams(2, 32),
        name="adaln",
    )(c_all, w_ada, b_ada.reshape(depth, 1, n))


def _inproj_kernel(x_ref, sh_ref, sc_ref, w_hbm, wg_ref, proj_ref, gates_ref, hin_ref, wb, stg, sem):
    j = pl.program_id(1)

    @pl.when((pl.program_id(0) == 0) & (j == 0))
    def _():
        _load_weight_bf16(w_hbm, wb, stg, sem)

    @pl.when(j == 0)
    def _():
        h = x_ref[...] * (1.0 + sc_ref[0]) + sh_ref[0]
        hb = h.astype(BF16)
        hin_ref[...] = hb
        h_lo = (h - hb.astype(F32)).astype(BF16)
        wg = wg_ref[...]
        wg_hi = wg.astype(BF16)
        wg_lo = (wg - wg_hi.astype(F32)).astype(BF16)
        gates_ref[...] = (jnp.dot(hb, wg_hi, preferred_element_type=F32)
                          + jnp.dot(h_lo, wg_hi, preferred_element_type=F32)
                          + jnp.dot(hb, wg_lo, preferred_element_type=F32))

    proj_ref[...] = jnp.dot(hin_ref[...], wb[j], preferred_element_type=F32).astype(proj_ref.dtype)


def _inproj(x, sh, sc, mod_idx, w_in, wg, n_main, tm, out_dtype):
    t, d = x.shape
    tn = 1024
    r = sh.shape[1]
    mspec = pl.BlockSpec((1, r, d), lambda i, j: (mod_idx(i), 0, 0))
    return pl.pallas_call(
        _inproj_kernel,
        grid=(t // tm, n_main // tn),
        in_specs=[pl.BlockSpec((tm, d), lambda i, j: (i, 0)), mspec, mspec,
                  pl.BlockSpec(memory_space=pl.ANY),
                  pl.BlockSpec(wg.shape, lambda i, j: (0, 0))],
        out_specs=[pl.BlockSpec((tm, tn), lambda i, j: (i, j)),
                   pl.BlockSpec((tm, wg.shape[1]), lambda i, j: (i, 0))],
        out_shape=[jax.ShapeDtypeStruct((t, n_main), out_dtype),
                   jax.ShapeDtypeStruct((t, wg.shape[1]), F32)],
        scratch_shapes=[pltpu.VMEM((tm, d), BF16), pltpu.VMEM((n_main // tn, d, tn), BF16),
                        pltpu.VMEM((2, STAGE_ROWS, n_main), F32), pltpu.SemaphoreType.DMA((2,))],
        compiler_params=_cparams(2, 56),
        name="mlstm_inproj",
    )(x, sh, sc, w_in, wg)


def _head_norm_gate(hh, nw, o):
    mu = jnp.mean(hh, axis=1, keepdims=True)
    xc = hh - mu
    var = jnp.mean(xc * xc, axis=1, keepdims=True)
    return jax.nn.sigmoid(o) * (xc * lax.rsqrt(var + LN_EPS) * nw)


def _mlstm_chunk_kernel(q_ref, k_ref, v_ref, o_ref, g_ref, gb_ref, nw_ref, c0_ref, n0_ref, m0_ref,
                        y_ref, cout_ref, nout_ref, mout_ref, c_scr, n_scr, m_scr):
    ci = pl.program_id(1)
    L = q_ref.shape[0]

    @pl.when(ci == 0)
    def _():
        c_scr[...] = c0_ref[0]
        n_scr[...] = n0_ref[0]
        m_scr[...] = m0_ref[0]

    g = g_ref[...] + gb_ref[...]
    gi = g[:, :LANES]
    lf = _log_sigmoid(g[:, LANES:])
    row = lax.broadcasted_iota(I32, (L, L), 0)
    col = lax.broadcasted_iota(I32, (L, L), 1)
    causal = col <= row
    ltri = jnp.where(causal, 1.0, 0.0).astype(BF16)
    hi, mid, lo = _split3(lf)
    bcum = (jnp.dot(ltri, hi, preferred_element_type=F32)
            + jnp.dot(ltri, mid, preferred_element_type=F32)
            + jnp.dot(ltri, lo, preferred_element_type=F32))
    a = gi - bcum
    a_t = a.T
    scale = DK ** -0.5
    nt = (((1,), (1,)), ((), ()))
    tn_dims = (((0,), (0,)), ((), ()))
    for h in range(N_HEADS):
        ks = slice(h * DK, (h + 1) * DK)
        vs = slice(h * DV, (h + 1) * DV)
        qf = q_ref[:, ks].astype(F32) * scale
        qb = qf.astype(BF16)
        kf = k_ref[:, ks].astype(F32)
        kb = kf.astype(BF16)
        vb = v_ref[:, vs].astype(BF16)
        a_row = a_t[h:h + 1, :]
        a_col = a[:, h:h + 1]
        b_col = bcum[:, h:h + 1]
        m_prev = m_scr[h:h + 1, 0:1]
        amat = jnp.where(causal, a_row, -jnp.inf)
        mx = jnp.max(amat, axis=1, keepdims=True)
        m_inter = b_col + m_prev
        m_t = jnp.maximum(m_inter, b_col + mx)
        dm = jnp.exp(amat + (b_col - m_t))
        s = lax.dot_general(qb, kb, nt, preferred_element_type=F32)
        scores = s * dm
        inter = jnp.exp(m_inter - m_t)
        c_old = c_scr[h]
        n_old = n_scr[h:h + 1, :]
        qc = jnp.dot(qb, c_old.astype(BF16), preferred_element_type=F32)
        num = jnp.dot(scores.astype(BF16), vb, preferred_element_type=F32) + inter * qc
        qn = jnp.sum(qf * n_old, axis=1, keepdims=True)
        den = jnp.sum(scores, axis=1, keepdims=True) + inter * qn
        hh = num / jnp.maximum(jnp.abs(den), jnp.exp(-m_t))
        m_new = m_t[L - 1:L, :]
        b_last = b_col[L - 1:L, :]
        w_col = jnp.exp(b_last + a_col - m_new)
        decay = jnp.exp(b_last + m_prev - m_new)
        kw = kf * w_col
        c_scr[h] = decay * c_old + lax.dot_general(kw.astype(BF16), vb, tn_dims,
                                                   preferred_element_type=F32)
        n_scr[h:h + 1, :] = decay * n_old + jnp.sum(kw, axis=0, keepdims=True)
        m_scr[h:h + 1, :] = jnp.broadcast_to(m_new, (1, LANES))
        y_ref[:, vs] = _head_norm_gate(hh, nw_ref[:, vs], o_ref[:, vs].astype(F32)).astype(y_ref.dtype)

    @pl.when(ci == pl.num_programs(1) - 1)
    def _():
        cout_ref[0] = c_scr[...]
        nout_ref[0] = n_scr[...]
        mout_ref[0] = m_scr[...]


def _mlstm_chunks(proj, gates, gb, nw, c0, n0, m0, batch, seq):
    nc = seq // CHUNK
    qk = N_HEADS * DK
    vd = N_HEADS * DV
    row = lambda b, c: b * nc + c
    st4 = pl.BlockSpec((1, N_HEADS, DK, DV), lambda b, c: (b, 0, 0, 0))
    st3 = pl.BlockSpec((1, N_HEADS, LANES), lambda b, c: (b, 0, 0))
    return pl.pallas_call(
        _mlstm_chunk_kernel,
        grid=(batch, nc),
        in_specs=[pl.BlockSpec((CHUNK, qk), lambda b, c: (row(b, c), 0)),
                  pl.BlockSpec((CHUNK, qk), lambda b, c: (row(b, c), 1)),
                  pl.BlockSpec((CHUNK, vd), lambda b, c: (row(b, c), 1)),
                  pl.BlockSpec((CHUNK, vd), lambda b, c: (row(b, c), 2)),
                  pl.BlockSpec((CHUNK, 2 * LANES), lambda b, c: (row(b, c), 0)),
                  pl.BlockSpec((1, 2 * LANES), lambda b, c: (0, 0)),
                  pl.BlockSpec((1, vd), lambda b, c: (0, 0)),
                  st4, st3, st3],
        out_specs=[pl.BlockSpec((CHUNK, vd), lambda b, c: (row(b, c), 0)), st4, st3, st3],
        out_shape=[jax.ShapeDtypeStruct((batch * seq, vd), BF16),
                   jax.ShapeDtypeStruct((batch, N_HEADS, DK, DV), F32),
                   jax.ShapeDtypeStruct((batch, N_HEADS, LANES), F32),
                   jax.ShapeDtypeStruct((batch, N_HEADS, LANES), F32)],
        scratch_shapes=[pltpu.VMEM((N_HEADS, DK, DV), F32),
                        pltpu.VMEM((N_HEADS, LANES), F32),
                        pltpu.VMEM((N_HEADS, LANES), F32)],
        compiler_params=_cparams(2, 40),
        name="mlstm_chunks",
    )(proj, proj, proj, proj, gates, gb, nw, c0, n0, m0)


def _per_head(x, width):
    return jnp.concatenate(
        [jnp.broadcast_to(x[..., h:h + 1], x.shape[:-1] + (width,)) for h in range(N_HEADS)], axis=-1)


def _head_sums(x, width):
    lane = lax.broadcasted_iota(I32, x.shape[:-1] + (LANES,), x.ndim - 1)
    out = jnp.zeros(x.shape[:-1] + (LANES,), F32)
    for h in range(N_HEADS):
        s = jnp.sum(x[..., h * width:(h + 1) * width], axis=-1, keepdims=True)
        out = jnp.where(lane == h, s, out)
    return out


def _mlstm_short_kernel(seq, q_ref, k_ref, v_ref, o_ref, g_ref, gb_ref, nw_ref, c0_ref, n0_ref, m0_ref,
                        y_ref, cout_ref, nout_ref, mout_ref, qc_scr, kw_scr):
    bt, rows, _ = q_ref.shape
    scale = DK ** -0.5
    tn_dims = (((0,), (0,)), ((), ()))
    row = lax.broadcasted_iota(I32, (bt, rows, LANES), 1)
    g = g_ref[...] + gb_ref[...]
    gi = g[:, :, :LANES]
    lf = _log_sigmoid(g[:, :, LANES:])
    bcum = jnp.zeros_like(lf)
    for s in range(seq):
        bcum = bcum + jnp.where(row >= s, lf[:, s:s + 1, :], 0.0)
    a = gi - bcum
    mx = jnp.full_like(a, -jnp.inf)
    for s in range(seq):
        mx = jnp.maximum(mx, jnp.where(row >= s, a[:, s:s + 1, :], -jnp.inf))
    m_prev = m0_ref[...]
    m_inter = bcum + m_prev
    m_t = jnp.maximum(m_inter, bcum + mx)
    cmt = bcum - m_t
    inter = jnp.exp(m_inter - m_t)
    einv = jnp.exp(-m_t)
    m_new = m_t[:, seq - 1:seq, :]
    b_last = bcum[:, seq - 1:seq, :]
    w = jnp.where(row < seq, jnp.exp(b_last + a - m_new), 0.0)
    decay = jnp.exp(b_last + m_prev - m_new)
    mout_ref[...] = m_new

    q = q_ref[...] * scale
    k = k_ref[...]
    v = v_ref[...]
    n_old = n0_ref[...]
    for b in range(bt):
        for h in range(N_HEADS):
            qc_scr[b, :, h * DV:(h + 1) * DV] = jnp.dot(
                (q_ref[b, :, h * DK:(h + 1) * DK] * scale).astype(BF16), c0_ref[b, h].astype(BF16),
                preferred_element_type=F32)
    den = inter * _head_sums(q * n_old, DK)
    num = _per_head(inter, DV) * qc_scr[...]
    for s in range(seq):
        p = _head_sums(q * k[:, s:s + 1, :], DK) * jnp.where(row >= s, jnp.exp(cmt + a[:, s:s + 1, :]), 0.0)
        den = den + p
        num = num + _per_head(p, DV) * v[:, s:s + 1, :]
    hh = num * _per_head(1.0 / jnp.maximum(jnp.abs(den), einv), DV)
    mu = _head_sums(hh, DV) * (1.0 / DV)
    xc = hh - _per_head(mu, DV)
    var = _head_sums(xc * xc, DV) * (1.0 / DV)
    hn = xc * _per_head(lax.rsqrt(var + LN_EPS), DV) * nw_ref[...]
    y_ref[...] = jax.nn.sigmoid(o_ref[...]) * hn

    kw = k * _per_head(w, DK)
    kw_scr[...] = kw
    nout_ref[...] = _per_head(decay, DK) * n_old + jnp.sum(kw, axis=1, keepdims=True)
    for b in range(bt):
        for h in range(N_HEADS):
            cout_ref[b, h] = (decay[b, :, h:h + 1] * c0_ref[b, h]
                              + lax.dot_general(kw_scr[b, :, h * DK:(h + 1) * DK],
                                                v_ref[b, :, h * DV:(h + 1) * DV],
                                                tn_dims, preferred_element_type=F32))


def _mlstm_short(proj, gates, gb, nw, c0, n0, m0, seq):
    batch, rows, _ = proj.shape
    qk = N_HEADS * DK
    vd = N_HEADS * DV
    bt = 8
    st4 = pl.BlockSpec((bt, N_HEADS, DK, DV), lambda i: (i, 0, 0, 0))
    st3 = pl.BlockSpec((bt, 1, qk), lambda i: (i, 0, 0))
    stm = pl.BlockSpec((bt, 1, LANES), lambda i: (i, 0, 0))
    return pl.pallas_call(
        functools.partial(_mlstm_short_kernel, seq),
        grid=(batch // bt,),
        in_specs=[pl.BlockSpec((bt, rows, qk), lambda i: (i, 0, 0)),
                  pl.BlockSpec((bt, rows, qk), lambda i: (i, 0, 1)),
                  pl.BlockSpec((bt, rows, vd), lambda i: (i, 0, 1)),
                  pl.BlockSpec((bt, rows, vd), lambda i: (i, 0, 2)),
                  pl.BlockSpec((bt, rows, 2 * LANES), lambda i: (i, 0, 0)),
                  pl.BlockSpec((1, 2 * LANES), lambda i: (0, 0)),
                  pl.BlockSpec((1, vd), lambda i: (0, 0)),
                  st4, st3, stm],
        out_specs=[pl.BlockSpec((bt, rows, vd), lambda i: (i, 0, 0)), st4, st3, stm],
        out_shape=[jax.ShapeDtypeStruct((batch, rows, vd), F32),
                   jax.ShapeDtypeStruct((batch, N_HEADS, DK, DV), F32),
                   jax.ShapeDtypeStruct((batch, 1, qk), F32),
                   jax.ShapeDtypeStruct((batch, 1, LANES), F32)],
        scratch_shapes=[pltpu.VMEM((bt, rows, vd), F32), pltpu.VMEM((bt, rows, qk), F32)],
        compiler_params=_cparams(1, 48),
        name="mlstm_short",
    )(proj, proj, proj, proj, gates, gb, nw, c0, n0, m0)


def _residual_ln(zbuf, xres_ref, gate_ref, lng_ref, lnb_ref, xo_ref, nxt):
    nc, tm, tn = zbuf.shape
    inv_d = 1.0 / (nc * tn)
    ssum = jnp.zeros((tm, 1), F32)
    for c in range(nc):
        sl = pl.ds(c * tn, tn)
        z = ALPHA * xres_ref[:, sl] + gate_ref[0, :, sl] * zbuf[c]
        zbuf[c] = z
        ssum = ssum + jnp.sum(z, axis=1, keepdims=True)
    mu = ssum * inv_d
    vsum = jnp.zeros((tm, 1), F32)
    for c in range(nc):
        zc = zbuf[c] - mu
        vsum = vsum + jnp.sum(zc * zc, axis=1, keepdims=True)
    rstd = lax.rsqrt(vsum * inv_d + LN_EPS)
    for c in range(nc):
        sl = pl.ds(c * tn, tn)
        xn = (zbuf[c] - mu) * rstd * lng_ref[:, sl] + lnb_ref[:, sl]
        xo_ref[:, sl] = xn
        if nxt is not None:
            sh_ref, sc_ref, ho_ref = nxt
            ho_ref[:, sl] = (xn * (1.0 + sc_ref[0, :, sl]) + sh_ref[0, :, sl]).astype(ho_ref.dtype)


def _proj_ln_kernel(lhs_ref, w_hbm, xres_ref, gate_ref, lng_ref, lnb_ref, shn_ref, scn_ref,
                    xo_ref, ho_ref, wb, stg, ybuf, sem):
    @pl.when(pl.program_id(0) == 0)
    def _():
        _load_weight_bf16(w_hbm, wb, stg, sem)

    lhs = lhs_ref[...].astype(BF16)
    for c in range(ybuf.shape[0]):
        ybuf[c] = jnp.dot(lhs, wb[c], preferred_element_type=F32)
    _residual_ln(ybuf, xres_ref, gate_ref, lng_ref, lnb_ref, xo_ref, (shn_ref, scn_ref, ho_ref))


def _proj_ln(lhs, w, xres, gate, lng, lnb, shn, scn, mod_idx, tm, tn):
    t, k = lhs.shape
    d = w.shape[1]
    r = gate.shape[1]
    mspec = pl.BlockSpec((1, r, d), lambda i: (mod_idx(i), 0, 0))
    vspec = pl.BlockSpec((1, d), lambda i: (0, 0))
    ospec = pl.BlockSpec((tm, d), lambda i: (i, 0))
    return pl.pallas_call(
        _proj_ln_kernel,
        grid=(t // tm,),
        in_specs=[pl.BlockSpec((tm, k), lambda i: (i, 0)),
                  pl.BlockSpec(memory_space=pl.ANY),
                  ospec, mspec, vspec, vspec, mspec, mspec],
        out_specs=[ospec, ospec],
        out_shape=[jax.ShapeDtypeStruct((t, d), F32), jax.ShapeDtypeStruct((t, d), F32)],
        scratch_shapes=[pltpu.VMEM((d // tn, k, tn), BF16), pltpu.VMEM((2, CAST_ROWS, d), F32),
                        pltpu.VMEM((d // tn, tm, tn), F32), pltpu.SemaphoreType.DMA((2,))],
        compiler_params=_cparams(1, 58),
        name="proj_ln",
    )(lhs, w, xres, gate, lng, lnb, shn, scn)


def _gelu_tanh(x):
    return x * (0.5 * (1.0 + jnp.tanh(0.7978845608028654 * (x + 0.044715 * (x * x * x)))))


def _gmlp_in_kernel(h_ref, w_hbm, b_ref, o_ref, wb, stg, sem):
    j = pl.program_id(1)

    @pl.when((pl.program_id(0) == 0) & (j == 0))
    def _():
        _load_weight_bf16(w_hbm, wb, stg, sem)

    acc = jnp.dot(h_ref[...].astype(BF16), wb[j], preferred_element_type=F32)
    o_ref[...] = _gelu_tanh(acc + b_ref[...]).astype(o_ref.dtype)


def _gmlp_in(h, w, b, tm, out_dtype):
    t, d = h.shape
    n = w.shape[1]
    tn = 1024
    return pl.pallas_call(
        _gmlp_in_kernel,
        grid=(t // tm, n // tn),
        in_specs=[pl.BlockSpec((tm, d), lambda i, j: (i, 0)),
                  pl.BlockSpec(memory_space=pl.ANY),
                  pl.BlockSpec((1, tn), lambda i, j: (0, j))],
        out_specs=pl.BlockSpec((tm, tn), lambda i, j: (i, j)),
        out_shape=jax.ShapeDtypeStruct((t, n), out_dtype),
        scratch_shapes=[pltpu.VMEM((n // tn, d, tn), BF16), pltpu.VMEM((2, STAGE_ROWS, n), F32),
                        pltpu.SemaphoreType.DMA((2,))],
        compiler_params=_cparams(2, 58),
        name="gmlp_in",
    )(h, w, b.reshape(1, n))


def _gmlp_mix_kernel(u_ref, v_ref, g_ref, b_ref, mix_ref, bias_ref, o_ref, vn_ref=None):
    v = v_ref[...].astype(F32)
    mu = jnp.mean(v, axis=1, keepdims=True)
    xc = v - mu
    var = jnp.mean(xc * xc, axis=1, keepdims=True)
    vn = xc * lax.rsqrt(var + LN_EPS) * g_ref[...] + b_ref[...]
    if vn_ref is not None:
        vn_ref[...] = vn
    gd = v.shape[1] // N_GROUPS_B
    for g in range(N_GROUPS_B):
        sl = slice(g * gd, (g + 1) * gd)
        mixed = jnp.dot(mix_ref[g], vn[:, sl].astype(BF16), preferred_element_type=F32) + bias_ref[g]
        o_ref[:, sl] = (u_ref[:, sl].astype(F32) * mixed).astype(o_ref.dtype)


def _gmlp_mix(uv, nv_g, nv_b, mix, bias, tm, emit_v):
    t, n2 = uv.shape
    di = n2 // 2
    return pl.pallas_call(
        _gmlp_mix_kernel,
        grid=(t // tm,),
        in_specs=[pl.BlockSpec((tm, di), lambda i: (i, 0)),
                  pl.BlockSpec((tm, di), lambda i: (i, 1)),
                  pl.BlockSpec((1, di), lambda i: (0, 0)),
                  pl.BlockSpec((1, di), lambda i: (0, 0)),
                  pl.BlockSpec(mix.shape, lambda i: (0, 0, 0)),
                  pl.BlockSpec(bias.shape, lambda i: (0, 0, 0))],
        out_specs=[pl.BlockSpec((tm, di), lambda i: (i, 0))] * (2 if emit_v else 1),
        out_shape=[jax.ShapeDtypeStruct((t, di), BF16), jax.ShapeDtypeStruct((t, di), F32)][:2 if emit_v else 1],
        compiler_params=_cparams(1, 48),
        name="gmlp_mix",
    )(uv, uv, nv_g.reshape(1, di), nv_b.reshape(1, di), mix, bias)


def _router_kernel(n_first, ha_ref, hb_ref, wr_ref, br_ref, idx_ref, wts_ref, cnt_ref, carry):
    i = pl.program_id(0)
    tr = ha_ref.shape[0]

    @pl.when(i == 0)
    def _():
        carry[...] = jnp.zeros(carry.shape, carry.dtype)

    nt = (((1,), (1,)), ((), ()))
    h = jnp.where(i < n_first, ha_ref[...], hb_ref[...])
    hb = h.astype(BF16)
    hl = (h - hb.astype(F32)).astype(BF16)
    wr = wr_ref[...]
    wb = wr.astype(BF16)
    wl = (wr - wb.astype(F32)).astype(BF16)
    logits = (lax.dot_general(wb, hb, nt, preferred_element_type=F32)
              + lax.dot_general(wb, hl, nt, preferred_element_type=F32)
              + lax.dot_general(wl, hb, nt, preferred_element_type=F32))
    s = jax.nn.sigmoid(logits)
    sel = s + br_ref[...]
    epg = EXPERTS_PER_GROUP
    r = [sel[e:e + 1, :] for e in range(N_EXPERTS)]
    su = [s[e:e + 1, :] for e in range(N_EXPERTS)]

    def top2sum(v):
        best = v[0] + v[1]
        for x in range(epg):
            for y in range(x + 1, epg):
                if (x, y) != (0, 1):
                    best = jnp.maximum(best, v[x] + v[y])
        return best

    gs = [top2sum(r[g * epg:(g + 1) * epg]) for g in range(N_EXPERT_GROUPS)]
    gbest = gs[0]
    gidx = jnp.zeros((1, tr), I32)
    for g in range(1, N_EXPERT_GROUPS):
        better = gs[g] > gbest
        gidx = jnp.where(better, g, gidx)
        gbest = jnp.where(better, gs[g], gbest)
    v = list(r[:epg])
    sv = list(su[:epg])
    for g in range(1, N_EXPERT_GROUPS):
        pick = gidx == g
        for x in range(epg):
            v[x] = jnp.where(pick, r[g * epg + x], v[x])
            sv[x] = jnp.where(pick, su[g * epg + x], sv[x])
    i1 = jnp.zeros((1, tr), I32)
    b1 = v[0]
    w1 = sv[0]
    for x in range(1, epg):
        better = v[x] > b1
        i1 = jnp.where(better, x, i1)
        b1 = jnp.where(better, v[x], b1)
        w1 = jnp.where(better, sv[x], w1)
    i2 = jnp.zeros((1, tr), I32)
    b2 = jnp.full((1, tr), -jnp.inf, F32)
    w2 = jnp.zeros((1, tr), F32)
    for x in range(epg):
        take = (i1 != x) & (v[x] > b2)
        i2 = jnp.where(take, x, i2)
        b2 = jnp.where(take, v[x], b2)
        w2 = jnp.where(take, sv[x], w2)
    wsum = w1 + w2
    e1 = gidx * epg + i1
    e2 = gidx * epg + i2
    eio = lax.broadcasted_iota(I32, (N_EXPERTS, tr), 0)
    hit1 = eio == e1
    hit2 = eio == e2
    oh = jnp.where(hit1 | hit2, 1.0, 0.0)
    ri = lax.broadcasted_iota(I32, (tr, tr), 0)
    cj = lax.broadcasted_iota(I32, (tr, tr), 1)
    before = jnp.where(ri < cj, 1.0, 0.0).astype(BF16)
    rank = jnp.dot(oh.astype(BF16), before, preferred_element_type=F32) + carry[:, 0:1]
    rank1 = jnp.sum(jnp.where(hit1, rank, 0.0), axis=0, keepdims=True)
    rank2 = jnp.sum(jnp.where(hit2, rank, 0.0), axis=0, keepdims=True)
    idx_ref[0:1, :] = e1
    idx_ref[1:2, :] = e2
    idx_ref[2:3, :] = rank1.astype(I32)
    idx_ref[3:4, :] = rank2.astype(I32)
    idx_ref[4:8, :] = jnp.zeros((4, tr), I32)
    wts_ref[0:1, :] = w1 / wsum
    wts_ref[1:2, :] = w2 / wsum
    wts_ref[2:8, :] = jnp.zeros((6, tr), F32)
    carry[...] = carry[...] + jnp.sum(oh, axis=1, keepdims=True)
    cnt_ref[...] = carry[...]


def _two_set_specs(ha, hb, tile):
    d = ha.shape[1]
    na = ha.shape[0] // tile
    nb = hb.shape[0] // tile
    return na, nb, [pl.BlockSpec((tile, d), lambda i, *_: (jnp.minimum(i, na - 1), 0)),
                    pl.BlockSpec((tile, d), lambda i, *_: (jnp.maximum(i - na, 0), 0))]


def _router(ha, hb, w_router, b_router, tr):
    d = ha.shape[1]
    t = ha.shape[0] + hb.shape[0]
    na, nb, hspecs = _two_set_specs(ha, hb, tr)
    return pl.pallas_call(
        functools.partial(_router_kernel, na),
        grid=(na + nb,),
        in_specs=hspecs + [pl.BlockSpec((N_EXPERTS, d), lambda i: (0, 0)),
                           pl.BlockSpec((N_EXPERTS, 1), lambda i: (0, 0))],
        out_specs=[pl.BlockSpec((SUBLANES, tr), lambda i: (0, i)),
                   pl.BlockSpec((SUBLANES, tr), lambda i: (0, i)),
                   pl.BlockSpec((N_EXPERTS, LANES), lambda i: (0, 0))],
        out_shape=[jax.ShapeDtypeStruct((SUBLANES, t), I32),
                   jax.ShapeDtypeStruct((SUBLANES, t), F32),
                   jax.ShapeDtypeStruct((N_EXPERTS, LANES), F32)],
        scratch_shapes=[pltpu.VMEM((N_EXPERTS, LANES), F32)],
        compiler_params=_cparams(1, 32),
        name="moe_router",
    )(ha, hb, w_router.T, b_router.reshape(N_EXPERTS, 1))


def _scatter_kernel(n_tok, n_first, pos_ref, cnt_ref, off_ref, ha_ref, hb_ref, o_hbm, zero_scr, sem):
    i = pl.program_id(0)
    ts = ha_ref.shape[0]
    base = i * ts

    def row_copy(src, r, p):
        return pltpu.make_async_copy(src.at[pl.ds(r, 1)], o_hbm.at[pl.ds(p, 1)], sem)

    def scatter_tile(h_ref):
        def start_rows(r, c):
            row_copy(h_ref, r, pos_ref[base + r]).start()
            row_copy(h_ref, r, pos_ref[n_tok + base + r]).start()
            return c

        lax.fori_loop(0, ts, start_rows, 0, unroll=8)
        for _ in range(2):
            pltpu.make_async_copy(h_ref, o_hbm.at[pl.ds(0, ts)], sem).wait()

    @pl.when(i < n_first)
    def _():
        scatter_tile(ha_ref)

    @pl.when(i >= n_first)
    def _():
        scatter_tile(hb_ref)

    @pl.when(i == 0)
    def _():
        zero_scr[...] = jnp.zeros(zero_scr.shape, zero_scr.dtype)
        for e in range(N_EXPERTS + 1):
            lo = off_ref[e] + cnt_ref[e]
            hi = off_ref[e + 1]

            def start_zero(p, c):
                row_copy(zero_scr, 0, p).start()
                return c

            def wait_zero(p, c):
                row_copy(zero_scr, 0, 0).wait()
                return c

            lax.fori_loop(lo, hi, start_zero, 0)
            lax.fori_loop(lo, hi, wait_zero, 0)


def _scatter_rows(ha, hb, pos, cnt, off, n_rows, ts):
    d = ha.shape[1]
    t = ha.shape[0] + hb.shape[0]
    na, nb, hspecs = _two_set_specs(ha, hb, ts)
    return pl.pallas_call(
        functools.partial(_scatter_kernel, t, na),
        grid_spec=pltpu.PrefetchScalarGridSpec(
            num_scalar_prefetch=3,
            grid=(na + nb,),
            in_specs=hspecs,
            out_specs=pl.BlockSpec(memory_space=pl.ANY),
            scratch_shapes=[pltpu.VMEM((SUBLANES, d), ha.dtype), pltpu.SemaphoreType.DMA]),
        out_shape=jax.ShapeDtypeStruct((n_rows, d), ha.dtype),
        compiler_params=_cparams(1, 32),
        name="moe_scatter",
    )(pos, cnt, off, ha, hb)


def _cast_rows(src, dst):
    def body(c, carry):
        r = pl.multiple_of(c * CAST_ROWS, CAST_ROWS)
        dst[pl.ds(r, CAST_ROWS), :] = src[pl.ds(r, CAST_ROWS), :].astype(BF16)
        return carry

    lax.fori_loop(0, src.shape[0] // CAST_ROWS, body, 0)


def _expert_kernel(layer, te_ref, nu_ref, first_ref, nxt_ref, x_ref, w1_hbm, w2_hbm, y_ref,
                   stg1, stg2, wb1, wb2, sem):
    i = pl.program_id(0)

    def fetch(e):
        return (pltpu.make_async_copy(w1_hbm.at[layer, e], stg1, sem.at[0]),
                pltpu.make_async_copy(w2_hbm.at[layer, e], stg2, sem.at[1]))

    @pl.when(i == 0)
    def _():
        for cp in fetch(te_ref[0]):
            cp.start()

    @pl.when(first_ref[i] == 1)
    def _():
        for cp in fetch(0):
            cp.wait()
        _cast_rows(stg1, wb1)
        _cast_rows(stg2, wb2)

        @pl.when(nxt_ref[i] >= 0)
        def _():
            for cp in fetch(nxt_ref[i]):
                cp.start()

    @pl.when(i < nu_ref[0])
    def _():
        hid = jnp.dot(x_ref[...].astype(BF16), wb1[...], preferred_element_type=F32)
        f = hid.shape[1] // 2
        a = hid[:, :f]
        g = hid[:, f:]
        z = (g * jax.nn.sigmoid(g) * a).astype(BF16)
        y_ref[...] = jnp.dot(z, wb2[...], preferred_element_type=F32)

    @pl.when(i >= nu_ref[0])
    def _():
        y_ref[...] = jnp.zeros(y_ref.shape, y_ref.dtype)


def _experts(xs, w1, w2, layer, tile_expert, n_used, first, nxt):
    p, d = xs.shape
    tm = MOE_TILE
    ff2 = w1.shape[3]
    ff = ff2 // 2
    row = lambda i, te, nu, *_: (jnp.minimum(i, nu[0] - 1), 0)
    return pl.pallas_call(
        functools.partial(_expert_kernel, layer),
        grid_spec=pltpu.PrefetchScalarGridSpec(
            num_scalar_prefetch=4,
            grid=(p // tm,),
            in_specs=[pl.BlockSpec((tm, d), row),
                      pl.BlockSpec(memory_space=pl.ANY),
                      pl.BlockSpec(memory_space=pl.ANY)],
            out_specs=pl.BlockSpec((tm, d), lambda i, *_: (i, 0)),
            scratch_shapes=[pltpu.VMEM((d, ff2), F32), pltpu.VMEM((ff, d), F32),
                            pltpu.VMEM((d, ff2), BF16), pltpu.VMEM((ff, d), BF16),
                            pltpu.SemaphoreType.DMA((2,))]),
        out_shape=jax.ShapeDtypeStruct((p, d), F32),
        compiler_params=_cparams(1, 56),
        name="moe_experts",
    )(tile_expert, n_used, first, nxt, xs, w1, w2)


def _combine_kernel(n_tok, tok_base, has_next, pos_ref, w_ref, y_hbm, xres_ref, gate_ref, lng_ref, lnb_ref,
                    *rest):
    if has_next:
        shn_ref, scn_ref, xo_ref, ho_ref, gbuf, fbuf, sem = rest
        nxt = (shn_ref, scn_ref, ho_ref)
    else:
        xo_ref, gbuf, fbuf, sem = rest
        nxt = None
    tc = xres_ref.shape[0]
    i = pl.program_id(0)
    n_steps = pl.num_programs(0)

    def gather_tile(step, slot):
        base = tok_base + step * tc

        def start_rows(r, c):
            for k in range(2):
                pltpu.make_async_copy(y_hbm.at[pl.ds(pos_ref[k * n_tok + base + r], 1)],
                                      gbuf.at[slot, k, pl.ds(r, 1)], sem.at[slot]).start()
            return c

        lax.fori_loop(0, tc, start_rows, 0, unroll=8)

    slot = i % 2

    @pl.when(i == 0)
    def _():
        gather_tile(0, 0)

    @pl.when(i + 1 < n_steps)
    def _():
        gather_tile(i + 1, 1 - slot)

    for k in range(2):
        pltpu.make_async_copy(y_hbm.at[pl.ds(0, tc)], gbuf.at[slot, k], sem.at[slot]).wait()
    fbuf[0] = w_ref[:, 0:1] * gbuf[slot, 0] + w_ref[:, 1:2] * gbuf[slot, 1]
    _residual_ln(fbuf, xres_ref, gate_ref, lng_ref, lnb_ref, xo_ref, nxt)


def _combine_ln(ys, pos, wts, n_tok, tok_base, xres, gate, lng, lnb, nxt_mods, mod_idx, tc):
    t, d = xres.shape
    r = gate.shape[1]
    has_next = nxt_mods is not None
    mspec = pl.BlockSpec((1, r, d), lambda i, *_: (mod_idx(i), 0, 0))
    vspec = pl.BlockSpec((1, d), lambda i, *_: (0, 0))
    ospec = pl.BlockSpec((tc, d), lambda i, *_: (i, 0))
    wbase = tok_base // tc
    in_specs = [pl.BlockSpec((tc, 2), lambda i, *_: (wbase + i, 0)),
                pl.BlockSpec(memory_space=pl.ANY), ospec, mspec, vspec, vspec]
    args = [wts, ys, xres, gate, lng, lnb]
    out_specs = [ospec]
    out_shape = [jax.ShapeDtypeStruct((t, d), F32)]
    if has_next:
        in_specs += [mspec, mspec]
        args += list(nxt_mods)
        out_specs.append(ospec)
        out_shape.append(jax.ShapeDtypeStruct((t, d), BF16))
    return pl.pallas_call(
        functools.partial(_combine_kernel, n_tok, tok_base, has_next),
        grid_spec=pltpu.PrefetchScalarGridSpec(
            num_scalar_prefetch=1,
            grid=(t // tc,),
            in_specs=in_specs,
            out_specs=out_specs,
            scratch_shapes=[pltpu.VMEM((2, 2, tc, d), F32), pltpu.VMEM((1, tc, d), F32),
                            pltpu.SemaphoreType.DMA((2,))]),
        out_shape=out_shape,
        compiler_params=_cparams(1, 40),
        name="moe_combine",
    )(pos, *args)


def _moe_plan(idx, cnt_pad, n_tok):
    tm = MOE_TILE
    n_tiles = (2 * n_tok + N_EXPERTS * (tm - 1) + tm - 1) // tm
    cnt = cnt_pad[:, 0].astype(I32)
    padded = ((cnt + tm - 1) // tm) * tm
    ends = jnp.cumsum(padded)
    off = ends - padded
    pos = jnp.concatenate([jnp.take(off, idx[0]) + idx[2], jnp.take(off, idx[1]) + idx[3]])
    n_used = jnp.maximum(ends[-1] // tm, 1)
    tile_start = jnp.minimum(jnp.arange(n_tiles, dtype=I32), n_used - 1) * tm
    tile_expert = jnp.minimum(jnp.sum(tile_start[:, None] >= ends[None, :], axis=1), N_EXPERTS - 1).astype(I32)
    tiles = jnp.arange(n_tiles, dtype=I32)
    prev = jnp.concatenate([jnp.full((1,), -1, I32), tile_expert[:-1]])
    first = ((tiles < n_used) & (tile_expert != prev)).astype(I32)
    eid = jnp.arange(N_EXPERTS, dtype=I32)
    later = jnp.where((cnt[None, :] > 0) & (eid[None, :] > eid[:, None]), eid[None, :], N_EXPERTS)
    nxt_e = jnp.min(later, axis=1)
    nxt_e = jnp.where(nxt_e == N_EXPERTS, -1, nxt_e).astype(I32)
    nxt = jnp.take(nxt_e, tile_expert)
    n_rows = jnp.full((1,), n_tiles * tm, I32)
    off_ext = jnp.concatenate([off, ends[-1:], n_rows]).astype(I32)
    cnt = jnp.concatenate([cnt, jnp.zeros((1,), I32)])
    return (pos.astype(I32), cnt, off_ext, tile_expert, n_used.reshape(1).astype(I32), first, nxt,
            n_tiles * tm)


class _Rows:
    def __init__(self, n_seq, seq_len, tile):
        self.n_seq, self.seq_len, self.tile = n_seq, seq_len, tile
        self.per_row = seq_len < tile

    def mods(self, m):
        if self.per_row:
            return jnp.repeat(m, self.seq_len, axis=0).reshape(-1, self.tile, m.shape[1])
        return m[:, None, :]

    def mod_idx(self, i):
        return i if self.per_row else (i * self.tile) // self.seq_len


def _split_mods(mod_l, n_prompt):
    d = mod_l.shape[1] // 6
    cols = [mod_l[:, k * d:(k + 1) * d] for k in range(6)]
    return [c[:n_prompt] for c in cols], [c[n_prompt:] for c in cols]


def kernel(x_prompt, x_sample, state_mlstm_C, state_mlstm_n, state_mlstm_m, c_prompt, c_sample, w_ada, b_ada, ln_g, ln_b, a_w_in, a_b_gates, a_norm_w, a_w_out, b_w_in, b_b_in, b_norm_g, b_norm_b, b_w_s, b_b_s, b_w_out, w_router, b_router, w_expert_in, w_expert_out):
    bp, sp, d = x_prompt.shape
    bs, ss, _ = x_sample.shape
    tp = bp * sp
    ts = bs * ss
    n_tok = tp + ts
    qk = N_HEADS * DK
    vd = N_HEADS * DV
    n_main = 2 * qk + 2 * vd

    n_seq = bp + bs
    pad = (-n_seq) % SUBLANES
    c_all = jnp.concatenate([c_prompt, c_sample, jnp.zeros((pad, d), F32)])
    mod = _adaln(c_all, w_ada, b_ada)[:, :n_seq]

    sets = {"p": _Rows(bp, sp, 512), "s": _Rows(bs, ss, 256)}
    base = {"p": 0, "s": tp}
    x = {"p": x_prompt.reshape(tp, d), "s": x_sample.reshape(ts, d)}
    hm = {}
    outs = {}
    for layer in range(DEPTH):
        j = layer // 2
        mp, ms = _split_mods(mod[layer], bp)
        md = {"p": mp, "s": ms}
        lng = ln_g[layer]
        lnb = ln_b[layer]
        lhs = {}
        if layer % 2 == 0:
            w_in = a_w_in[j]
            wg = jnp.zeros((d, 2 * LANES), F32)
            wg = wg.at[:, :N_HEADS].set(w_in[:, n_main:n_main + N_HEADS])
            wg = wg.at[:, LANES:LANES + N_HEADS].set(w_in[:, n_main + N_HEADS:])
            gb = jnp.zeros((1, 2 * LANES), F32)
            gb = gb.at[0, :N_HEADS].set(a_b_gates[j, :N_HEADS])
            gb = gb.at[0, LANES:LANES + N_HEADS].set(a_b_gates[j, N_HEADS:])
            nw = a_norm_w[j].reshape(1, vd)
            proj = {}
            gates = {}
            for k, rs in sets.items():
                proj[k], gates[k] = _inproj(x[k], rs.mods(md[k][0]), rs.mods(md[k][1]), rs.mod_idx,
                                            w_in, wg, n_main, rs.tile, BF16 if k == "p" else F32)
            lhs["p"], c_p, n_p, m_p = _mlstm_chunks(
                proj["p"], gates["p"], gb, nw,
                jnp.zeros((bp, N_HEADS, DK, DV), F32), jnp.zeros((bp, N_HEADS, LANES), F32),
                jnp.zeros((bp, N_HEADS, LANES), F32), bp, sp)
            rpad = SUBLANES - ss
            proj_s = jnp.pad(proj["s"].reshape(bs, ss, n_main), ((0, 0), (0, rpad), (0, 0)))
            gates_s = jnp.pad(gates["s"].reshape(bs, ss, 2 * LANES), ((0, 0), (0, rpad), (0, 0)))
            m0 = jnp.pad(state_mlstm_m[j], ((0, 0), (0, LANES - N_HEADS)))[:, None, :]
            ypre_s, c_s, n_s, m_s = _mlstm_short(proj_s, gates_s, gb, nw, state_mlstm_C[j],
                                                 state_mlstm_n[j].reshape(bs, 1, qk), m0, ss)
            n_s = n_s.reshape(bs, N_HEADS, DK)
            lhs["s"] = ypre_s[:, :ss].reshape(ts, vd)
            outs["C_p"], outs["n_p"], outs["m_p"] = c_p, n_p, m_p[:, :, 0]
            outs["C_s"], outs["n_s"], outs["m_s"] = c_s, n_s, m_s[:, 0, :N_HEADS]
            w_out = a_w_out[j]
        else:
            tmix = 256
            ws = b_w_s[j]
            bsv = b_b_s[j]

            def mixing(l):
                tri = jnp.tril(jnp.ones((l, l), bool))
                wsl = jnp.where(tri, ws[:, :l, :l], 0.0)
                eye = jnp.eye(tmix // l, dtype=F32)
                mats = jax.vmap(lambda m: jnp.kron(eye, m))(wsl).astype(BF16)
                bias = jnp.tile(bsv[:, :l], (1, tmix // l))[:, :, None]
                return mats, bias

            for k, rs in sets.items():
                uv = _gmlp_in(hm[k], b_w_in[j], b_b_in[j], min(1024, rs.n_seq * rs.seq_len),
                              BF16 if k == "p" else F32)
                mats, bias = mixing(min(CHUNK, rs.seq_len))
                res = _gmlp_mix(uv, b_norm_g[j], b_norm_b[j], mats, bias, tmix, k == "s")
                lhs[k] = res[0]
                if k == "s":
                    outs["v_s"] = res[1]
            w_out = b_w_out[j]
        x1 = {}
        hf = {}
        for k, rs in sets.items():
            rs = _Rows(rs.n_seq, rs.seq_len, min(rs.tile, 512 * 2048 // w_out.shape[0]))
            x1[k], hf[k] = _proj_ln(lhs[k], w_out, x[k], rs.mods(md[k][2]), lng[0:1], lnb[0:1],
                                    rs.mods(md[k][3]), rs.mods(md[k][4]), rs.mod_idx, rs.tile, 512)

        idx, wts, cnt = _router(hf["p"], hf["s"], w_router, b_router, 512)
        pos, cnt_i, off_ext, tile_expert, n_used, first, nxt, n_rows = _moe_plan(idx, cnt, n_tok)
        xsorted = _scatter_rows(hf["p"], hf["s"], pos, cnt_i, off_ext, n_rows, 256)
        ysorted = _experts(xsorted, w_expert_in, w_expert_out, layer, tile_expert, n_used, first, nxt)
        wts2 = wts[:2].T
        nxt = _split_mods(mod[layer + 1], bp) if layer + 1 < DEPTH else None
        for ki, (k, rs) in enumerate(sets.items()):
            rc = _Rows(rs.n_seq, rs.seq_len, 256)
            nxt_mods = None if nxt is None else (rc.mods(nxt[ki][0]), rc.mods(nxt[ki][1]))
            res = _combine_ln(ysorted, pos, wts2, n_tok, base[k], x1[k], rc.mods(md[k][5]),
                              lng[1:2], lnb[1:2], nxt_mods, rc.mod_idx, 256)
            x[k] = res[0]
            if nxt is not None:
                hm[k] = res[1]

    return (x["p"].reshape(bp, sp, d), x["s"].reshape(bs, ss, d),
            outs["C_p"][None], outs["n_p"][None], outs["m_p"][None],
            outs["C_s"][None], outs["n_s"][None], outs["m_s"][None],
            outs["v_s"].reshape(bs, ss, -1)[None])
```

```python
import functools

import jax
import jax.numpy as jnp
from jax import lax
from jax.experimental import pallas as pl
from jax.experimental.pallas import tpu as pltpu

F32 = jnp.float32
BF16 = jnp.bfloat16
I32 = jnp.int32

DEPTH = 2
N_HEADS = 8
DK = 128
DV = 256
CHUNK = 128
N_GROUPS_B = 8
N_EXPERTS = 16
N_EXPERT_GROUPS = 4
EXPERTS_PER_GROUP = 4
ALPHA = float((2 * DEPTH) ** 0.25)
LN_EPS = 1e-5

LANES = 128
SUBLANES = 8
MIB = 1024 * 1024
MOE_TILE = 256
CAST_ROWS = 256
STAGE_ROWS = 128


def _cparams(n_axes, vmem_mib):
    return pltpu.CompilerParams(
        dimension_semantics=("arbitrary",) * n_axes,
        vmem_limit_bytes=int(vmem_mib * MIB))


def _split3(x):
    hi = x.astype(BF16)
    r1 = x - hi.astype(F32)
    mid = r1.astype(BF16)
    lo = (r1 - mid.astype(F32)).astype(BF16)
    return hi, mid, lo


def _log_sigmoid(x):
    return jnp.minimum(x, 0.0) - jnp.log1p(jnp.exp(-jnp.abs(x)))


def _load_weight_bf16(w_hbm, wb, stg, sem):
    nj, k, tn = wb.shape
    ch = stg.shape[1]
    nch = k // ch

    def chunk(c):
        return pltpu.make_async_copy(w_hbm.at[pl.ds(c * ch, ch), pl.ds(0, nj * tn)], stg.at[c % 2],
                                     sem.at[c % 2])

    chunk(0).start()
    for c in range(nch):
        if c + 1 < nch:
            chunk(c + 1).start()
        chunk(c).wait()
        for jj in range(nj):
            wb[jj, c * ch:(c + 1) * ch, :] = stg[c % 2, :, jj * tn:(jj + 1) * tn].astype(BF16)


def _adaln_kernel(c_ref, w_ref, b_ref, o_ref):
    c = c_ref[...]
    a = (c * jax.nn.sigmoid(c)).astype(BF16)
    o_ref[0, 0] = jnp.dot(a, w_ref[0].astype(BF16), preferred_element_type=F32) + b_ref[0]


def _adaln(c_all, w_ada, b_ada):
    depth, d, n = w_ada.shape
    r = c_all.shape[0]
    tn = 1024
    return pl.pallas_call(
        _adaln_kernel,
        grid=(depth, n // tn),
        in_specs=[pl.BlockSpec((r, d), lambda l, j: (0, 0)),
                  pl.BlockSpec((1, d, tn), lambda l, j: (l, 0, j)),
                  pl.BlockSpec((1, 1, tn), lambda l, j: (l, 0, j))],
        out_specs=pl.BlockSpec((1, 1, r, tn), lambda l, j: (l, j // (d // tn), 0, j % (d // tn))),
        out_shape=jax.ShapeDtypeStruct((depth, n // d, r, d), F32),
        compiler_params=_cparams(2, 32),
        name="adaln",
    )(c_all, w_ada, b_ada.reshape(depth, 1, n))


def _inproj_kernel(x_ref, sh_ref, sc_ref, w_hbm, proj_ref, gates_ref, hin_ref, wb, stg, g16, wgs, sem):
    j = pl.program_id(1)

    @pl.when((pl.program_id(0) == 0) & (j == 0))
    def _():
        _load_weight_bf16(w_hbm, wb, stg, sem)
        n_main = wb.shape[0] * wb.shape[2]
        cp = pltpu.make_async_copy(w_hbm.at[:, pl.ds(n_main, 2 * N_HEADS)], g16, sem.at[0])
        cp.start()
        cp.wait()
        g = g16[...]
        z = jnp.zeros((g.shape[0], LANES - N_HEADS), F32)
        wg = jnp.concatenate([g[:, :N_HEADS], z, g[:, N_HEADS:], z], axis=1)
        hi = wg.astype(BF16)
        wgs[0] = hi
        wgs[1] = (wg - hi.astype(F32)).astype(BF16)

    @pl.when(j == 0)
    def _():
        h = x_ref[...] * (1.0 + sc_ref[0]) + sh_ref[0]
        hb = h.astype(BF16)
        hin_ref[...] = hb
        h_lo = (h - hb.astype(F32)).astype(BF16)
        gates_ref[...] = (jnp.dot(hb, wgs[0], preferred_element_type=F32)
                          + jnp.dot(h_lo, wgs[0], preferred_element_type=F32)
                          + jnp.dot(hb, wgs[1], preferred_element_type=F32))

    proj_ref[...] = jnp.dot(hin_ref[...], wb[j], preferred_element_type=F32).astype(proj_ref.dtype)


def _inproj(x, sh, sc, mod_idx, w_in, n_main, tm, out_dtype):
    t, d = x.shape
    tn = 1024
    r = sh.shape[1]
    mspec = pl.BlockSpec((1, r, d), lambda i, j: (mod_idx(i), 0, 0))
    return pl.pallas_call(
        _inproj_kernel,
        grid=(t // tm, n_main // tn),
        in_specs=[pl.BlockSpec((tm, d), lambda i, j: (i, 0)), mspec, mspec,
                  pl.BlockSpec(memory_space=pl.ANY)],
        out_specs=[pl.BlockSpec((tm, tn), lambda i, j: (i, j)),
                   pl.BlockSpec((tm, 2 * LANES), lambda i, j: (i, 0))],
        out_shape=[jax.ShapeDtypeStruct((t, n_main), out_dtype),
                   jax.ShapeDtypeStruct((t, 2 * LANES), F32)],
        scratch_shapes=[pltpu.VMEM((tm, d), BF16), pltpu.VMEM((n_main // tn, d, tn), BF16),
                        pltpu.VMEM((2, STAGE_ROWS, n_main), F32), pltpu.VMEM((d, 2 * N_HEADS), F32),
                        pltpu.VMEM((2, d, 2 * LANES), BF16), pltpu.SemaphoreType.DMA((2,))],
        compiler_params=_cparams(2, 56),
        name="mlstm_inproj",
    )(x, sh, sc, w_in)


def _head_norm_gate(hh, nw, o):
    mu = jnp.mean(hh, axis=1, keepdims=True)
    xc = hh - mu
    var = jnp.mean(xc * xc, axis=1, keepdims=True)
    return jax.nn.sigmoid(o) * (xc * lax.rsqrt(var + LN_EPS) * nw)


def _mlstm_chunk_kernel(q_ref, k_ref, v_ref, o_ref, g_ref, gb_ref, nw_ref, c0_ref, n0_ref, m0_ref,
                        y_ref, cout_ref, nout_ref, mout_ref, c_scr, n_scr, m_scr):
    ci = pl.program_id(1)
    L = q_ref.shape[0]

    @pl.when(ci == 0)
    def _():
        c_scr[...] = c0_ref[0]
        n_scr[...] = n0_ref[0]
        m_scr[...] = m0_ref[0]

    g = g_ref[...] + gb_ref[...]
    gi = g[:, :LANES]
    lf = _log_sigmoid(g[:, LANES:])
    row = lax.broadcasted_iota(I32, (L, L), 0)
    col = lax.broadcasted_iota(I32, (L, L), 1)
    causal = col <= row
    ltri = jnp.where(causal, 1.0, 0.0).astype(BF16)
    hi, mid, lo = _split3(lf)
    bcum = (jnp.dot(ltri, hi, preferred_element_type=F32)
            + jnp.dot(ltri, mid, preferred_element_type=F32)
            + jnp.dot(ltri, lo, preferred_element_type=F32))
    a = gi - bcum
    a_t = a.T
    scale = DK ** -0.5
    nt = (((1,), (1,)), ((), ()))
    tn_dims = (((0,), (0,)), ((), ()))
    for h in range(N_HEADS):
        ks = slice(h * DK, (h + 1) * DK)
        vs = slice(h * DV, (h + 1) * DV)
        qf = q_ref[:, ks].astype(F32) * scale
        qb = qf.astype(BF16)
        kf = k_ref[:, ks].astype(F32)
        kb = kf.astype(BF16)
        vb = v_ref[:, vs].astype(BF16)
        a_row = a_t[h:h + 1, :]
        a_col = a[:, h:h + 1]
        b_col = bcum[:, h:h + 1]
        m_prev = m_scr[h:h + 1, 0:1]
        amat = jnp.where(causal, a_row, -jnp.inf)
        mx = jnp.max(amat, axis=1, keepdims=True)
        m_inter = b_col + m_prev
        m_t = jnp.maximum(m_inter, b_col + mx)
        dm = jnp.exp(amat + (b_col - m_t))
        s = lax.dot_general(qb, kb, nt, preferred_element_type=F32)
        scores = s * dm
        inter = jnp.exp(m_inter - m_t)
        c_old = c_scr[h]
        n_old = n_scr[h:h + 1, :]
        qc = jnp.dot(qb, c_old.astype(BF16), preferred_element_type=F32)
        num = jnp.dot(scores.astype(BF16), vb, preferred_element_type=F32) + inter * qc
        qn = jnp.sum(qf * n_old, axis=1, keepdims=True)
        den = jnp.sum(scores, axis=1, keepdims=True) + inter * qn
        hh = num * (1.0 / jnp.maximum(jnp.abs(den), jnp.exp(-m_t)))
        m_new = m_t[L - 1:L, :]
        b_last = b_col[L - 1:L, :]
        w_col = jnp.exp(b_last + a_col - m_new)
        decay = jnp.exp(b_last + m_prev - m_new)
        kw = kf * w_col
        c_scr[h] = decay * c_old + lax.dot_general(kw.astype(BF16), vb, tn_dims,
                                                   preferred_element_type=F32)
        n_scr[h:h + 1, :] = decay * n_old + jnp.sum(kw, axis=0, keepdims=True)
        m_scr[h:h + 1, :] = jnp.broadcast_to(m_new, (1, LANES))
        y_ref[:, vs] = _head_norm_gate(hh, nw_ref[:, vs], o_ref[:, vs].astype(F32)).astype(y_ref.dtype)

    @pl.when(ci == pl.num_programs(1) - 1)
    def _():
        cout_ref[0] = c_scr[...]
        nout_ref[0] = n_scr[...]
        mout_ref[0] = m_scr[...]


def _mlstm_chunks(proj, gates, gb, nw, c0, n0, m0, batch, seq):
    nc = seq // CHUNK
    qk = N_HEADS * DK
    vd = N_HEADS * DV
    row = lambda b, c: b * nc + c
    st4 = pl.BlockSpec((1, N_HEADS, DK, DV), lambda b, c: (b, 0, 0, 0))
    st3 = pl.BlockSpec((1, N_HEADS, LANES), lambda b, c: (b, 0, 0))
    return pl.pallas_call(
        _mlstm_chunk_kernel,
        grid=(batch, nc),
        in_specs=[pl.BlockSpec((CHUNK, qk), lambda b, c: (row(b, c), 0)),
                  pl.BlockSpec((CHUNK, qk), lambda b, c: (row(b, c), 1)),
                  pl.BlockSpec((CHUNK, vd), lambda b, c: (row(b, c), 1)),
                  pl.BlockSpec((CHUNK, vd), lambda b, c: (row(b, c), 2)),
                  pl.BlockSpec((CHUNK, 2 * LANES), lambda b, c: (row(b, c), 0)),
                  pl.BlockSpec((1, 2 * LANES), lambda b, c: (0, 0)),
                  pl.BlockSpec((1, vd), lambda b, c: (0, 0)),
                  st4, st3, st3],
        out_specs=[pl.BlockSpec((CHUNK, vd), lambda b, c: (row(b, c), 0)), st4, st3, st3],
        out_shape=[jax.ShapeDtypeStruct((batch * seq, vd), BF16),
                   jax.ShapeDtypeStruct((batch, N_HEADS, DK, DV), F32),
                   jax.ShapeDtypeStruct((batch, N_HEADS, LANES), F32),
                   jax.ShapeDtypeStruct((batch, N_HEADS, LANES), F32)],
        scratch_shapes=[pltpu.VMEM((N_HEADS, DK, DV), F32),
                        pltpu.VMEM((N_HEADS, LANES), F32),
                        pltpu.VMEM((N_HEADS, LANES), F32)],
        compiler_params=_cparams(2, 40),
        name="mlstm_chunks",
    )(proj, proj, proj, proj, gates, gb, nw, c0, n0, m0)


def _per_head(x, width):
    return jnp.concatenate(
        [jnp.broadcast_to(x[..., h:h + 1], x.shape[:-1] + (width,)) for h in range(N_HEADS)], axis=-1)


def _head_sums(x, width):
    lane = lax.broadcasted_iota(I32, x.shape[:-1] + (LANES,), x.ndim - 1)
    out = jnp.zeros(x.shape[:-1] + (LANES,), F32)
    for h in range(N_HEADS):
        s = jnp.sum(x[..., h * width:(h + 1) * width], axis=-1, keepdims=True)
        out = jnp.where(lane == h, s, out)
    return out


def _mlstm_short_kernel(seq, q_ref, k_ref, v_ref, o_ref, g_ref, gb_ref, nw_ref, c0_ref, n0_ref, m0_ref,
                        y_ref, cout_ref, nout_ref, mout_ref, qc_scr, kw_scr):
    bt, rows, _ = q_ref.shape
    scale = DK ** -0.5
    tn_dims = (((0,), (0,)), ((), ()))
    row = lax.broadcasted_iota(I32, (bt, rows, LANES), 1)
    g = g_ref[...] + gb_ref[...]
    gi = g[:, :, :LANES]
    lf = _log_sigmoid(g[:, :, LANES:])
    bcum = jnp.zeros_like(lf)
    for s in range(seq):
        bcum = bcum + jnp.where(row >= s, lf[:, s:s + 1, :], 0.0)
    a = gi - bcum
    mx = jnp.full_like(a, -jnp.inf)
    for s in range(seq):
        mx = jnp.maximum(mx, jnp.where(row >= s, a[:, s:s + 1, :], -jnp.inf))
    m_prev = m0_ref[...]
    m_inter = bcum + m_prev
    m_t = jnp.maximum(m_inter, bcum + mx)
    cmt = bcum - m_t
    inter = jnp.exp(m_inter - m_t)
    einv = jnp.exp(-m_t)
    m_new = m_t[:, seq - 1:seq, :]
    b_last = bcum[:, seq - 1:seq, :]
    w = jnp.where(row < seq, jnp.exp(b_last + a - m_new), 0.0)
    decay = jnp.exp(b_last + m_prev - m_new)
    mout_ref[...] = m_new

    q = q_ref[...] * scale
    k = k_ref[...]
    v = v_ref[...]
    n_old = n0_ref[...]
    for b in range(bt):
        for h in range(N_HEADS):
            qc_scr[b, :, h * DV:(h + 1) * DV] = jnp.dot(
                (q_ref[b, :, h * DK:(h + 1) * DK] * scale).astype(BF16), c0_ref[b, h].astype(BF16),
                preferred_element_type=F32)
    den = inter * _head_sums(q * n_old, DK)
    num = _per_head(inter, DV) * qc_scr[...]
    for s in range(seq):
        p = _head_sums(q * k[:, s:s + 1, :], DK) * jnp.where(row >= s, jnp.exp(cmt + a[:, s:s + 1, :]), 0.0)
        den = den + p
        num = num + _per_head(p, DV) * v[:, s:s + 1, :]
    hh = num * _per_head(1.0 / jnp.maximum(jnp.abs(den), einv), DV)
    mu = _head_sums(hh, DV) * (1.0 / DV)
    xc = hh - _per_head(mu, DV)
    var = _head_sums(xc * xc, DV) * (1.0 / DV)
    hn = xc * _per_head(lax.rsqrt(var + LN_EPS), DV) * nw_ref[...]
    y_ref[...] = jax.nn.sigmoid(o_ref[...]) * hn

    kw = k * _per_head(w, DK)
    kw_scr[...] = kw
    nout_ref[...] = _per_head(decay, DK) * n_old + jnp.sum(kw, axis=1, keepdims=True)
    for b in range(bt):
        for h in range(N_HEADS):
            cout_ref[b, h] = (decay[b, :, h:h + 1] * c0_ref[b, h]
                              + lax.dot_general(kw_scr[b, :, h * DK:(h + 1) * DK],
                                                v_ref[b, :, h * DV:(h + 1) * DV],
                                                tn_dims, preferred_element_type=F32))


def _mlstm_short(proj, gates, gb, nw, c0, n0, m0, seq):
    batch, rows, _ = proj.shape
    qk = N_HEADS * DK
    vd = N_HEADS * DV
    bt = 8
    st4 = pl.BlockSpec((bt, N_HEADS, DK, DV), lambda i: (i, 0, 0, 0))
    st3 = pl.BlockSpec((bt, 1, qk), lambda i: (i, 0, 0))
    stm = pl.BlockSpec((bt, 1, LANES), lambda i: (i, 0, 0))
    return pl.pallas_call(
        functools.partial(_mlstm_short_kernel, seq),
        grid=(batch // bt,),
        in_specs=[pl.BlockSpec((bt, rows, qk), lambda i: (i, 0, 0)),
                  pl.BlockSpec((bt, rows, qk), lambda i: (i, 0, 1)),
                  pl.BlockSpec((bt, rows, vd), lambda i: (i, 0, 1)),
                  pl.BlockSpec((bt, rows, vd), lambda i: (i, 0, 2)),
                  pl.BlockSpec((bt, rows, 2 * LANES), lambda i: (i, 0, 0)),
                  pl.BlockSpec((1, 2 * LANES), lambda i: (0, 0)),
                  pl.BlockSpec((1, vd), lambda i: (0, 0)),
                  st4, st3, stm],
        out_specs=[pl.BlockSpec((bt, rows, vd), lambda i: (i, 0, 0)), st4, st3, stm],
        out_shape=[jax.ShapeDtypeStruct((batch, rows, vd), F32),
                   jax.ShapeDtypeStruct((batch, N_HEADS, DK, DV), F32),
                   jax.ShapeDtypeStruct((batch, 1, qk), F32),
                   jax.ShapeDtypeStruct((batch, 1, LANES), F32)],
        scratch_shapes=[pltpu.VMEM((bt, rows, vd), F32), pltpu.VMEM((bt, rows, qk), F32)],
        compiler_params=_cparams(1, 48),
        name="mlstm_short",
    )(proj, proj, proj, proj, gates, gb, nw, c0, n0, m0)


def _residual_ln(zbuf, xres_ref, gate_ref, lng_ref, lnb_ref, xo_ref, nxt):
    nc, tm, tn = zbuf.shape
    inv_d = 1.0 / (nc * tn)
    ssum = jnp.zeros((tm, 1), F32)
    for c in range(nc):
        sl = pl.ds(c * tn, tn)
        z = ALPHA * xres_ref[:, sl] + gate_ref[0, :, sl] * zbuf[c]
        zbuf[c] = z
        ssum = ssum + jnp.sum(z, axis=1, keepdims=True)
    mu = ssum * inv_d
    vsum = jnp.zeros((tm, 1), F32)
    for c in range(nc):
        zc = zbuf[c] - mu
        vsum = vsum + jnp.sum(zc * zc, axis=1, keepdims=True)
    rstd = lax.rsqrt(vsum * inv_d + LN_EPS)
    for c in range(nc):
        sl = pl.ds(c * tn, tn)
        xn = (zbuf[c] - mu) * rstd * lng_ref[:, sl] + lnb_ref[:, sl]
        xo_ref[:, sl] = xn
        if nxt is not None:
            sh_ref, sc_ref, ho_ref = nxt
            ho_ref[:, sl] = (xn * (1.0 + sc_ref[0, :, sl]) + sh_ref[0, :, sl]).astype(ho_ref.dtype)


def _proj_ln_kernel(lhs_ref, w_hbm, xres_ref, gate_ref, lng_ref, lnb_ref, shn_ref, scn_ref,
                    xo_ref, ho_ref, wb, stg, ybuf, sem):
    @pl.when(pl.program_id(0) == 0)
    def _():
        _load_weight_bf16(w_hbm, wb, stg, sem)

    lhs = lhs_ref[...].astype(BF16)
    for c in range(ybuf.shape[0]):
        ybuf[c] = jnp.dot(lhs, wb[c], preferred_element_type=F32)
    _residual_ln(ybuf, xres_ref, gate_ref, lng_ref, lnb_ref, xo_ref, (shn_ref, scn_ref, ho_ref))


def _proj_ln(lhs, w, xres, gate, lng, lnb, shn, scn, mod_idx, tm, tn):
    t, k = lhs.shape
    d = w.shape[1]
    r = gate.shape[1]
    mspec = pl.BlockSpec((1, r, d), lambda i: (mod_idx(i), 0, 0))
    vspec = pl.BlockSpec((1, d), lambda i: (0, 0))
    ospec = pl.BlockSpec((tm, d), lambda i: (i, 0))
    return pl.pallas_call(
        _proj_ln_kernel,
        grid=(t // tm,),
        in_specs=[pl.BlockSpec((tm, k), lambda i: (i, 0)),
                  pl.BlockSpec(memory_space=pl.ANY),
                  ospec, mspec, vspec, vspec, mspec, mspec],
        out_specs=[ospec, ospec],
        out_shape=[jax.ShapeDtypeStruct((t, d), F32), jax.ShapeDtypeStruct((t, d), F32)],
        scratch_shapes=[pltpu.VMEM((d // tn, k, tn), BF16), pltpu.VMEM((2, CAST_ROWS, d), F32),
                        pltpu.VMEM((d // tn, tm, tn), F32), pltpu.SemaphoreType.DMA((2,))],
        compiler_params=_cparams(1, 58),
        name="proj_ln",
    )(lhs, w, xres, gate, lng, lnb, shn, scn)


def _gelu_tanh(x):
    return x * (0.5 * (1.0 + jnp.tanh(0.7978845608028654 * (x + 0.044715 * (x * x * x)))))


def _gmlp_in_kernel(h_ref, w_hbm, b_ref, o_ref, wb, stg, sem):
    j = pl.program_id(1)

    @pl.when((pl.program_id(0) == 0) & (j == 0))
    def _():
        _load_weight_bf16(w_hbm, wb, stg, sem)

    acc = jnp.dot(h_ref[...].astype(BF16), wb[j], preferred_element_type=F32)
    o_ref[...] = _gelu_tanh(acc + b_ref[...]).astype(o_ref.dtype)


def _gmlp_in(h, w, b, tm, out_dtype):
    t, d = h.shape
    n = w.shape[1]
    tn = 1024
    return pl.pallas_call(
        _gmlp_in_kernel,
        grid=(t // tm, n // tn),
        in_specs=[pl.BlockSpec((tm, d), lambda i, j: (i, 0)),
                  pl.BlockSpec(memory_space=pl.ANY),
                  pl.BlockSpec((1, tn), lambda i, j: (0, j))],
        out_specs=pl.BlockSpec((tm, tn), lambda i, j: (i, j)),
        out_shape=jax.ShapeDtypeStruct((t, n), out_dtype),
        scratch_shapes=[pltpu.VMEM((n // tn, d, tn), BF16), pltpu.VMEM((2, STAGE_ROWS, n), F32),
                        pltpu.SemaphoreType.DMA((2,))],
        compiler_params=_cparams(2, 58),
        name="gmlp_in",
    )(h, w, b.reshape(1, n))


def _gmlp_mix_kernel(u_ref, v_ref, g_ref, b_ref, mix_ref, bias_ref, o_ref, vn_ref=None):
    v = v_ref[...].astype(F32)
    mu = jnp.mean(v, axis=1, keepdims=True)
    xc = v - mu
    var = jnp.mean(xc * xc, axis=1, keepdims=True)
    vn = xc * lax.rsqrt(var + LN_EPS) * g_ref[...] + b_ref[...]
    if vn_ref is not None:
        vn_ref[...] = vn
    gd = v.shape[1] // N_GROUPS_B
    for g in range(N_GROUPS_B):
        sl = slice(g * gd, (g + 1) * gd)
        mixed = jnp.dot(mix_ref[g], vn[:, sl].astype(BF16), preferred_element_type=F32) + bias_ref[g]
        o_ref[:, sl] = (u_ref[:, sl].astype(F32) * mixed).astype(o_ref.dtype)


def _gmlp_mix(uv, nv_g, nv_b, mix, bias, tm, emit_v):
    t, n2 = uv.shape
    di = n2 // 2
    return pl.pallas_call(
        _gmlp_mix_kernel,
        grid=(t // tm,),
        in_specs=[pl.BlockSpec((tm, di), lambda i: (i, 0)),
                  pl.BlockSpec((tm, di), lambda i: (i, 1)),
                  pl.BlockSpec((1, di), lambda i: (0, 0)),
                  pl.BlockSpec((1, di), lambda i: (0, 0)),
                  pl.BlockSpec(mix.shape, lambda i: (0, 0, 0)),
                  pl.BlockSpec(bias.shape, lambda i: (0, 0, 0))],
        out_specs=[pl.BlockSpec((tm, di), lambda i: (i, 0))] * (2 if emit_v else 1),
        out_shape=[jax.ShapeDtypeStruct((t, di), BF16), jax.ShapeDtypeStruct((t, di), F32)][:2 if emit_v else 1],
        compiler_params=_cparams(1, 48),
        name="gmlp_mix",
    )(uv, uv, nv_g.reshape(1, di), nv_b.reshape(1, di), mix, bias)


def _router_kernel(n_first, ha_ref, hb_ref, wr_ref, br_ref, idx_ref, wts_ref, cnt_ref, carry):
    i = pl.program_id(0)
    tr = ha_ref.shape[0]

    @pl.when(i == 0)
    def _():
        carry[...] = jnp.zeros(carry.shape, carry.dtype)

    nt = (((1,), (1,)), ((), ()))
    h = jnp.where(i < n_first, ha_ref[...], hb_ref[...])
    hb = h.astype(BF16)
    hl = (h - hb.astype(F32)).astype(BF16)
    wr = wr_ref[...]
    wb = wr.astype(BF16)
    wl = (wr - wb.astype(F32)).astype(BF16)
    logits = (lax.dot_general(wb, hb, nt, preferred_element_type=F32)
              + lax.dot_general(wb, hl, nt, preferred_element_type=F32)
              + lax.dot_general(wl, hb, nt, preferred_element_type=F32))
    s = jax.nn.sigmoid(logits)
    sel = s + br_ref[...]
    epg = EXPERTS_PER_GROUP
    r = [sel[e:e + 1, :] for e in range(N_EXPERTS)]
    su = [s[e:e + 1, :] for e in range(N_EXPERTS)]

    def top2sum(v):
        best = v[0] + v[1]
        for x in range(epg):
            for y in range(x + 1, epg):
                if (x, y) != (0, 1):
                    best = jnp.maximum(best, v[x] + v[y])
        return best

    gs = [top2sum(r[g * epg:(g + 1) * epg]) for g in range(N_EXPERT_GROUPS)]
    gbest = gs[0]
    gidx = jnp.zeros((1, tr), I32)
    for g in range(1, N_EXPERT_GROUPS):
        better = gs[g] > gbest
        gidx = jnp.where(better, g, gidx)
        gbest = jnp.where(better, gs[g], gbest)
    v = list(r[:epg])
    sv = list(su[:epg])
    for g in range(1, N_EXPERT_GROUPS):
        pick = gidx == g
        for x in range(epg):
            v[x] = jnp.where(pick, r[g * epg + x], v[x])
            sv[x] = jnp.where(pick, su[g * epg + x], sv[x])
    i1 = jnp.zeros((1, tr), I32)
    b1 = v[0]
    w1 = sv[0]
    for x in range(1, epg):
        better = v[x] > b1
        i1 = jnp.where(better, x, i1)
        b1 = jnp.where(better, v[x], b1)
        w1 = jnp.where(better, sv[x], w1)
    i2 = jnp.zeros((1, tr), I32)
    b2 = jnp.full((1, tr), -jnp.inf, F32)
    w2 = jnp.zeros((1, tr), F32)
    for x in range(epg):
        take = (i1 != x) & (v[x] > b2)
        i2 = jnp.where(take, x, i2)
        b2 = jnp.where(take, v[x], b2)
        w2 = jnp.where(take, sv[x], w2)
    wsum = w1 + w2
    e1 = gidx * epg + i1
    e2 = gidx * epg + i2
    eio = lax.broadcasted_iota(I32, (N_EXPERTS, tr), 0)
    hit1 = eio == e1
    hit2 = eio == e2
    oh = jnp.where(hit1 | hit2, 1.0, 0.0)
    ri = lax.broadcasted_iota(I32, (tr, tr), 0)
    cj = lax.broadcasted_iota(I32, (tr, tr), 1)
    before = jnp.where(ri < cj, 1.0, 0.0).astype(BF16)
    rank = jnp.dot(oh.astype(BF16), before, preferred_element_type=F32) + carry[:, 0:1]
    rank1 = jnp.sum(jnp.where(hit1, rank, 0.0), axis=0, keepdims=True)
    rank2 = jnp.sum(jnp.where(hit2, rank, 0.0), axis=0, keepdims=True)
    idx_ref[0:1, :] = e1
    idx_ref[1:2, :] = e2
    idx_ref[2:3, :] = rank1.astype(I32)
    idx_ref[3:4, :] = rank2.astype(I32)
    idx_ref[4:8, :] = jnp.zeros((4, tr), I32)
    wts_ref[0:1, :] = w1 / wsum
    wts_ref[1:2, :] = w2 / wsum
    wts_ref[2:8, :] = jnp.zeros((6, tr), F32)
    carry[...] = carry[...] + jnp.sum(oh, axis=1, keepdims=True)
    cnt_ref[...] = carry[...]


def _two_set_specs(ha, hb, tile):
    d = ha.shape[1]
    na = ha.shape[0] // tile
    nb = hb.shape[0] // tile
    return na, nb, [pl.BlockSpec((tile, d), lambda i, *_: (jnp.minimum(i, na - 1), 0)),
                    pl.BlockSpec((tile, d), lambda i, *_: (jnp.maximum(i - na, 0), 0))]


def _router(ha, hb, w_router, b_router, tr):
    d = ha.shape[1]
    t = ha.shape[0] + hb.shape[0]
    na, nb, hspecs = _two_set_specs(ha, hb, tr)
    return pl.pallas_call(
        functools.partial(_router_kernel, na),
        grid=(na + nb,),
        in_specs=hspecs + [pl.BlockSpec((N_EXPERTS, d), lambda i: (0, 0)),
                           pl.BlockSpec((N_EXPERTS, 1), lambda i: (0, 0))],
        out_specs=[pl.BlockSpec((SUBLANES, tr), lambda i: (0, i)),
                   pl.BlockSpec((SUBLANES, tr), lambda i: (0, i)),
                   pl.BlockSpec((N_EXPERTS, LANES), lambda i: (0, 0))],
        out_shape=[jax.ShapeDtypeStruct((SUBLANES, t), I32),
                   jax.ShapeDtypeStruct((SUBLANES, t), F32),
                   jax.ShapeDtypeStruct((N_EXPERTS, LANES), F32)],
        scratch_shapes=[pltpu.VMEM((N_EXPERTS, LANES), F32)],
        compiler_params=_cparams(1, 32),
        name="moe_router",
    )(ha, hb, w_router.T, b_router.reshape(N_EXPERTS, 1))


def _pair_pos(idx_ref, off_ref, n_tok, t, k):
    return off_ref[idx_ref[k * n_tok + t]] + idx_ref[(2 + k) * n_tok + t]


def _scatter_kernel(n_tok, n_first, idx_ref, cnt_ref, off_ref, ha_ref, hb_ref, o_hbm, zero_scr, sem):
    i = pl.program_id(0)
    ts = ha_ref.shape[0]
    base = i * ts

    def row_copy(src, r, p):
        return pltpu.make_async_copy(src.at[pl.ds(r, 1)], o_hbm.at[pl.ds(p, 1)], sem)

    def scatter_tile(h_ref):
        def start_rows(r, c):
            for k in range(2):
                row_copy(h_ref, r, _pair_pos(idx_ref, off_ref, n_tok, base + r, k)).start(priority=k)
            return c

        lax.fori_loop(0, ts, start_rows, 0, unroll=8)
        for _ in range(2):
            pltpu.make_async_copy(h_ref, o_hbm.at[pl.ds(0, ts)], sem).wait()

    @pl.when(i < n_first)
    def _():
        scatter_tile(ha_ref)

    @pl.when(i >= n_first)
    def _():
        scatter_tile(hb_ref)

    @pl.when(i == 0)
    def _():
        zero_scr[...] = jnp.zeros(zero_scr.shape, zero_scr.dtype)
        for e in range(N_EXPERTS + 1):
            lo = off_ref[e] + cnt_ref[e]
            hi = off_ref[e + 1]

            def start_zero(p, c):
                row_copy(zero_scr, 0, p).start()
                return c

            def wait_zero(p, c):
                row_copy(zero_scr, 0, 0).wait()
                return c

            lax.fori_loop(lo, hi, start_zero, 0)
            lax.fori_loop(lo, hi, wait_zero, 0)


def _scatter_rows(ha, hb, idx, cnt, off, n_rows, ts):
    d = ha.shape[1]
    t = ha.shape[0] + hb.shape[0]
    na, nb, hspecs = _two_set_specs(ha, hb, ts)
    return pl.pallas_call(
        functools.partial(_scatter_kernel, t, na),
        grid_spec=pltpu.PrefetchScalarGridSpec(
            num_scalar_prefetch=3,
            grid=(na + nb,),
            in_specs=hspecs,
            out_specs=pl.BlockSpec(memory_space=pl.ANY),
            scratch_shapes=[pltpu.VMEM((SUBLANES, d), ha.dtype), pltpu.SemaphoreType.DMA]),
        out_shape=jax.ShapeDtypeStruct((n_rows, d), ha.dtype),
        compiler_params=_cparams(1, 32),
        name="moe_scatter",
    )(idx, cnt, off, ha, hb)


def _cast_rows(src, dst):
    def body(c, carry):
        r = pl.multiple_of(c * CAST_ROWS, CAST_ROWS)
        dst[pl.ds(r, CAST_ROWS), :] = src[pl.ds(r, CAST_ROWS), :].astype(BF16)
        return carry

    lax.fori_loop(0, src.shape[0] // CAST_ROWS, body, 0)


def _expert_kernel(layer, te_ref, nu_ref, first_ref, nxt_ref, x_ref, w1_hbm, w2_hbm, y_ref,
                   stg1, stg2, wb1, wb2, sem):
    i = pl.program_id(0)

    def fetch(e):
        return (pltpu.make_async_copy(w1_hbm.at[layer, e], stg1, sem.at[0]),
                pltpu.make_async_copy(w2_hbm.at[layer, e], stg2, sem.at[1]))

    @pl.when(i == 0)
    def _():
        for cp in fetch(te_ref[0]):
            cp.start()

    @pl.when(first_ref[i] == 1)
    def _():
        for cp in fetch(0):
            cp.wait()
        _cast_rows(stg1, wb1)
        _cast_rows(stg2, wb2)

        @pl.when(nxt_ref[i] >= 0)
        def _():
            for cp in fetch(nxt_ref[i]):
                cp.start()

    @pl.when(i < nu_ref[0])
    def _():
        hid = jnp.dot(x_ref[...].astype(BF16), wb1[...], preferred_element_type=F32)
        f = hid.shape[1] // 2
        a = hid[:, :f]
        g = hid[:, f:]
        z = (g * jax.nn.sigmoid(g) * a).astype(BF16)
        y_ref[...] = jnp.dot(z, wb2[...], preferred_element_type=F32)

    @pl.when(i >= nu_ref[0])
    def _():
        y_ref[...] = jnp.zeros(y_ref.shape, y_ref.dtype)


def _experts(xs, w1, w2, layer, tile_expert, n_used, first, nxt):
    p, d = xs.shape
    tm = MOE_TILE
    ff2 = w1.shape[3]
    ff = ff2 // 2
    row = lambda i, te, nu, *_: (jnp.minimum(i, nu[0] - 1), 0)
    return pl.pallas_call(
        functools.partial(_expert_kernel, layer),
        grid_spec=pltpu.PrefetchScalarGridSpec(
            num_scalar_prefetch=4,
            grid=(p // tm,),
            in_specs=[pl.BlockSpec((tm, d), row),
                      pl.BlockSpec(memory_space=pl.ANY),
                      pl.BlockSpec(memory_space=pl.ANY)],
            out_specs=pl.BlockSpec((tm, d), lambda i, *_: (i, 0)),
            scratch_shapes=[pltpu.VMEM((d, ff2), F32), pltpu.VMEM((ff, d), F32),
                            pltpu.VMEM((d, ff2), BF16), pltpu.VMEM((ff, d), BF16),
                            pltpu.SemaphoreType.DMA((2,))]),
        out_shape=jax.ShapeDtypeStruct((p, d), F32),
        compiler_params=_cparams(1, 56),
        name="moe_experts",
    )(tile_expert, n_used, first, nxt, xs, w1, w2)


def _combine_kernel(n_tok, tok_base, has_next, idx_ref, off_ref, w_ref, y_hbm, xres_ref, gate_ref, lng_ref,
                    lnb_ref, *rest):
    if has_next:
        shn_ref, scn_ref, xo_ref, ho_ref, gbuf, fbuf, sem = rest
        nxt = (shn_ref, scn_ref, ho_ref)
    else:
        xo_ref, gbuf, fbuf, sem = rest
        nxt = None
    tc = xres_ref.shape[0]
    i = pl.program_id(0)
    n_steps = pl.num_programs(0)

    def gather_tile(step, slot):
        base = tok_base + step * tc

        def start_rows(r, c):
            for k in range(2):
                pltpu.make_async_copy(y_hbm.at[pl.ds(_pair_pos(idx_ref, off_ref, n_tok, base + r, k), 1)],
                                      gbuf.at[slot, k, pl.ds(r, 1)], sem.at[slot]).start(priority=k)
            return c

        lax.fori_loop(0, tc, start_rows, 0, unroll=8)

    slot = i % 2

    @pl.when(i == 0)
    def _():
        gather_tile(0, 0)

    @pl.when(i + 1 < n_steps)
    def _():
        gather_tile(i + 1, 1 - slot)

    for k in range(2):
        pltpu.make_async_copy(y_hbm.at[pl.ds(0, tc)], gbuf.at[slot, k], sem.at[slot]).wait()
    fbuf[0] = w_ref[:, 0:1] * gbuf[slot, 0] + w_ref[:, 1:2] * gbuf[slot, 1]
    _residual_ln(fbuf, xres_ref, gate_ref, lng_ref, lnb_ref, xo_ref, nxt)


def _combine_ln(ys, idx, off, wts, n_tok, tok_base, xres, gate, lng, lnb, nxt_mods, mod_idx, tc):
    t, d = xres.shape
    r = gate.shape[1]
    has_next = nxt_mods is not None
    mspec = pl.BlockSpec((1, r, d), lambda i, *_: (mod_idx(i), 0, 0))
    vspec = pl.BlockSpec((1, d), lambda i, *_: (0, 0))
    ospec = pl.BlockSpec((tc, d), lambda i, *_: (i, 0))
    wbase = tok_base // tc
    in_specs = [pl.BlockSpec((tc, 2), lambda i, *_: (wbase + i, 0)),
                pl.BlockSpec(memory_space=pl.ANY), ospec, mspec, vspec, vspec]
    args = [wts, ys, xres, gate, lng, lnb]
    out_specs = [ospec]
    out_shape = [jax.ShapeDtypeStruct((t, d), F32)]
    if has_next:
        in_specs += [mspec, mspec]
        args += list(nxt_mods)
        out_specs.append(ospec)
        out_shape.append(jax.ShapeDtypeStruct((t, d), BF16))
    return pl.pallas_call(
        functools.partial(_combine_kernel, n_tok, tok_base, has_next),
        grid_spec=pltpu.PrefetchScalarGridSpec(
            num_scalar_prefetch=2,
            grid=(t // tc,),
            in_specs=in_specs,
            out_specs=out_specs,
            scratch_shapes=[pltpu.VMEM((2, 2, tc, d), F32), pltpu.VMEM((1, tc, d), F32),
                            pltpu.SemaphoreType.DMA((2,))]),
        out_shape=out_shape,
        compiler_params=_cparams(1, 40),
        name="moe_combine",
    )(idx, off, *args)


def _moe_plan(idx, cnt_pad, n_tok):
    tm = MOE_TILE
    n_tiles = (2 * n_tok + N_EXPERTS * (tm - 1) + tm - 1) // tm
    cnt = cnt_pad[:, 0].astype(I32)
    padded = ((cnt + tm - 1) // tm) * tm
    ends = jnp.cumsum(padded)
    off = ends - padded
    n_used = jnp.maximum(ends[-1] // tm, 1)
    tile_start = jnp.minimum(jnp.arange(n_tiles, dtype=I32), n_used - 1) * tm
    tile_expert = jnp.minimum(jnp.sum(tile_start[:, None] >= ends[None, :], axis=1), N_EXPERTS - 1).astype(I32)
    tiles = jnp.arange(n_tiles, dtype=I32)
    prev = jnp.concatenate([jnp.full((1,), -1, I32), tile_expert[:-1]])
    first = ((tiles < n_used) & (tile_expert != prev)).astype(I32)
    eid = jnp.arange(N_EXPERTS, dtype=I32)
    later = jnp.where((cnt[None, :] > 0) & (eid[None, :] > eid[:, None]), eid[None, :], N_EXPERTS)
    nxt_e = jnp.min(later, axis=1)
    nxt_e = jnp.where(nxt_e == N_EXPERTS, -1, nxt_e).astype(I32)
    nxt = jnp.take(nxt_e, tile_expert)
    n_rows = jnp.full((1,), n_tiles * tm, I32)
    off_ext = jnp.concatenate([off, ends[-1:], n_rows]).astype(I32)
    cnt = jnp.concatenate([cnt, jnp.zeros((1,), I32)])
    return (idx[:4].reshape(-1), cnt, off_ext, tile_expert, n_used.reshape(1).astype(I32), first, nxt,
            n_tiles * tm)


class _Rows:
    def __init__(self, n_seq, seq_len, tile):
        self.n_seq, self.seq_len, self.tile = n_seq, seq_len, tile
        self.per_row = seq_len < tile

    def mods(self, m):
        if self.per_row:
            return jnp.repeat(m, self.seq_len, axis=0).reshape(-1, self.tile, m.shape[1])
        return m[:, None, :]

    def mod_idx(self, i):
        return i if self.per_row else (i * self.tile) // self.seq_len


def _split_mods(mod_l, n_prompt, n_seq):
    return ([mod_l[k, :n_prompt] for k in range(mod_l.shape[0])],
            [mod_l[k, n_prompt:n_seq] for k in range(mod_l.shape[0])])


def kernel(x_prompt, x_sample, state_mlstm_C, state_mlstm_n, state_mlstm_m, c_prompt, c_sample, w_ada, b_ada, ln_g, ln_b, a_w_in, a_b_gates, a_norm_w, a_w_out, b_w_in, b_b_in, b_norm_g, b_norm_b, b_w_s, b_b_s, b_w_out, w_router, b_router, w_expert_in, w_expert_out):
    bp, sp, d = x_prompt.shape
    bs, ss, _ = x_sample.shape
    tp = bp * sp
    ts = bs * ss
    n_tok = tp + ts
    qk = N_HEADS * DK
    vd = N_HEADS * DV
    n_main = 2 * qk + 2 * vd

    n_seq = bp + bs
    pad = (-n_seq) % SUBLANES
    c_all = jnp.concatenate([c_prompt, c_sample, jnp.zeros((pad, d), F32)])
    mod = _adaln(c_all, w_ada, b_ada)

    sets = {"p": _Rows(bp, sp, 512), "s": _Rows(bs, ss, 256)}
    base = {"p": 0, "s": tp}
    x = {"p": x_prompt.reshape(tp, d), "s": x_sample.reshape(ts, d)}
    hm = {}
    outs = {}
    for layer in range(DEPTH):
        j = layer // 2
        mp, ms = _split_mods(mod[layer], bp, n_seq)
        md = {"p": mp, "s": ms}
        lng = ln_g[layer]
        lnb = ln_b[layer]
        lhs = {}
        if layer % 2 == 0:
            w_in = a_w_in[j]
            gb = jnp.zeros((1, 2 * LANES), F32)
            gb = gb.at[0, :N_HEADS].set(a_b_gates[j, :N_HEADS])
            gb = gb.at[0, LANES:LANES + N_HEADS].set(a_b_gates[j, N_HEADS:])
            nw = a_norm_w[j].reshape(1, vd)
            proj = {}
            gates = {}
            for k, rs in sets.items():
                proj[k], gates[k] = _inproj(x[k], rs.mods(md[k][0]), rs.mods(md[k][1]), rs.mod_idx,
                                            w_in, n_main, rs.tile, BF16 if k == "p" else F32)
            lhs["p"], c_p, n_p, m_p = _mlstm_chunks(
                proj["p"], gates["p"], gb, nw,
                jnp.zeros((bp, N_HEADS, DK, DV), F32), jnp.zeros((bp, N_HEADS, LANES), F32),
                jnp.zeros((bp, N_HEADS, LANES), F32), bp, sp)
            rpad = SUBLANES - ss
            proj_s = jnp.pad(proj["s"].reshape(bs, ss, n_main), ((0, 0), (0, rpad), (0, 0)))
            gates_s = jnp.pad(gates["s"].reshape(bs, ss, 2 * LANES), ((0, 0), (0, rpad), (0, 0)))
            m0 = jnp.pad(state_mlstm_m[j], ((0, 0), (0, LANES - N_HEADS)))[:, None, :]
            ypre_s, c_s, n_s, m_s = _mlstm_short(proj_s, gates_s, gb, nw, state_mlstm_C[j],
                                                 state_mlstm_n[j].reshape(bs, 1, qk), m0, ss)
            n_s = n_s.reshape(bs, N_HEADS, DK)
            lhs["s"] = ypre_s[:, :ss].reshape(ts, vd)
            outs["C_p"], outs["n_p"], outs["m_p"] = c_p, n_p, m_p[:, :, 0]
            outs["C_s"], outs["n_s"], outs["m_s"] = c_s, n_s, m_s[:, 0, :N_HEADS]
            w_out = a_w_out[j]
        else:
            tmix = 256
            ws = b_w_s[j]
            bsv = b_b_s[j]

            def mixing(l):
                tri = jnp.tril(jnp.ones((l, l), bool))
                wsl = jnp.where(tri, ws[:, :l, :l], 0.0)
                eye = jnp.eye(tmix // l, dtype=F32)
                mats = jax.vmap(lambda m: jnp.kron(eye, m))(wsl).astype(BF16)
                bias = jnp.tile(bsv[:, :l], (1, tmix // l))[:, :, None]
                return mats, bias

            for k, rs in sets.items():
                uv = _gmlp_in(hm[k], b_w_in[j], b_b_in[j], min(1024, rs.n_seq * rs.seq_len),
                              BF16 if k == "p" else F32)
                mats, bias = mixing(min(CHUNK, rs.seq_len))
                res = _gmlp_mix(uv, b_norm_g[j], b_norm_b[j], mats, bias, tmix, k == "s")
                lhs[k] = res[0]
                if k == "s":
                    outs["v_s"] = res[1]
            w_out = b_w_out[j]
        x1 = {}
        hf = {}
        for k, rs in sets.items():
            rs = _Rows(rs.n_seq, rs.seq_len, min(rs.tile, 512 * 2048 // w_out.shape[0]))
            x1[k], hf[k] = _proj_ln(lhs[k], w_out, x[k], rs.mods(md[k][2]), lng[0:1], lnb[0:1],
                                    rs.mods(md[k][3]), rs.mods(md[k][4]), rs.mod_idx, rs.tile, 512)

        idx, wts, cnt = _router(hf["p"], hf["s"], w_router, b_router, 512)
        idx4, cnt_i, off_ext, tile_expert, n_used, first, nxt, n_rows = _moe_plan(idx, cnt, n_tok)
        xsorted = _scatter_rows(hf["p"], hf["s"], idx4, cnt_i, off_ext, n_rows, 256)
        ysorted = _experts(xsorted, w_expert_in, w_expert_out, layer, tile_expert, n_used, first, nxt)
        wts2 = wts[:2].T
        nxt = _split_mods(mod[layer + 1], bp, n_seq) if layer + 1 < DEPTH else None
        for ki, (k, rs) in enumerate(sets.items()):
            rc = _Rows(rs.n_seq, rs.seq_len, 256)
            nxt_mods = None if nxt is None else (rc.mods(nxt[ki][0]), rc.mods(nxt[ki][1]))
            res = _combine_ln(ysorted, idx4, off_ext, wts2, n_tok, base[k], x1[k], rc.mods(md[k][5]),
                              lng[1:2], lnb[1:2], nxt_mods, rc.mod_idx, 256)
            x[k] = res[0]
            if nxt is not None:
                hm[k] = res[1]

    return (x["p"].reshape(bp, sp, d), x["s"].reshape(bs, ss, d),
            outs["C_p"][None], outs["n_p"][None], outs["m_p"][None],
            outs["C_s"][None], outs["n_s"][None], outs["m_s"][None],
            outs["v_s"].reshape(bs, ss, -1)[None])
```

```python
import functools

import jax
import jax.numpy as jnp
from jax import lax
from jax.experimental import pallas as pl
from jax.experimental.pallas import tpu as pltpu

F32 = jnp.float32
BF16 = jnp.bfloat16
I32 = jnp.int32

DEPTH = 2
N_HEADS = 8
DK = 128
DV = 256
CHUNK = 128
N_GROUPS_B = 8
N_EXPERTS = 16
N_EXPERT_GROUPS = 4
EXPERTS_PER_GROUP = 4
ALPHA = float((2 * DEPTH) ** 0.25)
LN_EPS = 1e-5

LANES = 128
SUBLANES = 8
MIB = 1024 * 1024
MOE_TILE = 256
CAST_ROWS = 256
STAGE_ROWS = 128
META_TILE_EXPERT, META_FIRST, META_NEXT, META_N_USED, META_COUNT, META_OFFSET = range(6)


def _cparams(n_axes, vmem_mib):
    return pltpu.CompilerParams(
        dimension_semantics=("arbitrary",) * n_axes,
        vmem_limit_bytes=int(vmem_mib * MIB))


def _split3(x):
    hi = x.astype(BF16)
    r1 = x - hi.astype(F32)
    mid = r1.astype(BF16)
    lo = (r1 - mid.astype(F32)).astype(BF16)
    return hi, mid, lo


def _log_sigmoid(x):
    return jnp.minimum(x, 0.0) - jnp.log1p(jnp.exp(-jnp.abs(x)))


def _load_weight_bf16(w_hbm, wb, stg, sem):
    nj, k, tn = wb.shape
    ch = stg.shape[1]
    nch = k // ch

    def chunk(c):
        return pltpu.make_async_copy(w_hbm.at[pl.ds(c * ch, ch), pl.ds(0, nj * tn)], stg.at[c % 2],
                                     sem.at[c % 2])

    chunk(0).start()
    for c in range(nch):
        if c + 1 < nch:
            chunk(c + 1).start()
        chunk(c).wait()
        for jj in range(nj):
            wb[jj, c * ch:(c + 1) * ch, :] = stg[c % 2, :, jj * tn:(jj + 1) * tn].astype(BF16)


def _adaln_kernel(c_ref, w_ref, b_ref, o_ref):
    c = c_ref[...]
    a = (c * jax.nn.sigmoid(c)).astype(BF16)
    o_ref[0] = jnp.dot(a, w_ref[0].astype(BF16), preferred_element_type=F32) + b_ref[0]


def _adaln(c_all, w_ada, b_ada):
    depth, d, n = w_ada.shape
    r = c_all.shape[0]
    tn = 1024
    return pl.pallas_call(
        _adaln_kernel,
        grid=(depth, n // tn),
        in_specs=[pl.BlockSpec((r, d), lambda l, j: (0, 0)),
                  pl.BlockSpec((1, d, tn), lambda l, j: (l, 0, j)),
                  pl.BlockSpec((1, 1, tn), lambda l, j: (l, 0, j))],
        out_specs=pl.BlockSpec((1, r, tn), lambda l, j: (l, 0, j)),
        out_shape=jax.ShapeDtypeStruct((depth, r, n), F32),
        compiler_params=_cparams(2, 32),
        name="adaln",
    )(c_all, w_ada, b_ada.reshape(depth, 1, n))


def _inproj_kernel(x_ref, sh_ref, sc_ref, w_hbm, proj_ref, gates_ref, hin_ref, wb, stg, g16, wgs, sem):
    j = pl.program_id(1)

    @pl.when((pl.program_id(0) == 0) & (j == 0))
    def _():
        _load_weight_bf16(w_hbm, wb, stg, sem)
        n_main = wb.shape[0] * wb.shape[2]
        cp = pltpu.make_async_copy(w_hbm.at[:, pl.ds(n_main, 2 * N_HEADS)], g16, sem.at[0])
        cp.start()
        cp.wait()
        g = g16[...]
        z = jnp.zeros((g.shape[0], LANES - N_HEADS), F32)
        wg = jnp.concatenate([g[:, :N_HEADS], z, g[:, N_HEADS:], z], axis=1)
        hi = wg.astype(BF16)
        wgs[0] = hi
        wgs[1] = (wg - hi.astype(F32)).astype(BF16)

    @pl.when(j == 0)
    def _():
        h = x_ref[...] * (1.0 + sc_ref[0]) + sh_ref[0]
        hb = h.astype(BF16)
        hin_ref[...] = hb
        h_lo = (h - hb.astype(F32)).astype(BF16)
        gates_ref[...] = (jnp.dot(hb, wgs[0], preferred_element_type=F32)
                          + jnp.dot(h_lo, wgs[0], preferred_element_type=F32)
                          + jnp.dot(hb, wgs[1], preferred_element_type=F32))

    proj_ref[...] = jnp.dot(hin_ref[...], wb[j], preferred_element_type=F32).astype(proj_ref.dtype)


def _inproj(x, sh, sc, mod_idx, w_in, n_main, tm, out_dtype):
    t, d = x.shape
    tn = 1024
    r = sh.shape[1]
    mspec = pl.BlockSpec((1, r, d), lambda i, j: (mod_idx(i), 0, 0))
    return pl.pallas_call(
        _inproj_kernel,
        grid=(t // tm, n_main // tn),
        in_specs=[pl.BlockSpec((tm, d), lambda i, j: (i, 0)), mspec, mspec,
                  pl.BlockSpec(memory_space=pl.ANY)],
        out_specs=[pl.BlockSpec((tm, tn), lambda i, j: (i, j)),
                   pl.BlockSpec((tm, 2 * LANES), lambda i, j: (i, 0))],
        out_shape=[jax.ShapeDtypeStruct((t, n_main), out_dtype),
                   jax.ShapeDtypeStruct((t, 2 * LANES), F32)],
        scratch_shapes=[pltpu.VMEM((tm, d), BF16), pltpu.VMEM((n_main // tn, d, tn), BF16),
                        pltpu.VMEM((2, STAGE_ROWS, n_main), F32), pltpu.VMEM((d, 2 * N_HEADS), F32),
                        pltpu.VMEM((2, d, 2 * LANES), BF16), pltpu.SemaphoreType.DMA((2,))],
        compiler_params=_cparams(2, 56),
        name="mlstm_inproj",
    )(x, sh, sc, w_in)


def _head_norm_gate(hh, nw, o):
    mu = jnp.mean(hh, axis=1, keepdims=True)
    xc = hh - mu
    var = jnp.mean(xc * xc, axis=1, keepdims=True)
    return jax.nn.sigmoid(o) * (xc * lax.rsqrt(var + LN_EPS) * nw)


def _mlstm_chunk_kernel(q_ref, k_ref, v_ref, o_ref, g_ref, gb_ref, nw_ref, c0_ref, n0_ref, m0_ref,
                        y_ref, cout_ref, nout_ref, mout_ref, c_scr, n_scr, m_scr):
    ci = pl.program_id(1)
    L = q_ref.shape[0]

    @pl.when(ci == 0)
    def _():
        c_scr[...] = c0_ref[0]
        n_scr[...] = n0_ref[0]
        m_scr[...] = m0_ref[0]

    g = g_ref[...] + gb_ref[...]
    gi = g[:, :LANES]
    lf = _log_sigmoid(g[:, LANES:])
    row = lax.broadcasted_iota(I32, (L, L), 0)
    col = lax.broadcasted_iota(I32, (L, L), 1)
    causal = col <= row
    ltri = jnp.where(causal, 1.0, 0.0).astype(BF16)
    hi, mid, lo = _split3(lf)
    bcum = (jnp.dot(ltri, hi, preferred_element_type=F32)
            + jnp.dot(ltri, mid, preferred_element_type=F32)
            + jnp.dot(ltri, lo, preferred_element_type=F32))
    a = gi - bcum
    a_t = a.T
    scale = DK ** -0.5
    nt = (((1,), (1,)), ((), ()))
    tn_dims = (((0,), (0,)), ((), ()))
    for h in range(N_HEADS):
        ks = slice(h * DK, (h + 1) * DK)
        vs = slice(h * DV, (h + 1) * DV)
        qf = q_ref[:, ks].astype(F32) * scale
        qb = qf.astype(BF16)
        kf = k_ref[:, ks].astype(F32)
        kb = kf.astype(BF16)
        vb = v_ref[:, vs].astype(BF16)
        a_row = a_t[h:h + 1, :]
        a_col = a[:, h:h + 1]
        b_col = bcum[:, h:h + 1]
        m_prev = m_scr[h:h + 1, 0:1]
        amat = jnp.where(causal, a_row, -jnp.inf)
        mx = jnp.max(amat, axis=1, keepdims=True)
        m_inter = b_col + m_prev
        m_t = jnp.maximum(m_inter, b_col + mx)
        dm = jnp.exp(amat + (b_col - m_t))
        s = lax.dot_general(qb, kb, nt, preferred_element_type=F32)
        scores = s * dm
        inter = jnp.exp(m_inter - m_t)
        c_old = c_scr[h]
        n_old = n_scr[h:h + 1, :]
        qc = jnp.dot(qb, c_old.astype(BF16), preferred_element_type=F32)
        num = jnp.dot(scores.astype(BF16), vb, preferred_element_type=F32) + inter * qc
        qn = jnp.sum(qf * n_old, axis=1, keepdims=True)
        den = jnp.sum(scores, axis=1, keepdims=True) + inter * qn
        hh = num / jnp.maximum(jnp.abs(den), jnp.exp(-m_t))
        m_new = m_t[L - 1:L, :]
        b_last = b_col[L - 1:L, :]
        w_col = jnp.exp(b_last + a_col - m_new)
        decay = jnp.exp(b_last + m_prev - m_new)
        kw = kf * w_col
        c_scr[h] = decay * c_old + lax.dot_general(kw.astype(BF16), vb, tn_dims,
                                                   preferred_element_type=F32)
        n_scr[h:h + 1, :] = decay * n_old + jnp.sum(kw, axis=0, keepdims=True)
        m_scr[h:h + 1, :] = jnp.broadcast_to(m_new, (1, LANES))
        y_ref[:, vs] = _head_norm_gate(hh, nw_ref[:, vs], o_ref[:, vs].astype(F32)).astype(y_ref.dtype)

    @pl.when(ci == pl.num_programs(1) - 1)
    def _():
        cout_ref[0] = c_scr[...]
        nout_ref[0] = n_scr[...]
        mout_ref[0] = m_scr[...]


def _mlstm_chunks(proj, gates, gb, nw, c0, n0, m0, batch, seq):
    nc = seq // CHUNK
    qk = N_HEADS * DK
    vd = N_HEADS * DV
    row = lambda b, c: b * nc + c
    st4 = pl.BlockSpec((1, N_HEADS, DK, DV), lambda b, c: (b, 0, 0, 0))
    st3 = pl.BlockSpec((1, N_HEADS, LANES), lambda b, c: (b, 0, 0))
    return pl.pallas_call(
        _mlstm_chunk_kernel,
        grid=(batch, nc),
        in_specs=[pl.BlockSpec((CHUNK, qk), lambda b, c: (row(b, c), 0)),
                  pl.BlockSpec((CHUNK, qk), lambda b, c: (row(b, c), 1)),
                  pl.BlockSpec((CHUNK, vd), lambda b, c: (row(b, c), 1)),
                  pl.BlockSpec((CHUNK, vd), lambda b, c: (row(b, c), 2)),
                  pl.BlockSpec((CHUNK, 2 * LANES), lambda b, c: (row(b, c), 0)),
                  pl.BlockSpec((1, 2 * LANES), lambda b, c: (0, 0)),
                  pl.BlockSpec((1, vd), lambda b, c: (0, 0)),
                  st4, st3, st3],
        out_specs=[pl.BlockSpec((CHUNK, vd), lambda b, c: (row(b, c), 0)), st4, st3, st3],
        out_shape=[jax.ShapeDtypeStruct((batch * seq, vd), BF16),
                   jax.ShapeDtypeStruct((batch, N_HEADS, DK, DV), F32),
                   jax.ShapeDtypeStruct((batch, N_HEADS, LANES), F32),
                   jax.ShapeDtypeStruct((batch, N_HEADS, LANES), F32)],
        scratch_shapes=[pltpu.VMEM((N_HEADS, DK, DV), F32),
                        pltpu.VMEM((N_HEADS, LANES), F32),
                        pltpu.VMEM((N_HEADS, LANES), F32)],
        compiler_params=_cparams(2, 40),
        name="mlstm_chunks",
    )(proj, proj, proj, proj, gates, gb, nw, c0, n0, m0)


def _per_head(x, width):
    return jnp.concatenate(
        [jnp.broadcast_to(x[..., h:h + 1], x.shape[:-1] + (width,)) for h in range(N_HEADS)], axis=-1)


def _head_sums(x, width):
    lane = lax.broadcasted_iota(I32, x.shape[:-1] + (LANES,), x.ndim - 1)
    out = jnp.zeros(x.shape[:-1] + (LANES,), F32)
    for h in range(N_HEADS):
        s = jnp.sum(x[..., h * width:(h + 1) * width], axis=-1, keepdims=True)
        out = jnp.where(lane == h, s, out)
    return out


def _mlstm_short_kernel(seq, q_ref, k_ref, v_ref, o_ref, g_ref, gb_ref, nw_ref, c0_ref, n0_ref, m0_ref,
                        y_ref, cout_ref, nout_ref, mout_ref, qc_scr, kw_scr):
    bt, rows, _ = q_ref.shape
    scale = DK ** -0.5
    tn_dims = (((0,), (0,)), ((), ()))
    row = lax.broadcasted_iota(I32, (bt, rows, LANES), 1)
    g = g_ref[...] + gb_ref[...]
    gi = g[:, :, :LANES]
    lf = _log_sigmoid(g[:, :, LANES:])
    bcum = jnp.zeros_like(lf)
    for s in range(seq):
        bcum = bcum + jnp.where(row >= s, lf[:, s:s + 1, :], 0.0)
    a = gi - bcum
    mx = jnp.full_like(a, -jnp.inf)
    for s in range(seq):
        mx = jnp.maximum(mx, jnp.where(row >= s, a[:, s:s + 1, :], -jnp.inf))
    m_prev = m0_ref[...]
    m_inter = bcum + m_prev
    m_t = jnp.maximum(m_inter, bcum + mx)
    cmt = bcum - m_t
    inter = jnp.exp(m_inter - m_t)
    einv = jnp.exp(-m_t)
    m_new = m_t[:, seq - 1:seq, :]
    b_last = bcum[:, seq - 1:seq, :]
    w = jnp.where(row < seq, jnp.exp(b_last + a - m_new), 0.0)
    decay = jnp.exp(b_last + m_prev - m_new)
    mout_ref[...] = m_new

    q = q_ref[...] * scale
    k = k_ref[...]
    v = v_ref[...]
    n_old = n0_ref[...]
    for b in range(bt):
        for h in range(N_HEADS):
            qc_scr[b, :, h * DV:(h + 1) * DV] = jnp.dot(
                (q_ref[b, :, h * DK:(h + 1) * DK] * scale).astype(BF16), c0_ref[b, h].astype(BF16),
                preferred_element_type=F32)
    den = inter * _head_sums(q * n_old, DK)
    num = _per_head(inter, DV) * qc_scr[...]
    for s in range(seq):
        p = _head_sums(q * k[:, s:s + 1, :], DK) * jnp.where(row >= s, jnp.exp(cmt + a[:, s:s + 1, :]), 0.0)
        den = den + p
        num = num + _per_head(p, DV) * v[:, s:s + 1, :]
    hh = num * _per_head(1.0 / jnp.maximum(jnp.abs(den), einv), DV)
    mu = _head_sums(hh, DV) * (1.0 / DV)
    xc = hh - _per_head(mu, DV)
    var = _head_sums(xc * xc, DV) * (1.0 / DV)
    hn = xc * _per_head(lax.rsqrt(var + LN_EPS), DV) * nw_ref[...]
    y_ref[...] = jax.nn.sigmoid(o_ref[...]) * hn

    kw = k * _per_head(w, DK)
    kw_scr[...] = kw
    nout_ref[...] = _per_head(decay, DK) * n_old + jnp.sum(kw, axis=1, keepdims=True)
    for b in range(bt):
        for h in range(N_HEADS):
            cout_ref[b, h] = (decay[b, :, h:h + 1] * c0_ref[b, h]
                              + lax.dot_general(kw_scr[b, :, h * DK:(h + 1) * DK],
                                                v_ref[b, :, h * DV:(h + 1) * DV],
                                                tn_dims, preferred_element_type=F32))


def _mlstm_short(proj, gates, gb, nw, c0, n0, m0, seq):
    batch, rows, _ = proj.shape
    qk = N_HEADS * DK
    vd = N_HEADS * DV
    bt = 8
    st4 = pl.BlockSpec((bt, N_HEADS, DK, DV), lambda i: (i, 0, 0, 0))
    st3 = pl.BlockSpec((bt, 1, qk), lambda i: (i, 0, 0))
    stm = pl.BlockSpec((bt, 1, LANES), lambda i: (i, 0, 0))
    return pl.pallas_call(
        functools.partial(_mlstm_short_kernel, seq),
        grid=(batch // bt,),
        in_specs=[pl.BlockSpec((bt, rows, qk), lambda i: (i, 0, 0)),
                  pl.BlockSpec((bt, rows, qk), lambda i: (i, 0, 1)),
                  pl.BlockSpec((bt, rows, vd), lambda i: (i, 0, 1)),
                  pl.BlockSpec((bt, rows, vd), lambda i: (i, 0, 2)),
                  pl.BlockSpec((bt, rows, 2 * LANES), lambda i: (i, 0, 0)),
                  pl.BlockSpec((1, 2 * LANES), lambda i: (0, 0)),
                  pl.BlockSpec((1, vd), lambda i: (0, 0)),
                  st4, st3, stm],
        out_specs=[pl.BlockSpec((bt, rows, vd), lambda i: (i, 0, 0)), st4, st3, stm],
        out_shape=[jax.ShapeDtypeStruct((batch, rows, vd), F32),
                   jax.ShapeDtypeStruct((batch, N_HEADS, DK, DV), F32),
                   jax.ShapeDtypeStruct((batch, 1, qk), F32),
                   jax.ShapeDtypeStruct((batch, 1, LANES), F32)],
        scratch_shapes=[pltpu.VMEM((bt, rows, vd), F32), pltpu.VMEM((bt, rows, qk), F32)],
        compiler_params=_cparams(1, 48),
        name="mlstm_short",
    )(proj, proj, proj, proj, gates, gb, nw, c0, n0, m0)


def _residual_ln(zbuf, xres_ref, gate_ref, lng_ref, lnb_ref, xo_ref, nxt):
    nc, tm, tn = zbuf.shape
    inv_d = 1.0 / (nc * tn)
    ssum = jnp.zeros((tm, 1), F32)
    for c in range(nc):
        sl = pl.ds(c * tn, tn)
        z = ALPHA * xres_ref[:, sl] + gate_ref[0, :, sl] * zbuf[c]
        zbuf[c] = z
        ssum = ssum + jnp.sum(z, axis=1, keepdims=True)
    mu = ssum * inv_d
    vsum = jnp.zeros((tm, 1), F32)
    for c in range(nc):
        zc = zbuf[c] - mu
        vsum = vsum + jnp.sum(zc * zc, axis=1, keepdims=True)
    rstd = lax.rsqrt(vsum * inv_d + LN_EPS)
    for c in range(nc):
        sl = pl.ds(c * tn, tn)
        xn = (zbuf[c] - mu) * rstd * lng_ref[:, sl] + lnb_ref[:, sl]
        xo_ref[:, sl] = xn
        if nxt is not None:
            sh_ref, sc_ref, ho_ref = nxt
            ho_ref[:, sl] = (xn * (1.0 + sc_ref[0, :, sl]) + sh_ref[0, :, sl]).astype(ho_ref.dtype)


def _proj_ln_kernel(lhs_ref, w_hbm, xres_ref, gate_ref, lng_ref, lnb_ref, shn_ref, scn_ref,
                    xo_ref, ho_ref, wb, stg, ybuf, sem):
    @pl.when(pl.program_id(0) == 0)
    def _():
        _load_weight_bf16(w_hbm, wb, stg, sem)

    lhs = lhs_ref[...].astype(BF16)
    for c in range(ybuf.shape[0]):
        ybuf[c] = jnp.dot(lhs, wb[c], preferred_element_type=F32)
    _residual_ln(ybuf, xres_ref, gate_ref, lng_ref, lnb_ref, xo_ref, (shn_ref, scn_ref, ho_ref))


def _proj_ln(lhs, w, xres, gate, lng, lnb, shn, scn, mod_idx, tm, tn):
    t, k = lhs.shape
    d = w.shape[1]
    r = gate.shape[1]
    mspec = pl.BlockSpec((1, r, d), lambda i: (mod_idx(i), 0, 0))
    vspec = pl.BlockSpec((1, d), lambda i: (0, 0))
    ospec = pl.BlockSpec((tm, d), lambda i: (i, 0))
    return pl.pallas_call(
        _proj_ln_kernel,
        grid=(t // tm,),
        in_specs=[pl.BlockSpec((tm, k), lambda i: (i, 0)),
                  pl.BlockSpec(memory_space=pl.ANY),
                  ospec, mspec, vspec, vspec, mspec, mspec],
        out_specs=[ospec, ospec],
        out_shape=[jax.ShapeDtypeStruct((t, d), F32), jax.ShapeDtypeStruct((t, d), F32)],
        scratch_shapes=[pltpu.VMEM((d // tn, k, tn), BF16), pltpu.VMEM((2, CAST_ROWS, d), F32),
                        pltpu.VMEM((d // tn, tm, tn), F32), pltpu.SemaphoreType.DMA((2,))],
        compiler_params=_cparams(1, 58),
        name="proj_ln",
    )(lhs, w, xres, gate, lng, lnb, shn, scn)


def _gelu_tanh(x):
    return x * (0.5 * (1.0 + jnp.tanh(0.7978845608028654 * (x + 0.044715 * (x * x * x)))))


def _gmlp_in_kernel(h_ref, w_hbm, b_ref, o_ref, wb, stg, sem):
    j = pl.program_id(1)

    @pl.when((pl.program_id(0) == 0) & (j == 0))
    def _():
        _load_weight_bf16(w_hbm, wb, stg, sem)

    acc = jnp.dot(h_ref[...].astype(BF16), wb[j], preferred_element_type=F32)
    o_ref[...] = _gelu_tanh(acc + b_ref[...]).astype(o_ref.dtype)


def _gmlp_in(h, w, b, tm, out_dtype):
    t, d = h.shape
    n = w.shape[1]
    tn = 1024
    return pl.pallas_call(
        _gmlp_in_kernel,
        grid=(t // tm, n // tn),
        in_specs=[pl.BlockSpec((tm, d), lambda i, j: (i, 0)),
                  pl.BlockSpec(memory_space=pl.ANY),
                  pl.BlockSpec((1, tn), lambda i, j: (0, j))],
        out_specs=pl.BlockSpec((tm, tn), lambda i, j: (i, j)),
        out_shape=jax.ShapeDtypeStruct((t, n), out_dtype),
        scratch_shapes=[pltpu.VMEM((n // tn, d, tn), BF16), pltpu.VMEM((2, STAGE_ROWS, n), F32),
                        pltpu.SemaphoreType.DMA((2,))],
        compiler_params=_cparams(2, 58),
        name="gmlp_in",
    )(h, w, b.reshape(1, n))


def _gmlp_mix_kernel(u_ref, v_ref, g_ref, b_ref, mix_ref, bias_ref, o_ref, vn_ref=None):
    v = v_ref[...].astype(F32)
    mu = jnp.mean(v, axis=1, keepdims=True)
    xc = v - mu
    var = jnp.mean(xc * xc, axis=1, keepdims=True)
    vn = xc * lax.rsqrt(var + LN_EPS) * g_ref[...] + b_ref[...]
    if vn_ref is not None:
        vn_ref[...] = vn
    gd = v.shape[1] // N_GROUPS_B
    for g in range(N_GROUPS_B):
        sl = slice(g * gd, (g + 1) * gd)
        mixed = jnp.dot(mix_ref[g], vn[:, sl].astype(BF16), preferred_element_type=F32) + bias_ref[g]
        o_ref[:, sl] = (u_ref[:, sl].astype(F32) * mixed).astype(o_ref.dtype)


def _gmlp_mix(uv, nv_g, nv_b, mix, bias, tm, emit_v):
    t, n2 = uv.shape
    di = n2 // 2
    return pl.pallas_call(
        _gmlp_mix_kernel,
        grid=(t // tm,),
        in_specs=[pl.BlockSpec((tm, di), lambda i: (i, 0)),
                  pl.BlockSpec((tm, di), lambda i: (i, 1)),
                  pl.BlockSpec((1, di), lambda i: (0, 0)),
                  pl.BlockSpec((1, di), lambda i: (0, 0)),
                  pl.BlockSpec(mix.shape, lambda i: (0, 0, 0)),
                  pl.BlockSpec(bias.shape, lambda i: (0, 0, 0))],
        out_specs=[pl.BlockSpec((tm, di), lambda i: (i, 0))] * (2 if emit_v else 1),
        out_shape=[jax.ShapeDtypeStruct((t, di), BF16), jax.ShapeDtypeStruct((t, di), F32)][:2 if emit_v else 1],
        compiler_params=_cparams(1, 48),
        name="gmlp_mix",
    )(uv, uv, nv_g.reshape(1, di), nv_b.reshape(1, di), mix, bias)


def _router_kernel(n_first, ha_ref, hb_ref, wr_ref, br_ref, idx_ref, wts_ref, cnt_ref, carry):
    i = pl.program_id(0)
    tr = ha_ref.shape[0]

    @pl.when(i == 0)
    def _():
        carry[...] = jnp.zeros(carry.shape, carry.dtype)

    nt = (((1,), (1,)), ((), ()))
    h = jnp.where(i < n_first, ha_ref[...], hb_ref[...])
    hb = h.astype(BF16)
    hl = (h - hb.astype(F32)).astype(BF16)
    wr = wr_ref[...]
    wb = wr.astype(BF16)
    wl = (wr - wb.astype(F32)).astype(BF16)
    logits = (lax.dot_general(wb, hb, nt, preferred_element_type=F32)
              + lax.dot_general(wb, hl, nt, preferred_element_type=F32)
              + lax.dot_general(wl, hb, nt, preferred_element_type=F32))
    s = jax.nn.sigmoid(logits)
    sel = s + br_ref[...]
    epg = EXPERTS_PER_GROUP
    r = [sel[e:e + 1, :] for e in range(N_EXPERTS)]
    su = [s[e:e + 1, :] for e in range(N_EXPERTS)]

    def top2sum(v):
        best = v[0] + v[1]
        for x in range(epg):
            for y in range(x + 1, epg):
                if (x, y) != (0, 1):
                    best = jnp.maximum(best, v[x] + v[y])
        return best

    gs = [top2sum(r[g * epg:(g + 1) * epg]) for g in range(N_EXPERT_GROUPS)]
    gbest = gs[0]
    gidx = jnp.zeros((1, tr), I32)
    for g in range(1, N_EXPERT_GROUPS):
        better = gs[g] > gbest
        gidx = jnp.where(better, g, gidx)
        gbest = jnp.where(better, gs[g], gbest)
    v = list(r[:epg])
    sv = list(su[:epg])
    for g in range(1, N_EXPERT_GROUPS):
        pick = gidx == g
        for x in range(epg):
            v[x] = jnp.where(pick, r[g * epg + x], v[x])
            sv[x] = jnp.where(pick, su[g * epg + x], sv[x])
    i1 = jnp.zeros((1, tr), I32)
    b1 = v[0]
    w1 = sv[0]
    for x in range(1, epg):
        better = v[x] > b1
        i1 = jnp.where(better, x, i1)
        b1 = jnp.where(better, v[x], b1)
        w1 = jnp.where(better, sv[x], w1)
    i2 = jnp.zeros((1, tr), I32)
    b2 = jnp.full((1, tr), -jnp.inf, F32)
    w2 = jnp.zeros((1, tr), F32)
    for x in range(epg):
        take = (i1 != x) & (v[x] > b2)
        i2 = jnp.where(take, x, i2)
        b2 = jnp.where(take, v[x], b2)
        w2 = jnp.where(take, sv[x], w2)
    wsum = w1 + w2
    e1 = gidx * epg + i1
    e2 = gidx * epg + i2
    eio = lax.broadcasted_iota(I32, (N_EXPERTS, tr), 0)
    hit1 = eio == e1
    hit2 = eio == e2
    oh = jnp.where(hit1 | hit2, 1.0, 0.0)
    ri = lax.broadcasted_iota(I32, (tr, tr), 0)
    cj = lax.broadcasted_iota(I32, (tr, tr), 1)
    before = jnp.where(ri < cj, 1.0, 0.0).astype(BF16)
    rank = jnp.dot(oh.astype(BF16), before, preferred_element_type=F32) + carry[:, 0:1]
    rank1 = jnp.sum(jnp.where(hit1, rank, 0.0), axis=0, keepdims=True)
    rank2 = jnp.sum(jnp.where(hit2, rank, 0.0), axis=0, keepdims=True)
    idx_ref[0:1, :] = e1
    idx_ref[1:2, :] = e2
    idx_ref[2:3, :] = rank1.astype(I32)
    idx_ref[3:4, :] = rank2.astype(I32)
    idx_ref[4:8, :] = jnp.zeros((4, tr), I32)
    wts_ref[0:1, :] = w1 / wsum
    wts_ref[1:2, :] = w2 / wsum
    wts_ref[2:8, :] = jnp.zeros((6, tr), F32)
    carry[...] = carry[...] + jnp.sum(oh, axis=1, keepdims=True)
    cnt_ref[...] = carry[...]


def _two_set_specs(ha, hb, tile):
    d = ha.shape[1]
    na = ha.shape[0] // tile
    nb = hb.shape[0] // tile
    return na, nb, [pl.BlockSpec((tile, d), lambda i, *_: (jnp.minimum(i, na - 1), 0)),
                    pl.BlockSpec((tile, d), lambda i, *_: (jnp.maximum(i - na, 0), 0))]


def _router(ha, hb, w_router, b_router, tr):
    d = ha.shape[1]
    t = ha.shape[0] + hb.shape[0]
    na, nb, hspecs = _two_set_specs(ha, hb, tr)
    return pl.pallas_call(
        functools.partial(_router_kernel, na),
        grid=(na + nb,),
        in_specs=hspecs + [pl.BlockSpec((N_EXPERTS, d), lambda i: (0, 0)),
                           pl.BlockSpec((N_EXPERTS, 1), lambda i: (0, 0))],
        out_specs=[pl.BlockSpec((SUBLANES, tr), lambda i: (0, i)),
                   pl.BlockSpec((SUBLANES, tr), lambda i: (0, i)),
                   pl.BlockSpec((N_EXPERTS, LANES), lambda i: (0, 0))],
        out_shape=[jax.ShapeDtypeStruct((SUBLANES, t), I32),
                   jax.ShapeDtypeStruct((SUBLANES, t), F32),
                   jax.ShapeDtypeStruct((N_EXPERTS, LANES), F32)],
        scratch_shapes=[pltpu.VMEM((N_EXPERTS, LANES), F32)],
        compiler_params=_cparams(1, 32),
        name="moe_router",
    )(ha, hb, w_router.T, b_router.reshape(N_EXPERTS, 1))


def _scatter_kernel(n_first, pos_ref, meta_ref, ha_ref, hb_ref, o_hbm, zero_scr, sem):
    i = pl.program_id(0)
    ts = ha_ref.shape[0]
    base = i * ts

    def row_copy(src, r, p):
        return pltpu.make_async_copy(src.at[pl.ds(r, 1)], o_hbm.at[pl.ds(p, 1)], sem)

    def scatter_tile(h_ref):
        def start_rows(r, c):
            row_copy(h_ref, r, pos_ref[0, base + r]).start()
            row_copy(h_ref, r, pos_ref[1, base + r]).start()
            return c

        lax.fori_loop(0, ts, start_rows, 0, unroll=8)
        for _ in range(2):
            pltpu.make_async_copy(h_ref, o_hbm.at[pl.ds(0, ts)], sem).wait()

    @pl.when(i < n_first)
    def _():
        scatter_tile(ha_ref)

    @pl.when(i >= n_first)
    def _():
        scatter_tile(hb_ref)

    @pl.when(i == 0)
    def _():
        zero_scr[...] = jnp.zeros(zero_scr.shape, zero_scr.dtype)
        nz = zero_scr.shape[0]

        def block_copy(p):
            return pltpu.make_async_copy(zero_scr, o_hbm.at[pl.ds(pl.multiple_of(p, nz), nz)], sem)

        for e in range(N_EXPERTS + 1):
            lo = meta_ref[META_OFFSET, e] + meta_ref[META_COUNT, e]
            hi = meta_ref[META_OFFSET, e + 1]
            mid = jnp.minimum(((lo + (nz - 1)) // nz) * nz, hi)

            def start_row(p, c):
                row_copy(zero_scr, 0, p).start()
                return c

            def wait_row(p, c):
                row_copy(zero_scr, 0, 0).wait()
                return c

            def start_block(b, c):
                block_copy(mid + b * nz).start()
                return c

            def wait_block(b, c):
                block_copy(0).wait()
                return c

            lax.fori_loop(lo, mid, start_row, 0)
            lax.fori_loop(lo, mid, wait_row, 0)
            nblk = (hi - mid) // nz
            lax.fori_loop(0, nblk, start_block, 0)
            lax.fori_loop(0, nblk, wait_block, 0)


def _scatter_rows(ha, hb, pos, meta, n_rows, ts):
    d = ha.shape[1]
    na, nb, hspecs = _two_set_specs(ha, hb, ts)
    return pl.pallas_call(
        functools.partial(_scatter_kernel, na),
        grid_spec=pltpu.PrefetchScalarGridSpec(
            num_scalar_prefetch=2,
            grid=(na + nb,),
            in_specs=hspecs,
            out_specs=pl.BlockSpec(memory_space=pl.ANY),
            scratch_shapes=[pltpu.VMEM((SUBLANES, d), ha.dtype), pltpu.SemaphoreType.DMA]),
        out_shape=jax.ShapeDtypeStruct((n_rows, d), ha.dtype),
        compiler_params=_cparams(1, 32),
        name="moe_scatter",
    )(pos, meta, ha, hb)


def _cast_rows(src, dst):
    def body(c, carry):
        r = pl.multiple_of(c * CAST_ROWS, CAST_ROWS)
        dst[pl.ds(r, CAST_ROWS), :] = src[pl.ds(r, CAST_ROWS), :].astype(BF16)
        return carry

    lax.fori_loop(0, src.shape[0] // CAST_ROWS, body, 0)


def _expert_kernel(layer, meta_ref, x_ref, w1_hbm, w2_hbm, y_ref, stg1, stg2, wb1, wb2, sem):
    i = pl.program_id(0)

    def fetch(e):
        return (pltpu.make_async_copy(w1_hbm.at[layer, e], stg1, sem.at[0]),
                pltpu.make_async_copy(w2_hbm.at[layer, e], stg2, sem.at[1]))

    @pl.when(i == 0)
    def _():
        for cp in fetch(meta_ref[META_TILE_EXPERT, 0]):
            cp.start()

    @pl.when(meta_ref[META_FIRST, i] == 1)
    def _():
        for cp in fetch(0):
            cp.wait()
        _cast_rows(stg1, wb1)
        _cast_rows(stg2, wb2)

        @pl.when(meta_ref[META_NEXT, i] >= 0)
        def _():
            for cp in fetch(meta_ref[META_NEXT, i]):
                cp.start()

    @pl.when(i < meta_ref[META_N_USED, 0])
    def _():
        hid = jnp.dot(x_ref[...].astype(BF16), wb1[...], preferred_element_type=F32)
        f = hid.shape[1] // 2
        a = hid[:, :f]
        g = hid[:, f:]
        z = (g * jax.nn.sigmoid(g) * a).astype(BF16)
        y_ref[...] = jnp.dot(z, wb2[...], preferred_element_type=F32)

    @pl.when(i >= meta_ref[META_N_USED, 0])
    def _():
        y_ref[...] = jnp.zeros(y_ref.shape, y_ref.dtype)


def _experts(xs, w1, w2, layer, meta):
    p, d = xs.shape
    tm = MOE_TILE
    ff2 = w1.shape[3]
    ff = ff2 // 2
    row = lambda i, meta: (jnp.minimum(i, meta[META_N_USED, 0] - 1), 0)
    return pl.pallas_call(
        functools.partial(_expert_kernel, layer),
        grid_spec=pltpu.PrefetchScalarGridSpec(
            num_scalar_prefetch=1,
            grid=(p // tm,),
            in_specs=[pl.BlockSpec((tm, d), row),
                      pl.BlockSpec(memory_space=pl.ANY),
                      pl.BlockSpec(memory_space=pl.ANY)],
            out_specs=pl.BlockSpec((tm, d), lambda i, *_: (i, 0)),
            scratch_shapes=[pltpu.VMEM((d, ff2), F32), pltpu.VMEM((ff, d), F32),
                            pltpu.VMEM((d, ff2), BF16), pltpu.VMEM((ff, d), BF16),
                            pltpu.SemaphoreType.DMA((2,))]),
        out_shape=jax.ShapeDtypeStruct((p, d), F32),
        compiler_params=_cparams(1, 56),
        name="moe_experts",
    )(meta, xs, w1, w2)


def _combine_kernel(tok_base, has_next, pos_ref, w_ref, y_hbm, xres_ref, gate_ref, lng_ref, lnb_ref,
                    *rest):
    if has_next:
        shn_ref, scn_ref, xo_ref, ho_ref, gbuf, fbuf, sem = rest
        nxt = (shn_ref, scn_ref, ho_ref)
    else:
        xo_ref, gbuf, fbuf, sem = rest
        nxt = None
    tc = xres_ref.shape[0]
    i = pl.program_id(0)
    n_steps = pl.num_programs(0)

    def gather_tile(step, slot):
        base = tok_base + step * tc

        def start_rows(r, c):
            for k in range(2):
                pltpu.make_async_copy(y_hbm.at[pl.ds(pos_ref[k, base + r], 1)],
                                      gbuf.at[slot, k, pl.ds(r, 1)], sem.at[slot]).start()
            return c

        lax.fori_loop(0, tc, start_rows, 0, unroll=8)

    slot = i % 2

    @pl.when(i == 0)
    def _():
        gather_tile(0, 0)

    @pl.when(i + 1 < n_steps)
    def _():
        gather_tile(i + 1, 1 - slot)

    for k in range(2):
        pltpu.make_async_copy(y_hbm.at[pl.ds(0, tc)], gbuf.at[slot, k], sem.at[slot]).wait()
    fbuf[0] = w_ref[:, 0:1] * gbuf[slot, 0] + w_ref[:, 1:2] * gbuf[slot, 1]
    _residual_ln(fbuf, xres_ref, gate_ref, lng_ref, lnb_ref, xo_ref, nxt)


def _combine_ln(ys, pos, wts, tok_base, xres, gate, lng, lnb, nxt_mods, mod_idx, tc):
    t, d = xres.shape
    r = gate.shape[1]
    has_next = nxt_mods is not None
    mspec = pl.BlockSpec((1, r, d), lambda i, *_: (mod_idx(i), 0, 0))
    vspec = pl.BlockSpec((1, d), lambda i, *_: (0, 0))
    ospec = pl.BlockSpec((tc, d), lambda i, *_: (i, 0))
    wbase = tok_base // tc
    in_specs = [pl.BlockSpec((tc, 2), lambda i, *_: (wbase + i, 0)),
                pl.BlockSpec(memory_space=pl.ANY), ospec, mspec, vspec, vspec]
    args = [wts, ys, xres, gate, lng, lnb]
    out_specs = [ospec]
    out_shape = [jax.ShapeDtypeStruct((t, d), F32)]
    if has_next:
        in_specs += [mspec, mspec]
        args += list(nxt_mods)
        out_specs.append(ospec)
        out_shape.append(jax.ShapeDtypeStruct((t, d), BF16))
    return pl.pallas_call(
        functools.partial(_combine_kernel, tok_base, has_next),
        grid_spec=pltpu.PrefetchScalarGridSpec(
            num_scalar_prefetch=1,
            grid=(t // tc,),
            in_specs=in_specs,
            out_specs=out_specs,
            scratch_shapes=[pltpu.VMEM((2, 2, tc, d), F32), pltpu.VMEM((1, tc, d), F32),
                            pltpu.SemaphoreType.DMA((2,))]),
        out_shape=out_shape,
        compiler_params=_cparams(1, 40),
        name="moe_combine",
    )(pos, *args)


def _plan_kernel(n_tiles, idx_ref, cnt_ref, pos_ref, meta_ref):
    tm = MOE_TILE
    t = idx_ref.shape[1]
    sub = lax.broadcasted_iota(I32, (N_EXPERTS, LANES), 0)
    lane = lax.broadcasted_iota(I32, (N_EXPERTS, LANES), 1)
    cnt = cnt_ref[...]
    padded = jnp.floor((cnt + (tm - 1)) * (1.0 / tm)) * tm
    ends = padded
    sh = 1
    while sh < N_EXPERTS:
        ends = ends + jnp.where(sub >= sh, pltpu.roll(ends, sh, 0), 0.0)
        sh *= 2
    off = ends - padded
    total = ends[N_EXPERTS - 1:N_EXPERTS, :]
    n_used = jnp.maximum(total * (1.0 / tm), 1.0)

    def as_row(col):
        return jnp.sum(jnp.where(sub == lane, col, 0.0), axis=0, keepdims=True)

    eio = lax.broadcasted_iota(I32, (N_EXPERTS, t), 0)
    off_col = off[:, 0:1]
    for k in range(2):
        start = jnp.sum(jnp.where(eio == idx_ref[k:k + 1, :], off_col, 0.0), axis=0, keepdims=True)
        pos_ref[k:k + 1, :] = start.astype(I32) + idx_ref[2 + k:3 + k, :]
    pos_ref[2:SUBLANES, :] = jnp.zeros((SUBLANES - 2, t), I32)

    tile = lax.broadcasted_iota(I32, (1, LANES), 1).astype(F32)
    tile_start = jnp.minimum(tile, n_used - 1.0) * tm
    te = jnp.minimum(jnp.sum(jnp.where(ends <= tile_start, 1.0, 0.0), axis=0, keepdims=True), N_EXPERTS - 1.0)
    prev = jnp.where(tile == 0.0, -1.0, pltpu.roll(te, 1, 1))
    first = jnp.where((tile < n_used) & (te != prev), 1.0, 0.0)
    present = as_row(jnp.where(cnt > 0.0, 1.0, 0.0))
    later = jnp.where((present > 0.0) & (lane > sub) & (lane < N_EXPERTS), lane.astype(F32), float(N_EXPERTS))
    nxt_e = jnp.min(later, axis=1, keepdims=True)
    nxt_e = jnp.where(nxt_e == float(N_EXPERTS), -1.0, nxt_e)
    nxt = jnp.sum(jnp.where(sub.astype(F32) == te, nxt_e, 0.0), axis=0, keepdims=True)
    lane_row = lax.broadcasted_iota(I32, (1, LANES), 1)
    off_row = as_row(off)
    off_row = jnp.where(lane_row == N_EXPERTS, total, off_row)
    off_row = jnp.where(lane_row == N_EXPERTS + 1, float(n_tiles * tm), off_row)
    rows = {META_TILE_EXPERT: te, META_FIRST: first, META_NEXT: nxt, META_N_USED: n_used,
            META_COUNT: as_row(cnt), META_OFFSET: off_row}
    for r in range(SUBLANES):
        meta_ref[r:r + 1, :] = rows[r].astype(I32) if r in rows else jnp.zeros((1, LANES), I32)


def _moe_plan(idx, cnt, n_tok):
    tm = MOE_TILE
    n_tiles = (2 * n_tok + N_EXPERTS * (tm - 1) + tm - 1) // tm
    assert n_tiles <= LANES
    pos, meta = pl.pallas_call(
        functools.partial(_plan_kernel, n_tiles),
        out_shape=[jax.ShapeDtypeStruct((SUBLANES, n_tok), I32), jax.ShapeDtypeStruct((SUBLANES, LANES), I32)],
        compiler_params=_cparams(0, 32),
        name="moe_plan",
    )(idx, cnt)
    return pos, meta, n_tiles * tm


class _Rows:
    def __init__(self, n_seq, seq_len, tile):
        self.n_seq, self.seq_len, self.tile = n_seq, seq_len, tile
        self.per_row = seq_len < tile

    def mods(self, m):
        if self.per_row:
            return jnp.repeat(m, self.seq_len, axis=0).reshape(-1, self.tile, m.shape[1])
        return m[:, None, :]

    def mod_idx(self, i):
        return i if self.per_row else (i * self.tile) // self.seq_len


def _split_mods(mod_l, n_prompt):
    d = mod_l.shape[1] // 6
    cols = [mod_l[:, k * d:(k + 1) * d] for k in range(6)]
    return [c[:n_prompt] for c in cols], [c[n_prompt:] for c in cols]


def kernel(x_prompt, x_sample, state_mlstm_C, state_mlstm_n, state_mlstm_m, c_prompt, c_sample, w_ada, b_ada, ln_g, ln_b, a_w_in, a_b_gates, a_norm_w, a_w_out, b_w_in, b_b_in, b_norm_g, b_norm_b, b_w_s, b_b_s, b_w_out, w_router, b_router, w_expert_in, w_expert_out):
    bp, sp, d = x_prompt.shape
    bs, ss, _ = x_sample.shape
    tp = bp * sp
    ts = bs * ss
    n_tok = tp + ts
    qk = N_HEADS * DK
    vd = N_HEADS * DV
    n_main = 2 * qk + 2 * vd

    n_seq = bp + bs
    pad = (-n_seq) % SUBLANES
    c_all = jnp.concatenate([c_prompt, c_sample, jnp.zeros((pad, d), F32)])
    mod = _adaln(c_all, w_ada, b_ada)[:, :n_seq]

    sets = {"p": _Rows(bp, sp, 512), "s": _Rows(bs, ss, 256)}
    base = {"p": 0, "s": tp}
    x = {"p": x_prompt.reshape(tp, d), "s": x_sample.reshape(ts, d)}
    hm = {}
    outs = {}
    for layer in range(DEPTH):
        j = layer // 2
        mp, ms = _split_mods(mod[layer], bp)
        md = {"p": mp, "s": ms}
        lng = ln_g[layer]
        lnb = ln_b[layer]
        lhs = {}
        if layer % 2 == 0:
            w_in = a_w_in[j]
            gb = jnp.zeros((1, 2 * LANES), F32)
            gb = gb.at[0, :N_HEADS].set(a_b_gates[j, :N_HEADS])
            gb = gb.at[0, LANES:LANES + N_HEADS].set(a_b_gates[j, N_HEADS:])
            nw = a_norm_w[j].reshape(1, vd)
            proj = {}
            gates = {}
            for k, rs in sets.items():
                proj[k], gates[k] = _inproj(x[k], rs.mods(md[k][0]), rs.mods(md[k][1]), rs.mod_idx,
                                            w_in, n_main, rs.tile, BF16 if k == "p" else F32)
            lhs["p"], c_p, n_p, m_p = _mlstm_chunks(
                proj["p"], gates["p"], gb, nw,
                jnp.zeros((bp, N_HEADS, DK, DV), F32), jnp.zeros((bp, N_HEADS, LANES), F32),
                jnp.zeros((bp, N_HEADS, LANES), F32), bp, sp)
            rpad = SUBLANES - ss
            proj_s = jnp.pad(proj["s"].reshape(bs, ss, n_main), ((0, 0), (0, rpad), (0, 0)))
            gates_s = jnp.pad(gates["s"].reshape(bs, ss, 2 * LANES), ((0, 0), (0, rpad), (0, 0)))
            m0 = jnp.pad(state_mlstm_m[j], ((0, 0), (0, LANES - N_HEADS)))[:, None, :]
            ypre_s, c_s, n_s, m_s = _mlstm_short(proj_s, gates_s, gb, nw, state_mlstm_C[j],
                                                 state_mlstm_n[j].reshape(bs, 1, qk), m0, ss)
            n_s = n_s.reshape(bs, N_HEADS, DK)
            lhs["s"] = ypre_s[:, :ss].reshape(ts, vd)
            outs["C_p"], outs["n_p"], outs["m_p"] = c_p, n_p, m_p[:, :, 0]
            outs["C_s"], outs["n_s"], outs["m_s"] = c_s, n_s, m_s[:, 0, :N_HEADS]
            w_out = a_w_out[j]
        else:
            tmix = 256
            ws = b_w_s[j]
            bsv = b_b_s[j]

            def mixing(l):
                tri = jnp.tril(jnp.ones((l, l), bool))
                wsl = jnp.where(tri, ws[:, :l, :l], 0.0)
                eye = jnp.eye(tmix // l, dtype=F32)
                mats = jax.vmap(lambda m: jnp.kron(eye, m))(wsl).astype(BF16)
                bias = jnp.tile(bsv[:, :l], (1, tmix // l))[:, :, None]
                return mats, bias

            for k, rs in sets.items():
                uv = _gmlp_in(hm[k], b_w_in[j], b_b_in[j], min(1024, rs.n_seq * rs.seq_len),
                              BF16 if k == "p" else F32)
                mats, bias = mixing(min(CHUNK, rs.seq_len))
                res = _gmlp_mix(uv, b_norm_g[j], b_norm_b[j], mats, bias, tmix, k == "s")
                lhs[k] = res[0]
                if k == "s":
                    outs["v_s"] = res[1]
            w_out = b_w_out[j]
        x1 = {}
        hf = {}
        for k, rs in sets.items():
            rs = _Rows(rs.n_seq, rs.seq_len, min(rs.tile, 512 * 2048 // w_out.shape[0]))
            x1[k], hf[k] = _proj_ln(lhs[k], w_out, x[k], rs.mods(md[k][2]), lng[0:1], lnb[0:1],
                                    rs.mods(md[k][3]), rs.mods(md[k][4]), rs.mod_idx, rs.tile, 512)

        idx, wts, cnt = _router(hf["p"], hf["s"], w_router, b_router, 512)
        pos, meta, n_rows = _moe_plan(idx, cnt, n_tok)
        xsorted = _scatter_rows(hf["p"], hf["s"], pos, meta, n_rows, 256)
        ysorted = _experts(xsorted, w_expert_in, w_expert_out, layer, meta)
        wts2 = wts[:2].T
        nxt = _split_mods(mod[layer + 1], bp) if layer + 1 < DEPTH else None
        for ki, (k, rs) in enumerate(sets.items()):
            rc = _Rows(rs.n_seq, rs.seq_len, 256)
            nxt_mods = None if nxt is None else (rc.mods(nxt[ki][0]), rc.mods(nxt[ki][1]))
            res = _combine_ln(ysorted, pos, wts2, base[k], x1[k], rc.mods(md[k][5]),
                              lng[1:2], lnb[1:2], nxt_mods, rc.mod_idx, 256)
            x[k] = res[0]
            if nxt is not None:
                hm[k] = res[1]

    return (x["p"].reshape(bp, sp, d), x["s"].reshape(bs, ss, d),
            outs["C_p"][None], outs["n_p"][None], outs["m_p"][None],
            outs["C_s"][None], outs["n_s"][None], outs["m_s"][None],
            outs["v_s"].reshape(bs, ss, -1)[None])
```

```python
import functools

import jax
import jax.numpy as jnp
from jax import lax
from jax.experimental import pallas as pl
from jax.experimental.pallas import tpu as pltpu

F32 = jnp.float32
BF16 = jnp.bfloat16
I32 = jnp.int32

DEPTH = 2
N_HEADS = 8
DK = 128
DV = 256
CHUNK = 128
N_GROUPS_B = 8
N_EXPERTS = 16
N_EXPERT_GROUPS = 4
EXPERTS_PER_GROUP = 4
ALPHA = float((2 * DEPTH) ** 0.25)
LN_EPS = 1e-5

LANES = 128
SUBLANES = 8
MIB = 1024 * 1024
MOE_TILE = 256
CAST_ROWS = 256
STAGE_ROWS = 128
META_TILE_EXPERT, META_FIRST, META_NEXT, META_N_USED, META_COUNT, META_OFFSET = range(6)


def _cparams(n_axes, vmem_mib):
    return pltpu.CompilerParams(
        dimension_semantics=("arbitrary",) * n_axes,
        vmem_limit_bytes=int(vmem_mib * MIB))


def _split3(x):
    hi = x.astype(BF16)
    r1 = x - hi.astype(F32)
    mid = r1.astype(BF16)
    lo = (r1 - mid.astype(F32)).astype(BF16)
    return hi, mid, lo


def _log_sigmoid(x):
    return jnp.minimum(x, 0.0) - jnp.log1p(jnp.exp(-jnp.abs(x)))


def _load_weight_bf16(w_hbm, wb, stg, sem):
    nj, k, tn = wb.shape
    ch = stg.shape[1]
    nch = k // ch

    def chunk(c):
        return pltpu.make_async_copy(w_hbm.at[pl.ds(c * ch, ch), pl.ds(0, nj * tn)], stg.at[c % 2],
                                     sem.at[c % 2])

    chunk(0).start()
    for c in range(nch):
        if c + 1 < nch:
            chunk(c + 1).start()
        chunk(c).wait()
        for jj in range(nj):
            wb[jj, c * ch:(c + 1) * ch, :] = stg[c % 2, :, jj * tn:(jj + 1) * tn].astype(BF16)


def _adaln_kernel(c_ref, w_ref, b_ref, o_ref):
    c = c_ref[...]
    a = (c * jax.nn.sigmoid(c)).astype(BF16)
    o_ref[0] = jnp.dot(a, w_ref[0].astype(BF16), preferred_element_type=F32) + b_ref[0]


def _adaln(c_all, w_ada, b_ada):
    depth, d, n = w_ada.shape
    r = c_all.shape[0]
    tn = 1024
    return pl.pallas_call(
        _adaln_kernel,
        grid=(depth, n // tn),
        in_specs=[pl.BlockSpec((r, d), lambda l, j: (0, 0)),
                  pl.BlockSpec((1, d, tn), lambda l, j: (l, 0, j)),
                  pl.BlockSpec((1, 1, tn), lambda l, j: (l, 0, j))],
        out_specs=pl.BlockSpec((1, r, tn), lambda l, j: (l, 0, j)),
        out_shape=jax.ShapeDtypeStruct((depth, r, n), F32),
        compiler_params=_cparams(2, 32),
        name="adaln",
    )(c_all, w_ada, b_ada.reshape(depth, 1, n))


def _inproj_kernel(x_ref, sh_ref, sc_ref, w_hbm, proj_ref, gates_ref, hin_ref, wb, stg, g16, wgs, sem):
    j = pl.program_id(1)

    @pl.when((pl.program_id(0) == 0) & (j == 0))
    def _():
        _load_weight_bf16(w_hbm, wb, stg, sem)
        n_main = wb.shape[0] * wb.shape[2]
        cp = pltpu.make_async_copy(w_hbm.at[:, pl.ds(n_main, 2 * N_HEADS)], g16, sem.at[0])
        cp.start()
        cp.wait()
        g = g16[...]
        z = jnp.zeros((g.shape[0], LANES - N_HEADS), F32)
        wg = jnp.concatenate([g[:, :N_HEADS], z, g[:, N_HEADS:], z], axis=1)
        hi = wg.astype(BF16)
        wgs[0] = hi
        wgs[1] = (wg - hi.astype(F32)).astype(BF16)

    @pl.when(j == 0)
    def _():
        h = x_ref[...] * (1.0 + sc_ref[0]) + sh_ref[0]
        hb = h.astype(BF16)
        hin_ref[...] = hb
        h_lo = (h - hb.astype(F32)).astype(BF16)
        gates_ref[...] = (jnp.dot(hb, wgs[0], preferred_element_type=F32)
                          + jnp.dot(h_lo, wgs[0], preferred_element_type=F32)
                          + jnp.dot(hb, wgs[1], preferred_element_type=F32))

    proj_ref[...] = jnp.dot(hin_ref[...], wb[j], preferred_element_type=F32).astype(proj_ref.dtype)


def _inproj(x, sh, sc, mod_idx, w_in, n_main, tm, out_dtype):
    t, d = x.shape
    tn = 1024
    r = sh.shape[1]
    mspec = pl.BlockSpec((1, r, d), lambda i, j: (mod_idx(i), 0, 0))
    return pl.pallas_call(
        _inproj_kernel,
        grid=(t // tm, n_main // tn),
        in_specs=[pl.BlockSpec((tm, d), lambda i, j: (i, 0)), mspec, mspec,
                  pl.BlockSpec(memory_space=pl.ANY)],
        out_specs=[pl.BlockSpec((tm, tn), lambda i, j: (i, j)),
                   pl.BlockSpec((tm, 2 * LANES), lambda i, j: (i, 0))],
        out_shape=[jax.ShapeDtypeStruct((t, n_main), out_dtype),
                   jax.ShapeDtypeStruct((t, 2 * LANES), F32)],
        scratch_shapes=[pltpu.VMEM((tm, d), BF16), pltpu.VMEM((n_main // tn, d, tn), BF16),
                        pltpu.VMEM((2, STAGE_ROWS, n_main), F32), pltpu.VMEM((d, 2 * N_HEADS), F32),
                        pltpu.VMEM((2, d, 2 * LANES), BF16), pltpu.SemaphoreType.DMA((2,))],
        compiler_params=_cparams(2, 56),
        name="mlstm_inproj",
    )(x, sh, sc, w_in)


def _head_norm_gate(hh, nw, o):
    mu = jnp.mean(hh, axis=1, keepdims=True)
    xc = hh - mu
    var = jnp.mean(xc * xc, axis=1, keepdims=True)
    return jax.nn.sigmoid(o) * (xc * lax.rsqrt(var + LN_EPS) * nw)


def _mlstm_chunk_kernel(q_ref, k_ref, v_ref, o_ref, g_ref, gb_ref, nw_ref, c0_ref, n0_ref, m0_ref,
                        y_ref, cout_ref, nout_ref, mout_ref, c_scr, n_scr, m_scr):
    ci = pl.program_id(1)
    L = q_ref.shape[0]

    @pl.when(ci == 0)
    def _():
        c_scr[...] = c0_ref[0]
        n_scr[...] = n0_ref[0]
        m_scr[...] = m0_ref[0]

    g = g_ref[...] + gb_ref[...]
    gi = g[:, :LANES]
    lf = _log_sigmoid(g[:, LANES:])
    row = lax.broadcasted_iota(I32, (L, L), 0)
    col = lax.broadcasted_iota(I32, (L, L), 1)
    causal = col <= row
    ltri = jnp.where(causal, 1.0, 0.0).astype(BF16)
    hi, mid, lo = _split3(lf)
    bcum = (jnp.dot(ltri, hi, preferred_element_type=F32)
            + jnp.dot(ltri, mid, preferred_element_type=F32)
            + jnp.dot(ltri, lo, preferred_element_type=F32))
    a = gi - bcum
    a_t = a.T
    scale = DK ** -0.5
    nt = (((1,), (1,)), ((), ()))
    tn_dims = (((0,), (0,)), ((), ()))
    for h in range(N_HEADS):
        ks = slice(h * DK, (h + 1) * DK)
        vs = slice(h * DV, (h + 1) * DV)
        qf = q_ref[:, ks].astype(F32) * scale
        qb = qf.astype(BF16)
        kf = k_ref[:, ks].astype(F32)
        kb = kf.astype(BF16)
        vb = v_ref[:, vs].astype(BF16)
        a_row = a_t[h:h + 1, :]
        a_col = a[:, h:h + 1]
        b_col = bcum[:, h:h + 1]
        m_prev = m_scr[h:h + 1, 0:1]
        amat = jnp.where(causal, a_row, -jnp.inf)
        mx = jnp.max(amat, axis=1, keepdims=True)
        m_inter = b_col + m_prev
        m_t = jnp.maximum(m_inter, b_col + mx)
        dm = jnp.exp(amat + (b_col - m_t))
        s = lax.dot_general(qb, kb, nt, preferred_element_type=F32)
        scores = s * dm
        inter = jnp.exp(m_inter - m_t)
        c_old = c_scr[h]
        n_old = n_scr[h:h + 1, :]
        qc = jnp.dot(qb, c_old.astype(BF16), preferred_element_type=F32)
        num = jnp.dot(scores.astype(BF16), vb, preferred_element_type=F32) + inter * qc
        qn = jnp.sum(qf * n_old, axis=1, keepdims=True)
        den = jnp.sum(scores, axis=1, keepdims=True) + inter * qn
        hh = num / jnp.maximum(jnp.abs(den), jnp.exp(-m_t))
        m_new = m_t[L - 1:L, :]
        b_last = b_col[L - 1:L, :]
        w_col = jnp.exp(b_last + a_col - m_new)
        decay = jnp.exp(b_last + m_prev - m_new)
        kw = kf * w_col
        c_scr[h] = decay * c_old + lax.dot_general(kw.astype(BF16), vb, tn_dims,
                                                   preferred_element_type=F32)
        n_scr[h:h + 1, :] = decay * n_old + jnp.sum(kw, axis=0, keepdims=True)
        m_scr[h:h + 1, :] = jnp.broadcast_to(m_new, (1, LANES))
        y_ref[:, vs] = _head_norm_gate(hh, nw_ref[:, vs], o_ref[:, vs].astype(F32)).astype(y_ref.dtype)

    @pl.when(ci == pl.num_programs(1) - 1)
    def _():
        cout_ref[0] = c_scr[...]
        nout_ref[0] = n_scr[...]
        mout_ref[0] = m_scr[...]


def _mlstm_chunks(proj, gates, gb, nw, c0, n0, m0, batch, seq):
    nc = seq // CHUNK
    qk = N_HEADS * DK
    vd = N_HEADS * DV
    row = lambda b, c: b * nc + c
    st4 = pl.BlockSpec((1, N_HEADS, DK, DV), lambda b, c: (b, 0, 0, 0))
    st3 = pl.BlockSpec((1, N_HEADS, LANES), lambda b, c: (b, 0, 0))
    return pl.pallas_call(
        _mlstm_chunk_kernel,
        grid=(batch, nc),
        in_specs=[pl.BlockSpec((CHUNK, qk), lambda b, c: (row(b, c), 0)),
                  pl.BlockSpec((CHUNK, qk), lambda b, c: (row(b, c), 1)),
                  pl.BlockSpec((CHUNK, vd), lambda b, c: (row(b, c), 1)),
                  pl.BlockSpec((CHUNK, vd), lambda b, c: (row(b, c), 2)),
                  pl.BlockSpec((CHUNK, 2 * LANES), lambda b, c: (row(b, c), 0)),
                  pl.BlockSpec((1, 2 * LANES), lambda b, c: (0, 0)),
                  pl.BlockSpec((1, vd), lambda b, c: (0, 0)),
                  st4, st3, st3],
        out_specs=[pl.BlockSpec((CHUNK, vd), lambda b, c: (row(b, c), 0)), st4, st3, st3],
        out_shape=[jax.ShapeDtypeStruct((batch * seq, vd), BF16),
                   jax.ShapeDtypeStruct((batch, N_HEADS, DK, DV), F32),
                   jax.ShapeDtypeStruct((batch, N_HEADS, LANES), F32),
                   jax.ShapeDtypeStruct((batch, N_HEADS, LANES), F32)],
        scratch_shapes=[pltpu.VMEM((N_HEADS, DK, DV), F32),
                        pltpu.VMEM((N_HEADS, LANES), F32),
                        pltpu.VMEM((N_HEADS, LANES), F32)],
        compiler_params=_cparams(2, 40),
        name="mlstm_chunks",
    )(proj, proj, proj, proj, gates, gb, nw, c0, n0, m0)


def _per_head(x, width):
    return jnp.concatenate(
        [jnp.broadcast_to(x[..., h:h + 1], x.shape[:-1] + (width,)) for h in range(N_HEADS)], axis=-1)


def _head_sums(x, width):
    lane = lax.broadcasted_iota(I32, x.shape[:-1] + (LANES,), x.ndim - 1)
    out = jnp.zeros(x.shape[:-1] + (LANES,), F32)
    for h in range(N_HEADS):
        s = jnp.sum(x[..., h * width:(h + 1) * width], axis=-1, keepdims=True)
        out = jnp.where(lane == h, s, out)
    return out


def _mlstm_short_kernel(seq, q_ref, k_ref, v_ref, o_ref, g_ref, gb_ref, nw_ref, c0_ref, n0_ref, m0_ref,
                        y_ref, cout_ref, nout_ref, mout_ref, qc_scr, kw_scr):
    bt, rows, _ = q_ref.shape
    scale = DK ** -0.5
    tn_dims = (((0,), (0,)), ((), ()))
    row = lax.broadcasted_iota(I32, (bt, rows, LANES), 1)
    g = g_ref[...] + gb_ref[...]
    gi = g[:, :, :LANES]
    lf = _log_sigmoid(g[:, :, LANES:])
    bcum = jnp.zeros_like(lf)
    for s in range(seq):
        bcum = bcum + jnp.where(row >= s, lf[:, s:s + 1, :], 0.0)
    a = gi - bcum
    mx = jnp.full_like(a, -jnp.inf)
    for s in range(seq):
        mx = jnp.maximum(mx, jnp.where(row >= s, a[:, s:s + 1, :], -jnp.inf))
    m_prev = m0_ref[...]
    m_inter = bcum + m_prev
    m_t = jnp.maximum(m_inter, bcum + mx)
    cmt = bcum - m_t
    inter = jnp.exp(m_inter - m_t)
    einv = jnp.exp(-m_t)
    m_new = m_t[:, seq - 1:seq, :]
    b_last = bcum[:, seq - 1:seq, :]
    w = jnp.where(row < seq, jnp.exp(b_last + a - m_new), 0.0)
    decay = jnp.exp(b_last + m_prev - m_new)
    mout_ref[...] = m_new

    q = q_ref[...] * scale
    k = k_ref[...]
    v = v_ref[...]
    n_old = n0_ref[...]
    for b in range(bt):
        for h in range(N_HEADS):
            qc_scr[b, :, h * DV:(h + 1) * DV] = jnp.dot(
                (q_ref[b, :, h * DK:(h + 1) * DK] * scale).astype(BF16), c0_ref[b, h].astype(BF16),
                preferred_element_type=F32)
    den = inter * _head_sums(q * n_old, DK)
    num = _per_head(inter, DV) * qc_scr[...]
    for s in range(seq):
        p = _head_sums(q * k[:, s:s + 1, :], DK) * jnp.where(row >= s, jnp.exp(cmt + a[:, s:s + 1, :]), 0.0)
        den = den + p
        num = num + _per_head(p, DV) * v[:, s:s + 1, :]
    hh = num * _per_head(1.0 / jnp.maximum(jnp.abs(den), einv), DV)
    mu = _head_sums(hh, DV) * (1.0 / DV)
    xc = hh - _per_head(mu, DV)
    var = _head_sums(xc * xc, DV) * (1.0 / DV)
    hn = xc * _per_head(lax.rsqrt(var + LN_EPS), DV) * nw_ref[...]
    y_ref[...] = jax.nn.sigmoid(o_ref[...]) * hn

    kw = k * _per_head(w, DK)
    kw_scr[...] = kw
    nout_ref[...] = _per_head(decay, DK) * n_old + jnp.sum(kw, axis=1, keepdims=True)
    for b in range(bt):
        for h in range(N_HEADS):
            cout_ref[b, h] = (decay[b, :, h:h + 1] * c0_ref[b, h]
                              + lax.dot_general(kw_scr[b, :, h * DK:(h + 1) * DK],
                                                v_ref[b, :, h * DV:(h + 1) * DV],
                                                tn_dims, preferred_element_type=F32))


def _mlstm_short(proj, gates, gb, nw, c0, n0, m0, seq):
    batch, rows, _ = proj.shape
    qk = N_HEADS * DK
    vd = N_HEADS * DV
    bt = 8
    st4 = pl.BlockSpec((bt, N_HEADS, DK, DV), lambda i: (i, 0, 0, 0))
    st3 = pl.BlockSpec((bt, 1, qk), lambda i: (i, 0, 0))
    stm = pl.BlockSpec((bt, 1, LANES), lambda i: (i, 0, 0))
    return pl.pallas_call(
        functools.partial(_mlstm_short_kernel, seq),
        grid=(batch // bt,),
        in_specs=[pl.BlockSpec((bt, rows, qk), lambda i: (i, 0, 0)),
                  pl.BlockSpec((bt, rows, qk), lambda i: (i, 0, 1)),
                  pl.BlockSpec((bt, rows, vd), lambda i: (i, 0, 1)),
                  pl.BlockSpec((bt, rows, vd), lambda i: (i, 0, 2)),
                  pl.BlockSpec((bt, rows, 2 * LANES), lambda i: (i, 0, 0)),
                  pl.BlockSpec((1, 2 * LANES), lambda i: (0, 0)),
                  pl.BlockSpec((1, vd), lambda i: (0, 0)),
                  st4, st3, stm],
        out_specs=[pl.BlockSpec((bt, rows, vd), lambda i: (i, 0, 0)), st4, st3, stm],
        out_shape=[jax.ShapeDtypeStruct((batch, rows, vd), F32),
                   jax.ShapeDtypeStruct((batch, N_HEADS, DK, DV), F32),
                   jax.ShapeDtypeStruct((batch, 1, qk), F32),
                   jax.ShapeDtypeStruct((batch, 1, LANES), F32)],
        scratch_shapes=[pltpu.VMEM((bt, rows, vd), F32), pltpu.VMEM((bt, rows, qk), F32)],
        compiler_params=_cparams(1, 48),
        name="mlstm_short",
    )(proj, proj, proj, proj, gates, gb, nw, c0, n0, m0)


def _residual_ln(zbuf, xres_ref, gate_ref, lng_ref, lnb_ref, xo_ref, nxt):
    nc, tm, tn = zbuf.shape
    inv_d = 1.0 / (nc * tn)
    ssum = jnp.zeros((tm, 1), F32)
    for c in range(nc):
        sl = pl.ds(c * tn, tn)
        z = ALPHA * xres_ref[:, sl] + gate_ref[0, :, sl] * zbuf[c]
        zbuf[c] = z
        ssum = ssum + jnp.sum(z, axis=1, keepdims=True)
    mu = ssum * inv_d
    vsum = jnp.zeros((tm, 1), F32)
    for c in range(nc):
        zc = zbuf[c] - mu
        vsum = vsum + jnp.sum(zc * zc, axis=1, keepdims=True)
    rstd = lax.rsqrt(vsum * inv_d + LN_EPS)
    for c in range(nc):
        sl = pl.ds(c * tn, tn)
        xn = (zbuf[c] - mu) * rstd * lng_ref[:, sl] + lnb_ref[:, sl]
        xo_ref[:, sl] = xn
        if nxt is not None:
            sh_ref, sc_ref, ho_ref = nxt
            ho_ref[:, sl] = (xn * (1.0 + sc_ref[0, :, sl]) + sh_ref[0, :, sl]).astype(ho_ref.dtype)


def _proj_ln_kernel(lhs_ref, w_hbm, xres_ref, gate_ref, lng_ref, lnb_ref, shn_ref, scn_ref, wr_ref, br_ref,
                    cin_ref, xo_ref, ho_ref, idx_ref, wts_ref, cnt_ref, wb, stg, ybuf, carry, sem):
    @pl.when(pl.program_id(0) == 0)
    def _():
        _load_weight_bf16(w_hbm, wb, stg, sem)
        carry[...] = cin_ref[...]

    lhs = lhs_ref[...].astype(BF16)
    for c in range(ybuf.shape[0]):
        ybuf[c] = jnp.dot(lhs, wb[c], preferred_element_type=F32)
    _residual_ln(ybuf, xres_ref, gate_ref, lng_ref, lnb_ref, xo_ref, (shn_ref, scn_ref, ho_ref))
    _route(ho_ref[...], wr_ref, br_ref, carry, idx_ref, wts_ref)
    cnt_ref[...] = carry[...]


def _proj_ln(lhs, w, xres, gate, lng, lnb, shn, scn, mod_idx, tm, tn, w_router, b_router, cnt_in):
    t, k = lhs.shape
    d = w.shape[1]
    r = gate.shape[1]
    mspec = pl.BlockSpec((1, r, d), lambda i: (mod_idx(i), 0, 0))
    vspec = pl.BlockSpec((1, d), lambda i: (0, 0))
    ospec = pl.BlockSpec((tm, d), lambda i: (i, 0))
    rspec = pl.BlockSpec((SUBLANES, tm), lambda i: (0, i))
    cspec = pl.BlockSpec((N_EXPERTS, LANES), lambda i: (0, 0))
    return pl.pallas_call(
        _proj_ln_kernel,
        grid=(t // tm,),
        in_specs=[pl.BlockSpec((tm, k), lambda i: (i, 0)),
                  pl.BlockSpec(memory_space=pl.ANY),
                  ospec, mspec, vspec, vspec, mspec, mspec,
                  pl.BlockSpec((N_EXPERTS, d), lambda i: (0, 0)),
                  pl.BlockSpec((N_EXPERTS, 1), lambda i: (0, 0)),
                  cspec],
        out_specs=[ospec, ospec, rspec, rspec, cspec],
        out_shape=[jax.ShapeDtypeStruct((t, d), F32), jax.ShapeDtypeStruct((t, d), F32),
                   jax.ShapeDtypeStruct((SUBLANES, t), I32), jax.ShapeDtypeStruct((SUBLANES, t), F32),
                   jax.ShapeDtypeStruct((N_EXPERTS, LANES), F32)],
        scratch_shapes=[pltpu.VMEM((d // tn, k, tn), BF16), pltpu.VMEM((2, CAST_ROWS, d), F32),
                        pltpu.VMEM((d // tn, tm, tn), F32), pltpu.VMEM((N_EXPERTS, LANES), F32),
                        pltpu.SemaphoreType.DMA((2,))],
        compiler_params=_cparams(1, 58),
        name="proj_ln",
    )(lhs, w, xres, gate, lng, lnb, shn, scn, w_router.T, b_router.reshape(N_EXPERTS, 1), cnt_in)


def _gelu_tanh(x):
    return x * (0.5 * (1.0 + jnp.tanh(0.7978845608028654 * (x + 0.044715 * (x * x * x)))))


def _gmlp_in_kernel(h_ref, w_hbm, b_ref, o_ref, wb, stg, sem):
    j = pl.program_id(1)

    @pl.when((pl.program_id(0) == 0) & (j == 0))
    def _():
        _load_weight_bf16(w_hbm, wb, stg, sem)

    acc = jnp.dot(h_ref[...].astype(BF16), wb[j], preferred_element_type=F32)
    o_ref[...] = _gelu_tanh(acc + b_ref[...]).astype(o_ref.dtype)


def _gmlp_in(h, w, b, tm, out_dtype):
    t, d = h.shape
    n = w.shape[1]
    tn = 1024
    return pl.pallas_call(
        _gmlp_in_kernel,
        grid=(t // tm, n // tn),
        in_specs=[pl.BlockSpec((tm, d), lambda i, j: (i, 0)),
                  pl.BlockSpec(memory_space=pl.ANY),
                  pl.BlockSpec((1, tn), lambda i, j: (0, j))],
        out_specs=pl.BlockSpec((tm, tn), lambda i, j: (i, j)),
        out_shape=jax.ShapeDtypeStruct((t, n), out_dtype),
        scratch_shapes=[pltpu.VMEM((n // tn, d, tn), BF16), pltpu.VMEM((2, STAGE_ROWS, n), F32),
                        pltpu.SemaphoreType.DMA((2,))],
        compiler_params=_cparams(2, 58),
        name="gmlp_in",
    )(h, w, b.reshape(1, n))


def _gmlp_mix_kernel(u_ref, v_ref, g_ref, b_ref, mix_ref, bias_ref, o_ref, vn_ref=None):
    v = v_ref[...].astype(F32)
    mu = jnp.mean(v, axis=1, keepdims=True)
    xc = v - mu
    var = jnp.mean(xc * xc, axis=1, keepdims=True)
    vn = xc * lax.rsqrt(var + LN_EPS) * g_ref[...] + b_ref[...]
    if vn_ref is not None:
        vn_ref[...] = vn
    gd = v.shape[1] // N_GROUPS_B
    for g in range(N_GROUPS_B):
        sl = slice(g * gd, (g + 1) * gd)
        mixed = jnp.dot(mix_ref[g], vn[:, sl].astype(BF16), preferred_element_type=F32) + bias_ref[g]
        o_ref[:, sl] = (u_ref[:, sl].astype(F32) * mixed).astype(o_ref.dtype)


def _gmlp_mix(uv, nv_g, nv_b, mix, bias, tm, emit_v):
    t, n2 = uv.shape
    di = n2 // 2
    return pl.pallas_call(
        _gmlp_mix_kernel,
        grid=(t // tm,),
        in_specs=[pl.BlockSpec((tm, di), lambda i: (i, 0)),
                  pl.BlockSpec((tm, di), lambda i: (i, 1)),
                  pl.BlockSpec((1, di), lambda i: (0, 0)),
                  pl.BlockSpec((1, di), lambda i: (0, 0)),
                  pl.BlockSpec(mix.shape, lambda i: (0, 0, 0)),
                  pl.BlockSpec(bias.shape, lambda i: (0, 0, 0))],
        out_specs=[pl.BlockSpec((tm, di), lambda i: (i, 0))] * (2 if emit_v else 1),
        out_shape=[jax.ShapeDtypeStruct((t, di), BF16), jax.ShapeDtypeStruct((t, di), F32)][:2 if emit_v else 1],
        compiler_params=_cparams(1, 48),
        name="gmlp_mix",
    )(uv, uv, nv_g.reshape(1, di), nv_b.reshape(1, di), mix, bias)


def _route(h, wr_ref, br_ref, carry, idx_ref, wts_ref):
    tr = h.shape[0]
    nt = (((1,), (1,)), ((), ()))
    hb = h.astype(BF16)
    hl = (h - hb.astype(F32)).astype(BF16)
    wr = wr_ref[...]
    wb = wr.astype(BF16)
    wl = (wr - wb.astype(F32)).astype(BF16)
    logits = (lax.dot_general(wb, hb, nt, preferred_element_type=F32)
              + lax.dot_general(wb, hl, nt, preferred_element_type=F32)
              + lax.dot_general(wl, hb, nt, preferred_element_type=F32))
    s = jax.nn.sigmoid(logits)
    sel = s + br_ref[...]
    epg = EXPERTS_PER_GROUP
    r = [sel[e:e + 1, :] for e in range(N_EXPERTS)]
    su = [s[e:e + 1, :] for e in range(N_EXPERTS)]

    def top2sum(v):
        best = v[0] + v[1]
        for x in range(epg):
            for y in range(x + 1, epg):
                if (x, y) != (0, 1):
                    best = jnp.maximum(best, v[x] + v[y])
        return best

    gs = [top2sum(r[g * epg:(g + 1) * epg]) for g in range(N_EXPERT_GROUPS)]
    gbest = gs[0]
    gidx = jnp.zeros((1, tr), I32)
    for g in range(1, N_EXPERT_GROUPS):
        better = gs[g] > gbest
        gidx = jnp.where(better, g, gidx)
        gbest = jnp.where(better, gs[g], gbest)
    v = list(r[:epg])
    sv = list(su[:epg])
    for g in range(1, N_EXPERT_GROUPS):
        pick = gidx == g
        for x in range(epg):
            v[x] = jnp.where(pick, r[g * epg + x], v[x])
            sv[x] = jnp.where(pick, su[g * epg + x], sv[x])
    i1 = jnp.zeros((1, tr), I32)
    b1 = v[0]
    w1 = sv[0]
    for x in range(1, epg):
        better = v[x] > b1
        i1 = jnp.where(better, x, i1)
        b1 = jnp.where(better, v[x], b1)
        w1 = jnp.where(better, sv[x], w1)
    i2 = jnp.zeros((1, tr), I32)
    b2 = jnp.full((1, tr), -jnp.inf, F32)
    w2 = jnp.zeros((1, tr), F32)
    for x in range(epg):
        take = (i1 != x) & (v[x] > b2)
        i2 = jnp.where(take, x, i2)
        b2 = jnp.where(take, v[x], b2)
        w2 = jnp.where(take, sv[x], w2)
    wsum = w1 + w2
    e1 = gidx * epg + i1
    e2 = gidx * epg + i2
    eio = lax.broadcasted_iota(I32, (N_EXPERTS, tr), 0)
    hit1 = eio == e1
    hit2 = eio == e2
    oh = jnp.where(hit1 | hit2, 1.0, 0.0)
    ri = lax.broadcasted_iota(I32, (tr, tr), 0)
    cj = lax.broadcasted_iota(I32, (tr, tr), 1)
    before = jnp.where(ri < cj, 1.0, 0.0).astype(BF16)
    rank = jnp.dot(oh.astype(BF16), before, preferred_element_type=F32) + carry[:, 0:1]
    rank1 = jnp.sum(jnp.where(hit1, rank, 0.0), axis=0, keepdims=True)
    rank2 = jnp.sum(jnp.where(hit2, rank, 0.0), axis=0, keepdims=True)
    idx_ref[0:1, :] = e1
    idx_ref[1:2, :] = e2
    idx_ref[2:3, :] = rank1.astype(I32)
    idx_ref[3:4, :] = rank2.astype(I32)
    idx_ref[4:8, :] = jnp.zeros((4, tr), I32)
    wts_ref[0:1, :] = w1 / wsum
    wts_ref[1:2, :] = w2 / wsum
    wts_ref[2:8, :] = jnp.zeros((6, tr), F32)
    carry[...] = carry[...] + jnp.sum(oh, axis=1, keepdims=True)


def _two_set_specs(ha, hb, tile):
    d = ha.shape[1]
    na = ha.shape[0] // tile
    nb = hb.shape[0] // tile
    return na, nb, [pl.BlockSpec((tile, d), lambda i, *_: (jnp.minimum(i, na - 1), 0)),
                    pl.BlockSpec((tile, d), lambda i, *_: (jnp.maximum(i - na, 0), 0))]


def _scatter_kernel(n_first, pos_ref, meta_ref, ha_ref, hb_ref, o_hbm, zero_scr, sem):
    i = pl.program_id(0)
    ts = ha_ref.shape[0]
    base = i * ts

    def row_copy(src, r, p):
        return pltpu.make_async_copy(src.at[pl.ds(r, 1)], o_hbm.at[pl.ds(p, 1)], sem)

    def scatter_tile(h_ref):
        def start_rows(r, c):
            row_copy(h_ref, r, pos_ref[0, base + r]).start()
            row_copy(h_ref, r, pos_ref[1, base + r]).start()
            return c

        lax.fori_loop(0, ts, start_rows, 0, unroll=8)
        for _ in range(2):
            pltpu.make_async_copy(h_ref, o_hbm.at[pl.ds(0, ts)], sem).wait()

    @pl.when(i < n_first)
    def _():
        scatter_tile(ha_ref)

    @pl.when(i >= n_first)
    def _():
        scatter_tile(hb_ref)

    @pl.when(i == 0)
    def _():
        zero_scr[...] = jnp.zeros(zero_scr.shape, zero_scr.dtype)
        nz = zero_scr.shape[0]

        def block_copy(p):
            return pltpu.make_async_copy(zero_scr, o_hbm.at[pl.ds(pl.multiple_of(p, nz), nz)], sem)

        for e in range(N_EXPERTS + 1):
            lo = meta_ref[META_OFFSET, e] + meta_ref[META_COUNT, e]
            hi = meta_ref[META_OFFSET, e + 1]
            mid = jnp.minimum(((lo + (nz - 1)) // nz) * nz, hi)

            def start_row(p, c):
                row_copy(zero_scr, 0, p).start()
                return c

            def wait_row(p, c):
                row_copy(zero_scr, 0, 0).wait()
                return c

            def start_block(b, c):
                block_copy(mid + b * nz).start()
                return c

            def wait_block(b, c):
                block_copy(0).wait()
                return c

            lax.fori_loop(lo, mid, start_row, 0)
            lax.fori_loop(lo, mid, wait_row, 0)
            nblk = (hi - mid) // nz
            lax.fori_loop(0, nblk, start_block, 0)
            lax.fori_loop(0, nblk, wait_block, 0)


def _scatter_rows(ha, hb, pos, meta, n_rows, ts):
    d = ha.shape[1]
    na, nb, hspecs = _two_set_specs(ha, hb, ts)
    return pl.pallas_call(
        functools.partial(_scatter_kernel, na),
        grid_spec=pltpu.PrefetchScalarGridSpec(
            num_scalar_prefetch=2,
            grid=(na + nb,),
            in_specs=hspecs,
            out_specs=pl.BlockSpec(memory_space=pl.ANY),
            scratch_shapes=[pltpu.VMEM((SUBLANES, d), ha.dtype), pltpu.SemaphoreType.DMA]),
        out_shape=jax.ShapeDtypeStruct((n_rows, d), ha.dtype),
        compiler_params=_cparams(1, 32),
        name="moe_scatter",
    )(pos, meta, ha, hb)


def _cast_rows(src, dst):
    def body(c, carry):
        r = pl.multiple_of(c * CAST_ROWS, CAST_ROWS)
        dst[pl.ds(r, CAST_ROWS), :] = src[pl.ds(r, CAST_ROWS), :].astype(BF16)
        return carry

    lax.fori_loop(0, src.shape[0] // CAST_ROWS, body, 0)


def _expert_kernel(layer, meta_ref, x_ref, w1_hbm, w2_hbm, y_ref, stg1, stg2, wb1, wb2, sem):
    i = pl.program_id(0)

    def fetch(e):
        return (pltpu.make_async_copy(w1_hbm.at[layer, e], stg1, sem.at[0]),
                pltpu.make_async_copy(w2_hbm.at[layer, e], stg2, sem.at[1]))

    @pl.when(i == 0)
    def _():
        for cp in fetch(meta_ref[META_TILE_EXPERT, 0]):
            cp.start()

    @pl.when(meta_ref[META_FIRST, i] == 1)
    def _():
        for cp in fetch(0):
            cp.wait()
        _cast_rows(stg1, wb1)
        _cast_rows(stg2, wb2)

        @pl.when(meta_ref[META_NEXT, i] >= 0)
        def _():
            for cp in fetch(meta_ref[META_NEXT, i]):
                cp.start()

    @pl.when(i < meta_ref[META_N_USED, 0])
    def _():
        hid = jnp.dot(x_ref[...].astype(BF16), wb1[...], preferred_element_type=F32)
        f = hid.shape[1] // 2
        a = hid[:, :f]
        g = hid[:, f:]
        z = (g * jax.nn.sigmoid(g) * a).astype(BF16)
        y_ref[...] = jnp.dot(z, wb2[...], preferred_element_type=F32)

    @pl.when(i >= meta_ref[META_N_USED, 0])
    def _():
        y_ref[...] = jnp.zeros(y_ref.shape, y_ref.dtype)


def _experts(xs, w1, w2, layer, meta):
    p, d = xs.shape
    tm = MOE_TILE
    ff2 = w1.shape[3]
    ff = ff2 // 2
    row = lambda i, meta: (jnp.minimum(i, meta[META_N_USED, 0] - 1), 0)
    return pl.pallas_call(
        functools.partial(_expert_kernel, layer),
        grid_spec=pltpu.PrefetchScalarGridSpec(
            num_scalar_prefetch=1,
            grid=(p // tm,),
            in_specs=[pl.BlockSpec((tm, d), row),
                      pl.BlockSpec(memory_space=pl.ANY),
                      pl.BlockSpec(memory_space=pl.ANY)],
            out_specs=pl.BlockSpec((tm, d), lambda i, *_: (i, 0)),
            scratch_shapes=[pltpu.VMEM((d, ff2), F32), pltpu.VMEM((ff, d), F32),
                            pltpu.VMEM((d, ff2), BF16), pltpu.VMEM((ff, d), BF16),
                            pltpu.SemaphoreType.DMA((2,))]),
        out_shape=jax.ShapeDtypeStruct((p, d), F32),
        compiler_params=_cparams(1, 56),
        name="moe_experts",
    )(meta, xs, w1, w2)


def _combine_kernel(tok_base, has_next, pos_ref, w_ref, y_hbm, xres_ref, gate_ref, lng_ref, lnb_ref,
                    *rest):
    if has_next:
        shn_ref, scn_ref, xo_ref, ho_ref, gbuf, fbuf, sem = rest
        nxt = (shn_ref, scn_ref, ho_ref)
    else:
        xo_ref, gbuf, fbuf, sem = rest
        nxt = None
    tc = xres_ref.shape[0]
    i = pl.program_id(0)
    n_steps = pl.num_programs(0)

    def gather_tile(step, slot):
        base = tok_base + step * tc

        def start_rows(r, c):
            for k in range(2):
                pltpu.make_async_copy(y_hbm.at[pl.ds(pos_ref[k, base + r], 1)],
                                      gbuf.at[slot, k, pl.ds(r, 1)], sem.at[slot]).start()
            return c

        lax.fori_loop(0, tc, start_rows, 0, unroll=8)

    slot = i % 2

    @pl.when(i == 0)
    def _():
        gather_tile(0, 0)

    @pl.when(i + 1 < n_steps)
    def _():
        gather_tile(i + 1, 1 - slot)

    for k in range(2):
        pltpu.make_async_copy(y_hbm.at[pl.ds(0, tc)], gbuf.at[slot, k], sem.at[slot]).wait()
    fbuf[0] = w_ref[:, 0:1] * gbuf[slot, 0] + w_ref[:, 1:2] * gbuf[slot, 1]
    _residual_ln(fbuf, xres_ref, gate_ref, lng_ref, lnb_ref, xo_ref, nxt)


def _combine_ln(ys, pos, wts, tok_base, xres, gate, lng, lnb, nxt_mods, mod_idx, tc):
    t, d = xres.shape
    r = gate.shape[1]
    has_next = nxt_mods is not None
    mspec = pl.BlockSpec((1, r, d), lambda i, *_: (mod_idx(i), 0, 0))
    vspec = pl.BlockSpec((1, d), lambda i, *_: (0, 0))
    ospec = pl.BlockSpec((tc, d), lambda i, *_: (i, 0))
    in_specs = [pl.BlockSpec((tc, 2), lambda i, *_: (i, 0)),
                pl.BlockSpec(memory_space=pl.ANY), ospec, mspec, vspec, vspec]
    args = [wts, ys, xres, gate, lng, lnb]
    out_specs = [ospec]
    out_shape = [jax.ShapeDtypeStruct((t, d), F32)]
    if has_next:
        in_specs += [mspec, mspec]
        args += list(nxt_mods)
        out_specs.append(ospec)
        out_shape.append(jax.ShapeDtypeStruct((t, d), BF16))
    return pl.pallas_call(
        functools.partial(_combine_kernel, tok_base, has_next),
        grid_spec=pltpu.PrefetchScalarGridSpec(
            num_scalar_prefetch=1,
            grid=(t // tc,),
            in_specs=in_specs,
            out_specs=out_specs,
            scratch_shapes=[pltpu.VMEM((2, 2, tc, d), F32), pltpu.VMEM((1, tc, d), F32),
                            pltpu.SemaphoreType.DMA((2,))]),
        out_shape=out_shape,
        compiler_params=_cparams(1, 40),
        name="moe_combine",
    )(pos, *args)


def _plan_kernel(n_tiles, idxa_ref, idxb_ref, cnt_ref, pos_ref, meta_ref):
    tm = MOE_TILE
    t = pos_ref.shape[1]
    sub = lax.broadcasted_iota(I32, (N_EXPERTS, LANES), 0)
    lane = lax.broadcasted_iota(I32, (N_EXPERTS, LANES), 1)
    cnt = cnt_ref[...]
    padded = jnp.floor((cnt + (tm - 1)) * (1.0 / tm)) * tm
    ends = padded
    sh = 1
    while sh < N_EXPERTS:
        ends = ends + jnp.where(sub >= sh, pltpu.roll(ends, sh, 0), 0.0)
        sh *= 2
    off = ends - padded
    total = ends[N_EXPERTS - 1:N_EXPERTS, :]
    n_used = jnp.maximum(total * (1.0 / tm), 1.0)

    def as_row(col):
        return jnp.sum(jnp.where(sub == lane, col, 0.0), axis=0, keepdims=True)

    off_col = off[:, 0:1]
    lo = 0
    for idx_ref in (idxa_ref, idxb_ref):
        n = idx_ref.shape[1]
        eio = lax.broadcasted_iota(I32, (N_EXPERTS, n), 0)
        for k in range(2):
            start = jnp.sum(jnp.where(eio == idx_ref[k:k + 1, :], off_col, 0.0), axis=0, keepdims=True)
            pos_ref[k:k + 1, lo:lo + n] = start.astype(I32) + idx_ref[2 + k:3 + k, :]
        lo += n
    pos_ref[2:SUBLANES, :] = jnp.zeros((SUBLANES - 2, t), I32)

    tile = lax.broadcasted_iota(I32, (1, LANES), 1).astype(F32)
    tile_start = jnp.minimum(tile, n_used - 1.0) * tm
    te = jnp.minimum(jnp.sum(jnp.where(ends <= tile_start, 1.0, 0.0), axis=0, keepdims=True), N_EXPERTS - 1.0)
    prev = jnp.where(tile == 0.0, -1.0, pltpu.roll(te, 1, 1))
    first = jnp.where((tile < n_used) & (te != prev), 1.0, 0.0)
    present = as_row(jnp.where(cnt > 0.0, 1.0, 0.0))
    later = jnp.where((present > 0.0) & (lane > sub) & (lane < N_EXPERTS), lane.astype(F32), float(N_EXPERTS))
    nxt_e = jnp.min(later, axis=1, keepdims=True)
    nxt_e = jnp.where(nxt_e == float(N_EXPERTS), -1.0, nxt_e)
    nxt = jnp.sum(jnp.where(sub.astype(F32) == te, nxt_e, 0.0), axis=0, keepdims=True)
    lane_row = lax.broadcasted_iota(I32, (1, LANES), 1)
    off_row = as_row(off)
    off_row = jnp.where(lane_row == N_EXPERTS, total, off_row)
    off_row = jnp.where(lane_row == N_EXPERTS + 1, float(n_tiles * tm), off_row)
    rows = {META_TILE_EXPERT: te, META_FIRST: first, META_NEXT: nxt, META_N_USED: n_used,
            META_COUNT: as_row(cnt), META_OFFSET: off_row}
    for r in range(SUBLANES):
        meta_ref[r:r + 1, :] = rows[r].astype(I32) if r in rows else jnp.zeros((1, LANES), I32)


def _moe_plan(idx_a, idx_b, cnt):
    tm = MOE_TILE
    n_tok = idx_a.shape[1] + idx_b.shape[1]
    n_tiles = (2 * n_tok + N_EXPERTS * (tm - 1) + tm - 1) // tm
    assert n_tiles <= LANES
    pos, meta = pl.pallas_call(
        functools.partial(_plan_kernel, n_tiles),
        out_shape=[jax.ShapeDtypeStruct((SUBLANES, n_tok), I32), jax.ShapeDtypeStruct((SUBLANES, LANES), I32)],
        compiler_params=_cparams(0, 32),
        name="moe_plan",
    )(idx_a, idx_b, cnt)
    return pos, meta, n_tiles * tm


class _Rows:
    def __init__(self, n_seq, seq_len, tile):
        self.n_seq, self.seq_len, self.tile = n_seq, seq_len, tile
        self.per_row = seq_len < tile

    def mods(self, m):
        if self.per_row:
            return jnp.repeat(m, self.seq_len, axis=0).reshape(-1, self.tile, m.shape[1])
        return m[:, None, :]

    def mod_idx(self, i):
        return i if self.per_row else (i * self.tile) // self.seq_len


def _split_mods(mod_l, n_prompt):
    d = mod_l.shape[1] // 6
    cols = [mod_l[:, k * d:(k + 1) * d] for k in range(6)]
    return [c[:n_prompt] for c in cols], [c[n_prompt:] for c in cols]


def kernel(x_prompt, x_sample, state_mlstm_C, state_mlstm_n, state_mlstm_m, c_prompt, c_sample, w_ada, b_ada, ln_g, ln_b, a_w_in, a_b_gates, a_norm_w, a_w_out, b_w_in, b_b_in, b_norm_g, b_norm_b, b_w_s, b_b_s, b_w_out, w_router, b_router, w_expert_in, w_expert_out):
    bp, sp, d = x_prompt.shape
    bs, ss, _ = x_sample.shape
    tp = bp * sp
    ts = bs * ss
    n_tok = tp + ts
    qk = N_HEADS * DK
    vd = N_HEADS * DV
    n_main = 2 * qk + 2 * vd

    n_seq = bp + bs
    pad = (-n_seq) % SUBLANES
    c_all = jnp.concatenate([c_prompt, c_sample, jnp.zeros((pad, d), F32)])
    mod = _adaln(c_all, w_ada, b_ada)[:, :n_seq]

    sets = {"p": _Rows(bp, sp, 512), "s": _Rows(bs, ss, 256)}
    base = {"p": 0, "s": tp}
    x = {"p": x_prompt.reshape(tp, d), "s": x_sample.reshape(ts, d)}
    hm = {}
    outs = {}
    for layer in range(DEPTH):
        j = layer // 2
        mp, ms = _split_mods(mod[layer], bp)
        md = {"p": mp, "s": ms}
        lng = ln_g[layer]
        lnb = ln_b[layer]
        lhs = {}
        if layer % 2 == 0:
            w_in = a_w_in[j]
            gb = jnp.zeros((1, 2 * LANES), F32)
            gb = gb.at[0, :N_HEADS].set(a_b_gates[j, :N_HEADS])
            gb = gb.at[0, LANES:LANES + N_HEADS].set(a_b_gates[j, N_HEADS:])
            nw = a_norm_w[j].reshape(1, vd)
            proj = {}
            gates = {}
            for k, rs in sets.items():
                proj[k], gates[k] = _inproj(x[k], rs.mods(md[k][0]), rs.mods(md[k][1]), rs.mod_idx,
                                            w_in, n_main, rs.tile, BF16 if k == "p" else F32)
            lhs["p"], c_p, n_p, m_p = _mlstm_chunks(
                proj["p"], gates["p"], gb, nw,
                jnp.zeros((bp, N_HEADS, DK, DV), F32), jnp.zeros((bp, N_HEADS, LANES), F32),
                jnp.zeros((bp, N_HEADS, LANES), F32), bp, sp)
            rpad = SUBLANES - ss
            proj_s = jnp.pad(proj["s"].reshape(bs, ss, n_main), ((0, 0), (0, rpad), (0, 0)))
            gates_s = jnp.pad(gates["s"].reshape(bs, ss, 2 * LANES), ((0, 0), (0, rpad), (0, 0)))
            m0 = jnp.pad(state_mlstm_m[j], ((0, 0), (0, LANES - N_HEADS)))[:, None, :]
            ypre_s, c_s, n_s, m_s = _mlstm_short(proj_s, gates_s, gb, nw, state_mlstm_C[j],
                                                 state_mlstm_n[j].reshape(bs, 1, qk), m0, ss)
            n_s = n_s.reshape(bs, N_HEADS, DK)
            lhs["s"] = ypre_s[:, :ss].reshape(ts, vd)
            outs["C_p"], outs["n_p"], outs["m_p"] = c_p, n_p, m_p[:, :, 0]
            outs["C_s"], outs["n_s"], outs["m_s"] = c_s, n_s, m_s[:, 0, :N_HEADS]
            w_out = a_w_out[j]
        else:
            tmix = 256
            ws = b_w_s[j]
            bsv = b_b_s[j]

            def mixing(l):
                tri = jnp.tril(jnp.ones((l, l), bool))
                wsl = jnp.where(tri, ws[:, :l, :l], 0.0)
                eye = jnp.eye(tmix // l, dtype=F32)
                mats = jax.vmap(lambda m: jnp.kron(eye, m))(wsl).astype(BF16)
                bias = jnp.tile(bsv[:, :l], (1, tmix // l))[:, :, None]
                return mats, bias

            for k, rs in sets.items():
                uv = _gmlp_in(hm[k], b_w_in[j], b_b_in[j], min(1024, rs.n_seq * rs.seq_len),
                              BF16 if k == "p" else F32)
                mats, bias = mixing(min(CHUNK, rs.seq_len))
                res = _gmlp_mix(uv, b_norm_g[j], b_norm_b[j], mats, bias, tmix, k == "s")
                lhs[k] = res[0]
                if k == "s":
                    outs["v_s"] = res[1]
            w_out = b_w_out[j]
        x1 = {}
        hf = {}
        idx = {}
        wts = {}
        cnt = jnp.zeros((N_EXPERTS, LANES), F32)
        for k, rs in sets.items():
            rs = _Rows(rs.n_seq, rs.seq_len, min(rs.tile, 512 * 2048 // w_out.shape[0]))
            x1[k], hf[k], idx[k], wts[k], cnt = _proj_ln(
                lhs[k], w_out, x[k], rs.mods(md[k][2]), lng[0:1], lnb[0:1], rs.mods(md[k][3]),
                rs.mods(md[k][4]), rs.mod_idx, rs.tile, 512, w_router, b_router, cnt)

        pos, meta, n_rows = _moe_plan(idx["p"], idx["s"], cnt)
        xsorted = _scatter_rows(hf["p"], hf["s"], pos, meta, n_rows, 256)
        ysorted = _experts(xsorted, w_expert_in, w_expert_out, layer, meta)
        nxt = _split_mods(mod[layer + 1], bp) if layer + 1 < DEPTH else None
        for ki, (k, rs) in enumerate(sets.items()):
            rc = _Rows(rs.n_seq, rs.seq_len, 256)
            nxt_mods = None if nxt is None else (rc.mods(nxt[ki][0]), rc.mods(nxt[ki][1]))
            res = _combine_ln(ysorted, pos, wts[k][:2].T, base[k], x1[k], rc.mods(md[k][5]),
                              lng[1:2], lnb[1:2], nxt_mods, rc.mod_idx, 256)
            x[k] = res[0]
            if nxt is not None:
                hm[k] = res[1]

    return (x["p"].reshape(bp, sp, d), x["s"].reshape(bs, ss, d),
            outs["C_p"][None], outs["n_p"][None], outs["m_p"][None],
            outs["C_s"][None], outs["n_s"][None], outs["m_s"][None],
            outs["v_s"].reshape(bs, ss, -1)[None])
```

```python
import functools

import jax
import jax.numpy as jnp
from jax import lax
from jax.experimental import pallas as pl
from jax.experimental.pallas import tpu as pltpu

F32 = jnp.float32
BF16 = jnp.bfloat16
I32 = jnp.int32

DEPTH = 2
N_HEADS = 8
DK = 128
DV = 256
CHUNK = 128
N_GROUPS_B = 8
N_EXPERTS = 16
N_EXPERT_GROUPS = 4
EXPERTS_PER_GROUP = 4
ALPHA = float((2 * DEPTH) ** 0.25)
LN_EPS = 1e-5

LANES = 128
SUBLANES = 8
MIB = 1024 * 1024
MOE_TILE = 256
CAST_ROWS = 256
STAGE_ROWS = 128
META_TILE_EXPERT, META_FIRST, META_NEXT, META_N_USED, META_COUNT, META_OFFSET = range(6)


def _cparams(n_axes, vmem_mib):
    return pltpu.CompilerParams(
        dimension_semantics=("arbitrary",) * n_axes,
        vmem_limit_bytes=int(vmem_mib * MIB))


def _split3(x):
    hi = x.astype(BF16)
    r1 = x - hi.astype(F32)
    mid = r1.astype(BF16)
    lo = (r1 - mid.astype(F32)).astype(BF16)
    return hi, mid, lo


def _log_sigmoid(x):
    return jnp.minimum(x, 0.0) - jnp.log1p(jnp.exp(-jnp.abs(x)))


def _load_weight_bf16(w_hbm, wb, stg, sem):
    nj, k, tn = wb.shape
    ch = stg.shape[1]
    nch = k // ch

    def chunk(c):
        return pltpu.make_async_copy(w_hbm.at[pl.ds(c * ch, ch), pl.ds(0, nj * tn)], stg.at[c % 2],
                                     sem.at[c % 2])

    chunk(0).start()
    for c in range(nch):
        if c + 1 < nch:
            chunk(c + 1).start()
        chunk(c).wait()
        for jj in range(nj):
            wb[jj, c * ch:(c + 1) * ch, :] = stg[c % 2, :, jj * tn:(jj + 1) * tn].astype(BF16)


def _adaln_kernel(c_ref, w_ref, b_ref, o_ref):
    c = c_ref[...]
    a = (c * jax.nn.sigmoid(c)).astype(BF16)
    o_ref[0] = jnp.dot(a, w_ref[0].astype(BF16), preferred_element_type=F32) + b_ref[0]


def _adaln(c_all, w_ada, b_ada):
    depth, d, n = w_ada.shape
    r = c_all.shape[0]
    tn = 1024
    return pl.pallas_call(
        _adaln_kernel,
        grid=(depth, n // tn),
        in_specs=[pl.BlockSpec((r, d), lambda l, j: (0, 0)),
                  pl.BlockSpec((1, d, tn), lambda l, j: (l, 0, j)),
                  pl.BlockSpec((1, 1, tn), lambda l, j: (l, 0, j))],
        out_specs=pl.BlockSpec((1, r, tn), lambda l, j: (l, 0, j)),
        out_shape=jax.ShapeDtypeStruct((depth, r, n), F32),
        compiler_params=_cparams(2, 32),
        name="adaln",
    )(c_all, w_ada, b_ada.reshape(depth, 1, n))


def _inproj_kernel(x_ref, sh_ref, sc_ref, w_hbm, proj_ref, gates_ref, hin_ref, wb, stg, g16, wgs, sem):
    j = pl.program_id(1)

    @pl.when((pl.program_id(0) == 0) & (j == 0))
    def _():
        _load_weight_bf16(w_hbm, wb, stg, sem)
        n_main = wb.shape[0] * wb.shape[2]
        cp = pltpu.make_async_copy(w_hbm.at[:, pl.ds(n_main, 2 * N_HEADS)], g16, sem.at[0])
        cp.start()
        cp.wait()
        g = g16[...]
        z = jnp.zeros((g.shape[0], LANES - N_HEADS), F32)
        wg = jnp.concatenate([g[:, :N_HEADS], z, g[:, N_HEADS:], z], axis=1)
        hi = wg.astype(BF16)
        wgs[0] = hi
        wgs[1] = (wg - hi.astype(F32)).astype(BF16)

    @pl.when(j == 0)
    def _():
        h = x_ref[...] * (1.0 + sc_ref[0]) + sh_ref[0]
        hb = h.astype(BF16)
        hin_ref[...] = hb
        h_lo = (h - hb.astype(F32)).astype(BF16)
        gates_ref[...] = (jnp.dot(hb, wgs[0], preferred_element_type=F32)
                          + jnp.dot(h_lo, wgs[0], preferred_element_type=F32)
                          + jnp.dot(hb, wgs[1], preferred_element_type=F32))

    proj_ref[...] = jnp.dot(hin_ref[...], wb[j], preferred_element_type=F32).astype(proj_ref.dtype)


def _inproj(x, sh, sc, mod_idx, w_in, n_main, tm, out_dtype):
    t, d = x.shape
    tn = 1024
    r = sh.shape[1]
    mspec = pl.BlockSpec((1, r, d), lambda i, j: (mod_idx(i), 0, 0))
    return pl.pallas_call(
        _inproj_kernel,
        grid=(t // tm, n_main // tn),
        in_specs=[pl.BlockSpec((tm, d), lambda i, j: (i, 0)), mspec, mspec,
                  pl.BlockSpec(memory_space=pl.ANY)],
        out_specs=[pl.BlockSpec((tm, tn), lambda i, j: (i, j)),
                   pl.BlockSpec((tm, 2 * LANES), lambda i, j: (i, 0))],
        out_shape=[jax.ShapeDtypeStruct((t, n_main), out_dtype),
                   jax.ShapeDtypeStruct((t, 2 * LANES), F32)],
        scratch_shapes=[pltpu.VMEM((tm, d), BF16), pltpu.VMEM((n_main // tn, d, tn), BF16),
                        pltpu.VMEM((2, STAGE_ROWS, n_main), F32), pltpu.VMEM((d, 2 * N_HEADS), F32),
                        pltpu.VMEM((2, d, 2 * LANES), BF16), pltpu.SemaphoreType.DMA((2,))],
        compiler_params=_cparams(2, 56),
        name="mlstm_inproj",
    )(x, sh, sc, w_in)


def _head_norm_gate(hh, nw, o):
    mu = jnp.mean(hh, axis=1, keepdims=True)
    xc = hh - mu
    var = jnp.mean(xc * xc, axis=1, keepdims=True)
    return jax.nn.sigmoid(o) * (xc * lax.rsqrt(var + LN_EPS) * nw)


def _mlstm_chunk_kernel(q_ref, k_ref, v_ref, o_ref, g_ref, gb_ref, nw_ref, c0_ref, n0_ref, m0_ref,
                        y_ref, cout_ref, nout_ref, mout_ref, c_scr, n_scr, m_scr):
    ci = pl.program_id(1)
    L = q_ref.shape[0]

    @pl.when(ci == 0)
    def _():
        c_scr[...] = c0_ref[0]
        n_scr[...] = n0_ref[0]
        m_scr[...] = m0_ref[0]

    g = g_ref[...] + gb_ref[...]
    gi = g[:, :LANES]
    lf = _log_sigmoid(g[:, LANES:])
    row = lax.broadcasted_iota(I32, (L, L), 0)
    col = lax.broadcasted_iota(I32, (L, L), 1)
    causal = col <= row
    ltri = jnp.where(causal, 1.0, 0.0).astype(BF16)
    hi, mid, lo = _split3(lf)
    bcum = (jnp.dot(ltri, hi, preferred_element_type=F32)
            + jnp.dot(ltri, mid, preferred_element_type=F32)
            + jnp.dot(ltri, lo, preferred_element_type=F32))
    a = gi - bcum
    a_t = a.T
    scale = DK ** -0.5
    nt = (((1,), (1,)), ((), ()))
    tn_dims = (((0,), (0,)), ((), ()))
    for h in range(N_HEADS):
        ks = slice(h * DK, (h + 1) * DK)
        vs = slice(h * DV, (h + 1) * DV)
        qf = q_ref[:, ks].astype(F32) * scale
        qb = qf.astype(BF16)
        kf = k_ref[:, ks].astype(F32)
        kb = kf.astype(BF16)
        vb = v_ref[:, vs].astype(BF16)
        a_row = a_t[h:h + 1, :]
        a_col = a[:, h:h + 1]
        b_col = bcum[:, h:h + 1]
        m_prev = m_scr[h:h + 1, 0:1]
        amat = jnp.where(causal, a_row, -jnp.inf)
        mx = jnp.max(amat, axis=1, keepdims=True)
        m_inter = b_col + m_prev
        m_t = jnp.maximum(m_inter, b_col + mx)
        dm = jnp.exp(amat + (b_col - m_t))
        s = lax.dot_general(qb, kb, nt, preferred_element_type=F32)
        scores = s * dm
        inter = jnp.exp(m_inter - m_t)
        c_old = c_scr[h]
        n_old = n_scr[h:h + 1, :]
        qc = jnp.dot(qb, c_old.astype(BF16), preferred_element_type=F32)
        num = jnp.dot(scores.astype(BF16), vb, preferred_element_type=F32) + inter * qc
        qn = jnp.sum(qf * n_old, axis=1, keepdims=True)
        den = jnp.sum(scores, axis=1, keepdims=True) + inter * qn
        hh = num / jnp.maximum(jnp.abs(den), jnp.exp(-m_t))
        m_new = m_t[L - 1:L, :]
        b_last = b_col[L - 1:L, :]
        w_col = jnp.exp(b_last + a_col - m_new)
        decay = jnp.exp(b_last + m_prev - m_new)
        kw = kf * w_col
        c_scr[h] = decay * c_old + lax.dot_general(kw.astype(BF16), vb, tn_dims,
                                                   preferred_element_type=F32)
        n_scr[h:h + 1, :] = decay * n_old + jnp.sum(kw, axis=0, keepdims=True)
        m_scr[h:h + 1, :] = jnp.broadcast_to(m_new, (1, LANES))
        y_ref[:, vs] = _head_norm_gate(hh, nw_ref[:, vs], o_ref[:, vs].astype(F32)).astype(y_ref.dtype)

    @pl.when(ci == pl.num_programs(1) - 1)
    def _():
        cout_ref[0] = c_scr[...]
        nout_ref[0] = n_scr[...]
        mout_ref[0] = m_scr[...]


def _mlstm_chunks(proj, gates, gb, nw, c0, n0, m0, batch, seq):
    nc = seq // CHUNK
    qk = N_HEADS * DK
    vd = N_HEADS * DV
    row = lambda b, c: b * nc + c
    st4 = pl.BlockSpec((1, N_HEADS, DK, DV), lambda b, c: (b, 0, 0, 0))
    st3 = pl.BlockSpec((1, N_HEADS, LANES), lambda b, c: (b, 0, 0))
    return pl.pallas_call(
        _mlstm_chunk_kernel,
        grid=(batch, nc),
        in_specs=[pl.BlockSpec((CHUNK, qk), lambda b, c: (row(b, c), 0)),
                  pl.BlockSpec((CHUNK, qk), lambda b, c: (row(b, c), 1)),
                  pl.BlockSpec((CHUNK, vd), lambda b, c: (row(b, c), 1)),
                  pl.BlockSpec((CHUNK, vd), lambda b, c: (row(b, c), 2)),
                  pl.BlockSpec((CHUNK, 2 * LANES), lambda b, c: (row(b, c), 0)),
                  pl.BlockSpec((1, 2 * LANES), lambda b, c: (0, 0)),
                  pl.BlockSpec((1, vd), lambda b, c: (0, 0)),
                  st4, st3, st3],
        out_specs=[pl.BlockSpec((CHUNK, vd), lambda b, c: (row(b, c), 0)), st4, st3, st3],
        out_shape=[jax.ShapeDtypeStruct((batch * seq, vd), BF16),
                   jax.ShapeDtypeStruct((batch, N_HEADS, DK, DV), F32),
                   jax.ShapeDtypeStruct((batch, N_HEADS, LANES), F32),
                   jax.ShapeDtypeStruct((batch, N_HEADS, LANES), F32)],
        scratch_shapes=[pltpu.VMEM((N_HEADS, DK, DV), F32),
                        pltpu.VMEM((N_HEADS, LANES), F32),
                        pltpu.VMEM((N_HEADS, LANES), F32)],
        compiler_params=_cparams(2, 40),
        name="mlstm_chunks",
    )(proj, proj, proj, proj, gates, gb, nw, c0, n0, m0)


def _per_head(x, width):
    return jnp.concatenate(
        [jnp.broadcast_to(x[..., h:h + 1], x.shape[:-1] + (width,)) for h in range(N_HEADS)], axis=-1)


def _head_sums(x, width):
    lane = lax.broadcasted_iota(I32, x.shape[:-1] + (LANES,), x.ndim - 1)
    out = jnp.zeros(x.shape[:-1] + (LANES,), F32)
    for h in range(N_HEADS):
        s = jnp.sum(x[..., h * width:(h + 1) * width], axis=-1, keepdims=True)
        out = jnp.where(lane == h, s, out)
    return out


def _mlstm_short_kernel(seq, q_ref, k_ref, v_ref, o_ref, g_ref, gb_ref, nw_ref, c0_ref, n0_ref, m0_ref,
                        y_ref, cout_ref, nout_ref, mout_ref, qc_scr, kw_scr):
    bt, rows, _ = q_ref.shape
    scale = DK ** -0.5
    tn_dims = (((0,), (0,)), ((), ()))
    row = lax.broadcasted_iota(I32, (bt, rows, LANES), 1)
    g = g_ref[...] + gb_ref[...]
    gi = g[:, :, :LANES]
    lf = _log_sigmoid(g[:, :, LANES:])
    bcum = jnp.zeros_like(lf)
    for s in range(seq):
        bcum = bcum + jnp.where(row >= s, lf[:, s:s + 1, :], 0.0)
    a = gi - bcum
    mx = jnp.full_like(a, -jnp.inf)
    for s in range(seq):
        mx = jnp.maximum(mx, jnp.where(row >= s, a[:, s:s + 1, :], -jnp.inf))
    m_prev = m0_ref[...]
    m_inter = bcum + m_prev
    m_t = jnp.maximum(m_inter, bcum + mx)
    cmt = bcum - m_t
    inter = jnp.exp(m_inter - m_t)
    einv = jnp.exp(-m_t)
    m_new = m_t[:, seq - 1:seq, :]
    b_last = bcum[:, seq - 1:seq, :]
    w = jnp.where(row < seq, jnp.exp(b_last + a - m_new), 0.0)
    decay = jnp.exp(b_last + m_prev - m_new)
    mout_ref[...] = m_new

    q = q_ref[...] * scale
    k = k_ref[...]
    v = v_ref[...]
    n_old = n0_ref[...]
    for b in range(bt):
        for h in range(N_HEADS):
            qc_scr[b, :, h * DV:(h + 1) * DV] = jnp.dot(
                (q_ref[b, :, h * DK:(h + 1) * DK] * scale).astype(BF16), c0_ref[b, h].astype(BF16),
                preferred_element_type=F32)
    den = inter * _head_sums(q * n_old, DK)
    num = _per_head(inter, DV) * qc_scr[...]
    for s in range(seq):
        p = _head_sums(q * k[:, s:s + 1, :], DK) * jnp.where(row >= s, jnp.exp(cmt + a[:, s:s + 1, :]), 0.0)
        den = den + p
        num = num + _per_head(p, DV) * v[:, s:s + 1, :]
    hh = num * _per_head(1.0 / jnp.maximum(jnp.abs(den), einv), DV)
    mu = _head_sums(hh, DV) * (1.0 / DV)
    xc = hh - _per_head(mu, DV)
    var = _head_sums(xc * xc, DV) * (1.0 / DV)
    hn = xc * _per_head(lax.rsqrt(var + LN_EPS), DV) * nw_ref[...]
    y_ref[...] = jax.nn.sigmoid(o_ref[...]) * hn

    kw = k * _per_head(w, DK)
    kw_scr[...] = kw
    nout_ref[...] = _per_head(decay, DK) * n_old + jnp.sum(kw, axis=1, keepdims=True)
    for b in range(bt):
        for h in range(N_HEADS):
            cout_ref[b, h] = (decay[b, :, h:h + 1] * c0_ref[b, h]
                              + lax.dot_general(kw_scr[b, :, h * DK:(h + 1) * DK],
                                                v_ref[b, :, h * DV:(h + 1) * DV],
                                                tn_dims, preferred_element_type=F32))


def _mlstm_short(proj, gates, gb, nw, c0, n0, m0, seq):
    batch, rows, _ = proj.shape
    qk = N_HEADS * DK
    vd = N_HEADS * DV
    bt = 8
    st4 = pl.BlockSpec((bt, N_HEADS, DK, DV), lambda i: (i, 0, 0, 0))
    st3 = pl.BlockSpec((bt, 1, qk), lambda i: (i, 0, 0))
    stm = pl.BlockSpec((bt, 1, LANES), lambda i: (i, 0, 0))
    return pl.pallas_call(
        functools.partial(_mlstm_short_kernel, seq),
        grid=(batch // bt,),
        in_specs=[pl.BlockSpec((bt, rows, qk), lambda i: (i, 0, 0)),
                  pl.BlockSpec((bt, rows, qk), lambda i: (i, 0, 1)),
                  pl.BlockSpec((bt, rows, vd), lambda i: (i, 0, 1)),
                  pl.BlockSpec((bt, rows, vd), lambda i: (i, 0, 2)),
                  pl.BlockSpec((bt, rows, 2 * LANES), lambda i: (i, 0, 0)),
                  pl.BlockSpec((1, 2 * LANES), lambda i: (0, 0)),
                  pl.BlockSpec((1, vd), lambda i: (0, 0)),
                  st4, st3, stm],
        out_specs=[pl.BlockSpec((bt, rows, vd), lambda i: (i, 0, 0)), st4, st3, stm],
        out_shape=[jax.ShapeDtypeStruct((batch, rows, vd), F32),
                   jax.ShapeDtypeStruct((batch, N_HEADS, DK, DV), F32),
                   jax.ShapeDtypeStruct((batch, 1, qk), F32),
                   jax.ShapeDtypeStruct((batch, 1, LANES), F32)],
        scratch_shapes=[pltpu.VMEM((bt, rows, vd), F32), pltpu.VMEM((bt, rows, qk), F32)],
        compiler_params=_cparams(1, 48),
        name="mlstm_short",
    )(proj, proj, proj, proj, gates, gb, nw, c0, n0, m0)


def _residual_ln(zbuf, xres_ref, gate_ref, lng_ref, lnb_ref, xo_ref, nxt):
    nc, tm, tn = zbuf.shape
    inv_d = 1.0 / (nc * tn)
    ssum = jnp.zeros((tm, 1), F32)
    for c in range(nc):
        sl = pl.ds(c * tn, tn)
        z = ALPHA * xres_ref[:, sl] + gate_ref[0, :, sl] * zbuf[c]
        zbuf[c] = z
        ssum = ssum + jnp.sum(z, axis=1, keepdims=True)
    mu = ssum * inv_d
    vsum = jnp.zeros((tm, 1), F32)
    for c in range(nc):
        zc = zbuf[c] - mu
        vsum = vsum + jnp.sum(zc * zc, axis=1, keepdims=True)
    rstd = lax.rsqrt(vsum * inv_d + LN_EPS)
    for c in range(nc):
        sl = pl.ds(c * tn, tn)
        xn = (zbuf[c] - mu) * rstd * lng_ref[:, sl] + lnb_ref[:, sl]
        xo_ref[:, sl] = xn
        if nxt is not None:
            sh_ref, sc_ref, ho_ref = nxt
            ho_ref[:, sl] = (xn * (1.0 + sc_ref[0, :, sl]) + sh_ref[0, :, sl]).astype(ho_ref.dtype)


def _proj_ln_kernel(lhs_ref, w_hbm, xres_ref, gate_ref, lng_ref, lnb_ref, shn_ref, scn_ref,
                    xo_ref, ho_ref, wb, stg, ybuf, sem):
    @pl.when(pl.program_id(0) == 0)
    def _():
        _load_weight_bf16(w_hbm, wb, stg, sem)

    lhs = lhs_ref[...].astype(BF16)
    for c in range(ybuf.shape[0]):
        ybuf[c] = jnp.dot(lhs, wb[c], preferred_element_type=F32)
    _residual_ln(ybuf, xres_ref, gate_ref, lng_ref, lnb_ref, xo_ref, (shn_ref, scn_ref, ho_ref))


def _proj_ln(lhs, w, xres, gate, lng, lnb, shn, scn, mod_idx, tm, tn):
    t, k = lhs.shape
    d = w.shape[1]
    r = gate.shape[1]
    mspec = pl.BlockSpec((1, r, d), lambda i: (mod_idx(i), 0, 0))
    vspec = pl.BlockSpec((1, d), lambda i: (0, 0))
    ospec = pl.BlockSpec((tm, d), lambda i: (i, 0))
    return pl.pallas_call(
        _proj_ln_kernel,
        grid=(t // tm,),
        in_specs=[pl.BlockSpec((tm, k), lambda i: (i, 0)),
                  pl.BlockSpec(memory_space=pl.ANY),
                  ospec, mspec, vspec, vspec, mspec, mspec],
        out_specs=[ospec, ospec],
        out_shape=[jax.ShapeDtypeStruct((t, d), F32), jax.ShapeDtypeStruct((t, d), F32)],
        scratch_shapes=[pltpu.VMEM((d // tn, k, tn), BF16), pltpu.VMEM((2, CAST_ROWS, d), F32),
                        pltpu.VMEM((d // tn, tm, tn), F32), pltpu.SemaphoreType.DMA((2,))],
        compiler_params=_cparams(1, 58),
        name="proj_ln",
    )(lhs, w, xres, gate, lng, lnb, shn, scn)


def _gelu_tanh(x):
    return x * (0.5 * (1.0 + jnp.tanh(0.7978845608028654 * (x + 0.044715 * (x * x * x)))))


def _gmlp_in_kernel(h_ref, w_hbm, b_ref, o_ref, wb, stg, sem):
    j = pl.program_id(1)

    @pl.when((pl.program_id(0) == 0) & (j == 0))
    def _():
        _load_weight_bf16(w_hbm, wb, stg, sem)

    acc = jnp.dot(h_ref[...].astype(BF16), wb[j], preferred_element_type=F32)
    o_ref[...] = _gelu_tanh(acc + b_ref[...]).astype(o_ref.dtype)


def _gmlp_in(h, w, b, tm, out_dtype):
    t, d = h.shape
    n = w.shape[1]
    tn = 1024
    return pl.pallas_call(
        _gmlp_in_kernel,
        grid=(t // tm, n // tn),
        in_specs=[pl.BlockSpec((tm, d), lambda i, j: (i, 0)),
                  pl.BlockSpec(memory_space=pl.ANY),
                  pl.BlockSpec((1, tn), lambda i, j: (0, j))],
        out_specs=pl.BlockSpec((tm, tn), lambda i, j: (i, j)),
        out_shape=jax.ShapeDtypeStruct((t, n), out_dtype),
        scratch_shapes=[pltpu.VMEM((n // tn, d, tn), BF16), pltpu.VMEM((2, STAGE_ROWS, n), F32),
                        pltpu.SemaphoreType.DMA((2,))],
        compiler_params=_cparams(2, 58),
        name="gmlp_in",
    )(h, w, b.reshape(1, n))


def _gmlp_mix_rows(u_ref, v_ref, g_ref, b_ref, mix_ref, bias_ref, o_ref, vn_ref):
    v = v_ref[...].astype(F32)
    mu = jnp.mean(v, axis=1, keepdims=True)
    xc = v - mu
    var = jnp.mean(xc * xc, axis=1, keepdims=True)
    vn = xc * lax.rsqrt(var + LN_EPS) * g_ref[...] + b_ref[...]
    if vn_ref is not None:
        vn_ref[...] = vn
    gd = v.shape[1] // N_GROUPS_B
    for g in range(N_GROUPS_B):
        sl = slice(g * gd, (g + 1) * gd)
        mixed = jnp.dot(mix_ref[g], vn[:, sl].astype(BF16), preferred_element_type=F32) + bias_ref[g]
        o_ref[:, sl] = (u_ref[:, sl].astype(F32) * mixed).astype(o_ref.dtype)


def _mix_proj_ln_kernel(emit_v, u_ref, v_ref, g_ref, b_ref, mix_ref, bias_ref, w_hbm, xres_ref, gate_ref,
                        lng_ref, lnb_ref, shn_ref, scn_ref, *rest):
    if emit_v:
        xo_ref, ho_ref, vn_ref, lhs_scr, wb, stg, ybuf, sem = rest
    else:
        xo_ref, ho_ref, lhs_scr, wb, stg, ybuf, sem = rest
        vn_ref = None

    @pl.when(pl.program_id(0) == 0)
    def _():
        _load_weight_bf16(w_hbm, wb, stg, sem)

    _gmlp_mix_rows(u_ref, v_ref, g_ref, b_ref, mix_ref, bias_ref, lhs_scr, vn_ref)
    lhs = lhs_scr[...]
    for c in range(ybuf.shape[0]):
        ybuf[c] = jnp.dot(lhs, wb[c], preferred_element_type=F32)
    _residual_ln(ybuf, xres_ref, gate_ref, lng_ref, lnb_ref, xo_ref, (shn_ref, scn_ref, ho_ref))


def _mix_proj_ln(uv, nv_g, nv_b, mix, bias, w, xres, gate, lng, lnb, shn, scn, mod_idx, tm, tn, emit_v):
    t, n2 = uv.shape
    di = n2 // 2
    d = w.shape[1]
    r = gate.shape[1]
    mspec = pl.BlockSpec((1, r, d), lambda i: (mod_idx(i), 0, 0))
    vspec = pl.BlockSpec((1, d), lambda i: (0, 0))
    ospec = pl.BlockSpec((tm, d), lambda i: (i, 0))
    nspec = pl.BlockSpec((1, di), lambda i: (0, 0))
    out_specs = [ospec, ospec]
    out_shape = [jax.ShapeDtypeStruct((t, d), F32), jax.ShapeDtypeStruct((t, d), F32)]
    if emit_v:
        out_specs.append(pl.BlockSpec((tm, di), lambda i: (i, 0)))
        out_shape.append(jax.ShapeDtypeStruct((t, di), F32))
    return pl.pallas_call(
        functools.partial(_mix_proj_ln_kernel, emit_v),
        grid=(t // tm,),
        in_specs=[pl.BlockSpec((tm, di), lambda i: (i, 0)),
                  pl.BlockSpec((tm, di), lambda i: (i, 1)),
                  nspec, nspec,
                  pl.BlockSpec(mix.shape, lambda i: (0, 0, 0)),
                  pl.BlockSpec(bias.shape, lambda i: (0, 0, 0)),
                  pl.BlockSpec(memory_space=pl.ANY),
                  ospec, mspec, vspec, vspec, mspec, mspec],
        out_specs=out_specs,
        out_shape=out_shape,
        scratch_shapes=[pltpu.VMEM((tm, di), BF16), pltpu.VMEM((d // tn, di, tn), BF16),
                        pltpu.VMEM((2, CAST_ROWS, d), F32), pltpu.VMEM((d // tn, tm, tn), F32),
                        pltpu.SemaphoreType.DMA((2,))],
        compiler_params=_cparams(1, 58),
        name="gmlp_mix_proj_ln",
    )(uv, uv, nv_g.reshape(1, di), nv_b.reshape(1, di), mix, bias, w, xres, gate, lng, lnb, shn, scn)


def _router_kernel(n_first, ha_ref, hb_ref, wr_ref, br_ref, idx_ref, wts_ref, cnt_ref, carry):
    i = pl.program_id(0)
    tr = ha_ref.shape[0]

    @pl.when(i == 0)
    def _():
        carry[...] = jnp.zeros(carry.shape, carry.dtype)

    nt = (((1,), (1,)), ((), ()))
    h = jnp.where(i < n_first, ha_ref[...], hb_ref[...])
    hb = h.astype(BF16)
    hl = (h - hb.astype(F32)).astype(BF16)
    wr = wr_ref[...]
    wb = wr.astype(BF16)
    wl = (wr - wb.astype(F32)).astype(BF16)
    logits = (lax.dot_general(wb, hb, nt, preferred_element_type=F32)
              + lax.dot_general(wb, hl, nt, preferred_element_type=F32)
              + lax.dot_general(wl, hb, nt, preferred_element_type=F32))
    s = jax.nn.sigmoid(logits)
    sel = s + br_ref[...]
    epg = EXPERTS_PER_GROUP
    r = [sel[e:e + 1, :] for e in range(N_EXPERTS)]
    su = [s[e:e + 1, :] for e in range(N_EXPERTS)]

    def top2sum(v):
        best = v[0] + v[1]
        for x in range(epg):
            for y in range(x + 1, epg):
                if (x, y) != (0, 1):
                    best = jnp.maximum(best, v[x] + v[y])
        return best

    gs = [top2sum(r[g * epg:(g + 1) * epg]) for g in range(N_EXPERT_GROUPS)]
    gbest = gs[0]
    gidx = jnp.zeros((1, tr), I32)
    for g in range(1, N_EXPERT_GROUPS):
        better = gs[g] > gbest
        gidx = jnp.where(better, g, gidx)
        gbest = jnp.where(better, gs[g], gbest)
    v = list(r[:epg])
    sv = list(su[:epg])
    for g in range(1, N_EXPERT_GROUPS):
        pick = gidx == g
        for x in range(epg):
            v[x] = jnp.where(pick, r[g * epg + x], v[x])
            sv[x] = jnp.where(pick, su[g * epg + x], sv[x])
    i1 = jnp.zeros((1, tr), I32)
    b1 = v[0]
    w1 = sv[0]
    for x in range(1, epg):
        better = v[x] > b1
        i1 = jnp.where(better, x, i1)
        b1 = jnp.where(better, v[x], b1)
        w1 = jnp.where(better, sv[x], w1)
    i2 = jnp.zeros((1, tr), I32)
    b2 = jnp.full((1, tr), -jnp.inf, F32)
    w2 = jnp.zeros((1, tr), F32)
    for x in range(epg):
        take = (i1 != x) & (v[x] > b2)
        i2 = jnp.where(take, x, i2)
        b2 = jnp.where(take, v[x], b2)
        w2 = jnp.where(take, sv[x], w2)
    wsum = w1 + w2
    e1 = gidx * epg + i1
    e2 = gidx * epg + i2
    eio = lax.broadcasted_iota(I32, (N_EXPERTS, tr), 0)
    hit1 = eio == e1
    hit2 = eio == e2
    oh = jnp.where(hit1 | hit2, 1.0, 0.0)
    ri = lax.broadcasted_iota(I32, (tr, tr), 0)
    cj = lax.broadcasted_iota(I32, (tr, tr), 1)
    before = jnp.where(ri < cj, 1.0, 0.0).astype(BF16)
    rank = jnp.dot(oh.astype(BF16), before, preferred_element_type=F32) + carry[:, 0:1]
    rank1 = jnp.sum(jnp.where(hit1, rank, 0.0), axis=0, keepdims=True)
    rank2 = jnp.sum(jnp.where(hit2, rank, 0.0), axis=0, keepdims=True)
    idx_ref[0:1, :] = e1
    idx_ref[1:2, :] = e2
    idx_ref[2:3, :] = rank1.astype(I32)
    idx_ref[3:4, :] = rank2.astype(I32)
    idx_ref[4:8, :] = jnp.zeros((4, tr), I32)
    wts_ref[0:1, :] = w1 / wsum
    wts_ref[1:2, :] = w2 / wsum
    wts_ref[2:8, :] = jnp.zeros((6, tr), F32)
    carry[...] = carry[...] + jnp.sum(oh, axis=1, keepdims=True)
    cnt_ref[...] = carry[...]


def _two_set_specs(ha, hb, tile):
    d = ha.shape[1]
    na = ha.shape[0] // tile
    nb = hb.shape[0] // tile
    return na, nb, [pl.BlockSpec((tile, d), lambda i, *_: (jnp.minimum(i, na - 1), 0)),
                    pl.BlockSpec((tile, d), lambda i, *_: (jnp.maximum(i - na, 0), 0))]


def _router(ha, hb, w_router, b_router, tr):
    d = ha.shape[1]
    t = ha.shape[0] + hb.shape[0]
    na, nb, hspecs = _two_set_specs(ha, hb, tr)
    return pl.pallas_call(
        functools.partial(_router_kernel, na),
        grid=(na + nb,),
        in_specs=hspecs + [pl.BlockSpec((N_EXPERTS, d), lambda i: (0, 0)),
                           pl.BlockSpec((N_EXPERTS, 1), lambda i: (0, 0))],
        out_specs=[pl.BlockSpec((SUBLANES, tr), lambda i: (0, i)),
                   pl.BlockSpec((SUBLANES, tr), lambda i: (0, i)),
                   pl.BlockSpec((N_EXPERTS, LANES), lambda i: (0, 0))],
        out_shape=[jax.ShapeDtypeStruct((SUBLANES, t), I32),
                   jax.ShapeDtypeStruct((SUBLANES, t), F32),
                   jax.ShapeDtypeStruct((N_EXPERTS, LANES), F32)],
        scratch_shapes=[pltpu.VMEM((N_EXPERTS, LANES), F32)],
        compiler_params=_cparams(1, 32),
        name="moe_router",
    )(ha, hb, w_router.T, b_router.reshape(N_EXPERTS, 1))


def _scatter_kernel(n_tok, n_first, pos_ref, meta_ref, ha_ref, hb_ref, o_hbm, zero_scr, sem):
    i = pl.program_id(0)
    ts = ha_ref.shape[0]
    base = i * ts

    def row_copy(src, r, p):
        return pltpu.make_async_copy(src.at[pl.ds(r, 1)], o_hbm.at[pl.ds(p, 1)], sem)

    def scatter_tile(h_ref):
        def start_rows(r, c):
            row_copy(h_ref, r, pos_ref[base + r]).start()
            row_copy(h_ref, r, pos_ref[n_tok + base + r]).start()
            return c

        lax.fori_loop(0, ts, start_rows, 0, unroll=8)
        for _ in range(2):
            pltpu.make_async_copy(h_ref, o_hbm.at[pl.ds(0, ts)], sem).wait()

    @pl.when(i < n_first)
    def _():
        scatter_tile(ha_ref)

    @pl.when(i >= n_first)
    def _():
        scatter_tile(hb_ref)

    @pl.when(i == 0)
    def _():
        zero_scr[...] = jnp.zeros(zero_scr.shape, zero_scr.dtype)
        nz = zero_scr.shape[0]

        def block_copy(p):
            return pltpu.make_async_copy(zero_scr, o_hbm.at[pl.ds(pl.multiple_of(p, nz), nz)], sem)

        for e in range(N_EXPERTS + 1):
            lo = meta_ref[META_OFFSET, e] + meta_ref[META_COUNT, e]
            hi = meta_ref[META_OFFSET, e + 1]
            mid = jnp.minimum(((lo + (nz - 1)) // nz) * nz, hi)

            def start_row(p, c):
                row_copy(zero_scr, 0, p).start()
                return c

            def wait_row(p, c):
                row_copy(zero_scr, 0, 0).wait()
                return c

            def start_block(b, c):
                block_copy(mid + b * nz).start()
                return c

            def wait_block(b, c):
                block_copy(0).wait()
                return c

            lax.fori_loop(lo, mid, start_row, 0)
            lax.fori_loop(lo, mid, wait_row, 0)
            nblk = (hi - mid) // nz
            lax.fori_loop(0, nblk, start_block, 0)
            lax.fori_loop(0, nblk, wait_block, 0)


def _scatter_rows(ha, hb, pos, meta, n_rows, ts):
    d = ha.shape[1]
    na, nb, hspecs = _two_set_specs(ha, hb, ts)
    return pl.pallas_call(
        functools.partial(_scatter_kernel, ha.shape[0] + hb.shape[0], na),
        grid_spec=pltpu.PrefetchScalarGridSpec(
            num_scalar_prefetch=2,
            grid=(na + nb,),
            in_specs=hspecs,
            out_specs=pl.BlockSpec(memory_space=pl.ANY),
            scratch_shapes=[pltpu.VMEM((SUBLANES, d), ha.dtype), pltpu.SemaphoreType.DMA]),
        out_shape=jax.ShapeDtypeStruct((n_rows, d), ha.dtype),
        compiler_params=_cparams(1, 32),
        name="moe_scatter",
    )(pos, meta, ha, hb)


def _cast_rows(src, dst):
    def body(c, carry):
        r = pl.multiple_of(c * CAST_ROWS, CAST_ROWS)
        dst[pl.ds(r, CAST_ROWS), :] = src[pl.ds(r, CAST_ROWS), :].astype(BF16)
        return carry

    lax.fori_loop(0, src.shape[0] // CAST_ROWS, body, 0)


def _expert_kernel(layer, meta_ref, x_ref, w1_hbm, w2_hbm, y_ref, stg1, stg2, wb1, wb2, sem):
    i = pl.program_id(0)

    def fetch(e):
        return (pltpu.make_async_copy(w1_hbm.at[layer, e], stg1, sem.at[0]),
                pltpu.make_async_copy(w2_hbm.at[layer, e], stg2, sem.at[1]))

    @pl.when(i == 0)
    def _():
        for cp in fetch(meta_ref[META_TILE_EXPERT, 0]):
            cp.start()

    @pl.when(meta_ref[META_FIRST, i] == 1)
    def _():
        for cp in fetch(0):
            cp.wait()
        _cast_rows(stg1, wb1)
        _cast_rows(stg2, wb2)

        @pl.when(meta_ref[META_NEXT, i] >= 0)
        def _():
            for cp in fetch(meta_ref[META_NEXT, i]):
                cp.start()

    @pl.when(i < meta_ref[META_N_USED, 0])
    def _():
        hid = jnp.dot(x_ref[...].astype(BF16), wb1[...], preferred_element_type=F32)
        f = hid.shape[1] // 2
        a = hid[:, :f]
        g = hid[:, f:]
        z = (g * jax.nn.sigmoid(g) * a).astype(BF16)
        y_ref[...] = jnp.dot(z, wb2[...], preferred_element_type=F32)

    @pl.when(i >= meta_ref[META_N_USED, 0])
    def _():
        y_ref[...] = jnp.zeros(y_ref.shape, y_ref.dtype)


def _experts(xs, w1, w2, layer, meta):
    p, d = xs.shape
    tm = MOE_TILE
    ff2 = w1.shape[3]
    ff = ff2 // 2
    row = lambda i, meta: (jnp.minimum(i, meta[META_N_USED, 0] - 1), 0)
    return pl.pallas_call(
        functools.partial(_expert_kernel, layer),
        grid_spec=pltpu.PrefetchScalarGridSpec(
            num_scalar_prefetch=1,
            grid=(p // tm,),
            in_specs=[pl.BlockSpec((tm, d), row),
                      pl.BlockSpec(memory_space=pl.ANY),
                      pl.BlockSpec(memory_space=pl.ANY)],
            out_specs=pl.BlockSpec((tm, d), lambda i, *_: (i, 0)),
            scratch_shapes=[pltpu.VMEM((d, ff2), F32), pltpu.VMEM((ff, d), F32),
                            pltpu.VMEM((d, ff2), BF16), pltpu.VMEM((ff, d), BF16),
                            pltpu.SemaphoreType.DMA((2,))]),
        out_shape=jax.ShapeDtypeStruct((p, d), F32),
        compiler_params=_cparams(1, 56),
        name="moe_experts",
    )(meta, xs, w1, w2)


def _combine_kernel(n_tok, tok_base, has_next, pos_ref, w_ref, y_hbm, xres_ref, gate_ref, lng_ref, lnb_ref,
                    *rest):
    if has_next:
        shn_ref, scn_ref, xo_ref, ho_ref, gbuf, fbuf, sem = rest
        nxt = (shn_ref, scn_ref, ho_ref)
    else:
        xo_ref, gbuf, fbuf, sem = rest
        nxt = None
    tc = xres_ref.shape[0]
    i = pl.program_id(0)
    n_steps = pl.num_programs(0)

    def gather_tile(step, slot):
        base = tok_base + step * tc

        def start_rows(r, c):
            for k in range(2):
                pltpu.make_async_copy(y_hbm.at[pl.ds(pos_ref[k * n_tok + base + r], 1)],
                                      gbuf.at[slot, k, pl.ds(r, 1)], sem.at[slot]).start()
            return c

        lax.fori_loop(0, tc, start_rows, 0, unroll=8)

    slot = i % 2

    @pl.when(i == 0)
    def _():
        gather_tile(0, 0)

    @pl.when(i + 1 < n_steps)
    def _():
        gather_tile(i + 1, 1 - slot)

    for k in range(2):
        pltpu.make_async_copy(y_hbm.at[pl.ds(0, tc)], gbuf.at[slot, k], sem.at[slot]).wait()
    fbuf[0] = w_ref[:, 0:1] * gbuf[slot, 0] + w_ref[:, 1:2] * gbuf[slot, 1]
    _residual_ln(fbuf, xres_ref, gate_ref, lng_ref, lnb_ref, xo_ref, nxt)


def _combine_ln(ys, pos, wts, tok_base, xres, gate, lng, lnb, nxt_mods, mod_idx, tc):
    t, d = xres.shape
    r = gate.shape[1]
    has_next = nxt_mods is not None
    mspec = pl.BlockSpec((1, r, d), lambda i, *_: (mod_idx(i), 0, 0))
    vspec = pl.BlockSpec((1, d), lambda i, *_: (0, 0))
    ospec = pl.BlockSpec((tc, d), lambda i, *_: (i, 0))
    wbase = tok_base // tc
    in_specs = [pl.BlockSpec((tc, 2), lambda i, *_: (wbase + i, 0)),
                pl.BlockSpec(memory_space=pl.ANY), ospec, mspec, vspec, vspec]
    args = [wts, ys, xres, gate, lng, lnb]
    out_specs = [ospec]
    out_shape = [jax.ShapeDtypeStruct((t, d), F32)]
    if has_next:
        in_specs += [mspec, mspec]
        args += list(nxt_mods)
        out_specs.append(ospec)
        out_shape.append(jax.ShapeDtypeStruct((t, d), BF16))
    return pl.pallas_call(
        functools.partial(_combine_kernel, pos.shape[0] // 2, tok_base, has_next),
        grid_spec=pltpu.PrefetchScalarGridSpec(
            num_scalar_prefetch=1,
            grid=(t // tc,),
            in_specs=in_specs,
            out_specs=out_specs,
            scratch_shapes=[pltpu.VMEM((2, 2, tc, d), F32), pltpu.VMEM((1, tc, d), F32),
                            pltpu.SemaphoreType.DMA((2,))]),
        out_shape=out_shape,
        compiler_params=_cparams(1, 40),
        name="moe_combine",
    )(pos, *args)


def _plan_kernel(n_tiles, idx_ref, cnt_ref, pos_ref, meta_ref):
    tm = MOE_TILE
    t = idx_ref.shape[1]
    sub = lax.broadcasted_iota(I32, (N_EXPERTS, LANES), 0)
    lane = lax.broadcasted_iota(I32, (N_EXPERTS, LANES), 1)
    cnt = cnt_ref[...]
    padded = jnp.floor((cnt + (tm - 1)) * (1.0 / tm)) * tm
    ends = padded
    sh = 1
    while sh < N_EXPERTS:
        ends = ends + jnp.where(sub >= sh, pltpu.roll(ends, sh, 0), 0.0)
        sh *= 2
    off = ends - padded
    total = ends[N_EXPERTS - 1:N_EXPERTS, :]
    n_used = jnp.maximum(total * (1.0 / tm), 1.0)

    def as_row(col):
        return jnp.sum(jnp.where(sub == lane, col, 0.0), axis=0, keepdims=True)

    eio = lax.broadcasted_iota(I32, (N_EXPERTS, t), 0)
    off_col = off[:, 0:1]
    for k in range(2):
        start = jnp.sum(jnp.where(eio == idx_ref[k:k + 1, :], off_col, 0.0), axis=0, keepdims=True)
        pos_ref[k:k + 1, :] = start.astype(I32) + idx_ref[2 + k:3 + k, :]
    pos_ref[2:SUBLANES, :] = jnp.zeros((SUBLANES - 2, t), I32)

    tile = lax.broadcasted_iota(I32, (1, LANES), 1).astype(F32)
    tile_start = jnp.minimum(tile, n_used - 1.0) * tm
    te = jnp.minimum(jnp.sum(jnp.where(ends <= tile_start, 1.0, 0.0), axis=0, keepdims=True), N_EXPERTS - 1.0)
    prev = jnp.where(tile == 0.0, -1.0, pltpu.roll(te, 1, 1))
    first = jnp.where((tile < n_used) & (te != prev), 1.0, 0.0)
    present = as_row(jnp.where(cnt > 0.0, 1.0, 0.0))
    later = jnp.where((present > 0.0) & (lane > sub) & (lane < N_EXPERTS), lane.astype(F32), float(N_EXPERTS))
    nxt_e = jnp.min(later, axis=1, keepdims=True)
    nxt_e = jnp.where(nxt_e == float(N_EXPERTS), -1.0, nxt_e)
    nxt = jnp.sum(jnp.where(sub.astype(F32) == te, nxt_e, 0.0), axis=0, keepdims=True)
    lane_row = lax.broadcasted_iota(I32, (1, LANES), 1)
    off_row = as_row(off)
    off_row = jnp.where(lane_row == N_EXPERTS, total, off_row)
    off_row = jnp.where(lane_row == N_EXPERTS + 1, float(n_tiles * tm), off_row)
    rows = {META_TILE_EXPERT: te, META_FIRST: first, META_NEXT: nxt, META_N_USED: n_used,
            META_COUNT: as_row(cnt), META_OFFSET: off_row}
    for r in range(SUBLANES):
        meta_ref[r:r + 1, :] = rows[r].astype(I32) if r in rows else jnp.zeros((1, LANES), I32)


def _moe_plan(idx, cnt, n_tok):
    tm = MOE_TILE
    n_tiles = (2 * n_tok + N_EXPERTS * (tm - 1) + tm - 1) // tm
    assert n_tiles <= LANES
    pos, meta = pl.pallas_call(
        functools.partial(_plan_kernel, n_tiles),
        out_shape=[jax.ShapeDtypeStruct((SUBLANES, n_tok), I32), jax.ShapeDtypeStruct((SUBLANES, LANES), I32)],
        compiler_params=_cparams(0, 32),
        name="moe_plan",
    )(idx, cnt)
    return pos, meta, n_tiles * tm


class _Rows:
    def __init__(self, n_seq, seq_len, tile):
        self.n_seq, self.seq_len, self.tile = n_seq, seq_len, tile
        self.per_row = seq_len < tile

    def mods(self, m):
        if self.per_row:
            return jnp.repeat(m, self.seq_len, axis=0).reshape(-1, self.tile, m.shape[1])
        return m[:, None, :]

    def mod_idx(self, i):
        return i if self.per_row else (i * self.tile) // self.seq_len


def _split_mods(mod_l, n_prompt):
    d = mod_l.shape[1] // 6
    cols = [mod_l[:, k * d:(k + 1) * d] for k in range(6)]
    return [c[:n_prompt] for c in cols], [c[n_prompt:] for c in cols]


def kernel(x_prompt, x_sample, state_mlstm_C, state_mlstm_n, state_mlstm_m, c_prompt, c_sample, w_ada, b_ada, ln_g, ln_b, a_w_in, a_b_gates, a_norm_w, a_w_out, b_w_in, b_b_in, b_norm_g, b_norm_b, b_w_s, b_b_s, b_w_out, w_router, b_router, w_expert_in, w_expert_out):
    bp, sp, d = x_prompt.shape
    bs, ss, _ = x_sample.shape
    tp = bp * sp
    ts = bs * ss
    n_tok = tp + ts
    qk = N_HEADS * DK
    vd = N_HEADS * DV
    n_main = 2 * qk + 2 * vd

    n_seq = bp + bs
    pad = (-n_seq) % SUBLANES
    c_all = jnp.concatenate([c_prompt, c_sample, jnp.zeros((pad, d), F32)])
    mod = _adaln(c_all, w_ada, b_ada)[:, :n_seq]

    sets = {"p": _Rows(bp, sp, 512), "s": _Rows(bs, ss, 256)}
    base = {"p": 0, "s": tp}
    x = {"p": x_prompt.reshape(tp, d), "s": x_sample.reshape(ts, d)}
    hm = {}
    outs = {}
    for layer in range(DEPTH):
        j = layer // 2
        mp, ms = _split_mods(mod[layer], bp)
        md = {"p": mp, "s": ms}
        lng = ln_g[layer]
        lnb = ln_b[layer]
        lhs = {}
        if layer % 2 == 0:
            w_in = a_w_in[j]
            gb = jnp.zeros((1, 2 * LANES), F32)
            gb = gb.at[0, :N_HEADS].set(a_b_gates[j, :N_HEADS])
            gb = gb.at[0, LANES:LANES + N_HEADS].set(a_b_gates[j, N_HEADS:])
            nw = a_norm_w[j].reshape(1, vd)
            proj = {}
            gates = {}
            for k, rs in sets.items():
                proj[k], gates[k] = _inproj(x[k], rs.mods(md[k][0]), rs.mods(md[k][1]), rs.mod_idx,
                                            w_in, n_main, rs.tile, BF16 if k == "p" else F32)
            lhs["p"], c_p, n_p, m_p = _mlstm_chunks(
                proj["p"], gates["p"], gb, nw,
                jnp.zeros((bp, N_HEADS, DK, DV), F32), jnp.zeros((bp, N_HEADS, LANES), F32),
                jnp.zeros((bp, N_HEADS, LANES), F32), bp, sp)
            rpad = SUBLANES - ss
            proj_s = jnp.pad(proj["s"].reshape(bs, ss, n_main), ((0, 0), (0, rpad), (0, 0)))
            gates_s = jnp.pad(gates["s"].reshape(bs, ss, 2 * LANES), ((0, 0), (0, rpad), (0, 0)))
            m0 = jnp.pad(state_mlstm_m[j], ((0, 0), (0, LANES - N_HEADS)))[:, None, :]
            ypre_s, c_s, n_s, m_s = _mlstm_short(proj_s, gates_s, gb, nw, state_mlstm_C[j],
                                                 state_mlstm_n[j].reshape(bs, 1, qk), m0, ss)
            n_s = n_s.reshape(bs, N_HEADS, DK)
            lhs["s"] = ypre_s[:, :ss].reshape(ts, vd)
            outs["C_p"], outs["n_p"], outs["m_p"] = c_p, n_p, m_p[:, :, 0]
            outs["C_s"], outs["n_s"], outs["m_s"] = c_s, n_s, m_s[:, 0, :N_HEADS]
            w_out = a_w_out[j]
        else:
            ws = b_w_s[j]
            bsv = b_b_s[j]
            w_out = b_w_out[j]

            def mixing(l, tmix):
                tri = jnp.tril(jnp.ones((l, l), bool))
                wsl = jnp.where(tri, ws[:, :l, :l], 0.0)
                eye = jnp.eye(tmix // l, dtype=F32)
                mats = jax.vmap(lambda m: jnp.kron(eye, m))(wsl).astype(BF16)
                bias = jnp.tile(bsv[:, :l], (1, tmix // l))[:, :, None]
                return mats, bias

            uv = {k: _gmlp_in(hm[k], b_w_in[j], b_b_in[j], min(1024, rs.n_seq * rs.seq_len),
                              BF16 if k == "p" else F32) for k, rs in sets.items()}
        x1 = {}
        hf = {}
        for k, rs in sets.items():
            if layer % 2 == 0:
                x1[k], hf[k] = _proj_ln(lhs[k], w_out, x[k], rs.mods(md[k][2]), lng[0:1], lnb[0:1],
                                        rs.mods(md[k][3]), rs.mods(md[k][4]), rs.mod_idx, rs.tile, 512)
            else:
                rs = _Rows(rs.n_seq, rs.seq_len, rs.tile // 2)
                mats, bias = mixing(min(CHUNK, rs.seq_len), rs.tile)
                res = _mix_proj_ln(uv[k], b_norm_g[j], b_norm_b[j], mats, bias, w_out, x[k], rs.mods(md[k][2]),
                                   lng[0:1], lnb[0:1], rs.mods(md[k][3]), rs.mods(md[k][4]), rs.mod_idx,
                                   rs.tile, 512, k == "s")
                x1[k], hf[k] = res[0], res[1]
                if k == "s":
                    outs["v_s"] = res[2]

        idx, wts, cnt = _router(hf["p"], hf["s"], w_router, b_router, 512)
        pos, meta, n_rows = _moe_plan(idx, cnt, n_tok)
        pos = pos[:2].reshape(-1)
        xsorted = _scatter_rows(hf["p"], hf["s"], pos, meta, n_rows, 256)
        ysorted = _experts(xsorted, w_expert_in, w_expert_out, layer, meta)
        wts2 = wts[:2].T
        nxt = _split_mods(mod[layer + 1], bp) if layer + 1 < DEPTH else None
        for ki, (k, rs) in enumerate(sets.items()):
            rc = _Rows(rs.n_seq, rs.seq_len, 256)
            nxt_mods = None if nxt is None else (rc.mods(nxt[ki][0]), rc.mods(nxt[ki][1]))
            res = _combine_ln(ysorted, pos, wts2, base[k], x1[k], rc.mods(md[k][5]),
                              lng[1:2], lnb[1:2], nxt_mods, rc.mod_idx, 256)
            x[k] = res[0]
            if nxt is not None:
                hm[k] = res[1]

    return (x["p"].reshape(bp, sp, d), x["s"].reshape(bs, ss, d),
            outs["C_p"][None], outs["n_p"][None], outs["m_p"][None],
            outs["C_s"][None], outs["n_s"][None], outs["m_s"][None],
            outs["v_s"].reshape(bs, ss, -1)[None])
```

```python
import functools

import jax
import jax.numpy as jnp
from jax import lax
from jax.experimental import pallas as pl
from jax.experimental.pallas import tpu as pltpu

F32 = jnp.float32
BF16 = jnp.bfloat16
I32 = jnp.int32

DEPTH = 2
N_HEADS = 8
DK = 128
DV = 256
CHUNK = 128
N_GROUPS_B = 8
N_EXPERTS = 16
N_EXPERT_GROUPS = 4
EXPERTS_PER_GROUP = 4
ALPHA = float((2 * DEPTH) ** 0.25)
LN_EPS = 1e-5

LANES = 128
SUBLANES = 8
MIB = 1024 * 1024
MOE_TILE = 256
CAST_ROWS = 256
STAGE_ROWS = 128
META_TILE_EXPERT, META_FIRST, META_NEXT, META_N_USED, META_COUNT, META_OFFSET = range(6)


def _cparams(n_axes, vmem_mib):
    return pltpu.CompilerParams(
        dimension_semantics=("arbitrary",) * n_axes,
        vmem_limit_bytes=int(vmem_mib * MIB))


def _split3(x):
    hi = x.astype(BF16)
    r1 = x - hi.astype(F32)
    mid = r1.astype(BF16)
    lo = (r1 - mid.astype(F32)).astype(BF16)
    return hi, mid, lo


def _log_sigmoid(x):
    return jnp.minimum(x, 0.0) - jnp.log1p(jnp.exp(-jnp.abs(x)))


def _load_weight_bf16(w_hbm, wb, stg, sem):
    nj, k, tn = wb.shape
    ch = stg.shape[1]
    nch = k // ch

    def chunk(c):
        return pltpu.make_async_copy(w_hbm.at[pl.ds(c * ch, ch), pl.ds(0, nj * tn)], stg.at[c % 2],
                                     sem.at[c % 2])

    chunk(0).start()
    for c in range(nch):
        if c + 1 < nch:
            chunk(c + 1).start()
        chunk(c).wait()
        for jj in range(nj):
            wb[jj, c * ch:(c + 1) * ch, :] = stg[c % 2, :, jj * tn:(jj + 1) * tn].astype(BF16)


def _adaln_kernel(c_ref, w_ref, b_ref, o_ref):
    c = c_ref[...]
    a = (c * jax.nn.sigmoid(c)).astype(BF16)
    o_ref[0] = jnp.dot(a, w_ref[0].astype(BF16), preferred_element_type=F32) + b_ref[0]


def _adaln(c_all, w_ada, b_ada):
    depth, d, n = w_ada.shape
    r = c_all.shape[0]
    tn = 1024
    return pl.pallas_call(
        _adaln_kernel,
        grid=(depth, n // tn),
        in_specs=[pl.BlockSpec((r, d), lambda l, j: (0, 0)),
                  pl.BlockSpec((1, d, tn), lambda l, j: (l, 0, j)),
                  pl.BlockSpec((1, 1, tn), lambda l, j: (l, 0, j))],
        out_specs=pl.BlockSpec((1, r, tn), lambda l, j: (l, 0, j)),
        out_shape=jax.ShapeDtypeStruct((depth, r, n), F32),
        compiler_params=_cparams(2, 32),
        name="adaln",
    )(c_all, w_ada, b_ada.reshape(depth, 1, n))


def _inproj_kernel(x_ref, sh_ref, sc_ref, w_hbm, proj_ref, gates_ref, hin_ref, wb, stg, g16, wgs, sem):
    j = pl.program_id(1)

    @pl.when((pl.program_id(0) == 0) & (j == 0))
    def _():
        _load_weight_bf16(w_hbm, wb, stg, sem)
        n_main = wb.shape[0] * wb.shape[2]
        cp = pltpu.make_async_copy(w_hbm.at[:, pl.ds(n_main, 2 * N_HEADS)], g16, sem.at[0])
        cp.start()
        cp.wait()
        g = g16[...]
        z = jnp.zeros((g.shape[0], LANES - N_HEADS), F32)
        wg = jnp.concatenate([g[:, :N_HEADS], z, g[:, N_HEADS:], z], axis=1)
        hi = wg.astype(BF16)
        wgs[0] = hi
        wgs[1] = (wg - hi.astype(F32)).astype(BF16)

    @pl.when(j == 0)
    def _():
        h = x_ref[...] * (1.0 + sc_ref[0]) + sh_ref[0]
        hb = h.astype(BF16)
        hin_ref[...] = hb
        h_lo = (h - hb.astype(F32)).astype(BF16)
        gates_ref[...] = (jnp.dot(hb, wgs[0], preferred_element_type=F32)
                          + jnp.dot(h_lo, wgs[0], preferred_element_type=F32)
                          + jnp.dot(hb, wgs[1], preferred_element_type=F32))

    proj_ref[...] = jnp.dot(hin_ref[...], wb[j], preferred_element_type=F32).astype(proj_ref.dtype)


def _inproj(x, sh, sc, mod_idx, w_in, n_main, tm, out_dtype):
    t, d = x.shape
    tn = 1024
    r = sh.shape[1]
    mspec = pl.BlockSpec((1, r, d), lambda i, j: (mod_idx(i), 0, 0))
    return pl.pallas_call(
        _inproj_kernel,
        grid=(t // tm, n_main // tn),
        in_specs=[pl.BlockSpec((tm, d), lambda i, j: (i, 0)), mspec, mspec,
                  pl.BlockSpec(memory_space=pl.ANY)],
        out_specs=[pl.BlockSpec((tm, tn), lambda i, j: (i, j)),
                   pl.BlockSpec((tm, 2 * LANES), lambda i, j: (i, 0))],
        out_shape=[jax.ShapeDtypeStruct((t, n_main), out_dtype),
                   jax.ShapeDtypeStruct((t, 2 * LANES), F32)],
        scratch_shapes=[pltpu.VMEM((tm, d), BF16), pltpu.VMEM((n_main // tn, d, tn), BF16),
                        pltpu.VMEM((2, STAGE_ROWS, n_main), F32), pltpu.VMEM((d, 2 * N_HEADS), F32),
                        pltpu.VMEM((2, d, 2 * LANES), BF16), pltpu.SemaphoreType.DMA((2,))],
        compiler_params=_cparams(2, 56),
        name="mlstm_inproj",
    )(x, sh, sc, w_in)


def _head_norm_gate(hh, nw, o):
    mu = jnp.mean(hh, axis=1, keepdims=True)
    xc = hh - mu
    var = jnp.mean(xc * xc, axis=1, keepdims=True)
    return jax.nn.sigmoid(o) * (xc * lax.rsqrt(var + LN_EPS) * nw)


def _mlstm_chunk_kernel(q_ref, k_ref, v_ref, o_ref, g_ref, gb_ref, nw_ref, c0_ref, n0_ref, m0_ref,
                        y_ref, cout_ref, nout_ref, mout_ref, c_scr, n_scr, m_scr):
    ci = pl.program_id(1)
    L = q_ref.shape[0]

    @pl.when(ci == 0)
    def _():
        c_scr[...] = c0_ref[0]
        n_scr[...] = n0_ref[0]
        m_scr[...] = m0_ref[0]

    g = g_ref[...] + gb_ref[...]
    gi = g[:, :LANES]
    lf = _log_sigmoid(g[:, LANES:])
    row = lax.broadcasted_iota(I32, (L, L), 0)
    col = lax.broadcasted_iota(I32, (L, L), 1)
    causal = col <= row
    ltri = jnp.where(causal, 1.0, 0.0).astype(BF16)
    hi, mid, lo = _split3(lf)
    bcum = (jnp.dot(ltri, hi, preferred_element_type=F32)
            + jnp.dot(ltri, mid, preferred_element_type=F32)
            + jnp.dot(ltri, lo, preferred_element_type=F32))
    a = gi - bcum
    a_t = a.T
    scale = DK ** -0.5
    nt = (((1,), (1,)), ((), ()))
    tn_dims = (((0,), (0,)), ((), ()))
    for h in range(N_HEADS):
        ks = slice(h * DK, (h + 1) * DK)
        vs = slice(h * DV, (h + 1) * DV)
        qf = q_ref[:, ks].astype(F32) * scale
        qb = qf.astype(BF16)
        kf = k_ref[:, ks].astype(F32)
        kb = kf.astype(BF16)
        vb = v_ref[:, vs].astype(BF16)
        a_row = a_t[h:h + 1, :]
        a_col = a[:, h:h + 1]
        b_col = bcum[:, h:h + 1]
        m_prev = m_scr[h:h + 1, 0:1]
        amat = jnp.where(causal, a_row, -jnp.inf)
        mx = jnp.max(amat, axis=1, keepdims=True)
        m_inter = b_col + m_prev
        m_t = jnp.maximum(m_inter, b_col + mx)
        dm = jnp.exp(amat + (b_col - m_t))
        s = lax.dot_general(qb, kb, nt, preferred_element_type=F32)
        scores = s * dm
        inter = jnp.exp(m_inter - m_t)
        c_old = c_scr[h]
        n_old = n_scr[h:h + 1, :]
        qc = jnp.dot(qb, c_old.astype(BF16), preferred_element_type=F32)
        num = jnp.dot(scores.astype(BF16), vb, preferred_element_type=F32) + inter * qc
        qn = jnp.sum(qf * n_old, axis=1, keepdims=True)
        den = jnp.sum(scores, axis=1, keepdims=True) + inter * qn
        hh = num / jnp.maximum(jnp.abs(den), jnp.exp(-m_t))
        m_new = m_t[L - 1:L, :]
        b_last = b_col[L - 1:L, :]
        w_col = jnp.exp(b_last + a_col - m_new)
        decay = jnp.exp(b_last + m_prev - m_new)
        kw = kf * w_col
        c_scr[h] = decay * c_old + lax.dot_general(kw.astype(BF16), vb, tn_dims,
                                                   preferred_element_type=F32)
        n_scr[h:h + 1, :] = decay * n_old + jnp.sum(kw, axis=0, keepdims=True)
        m_scr[h:h + 1, :] = jnp.broadcast_to(m_new, (1, LANES))
        y_ref[:, vs] = _head_norm_gate(hh, nw_ref[:, vs], o_ref[:, vs].astype(F32)).astype(y_ref.dtype)

    @pl.when(ci == pl.num_programs(1) - 1)
    def _():
        cout_ref[0] = c_scr[...]
        nout_ref[0] = n_scr[...]
        mout_ref[0] = m_scr[...]


def _mlstm_chunks(proj, gates, gb, nw, c0, n0, m0, batch, seq):
    nc = seq // CHUNK
    qk = N_HEADS * DK
    vd = N_HEADS * DV
    row = lambda b, c: b * nc + c
    st4 = pl.BlockSpec((1, N_HEADS, DK, DV), lambda b, c: (b, 0, 0, 0))
    st3 = pl.BlockSpec((1, N_HEADS, LANES), lambda b, c: (b, 0, 0))
    return pl.pallas_call(
        _mlstm_chunk_kernel,
        grid=(batch, nc),
        in_specs=[pl.BlockSpec((CHUNK, qk), lambda b, c: (row(b, c), 0)),
                  pl.BlockSpec((CHUNK, qk), lambda b, c: (row(b, c), 1)),
                  pl.BlockSpec((CHUNK, vd), lambda b, c: (row(b, c), 1)),
                  pl.BlockSpec((CHUNK, vd), lambda b, c: (row(b, c), 2)),
                  pl.BlockSpec((CHUNK, 2 * LANES), lambda b, c: (row(b, c), 0)),
                  pl.BlockSpec((1, 2 * LANES), lambda b, c: (0, 0)),
                  pl.BlockSpec((1, vd), lambda b, c: (0, 0)),
                  st4, st3, st3],
        out_specs=[pl.BlockSpec((CHUNK, vd), lambda b, c: (row(b, c), 0)), st4, st3, st3],
        out_shape=[jax.ShapeDtypeStruct((batch * seq, vd), BF16),
                   jax.ShapeDtypeStruct((batch, N_HEADS, DK, DV), F32),
                   jax.ShapeDtypeStruct((batch, N_HEADS, LANES), F32),
                   jax.ShapeDtypeStruct((batch, N_HEADS, LANES), F32)],
        scratch_shapes=[pltpu.VMEM((N_HEADS, DK, DV), F32),
                        pltpu.VMEM((N_HEADS, LANES), F32),
                        pltpu.VMEM((N_HEADS, LANES), F32)],
        compiler_params=_cparams(2, 40),
        name="mlstm_chunks",
    )(proj, proj, proj, proj, gates, gb, nw, c0, n0, m0)


def _per_head(x, width):
    return jnp.concatenate(
        [jnp.broadcast_to(x[..., h:h + 1], x.shape[:-1] + (width,)) for h in range(N_HEADS)], axis=-1)


def _head_sums(x, width):
    lane = lax.broadcasted_iota(I32, x.shape[:-1] + (LANES,), x.ndim - 1)
    out = jnp.zeros(x.shape[:-1] + (LANES,), F32)
    for h in range(N_HEADS):
        s = jnp.sum(x[..., h * width:(h + 1) * width], axis=-1, keepdims=True)
        out = jnp.where(lane == h, s, out)
    return out


def _mlstm_short_kernel(seq, q_ref, k_ref, v_ref, o_ref, g_ref, gb_ref, nw_ref, c0_ref, n0_ref, m0_ref,
                        y_ref, cout_ref, nout_ref, mout_ref, qc_scr, kw_scr):
    bt, rows, _ = q_ref.shape
    scale = DK ** -0.5
    tn_dims = (((0,), (0,)), ((), ()))
    row = lax.broadcasted_iota(I32, (bt, rows, LANES), 1)
    g = g_ref[...] + gb_ref[...]
    gi = g[:, :, :LANES]
    lf = _log_sigmoid(g[:, :, LANES:])
    bcum = jnp.zeros_like(lf)
    for s in range(seq):
        bcum = bcum + jnp.where(row >= s, lf[:, s:s + 1, :], 0.0)
    a = gi - bcum
    mx = jnp.full_like(a, -jnp.inf)
    for s in range(seq):
        mx = jnp.maximum(mx, jnp.where(row >= s, a[:, s:s + 1, :], -jnp.inf))
    m_prev = m0_ref[...]
    m_inter = bcum + m_prev
    m_t = jnp.maximum(m_inter, bcum + mx)
    cmt = bcum - m_t
    inter = jnp.exp(m_inter - m_t)
    einv = jnp.exp(-m_t)
    m_new = m_t[:, seq - 1:seq, :]
    b_last = bcum[:, seq - 1:seq, :]
    w = jnp.where(row < seq, jnp.exp(b_last + a - m_new), 0.0)
    decay = jnp.exp(b_last + m_prev - m_new)
    mout_ref[...] = m_new

    q = q_ref[...] * scale
    k = k_ref[...]
    v = v_ref[...]
    n_old = n0_ref[...]
    for b in range(bt):
        for h in range(N_HEADS):
            qc_scr[b, :, h * DV:(h + 1) * DV] = jnp.dot(
                (q_ref[b, :, h * DK:(h + 1) * DK] * scale).astype(BF16), c0_ref[b, h].astype(BF16),
                preferred_element_type=F32)
    den = inter * _head_sums(q * n_old, DK)
    num = _per_head(inter, DV) * qc_scr[...]
    for s in range(seq):
        p = _head_sums(q * k[:, s:s + 1, :], DK) * jnp.where(row >= s, jnp.exp(cmt + a[:, s:s + 1, :]), 0.0)
        den = den + p
        num = num + _per_head(p, DV) * v[:, s:s + 1, :]
    hh = num * _per_head(1.0 / jnp.maximum(jnp.abs(den), einv), DV)
    mu = _head_sums(hh, DV) * (1.0 / DV)
    xc = hh - _per_head(mu, DV)
    var = _head_sums(xc * xc, DV) * (1.0 / DV)
    hn = xc * _per_head(lax.rsqrt(var + LN_EPS), DV) * nw_ref[...]
    y_ref[...] = jax.nn.sigmoid(o_ref[...]) * hn

    kw = k * _per_head(w, DK)
    kw_scr[...] = kw
    nout_ref[...] = _per_head(decay, DK) * n_old + jnp.sum(kw, axis=1, keepdims=True)
    for b in range(bt):
        for h in range(N_HEADS):
            cout_ref[b, h] = (decay[b, :, h:h + 1] * c0_ref[b, h]
                              + lax.dot_general(kw_scr[b, :, h * DK:(h + 1) * DK],
                                                v_ref[b, :, h * DV:(h + 1) * DV],
                                                tn_dims, preferred_element_type=F32))


def _mlstm_short(proj, gates, gb, nw, c0, n0, m0, seq):
    batch, rows, _ = proj.shape
    qk = N_HEADS * DK
    vd = N_HEADS * DV
    bt = 8
    st4 = pl.BlockSpec((bt, N_HEADS, DK, DV), lambda i: (i, 0, 0, 0))
    st3 = pl.BlockSpec((bt, 1, qk), lambda i: (i, 0, 0))
    stm = pl.BlockSpec((bt, 1, LANES), lambda i: (i, 0, 0))
    return pl.pallas_call(
        functools.partial(_mlstm_short_kernel, seq),
        grid=(batch // bt,),
        in_specs=[pl.BlockSpec((bt, rows, qk), lambda i: (i, 0, 0)),
                  pl.BlockSpec((bt, rows, qk), lambda i: (i, 0, 1)),
                  pl.BlockSpec((bt, rows, vd), lambda i: (i, 0, 1)),
                  pl.BlockSpec((bt, rows, vd), lambda i: (i, 0, 2)),
                  pl.BlockSpec((bt, rows, 2 * LANES), lambda i: (i, 0, 0)),
                  pl.BlockSpec((1, 2 * LANES), lambda i: (0, 0)),
                  pl.BlockSpec((1, vd), lambda i: (0, 0)),
                  st4, st3, stm],
        out_specs=[pl.BlockSpec((bt, rows, vd), lambda i: (i, 0, 0)), st4, st3, stm],
        out_shape=[jax.ShapeDtypeStruct((batch, rows, vd), F32),
                   jax.ShapeDtypeStruct((batch, N_HEADS, DK, DV), F32),
                   jax.ShapeDtypeStruct((batch, 1, qk), F32),
                   jax.ShapeDtypeStruct((batch, 1, LANES), F32)],
        scratch_shapes=[pltpu.VMEM((bt, rows, vd), F32), pltpu.VMEM((bt, rows, qk), F32)],
        compiler_params=_cparams(1, 48),
        name="mlstm_short",
    )(proj, proj, proj, proj, gates, gb, nw, c0, n0, m0)


def _residual_ln(zbuf, xres_ref, gate_ref, lng_ref, lnb_ref, xo_ref, nxt):
    nc, tm, tn = zbuf.shape
    inv_d = 1.0 / (nc * tn)
    ssum = jnp.zeros((tm, 1), F32)
    for c in range(nc):
        sl = pl.ds(c * tn, tn)
        z = ALPHA * xres_ref[:, sl] + gate_ref[0, :, sl] * zbuf[c]
        zbuf[c] = z
        ssum = ssum + jnp.sum(z, axis=1, keepdims=True)
    mu = ssum * inv_d
    vsum = jnp.zeros((tm, 1), F32)
    for c in range(nc):
        zc = zbuf[c] - mu
        vsum = vsum + jnp.sum(zc * zc, axis=1, keepdims=True)
    rstd = lax.rsqrt(vsum * inv_d + LN_EPS)
    for c in range(nc):
        sl = pl.ds(c * tn, tn)
        xn = (zbuf[c] - mu) * rstd * lng_ref[:, sl] + lnb_ref[:, sl]
        xo_ref[:, sl] = xn
        if nxt is not None:
            sh_ref, sc_ref, ho_ref = nxt
            ho_ref[:, sl] = (xn * (1.0 + sc_ref[0, :, sl]) + sh_ref[0, :, sl]).astype(ho_ref.dtype)


def _proj_ln_kernel(lhs_ref, w_hbm, xres_ref, gate_ref, lng_ref, lnb_ref, shn_ref, scn_ref,
                    xo_ref, ho_ref, wb, stg, ybuf, sem):
    @pl.when(pl.program_id(0) == 0)
    def _():
        _load_weight_bf16(w_hbm, wb, stg, sem)

    lhs = lhs_ref[...].astype(BF16)
    for c in range(ybuf.shape[0]):
        ybuf[c] = jnp.dot(lhs, wb[c], preferred_element_type=F32)
    _residual_ln(ybuf, xres_ref, gate_ref, lng_ref, lnb_ref, xo_ref, (shn_ref, scn_ref, ho_ref))


def _proj_ln(lhs, w, xres, gate, lng, lnb, shn, scn, mod_idx, tm, tn):
    t, k = lhs.shape
    d = w.shape[1]
    r = gate.shape[1]
    mspec = pl.BlockSpec((1, r, d), lambda i: (mod_idx(i), 0, 0))
    vspec = pl.BlockSpec((1, d), lambda i: (0, 0))
    ospec = pl.BlockSpec((tm, d), lambda i: (i, 0))
    return pl.pallas_call(
        _proj_ln_kernel,
        grid=(t // tm,),
        in_specs=[pl.BlockSpec((tm, k), lambda i: (i, 0)),
                  pl.BlockSpec(memory_space=pl.ANY),
                  ospec, mspec, vspec, vspec, mspec, mspec],
        out_specs=[ospec, ospec],
        out_shape=[jax.ShapeDtypeStruct((t, d), F32), jax.ShapeDtypeStruct((t, d), F32)],
        scratch_shapes=[pltpu.VMEM((d // tn, k, tn), BF16), pltpu.VMEM((2, CAST_ROWS, d), F32),
                        pltpu.VMEM((d // tn, tm, tn), F32), pltpu.SemaphoreType.DMA((2,))],
        compiler_params=_cparams(1, 58),
        name="proj_ln",
    )(lhs, w, xres, gate, lng, lnb, shn, scn)


def _gelu_tanh(x):
    return x * (0.5 * (1.0 + jnp.tanh(0.7978845608028654 * (x + 0.044715 * (x * x * x)))))


def _gmlp_in_kernel(h_ref, w_hbm, b_ref, o_ref, wb, stg, sem):
    j = pl.program_id(1)

    @pl.when((pl.program_id(0) == 0) & (j == 0))
    def _():
        _load_weight_bf16(w_hbm, wb, stg, sem)

    acc = jnp.dot(h_ref[...].astype(BF16), wb[j], preferred_element_type=F32)
    o_ref[...] = _gelu_tanh(acc + b_ref[...]).astype(o_ref.dtype)


def _gmlp_in(h, w, b, tm, out_dtype):
    t, d = h.shape
    n = w.shape[1]
    tn = 1024
    return pl.pallas_call(
        _gmlp_in_kernel,
        grid=(t // tm, n // tn),
        in_specs=[pl.BlockSpec((tm, d), lambda i, j: (i, 0)),
                  pl.BlockSpec(memory_space=pl.ANY),
                  pl.BlockSpec((1, tn), lambda i, j: (0, j))],
        out_specs=pl.BlockSpec((tm, tn), lambda i, j: (i, j)),
        out_shape=jax.ShapeDtypeStruct((t, n), out_dtype),
        scratch_shapes=[pltpu.VMEM((n // tn, d, tn), BF16), pltpu.VMEM((2, STAGE_ROWS, n), F32),
                        pltpu.SemaphoreType.DMA((2,))],
        compiler_params=_cparams(2, 58),
        name="gmlp_in",
    )(h, w, b.reshape(1, n))


def _gmlp_mix_rows(u_ref, v_ref, g_ref, b_ref, mix_ref, bias_ref, o_ref, vn_ref):
    v = v_ref[...].astype(F32)
    mu = jnp.mean(v, axis=1, keepdims=True)
    xc = v - mu
    var = jnp.mean(xc * xc, axis=1, keepdims=True)
    vn = xc * lax.rsqrt(var + LN_EPS) * g_ref[...] + b_ref[...]
    if vn_ref is not None:
        vn_ref[...] = vn
    gd = v.shape[1] // N_GROUPS_B
    for g in range(N_GROUPS_B):
        sl = slice(g * gd, (g + 1) * gd)
        mixed = jnp.dot(mix_ref[g], vn[:, sl].astype(BF16), preferred_element_type=F32) + bias_ref[g]
        o_ref[:, sl] = (u_ref[:, sl].astype(F32) * mixed).astype(o_ref.dtype)


def _mix_proj_ln_kernel(emit_v, u_ref, v_ref, g_ref, b_ref, mix_ref, bias_ref, w_hbm, xres_ref, gate_ref,
                        lng_ref, lnb_ref, shn_ref, scn_ref, *rest):
    if emit_v:
        xo_ref, ho_ref, vn_ref, lhs_scr, wb, stg, ybuf, sem = rest
    else:
        xo_ref, ho_ref, lhs_scr, wb, stg, ybuf, sem = rest
        vn_ref = None

    @pl.when(pl.program_id(0) == 0)
    def _():
        _load_weight_bf16(w_hbm, wb, stg, sem)

    _gmlp_mix_rows(u_ref, v_ref, g_ref, b_ref, mix_ref, bias_ref, lhs_scr, vn_ref)
    lhs = lhs_scr[...]
    for c in range(ybuf.shape[0]):
        ybuf[c] = jnp.dot(lhs, wb[c], preferred_element_type=F32)
    _residual_ln(ybuf, xres_ref, gate_ref, lng_ref, lnb_ref, xo_ref, (shn_ref, scn_ref, ho_ref))


def _mix_proj_ln(uv, nv_g, nv_b, mix, bias, w, xres, gate, lng, lnb, shn, scn, mod_idx, tm, tn, emit_v):
    t, n2 = uv.shape
    di = n2 // 2
    d = w.shape[1]
    r = gate.shape[1]
    mspec = pl.BlockSpec((1, r, d), lambda i: (mod_idx(i), 0, 0))
    vspec = pl.BlockSpec((1, d), lambda i: (0, 0))
    ospec = pl.BlockSpec((tm, d), lambda i: (i, 0))
    nspec = pl.BlockSpec((1, di), lambda i: (0, 0))
    out_specs = [ospec, ospec]
    out_shape = [jax.ShapeDtypeStruct((t, d), F32), jax.ShapeDtypeStruct((t, d), F32)]
    if emit_v:
        out_specs.append(pl.BlockSpec((tm, di), lambda i: (i, 0)))
        out_shape.append(jax.ShapeDtypeStruct((t, di), F32))
    return pl.pallas_call(
        functools.partial(_mix_proj_ln_kernel, emit_v),
        grid=(t // tm,),
        in_specs=[pl.BlockSpec((tm, di), lambda i: (i, 0)),
                  pl.BlockSpec((tm, di), lambda i: (i, 1)),
                  nspec, nspec,
                  pl.BlockSpec(mix.shape, lambda i: (0, 0, 0)),
                  pl.BlockSpec(bias.shape, lambda i: (0, 0, 0)),
                  pl.BlockSpec(memory_space=pl.ANY),
                  ospec, mspec, vspec, vspec, mspec, mspec],
        out_specs=out_specs,
        out_shape=out_shape,
        scratch_shapes=[pltpu.VMEM((tm, di), BF16), pltpu.VMEM((d // tn, di, tn), BF16),
                        pltpu.VMEM((2, CAST_ROWS, d), F32), pltpu.VMEM((d // tn, tm, tn), F32),
                        pltpu.SemaphoreType.DMA((2,))],
        compiler_params=_cparams(1, 58),
        name="gmlp_mix_proj_ln",
    )(uv, uv, nv_g.reshape(1, di), nv_b.reshape(1, di), mix, bias, w, xres, gate, lng, lnb, shn, scn)


def _router_kernel(n_first, ha_ref, hb_ref, wr_ref, br_ref, idx_ref, wts_ref, cnt_ref, carry):
    i = pl.program_id(0)
    tr = ha_ref.shape[0]

    @pl.when(i == 0)
    def _():
        carry[...] = jnp.zeros(carry.shape, carry.dtype)

    nt = (((1,), (1,)), ((), ()))
    h = jnp.where(i < n_first, ha_ref[...], hb_ref[...])
    hb = h.astype(BF16)
    hl = (h - hb.astype(F32)).astype(BF16)
    wr = wr_ref[...]
    wb = wr.astype(BF16)
    wl = (wr - wb.astype(F32)).astype(BF16)
    logits = (lax.dot_general(wb, hb, nt, preferred_element_type=F32)
              + lax.dot_general(wb, hl, nt, preferred_element_type=F32)
              + lax.dot_general(wl, hb, nt, preferred_element_type=F32))
    s = jax.nn.sigmoid(logits)
    sel = s + br_ref[...]
    epg = EXPERTS_PER_GROUP
    r = [sel[e:e + 1, :] for e in range(N_EXPERTS)]
    su = [s[e:e + 1, :] for e in range(N_EXPERTS)]

    def top2sum(v):
        best = v[0] + v[1]
        for x in range(epg):
            for y in range(x + 1, epg):
                if (x, y) != (0, 1):
                    best = jnp.maximum(best, v[x] + v[y])
        return best

    gs = [top2sum(r[g * epg:(g + 1) * epg]) for g in range(N_EXPERT_GROUPS)]
    gbest = gs[0]
    gidx = jnp.zeros((1, tr), I32)
    for g in range(1, N_EXPERT_GROUPS):
        better = gs[g] > gbest
        gidx = jnp.where(better, g, gidx)
        gbest = jnp.where(better, gs[g], gbest)
    v = list(r[:epg])
    sv = list(su[:epg])
    for g in range(1, N_EXPERT_GROUPS):
        pick = gidx == g
        for x in range(epg):
            v[x] = jnp.where(pick, r[g * epg + x], v[x])
            sv[x] = jnp.where(pick, su[g * epg + x], sv[x])
    i1 = jnp.zeros((1, tr), I32)
    b1 = v[0]
    w1 = sv[0]
    for x in range(1, epg):
        better = v[x] > b1
        i1 = jnp.where(better, x, i1)
        b1 = jnp.where(better, v[x], b1)
        w1 = jnp.where(better, sv[x], w1)
    i2 = jnp.zeros((1, tr), I32)
    b2 = jnp.full((1, tr), -jnp.inf, F32)
    w2 = jnp.zeros((1, tr), F32)
    for x in range(epg):
        take = (i1 != x) & (v[x] > b2)
        i2 = jnp.where(take, x, i2)
        b2 = jnp.where(take, v[x], b2)
        w2 = jnp.where(take, sv[x], w2)
    wsum = w1 + w2
    e1 = gidx * epg + i1
    e2 = gidx * epg + i2
    eio = lax.broadcasted_iota(I32, (N_EXPERTS, tr), 0)
    hit1 = eio == e1
    hit2 = eio == e2
    oh = jnp.where(hit1 | hit2, 1.0, 0.0)
    ri = lax.broadcasted_iota(I32, (tr, tr), 0)
    cj = lax.broadcasted_iota(I32, (tr, tr), 1)
    before = jnp.where(ri < cj, 1.0, 0.0).astype(BF16)
    rank = jnp.dot(oh.astype(BF16), before, preferred_element_type=F32) + carry[:, 0:1]
    rank1 = jnp.sum(jnp.where(hit1, rank, 0.0), axis=0, keepdims=True)
    rank2 = jnp.sum(jnp.where(hit2, rank, 0.0), axis=0, keepdims=True)
    idx_ref[0:1, :] = e1
    idx_ref[1:2, :] = e2
    idx_ref[2:3, :] = rank1.astype(I32)
    idx_ref[3:4, :] = rank2.astype(I32)
    idx_ref[4:8, :] = jnp.zeros((4, tr), I32)
    wts_ref[0:1, :] = w1 / wsum
    wts_ref[1:2, :] = w2 / wsum
    wts_ref[2:8, :] = jnp.zeros((6, tr), F32)
    carry[...] = carry[...] + jnp.sum(oh, axis=1, keepdims=True)
    cnt_ref[...] = carry[...]


def _two_set_specs(ha, hb, tile):
    d = ha.shape[1]
    na = ha.shape[0] // tile
    nb = hb.shape[0] // tile
    return na, nb, [pl.BlockSpec((tile, d), lambda i, *_: (jnp.minimum(i, na - 1), 0)),
                    pl.BlockSpec((tile, d), lambda i, *_: (jnp.maximum(i - na, 0), 0))]


def _router(ha, hb, w_router, b_router, tr):
    d = ha.shape[1]
    t = ha.shape[0] + hb.shape[0]
    na, nb, hspecs = _two_set_specs(ha, hb, tr)
    return pl.pallas_call(
        functools.partial(_router_kernel, na),
        grid=(na + nb,),
        in_specs=hspecs + [pl.BlockSpec((N_EXPERTS, d), lambda i: (0, 0)),
                           pl.BlockSpec((N_EXPERTS, 1), lambda i: (0, 0))],
        out_specs=[pl.BlockSpec((SUBLANES, tr), lambda i: (0, i)),
                   pl.BlockSpec((SUBLANES, tr), lambda i: (0, i)),
                   pl.BlockSpec((N_EXPERTS, LANES), lambda i: (0, 0))],
        out_shape=[jax.ShapeDtypeStruct((SUBLANES, t), I32),
                   jax.ShapeDtypeStruct((SUBLANES, t), F32),
                   jax.ShapeDtypeStruct((N_EXPERTS, LANES), F32)],
        scratch_shapes=[pltpu.VMEM((N_EXPERTS, LANES), F32)],
        compiler_params=_cparams(1, 32),
        name="moe_router",
    )(ha, hb, w_router.T, b_router.reshape(N_EXPERTS, 1))


def _scatter_kernel(n_tok, n_first, pos_ref, meta_ref, ha_ref, hb_ref, o_hbm, zero_scr, sem):
    i = pl.program_id(0)
    ts = ha_ref.shape[0]
    base = i * ts

    def row_copy(src, r, p):
        return pltpu.make_async_copy(src.at[pl.ds(r, 1)], o_hbm.at[pl.ds(p, 1)], sem)

    def scatter_tile(h_ref):
        def start_rows(r, c):
            row_copy(h_ref, r, pos_ref[base + r]).start()
            row_copy(h_ref, r, pos_ref[n_tok + base + r]).start()
            return c

        lax.fori_loop(0, ts, start_rows, 0, unroll=8)
        for _ in range(2):
            pltpu.make_async_copy(h_ref, o_hbm.at[pl.ds(0, ts)], sem).wait()

    @pl.when(i < n_first)
    def _():
        scatter_tile(ha_ref)

    @pl.when(i >= n_first)
    def _():
        scatter_tile(hb_ref)

    @pl.when(i == 0)
    def _():
        zero_scr[...] = jnp.zeros(zero_scr.shape, zero_scr.dtype)
        nz = zero_scr.shape[0]

        def block_copy(p):
            return pltpu.make_async_copy(zero_scr, o_hbm.at[pl.ds(pl.multiple_of(p, nz), nz)], sem)

        for e in range(N_EXPERTS + 1):
            lo = meta_ref[META_OFFSET, e] + meta_ref[META_COUNT, e]
            hi = meta_ref[META_OFFSET, e + 1]
            mid = jnp.minimum(((lo + (nz - 1)) // nz) * nz, hi)

            def start_row(p, c):
                row_copy(zero_scr, 0, p).start()
                return c

            def wait_row(p, c):
                row_copy(zero_scr, 0, 0).wait()
                return c

            def start_block(b, c):
                block_copy(mid + b * nz).start()
                return c

            def wait_block(b, c):
                block_copy(0).wait()
                return c

            lax.fori_loop(lo, mid, start_row, 0)
            lax.fori_loop(lo, mid, wait_row, 0)
            nblk = (hi - mid) // nz
            lax.fori_loop(0, nblk, start_block, 0)
            lax.fori_loop(0, nblk, wait_block, 0)


def _scatter_rows(ha, hb, pos, meta, n_rows, ts):
    d = ha.shape[1]
    na, nb, hspecs = _two_set_specs(ha, hb, ts)
    return pl.pallas_call(
        functools.partial(_scatter_kernel, ha.shape[0] + hb.shape[0], na),
        grid_spec=pltpu.PrefetchScalarGridSpec(
            num_scalar_prefetch=2,
            grid=(na + nb,),
            in_specs=hspecs,
            out_specs=pl.BlockSpec(memory_space=pl.ANY),
            scratch_shapes=[pltpu.VMEM((SUBLANES, d), ha.dtype), pltpu.SemaphoreType.DMA]),
        out_shape=jax.ShapeDtypeStruct((n_rows, d), ha.dtype),
        compiler_params=_cparams(1, 32),
        name="moe_scatter",
    )(pos, meta, ha, hb)


def _cast_rows(src, dst):
    def body(c, carry):
        r = pl.multiple_of(c * CAST_ROWS, CAST_ROWS)
        dst[pl.ds(r, CAST_ROWS), :] = src[pl.ds(r, CAST_ROWS), :].astype(BF16)
        return carry

    lax.fori_loop(0, src.shape[0] // CAST_ROWS, body, 0)


def _expert_kernel(layer, meta_ref, x_ref, w1_hbm, w2_hbm, y_ref, stg1, stg2, wb1, wb2, sem):
    i = pl.program_id(0)

    def fetch(e):
        return (pltpu.make_async_copy(w1_hbm.at[layer, e], stg1, sem.at[0]),
                pltpu.make_async_copy(w2_hbm.at[layer, e], stg2, sem.at[1]))

    @pl.when(i == 0)
    def _():
        for cp in fetch(meta_ref[META_TILE_EXPERT, 0]):
            cp.start()

    @pl.when(meta_ref[META_FIRST, i] == 1)
    def _():
        for cp in fetch(0):
            cp.wait()
        _cast_rows(stg1, wb1)
        _cast_rows(stg2, wb2)

        @pl.when(meta_ref[META_NEXT, i] >= 0)
        def _():
            for cp in fetch(meta_ref[META_NEXT, i]):
                cp.start(priority=1)

    @pl.when(i < meta_ref[META_N_USED, 0])
    def _():
        hid = jnp.dot(x_ref[...].astype(BF16), wb1[...], preferred_element_type=F32)
        f = hid.shape[1] // 2
        a = hid[:, :f]
        g = hid[:, f:]
        z = (g * jax.nn.sigmoid(g) * a).astype(BF16)
        y_ref[...] = jnp.dot(z, wb2[...], preferred_element_type=F32)

    @pl.when(i >= meta_ref[META_N_USED, 0])
    def _():
        y_ref[...] = jnp.zeros(y_ref.shape, y_ref.dtype)


def _experts(xs, w1, w2, layer, meta):
    p, d = xs.shape
    tm = MOE_TILE
    ff2 = w1.shape[3]
    ff = ff2 // 2
    row = lambda i, meta: (jnp.minimum(i, meta[META_N_USED, 0] - 1), 0)
    return pl.pallas_call(
        functools.partial(_expert_kernel, layer),
        grid_spec=pltpu.PrefetchScalarGridSpec(
            num_scalar_prefetch=1,
            grid=(p // tm,),
            in_specs=[pl.BlockSpec((tm, d), row),
                      pl.BlockSpec(memory_space=pl.ANY),
                      pl.BlockSpec(memory_space=pl.ANY)],
            out_specs=pl.BlockSpec((tm, d), lambda i, *_: (i, 0)),
            scratch_shapes=[pltpu.VMEM((d, ff2), F32), pltpu.VMEM((ff, d), F32),
                            pltpu.VMEM((d, ff2), BF16), pltpu.VMEM((ff, d), BF16),
                            pltpu.SemaphoreType.DMA((2,))]),
        out_shape=jax.ShapeDtypeStruct((p, d), F32),
        compiler_params=_cparams(1, 56),
        name="moe_experts",
    )(meta, xs, w1, w2)


def _combine_kernel(n_tok, tok_base, has_next, pos_ref, w_ref, y_hbm, xres_ref, gate_ref, lng_ref, lnb_ref,
                    *rest):
    if has_next:
        shn_ref, scn_ref, xo_ref, ho_ref, gbuf, fbuf, sem = rest
        nxt = (shn_ref, scn_ref, ho_ref)
    else:
        xo_ref, gbuf, fbuf, sem = rest
        nxt = None
    tc = xres_ref.shape[0]
    i = pl.program_id(0)
    n_steps = pl.num_programs(0)

    def gather_tile(step, slot):
        base = tok_base + step * tc

        def start_rows(r, c):
            for k in range(2):
                pltpu.make_async_copy(y_hbm.at[pl.ds(pos_ref[k * n_tok + base + r], 1)],
                                      gbuf.at[slot, k, pl.ds(r, 1)], sem.at[slot]).start()
            return c

        lax.fori_loop(0, tc, start_rows, 0, unroll=8)

    slot = i % 2

    @pl.when(i == 0)
    def _():
        gather_tile(0, 0)

    @pl.when(i + 1 < n_steps)
    def _():
        gather_tile(i + 1, 1 - slot)

    for k in range(2):
        pltpu.make_async_copy(y_hbm.at[pl.ds(0, tc)], gbuf.at[slot, k], sem.at[slot]).wait()
    fbuf[0] = w_ref[:, 0:1] * gbuf[slot, 0] + w_ref[:, 1:2] * gbuf[slot, 1]
    _residual_ln(fbuf, xres_ref, gate_ref, lng_ref, lnb_ref, xo_ref, nxt)


def _combine_ln(ys, pos, wts, tok_base, xres, gate, lng, lnb, nxt_mods, mod_idx, tc):
    t, d = xres.shape
    r = gate.shape[1]
    has_next = nxt_mods is not None
    mspec = pl.BlockSpec((1, r, d), lambda i, *_: (mod_idx(i), 0, 0))
    vspec = pl.BlockSpec((1, d), lambda i, *_: (0, 0))
    ospec = pl.BlockSpec((tc, d), lambda i, *_: (i, 0))
    wbase = tok_base // tc
    in_specs = [pl.BlockSpec((tc, 2), lambda i, *_: (wbase + i, 0)),
                pl.BlockSpec(memory_space=pl.ANY), ospec, mspec, vspec, vspec]
    args = [wts, ys, xres, gate, lng, lnb]
    out_specs = [ospec]
    out_shape = [jax.ShapeDtypeStruct((t, d), F32)]
    if has_next:
        in_specs += [mspec, mspec]
        args += list(nxt_mods)
        out_specs.append(ospec)
        out_shape.append(jax.ShapeDtypeStruct((t, d), BF16))
    return pl.pallas_call(
        functools.partial(_combine_kernel, pos.shape[0] // 2, tok_base, has_next),
        grid_spec=pltpu.PrefetchScalarGridSpec(
            num_scalar_prefetch=1,
            grid=(t // tc,),
            in_specs=in_specs,
            out_specs=out_specs,
            scratch_shapes=[pltpu.VMEM((2, 2, tc, d), F32), pltpu.VMEM((1, tc, d), F32),
                            pltpu.SemaphoreType.DMA((2,))]),
        out_shape=out_shape,
        compiler_params=_cparams(1, 40),
        name="moe_combine",
    )(pos, *args)


def _plan_kernel(n_tiles, idx_ref, cnt_ref, pos_ref, meta_ref):
    tm = MOE_TILE
    t = idx_ref.shape[1]
    sub = lax.broadcasted_iota(I32, (N_EXPERTS, LANES), 0)
    lane = lax.broadcasted_iota(I32, (N_EXPERTS, LANES), 1)
    cnt = cnt_ref[...]
    padded = jnp.floor((cnt + (tm - 1)) * (1.0 / tm)) * tm
    ends = padded
    sh = 1
    while sh < N_EXPERTS:
        ends = ends + jnp.where(sub >= sh, pltpu.roll(ends, sh, 0), 0.0)
        sh *= 2
    off = ends - padded
    total = ends[N_EXPERTS - 1:N_EXPERTS, :]
    n_used = jnp.maximum(total * (1.0 / tm), 1.0)

    def as_row(col):
        return jnp.sum(jnp.where(sub == lane, col, 0.0), axis=0, keepdims=True)

    eio = lax.broadcasted_iota(I32, (N_EXPERTS, t), 0)
    off_col = off[:, 0:1]
    for k in range(2):
        start = jnp.sum(jnp.where(eio == idx_ref[k:k + 1, :], off_col, 0.0), axis=0, keepdims=True)
        pos_ref[k:k + 1, :] = start.astype(I32) + idx_ref[2 + k:3 + k, :]
    pos_ref[2:SUBLANES, :] = jnp.zeros((SUBLANES - 2, t), I32)

    tile = lax.broadcasted_iota(I32, (1, LANES), 1).astype(F32)
    tile_start = jnp.minimum(tile, n_used - 1.0) * tm
    te = jnp.minimum(jnp.sum(jnp.where(ends <= tile_start, 1.0, 0.0), axis=0, keepdims=True), N_EXPERTS - 1.0)
    prev = jnp.where(tile == 0.0, -1.0, pltpu.roll(te, 1, 1))
    first = jnp.where((tile < n_used) & (te != prev), 1.0, 0.0)
    present = as_row(jnp.where(cnt > 0.0, 1.0, 0.0))
    later = jnp.where((present > 0.0) & (lane > sub) & (lane < N_EXPERTS), lane.astype(F32), float(N_EXPERTS))
    nxt_e = jnp.min(later, axis=1, keepdims=True)
    nxt_e = jnp.where(nxt_e == float(N_EXPERTS), -1.0, nxt_e)
    nxt = jnp.sum(jnp.where(sub.astype(F32) == te, nxt_e, 0.0), axis=0, keepdims=True)
    lane_row = lax.broadcasted_iota(I32, (1, LANES), 1)
    off_row = as_row(off)
    off_row = jnp.where(lane_row == N_EXPERTS, total, off_row)
    off_row = jnp.where(lane_row == N_EXPERTS + 1, float(n_tiles * tm), off_row)
    rows = {META_TILE_EXPERT: te, META_FIRST: first, META_NEXT: nxt, META_N_USED: n_used,
            META_COUNT: as_row(cnt), META_OFFSET: off_row}
    for r in range(SUBLANES):
        meta_ref[r:r + 1, :] = rows[r].astype(I32) if r in rows else jnp.zeros((1, LANES), I32)


def _moe_plan(idx, cnt, n_tok):
    tm = MOE_TILE
    n_tiles = (2 * n_tok + N_EXPERTS * (tm - 1) + tm - 1) // tm
    assert n_tiles <= LANES
    pos, meta = pl.pallas_call(
        functools.partial(_plan_kernel, n_tiles),
        out_shape=[jax.ShapeDtypeStruct((SUBLANES, n_tok), I32), jax.ShapeDtypeStruct((SUBLANES, LANES), I32)],
        compiler_params=_cparams(0, 32),
        name="moe_plan",
    )(idx, cnt)
    return pos, meta, n_tiles * tm


class _Rows:
    def __init__(self, n_seq, seq_len, tile):
        self.n_seq, self.seq_len, self.tile = n_seq, seq_len, tile
        self.per_row = seq_len < tile

    def mods(self, m):
        if self.per_row:
            return jnp.repeat(m, self.seq_len, axis=0).reshape(-1, self.tile, m.shape[1])
        return m[:, None, :]

    def mod_idx(self, i):
        return i if self.per_row else (i * self.tile) // self.seq_len


def _split_mods(mod_l, n_prompt):
    d = mod_l.shape[1] // 6
    cols = [mod_l[:, k * d:(k + 1) * d] for k in range(6)]
    return [c[:n_prompt] for c in cols], [c[n_prompt:] for c in cols]


def kernel(x_prompt, x_sample, state_mlstm_C, state_mlstm_n, state_mlstm_m, c_prompt, c_sample, w_ada, b_ada, ln_g, ln_b, a_w_in, a_b_gates, a_norm_w, a_w_out, b_w_in, b_b_in, b_norm_g, b_norm_b, b_w_s, b_b_s, b_w_out, w_router, b_router, w_expert_in, w_expert_out):
    bp, sp, d = x_prompt.shape
    bs, ss, _ = x_sample.shape
    tp = bp * sp
    ts = bs * ss
    n_tok = tp + ts
    qk = N_HEADS * DK
    vd = N_HEADS * DV
    n_main = 2 * qk + 2 * vd

    n_seq = bp + bs
    pad = (-n_seq) % SUBLANES
    c_all = jnp.concatenate([c_prompt, c_sample, jnp.zeros((pad, d), F32)])
    mod = _adaln(c_all, w_ada, b_ada)[:, :n_seq]

    sets = {"p": _Rows(bp, sp, 512), "s": _Rows(bs, ss, 256)}
    base = {"p": 0, "s": tp}
    x = {"p": x_prompt.reshape(tp, d), "s": x_sample.reshape(ts, d)}
    hm = {}
    outs = {}
    for layer in range(DEPTH):
        j = layer // 2
        mp, ms = _split_mods(mod[layer], bp)
        md = {"p": mp, "s": ms}
        lng = ln_g[layer]
        lnb = ln_b[layer]
        lhs = {}
        if layer % 2 == 0:
            w_in = a_w_in[j]
            gb = jnp.zeros((1, 2 * LANES), F32)
            gb = gb.at[0, :N_HEADS].set(a_b_gates[j, :N_HEADS])
            gb = gb.at[0, LANES:LANES + N_HEADS].set(a_b_gates[j, N_HEADS:])
            nw = a_norm_w[j].reshape(1, vd)
            proj = {}
            gates = {}
            for k, rs in sets.items():
                proj[k], gates[k] = _inproj(x[k], rs.mods(md[k][0]), rs.mods(md[k][1]), rs.mod_idx,
                                            w_in, n_main, rs.tile, BF16 if k == "p" else F32)
            lhs["p"], c_p, n_p, m_p = _mlstm_chunks(
                proj["p"], gates["p"], gb, nw,
                jnp.zeros((bp, N_HEADS, DK, DV), F32), jnp.zeros((bp, N_HEADS, LANES), F32),
                jnp.zeros((bp, N_HEADS, LANES), F32), bp, sp)
            rpad = SUBLANES - ss
            proj_s = jnp.pad(proj["s"].reshape(bs, ss, n_main), ((0, 0), (0, rpad), (0, 0)))
            gates_s = jnp.pad(gates["s"].reshape(bs, ss, 2 * LANES), ((0, 0), (0, rpad), (0, 0)))
            m0 = jnp.pad(state_mlstm_m[j], ((0, 0), (0, LANES - N_HEADS)))[:, None, :]
            ypre_s, c_s, n_s, m_s = _mlstm_short(proj_s, gates_s, gb, nw, state_mlstm_C[j],
                                                 state_mlstm_n[j].reshape(bs, 1, qk), m0, ss)
            n_s = n_s.reshape(bs, N_HEADS, DK)
            lhs["s"] = ypre_s[:, :ss].reshape(ts, vd)
            outs["C_p"], outs["n_p"], outs["m_p"] = c_p, n_p, m_p[:, :, 0]
            outs["C_s"], outs["n_s"], outs["m_s"] = c_s, n_s, m_s[:, 0, :N_HEADS]
            w_out = a_w_out[j]
        else:
            ws = b_w_s[j]
            bsv = b_b_s[j]
            w_out = b_w_out[j]

            def mixing(l, tmix):
                tri = jnp.tril(jnp.ones((l, l), bool))
                wsl = jnp.where(tri, ws[:, :l, :l], 0.0)
                eye = jnp.eye(tmix // l, dtype=F32)
                mats = jax.vmap(lambda m: jnp.kron(eye, m))(wsl).astype(BF16)
                bias = jnp.tile(bsv[:, :l], (1, tmix // l))[:, :, None]
                return mats, bias

            uv = {k: _gmlp_in(hm[k], b_w_in[j], b_b_in[j], min(1024, rs.n_seq * rs.seq_len),
                              BF16 if k == "p" else F32) for k, rs in sets.items()}
        x1 = {}
        hf = {}
        for k, rs in sets.items():
            if layer % 2 == 0:
                x1[k], hf[k] = _proj_ln(lhs[k], w_out, x[k], rs.mods(md[k][2]), lng[0:1], lnb[0:1],
                                        rs.mods(md[k][3]), rs.mods(md[k][4]), rs.mod_idx, rs.tile, 512)
            else:
                rs = _Rows(rs.n_seq, rs.seq_len, rs.tile // 2)
                mats, bias = mixing(min(CHUNK, rs.seq_len), rs.tile)
                res = _mix_proj_ln(uv[k], b_norm_g[j], b_norm_b[j], mats, bias, w_out, x[k], rs.mods(md[k][2]),
                                   lng[0:1], lnb[0:1], rs.mods(md[k][3]), rs.mods(md[k][4]), rs.mod_idx,
                                   rs.tile, 512, k == "s")
                x1[k], hf[k] = res[0], res[1]
                if k == "s":
                    outs["v_s"] = res[2]

        idx, wts, cnt = _router(hf["p"], hf["s"], w_router, b_router, 512)
        pos, meta, n_rows = _moe_plan(idx, cnt, n_tok)
        pos = pos[:2].reshape(-1)
        xsorted = _scatter_rows(hf["p"], hf["s"], pos, meta, n_rows, 256)
        ysorted = _experts(xsorted, w_expert_in, w_expert_out, layer, meta)
        wts2 = wts[:2].T
        nxt = _split_mods(mod[layer + 1], bp) if layer + 1 < DEPTH else None
        for ki, (k, rs) in enumerate(sets.items()):
            rc = _Rows(rs.n_seq, rs.seq_len, 256)
            nxt_mods = None if nxt is None else (rc.mods(nxt[ki][0]), rc.mods(nxt[ki][1]))
            res = _combine_ln(ysorted, pos, wts2, base[k], x1[k], rc.mods(md[k][5]),
                              lng[1:2], lnb[1:2], nxt_mods, rc.mod_idx, 256)
            x[k] = res[0]
            if nxt is not None:
                hm[k] = res[1]

    return (x["p"].reshape(bp, sp, d), x["s"].reshape(bs, ss, d),
            outs["C_p"][None], outs["n_p"][None], outs["m_p"][None],
            outs["C_s"][None], outs["n_s"][None], outs["m_s"][None],
            outs["v_s"].reshape(bs, ss, -1)[None])
```

```python
import functools

import jax
import jax.numpy as jnp
from jax import lax
from jax.experimental import pallas as pl
from jax.experimental.pallas import tpu as pltpu

F32 = jnp.float32
BF16 = jnp.bfloat16
I32 = jnp.int32

DEPTH = 2
N_HEADS = 8
DK = 128
DV = 256
CHUNK = 128
N_GROUPS_B = 8
N_EXPERTS = 16
N_EXPERT_GROUPS = 4
EXPERTS_PER_GROUP = 4
ALPHA = float((2 * DEPTH) ** 0.25)
LN_EPS = 1e-5

LANES = 128
SUBLANES = 8
MIB = 1024 * 1024
MOE_TILE = 256
CAST_ROWS = 256
STAGE_ROWS = 128
META_TILE_EXPERT, META_FIRST, META_NEXT, META_N_USED, META_COUNT, META_OFFSET = range(6)


def _cparams(n_axes, vmem_mib):
    return pltpu.CompilerParams(
        dimension_semantics=("arbitrary",) * n_axes,
        vmem_limit_bytes=int(vmem_mib * MIB))


def _split3(x):
    hi = x.astype(BF16)
    r1 = x - hi.astype(F32)
    mid = r1.astype(BF16)
    lo = (r1 - mid.astype(F32)).astype(BF16)
    return hi, mid, lo


def _log_sigmoid(x):
    return jnp.minimum(x, 0.0) - jnp.log1p(jnp.exp(-jnp.abs(x)))


def _load_weight_bf16(w_hbm, wb, stg, sem):
    nj, k, tn = wb.shape
    ch = stg.shape[1]
    nch = k // ch

    def chunk(c):
        return pltpu.make_async_copy(w_hbm.at[pl.ds(c * ch, ch), pl.ds(0, nj * tn)], stg.at[c % 2],
                                     sem.at[c % 2])

    chunk(0).start()
    for c in range(nch):
        if c + 1 < nch:
            chunk(c + 1).start()
        chunk(c).wait()
        for jj in range(nj):
            wb[jj, c * ch:(c + 1) * ch, :] = stg[c % 2, :, jj * tn:(jj + 1) * tn].astype(BF16)


def _adaln_kernel(c_ref, w_ref, b_ref, o_ref):
    c = c_ref[...]
    a = (c * jax.nn.sigmoid(c)).astype(BF16)
    o_ref[0] = jnp.dot(a, w_ref[0].astype(BF16), preferred_element_type=F32) + b_ref[0]


def _adaln(c_all, w_ada, b_ada):
    depth, d, n = w_ada.shape
    r = c_all.shape[0]
    tn = 1024
    return pl.pallas_call(
        _adaln_kernel,
        grid=(depth, n // tn),
        in_specs=[pl.BlockSpec((r, d), lambda l, j: (0, 0)),
                  pl.BlockSpec((1, d, tn), lambda l, j: (l, 0, j)),
                  pl.BlockSpec((1, 1, tn), lambda l, j: (l, 0, j))],
        out_specs=pl.BlockSpec((1, r, tn), lambda l, j: (l, 0, j)),
        out_shape=jax.ShapeDtypeStruct((depth, r, n), F32),
        compiler_params=_cparams(2, 32),
        name="adaln",
    )(c_all, w_ada, b_ada.reshape(depth, 1, n))


def _inproj_kernel(x_ref, sh_ref, sc_ref, w_hbm, proj_ref, gates_ref, hin_ref, wb, stg, g16, wgs, sem):
    j = pl.program_id(1)

    @pl.when((pl.program_id(0) == 0) & (j == 0))
    def _():
        _load_weight_bf16(w_hbm, wb, stg, sem)
        n_main = wb.shape[0] * wb.shape[2]
        cp = pltpu.make_async_copy(w_hbm.at[:, pl.ds(n_main, 2 * N_HEADS)], g16, sem.at[0])
        cp.start()
        cp.wait()
        g = g16[...]
        z = jnp.zeros((g.shape[0], LANES - N_HEADS), F32)
        wg = jnp.concatenate([g[:, :N_HEADS], z, g[:, N_HEADS:], z], axis=1)
        hi = wg.astype(BF16)
        wgs[0] = hi
        wgs[1] = (wg - hi.astype(F32)).astype(BF16)

    @pl.when(j == 0)
    def _():
        h = x_ref[...] * (1.0 + sc_ref[0]) + sh_ref[0]
        hb = h.astype(BF16)
        hin_ref[...] = hb
        h_lo = (h - hb.astype(F32)).astype(BF16)
        gates_ref[...] = (jnp.dot(hb, wgs[0], preferred_element_type=F32)
                          + jnp.dot(h_lo, wgs[0], preferred_element_type=F32)
                          + jnp.dot(hb, wgs[1], preferred_element_type=F32))

    proj_ref[...] = jnp.dot(hin_ref[...], wb[j], preferred_element_type=F32).astype(proj_ref.dtype)


def _inproj(x, sh, sc, mod_idx, w_in, n_main, tm, out_dtype):
    t, d = x.shape
    tn = 1024
    r = sh.shape[1]
    mspec = pl.BlockSpec((1, r, d), lambda i, j: (mod_idx(i), 0, 0))
    return pl.pallas_call(
        _inproj_kernel,
        grid=(t // tm, n_main // tn),
        in_specs=[pl.BlockSpec((tm, d), lambda i, j: (i, 0)), mspec, mspec,
                  pl.BlockSpec(memory_space=pl.ANY)],
        out_specs=[pl.BlockSpec((tm, tn), lambda i, j: (i, j)),
                   pl.BlockSpec((tm, 2 * LANES), lambda i, j: (i, 0))],
        out_shape=[jax.ShapeDtypeStruct((t, n_main), out_dtype),
                   jax.ShapeDtypeStruct((t, 2 * LANES), F32)],
        scratch_shapes=[pltpu.VMEM((tm, d), BF16), pltpu.VMEM((n_main // tn, d, tn), BF16),
                        pltpu.VMEM((2, STAGE_ROWS, n_main), F32), pltpu.VMEM((d, 2 * N_HEADS), F32),
                        pltpu.VMEM((2, d, 2 * LANES), BF16), pltpu.SemaphoreType.DMA((2,))],
        compiler_params=_cparams(2, 56),
        name="mlstm_inproj",
    )(x, sh, sc, w_in)


def _head_norm_gate(hh, nw, o):
    mu = jnp.mean(hh, axis=1, keepdims=True)
    xc = hh - mu
    var = jnp.mean(xc * xc, axis=1, keepdims=True)
    return jax.nn.sigmoid(o) * (xc * lax.rsqrt(var + LN_EPS) * nw)


def _mlstm_chunk_kernel(q_ref, k_ref, v_ref, o_ref, g_ref, gb_ref, nw_ref, c0_ref, n0_ref, m0_ref,
                        y_ref, cout_ref, nout_ref, mout_ref, c_scr, n_scr, m_scr):
    ci = pl.program_id(1)
    L = q_ref.shape[0]

    @pl.when(ci == 0)
    def _():
        c_scr[...] = c0_ref[0]
        n_scr[...] = n0_ref[0]
        m_scr[...] = m0_ref[0]

    g = g_ref[...] + gb_ref[...]
    gi = g[:, :LANES]
    lf = _log_sigmoid(g[:, LANES:])
    row = lax.broadcasted_iota(I32, (L, L), 0)
    col = lax.broadcasted_iota(I32, (L, L), 1)
    causal = col <= row
    ltri = jnp.where(causal, 1.0, 0.0).astype(BF16)
    hi, mid, lo = _split3(lf)
    bcum = (jnp.dot(ltri, hi, preferred_element_type=F32)
            + jnp.dot(ltri, mid, preferred_element_type=F32)
            + jnp.dot(ltri, lo, preferred_element_type=F32))
    a = gi - bcum
    a_t = a.T
    scale = DK ** -0.5
    nt = (((1,), (1,)), ((), ()))
    tn_dims = (((0,), (0,)), ((), ()))
    for h in range(N_HEADS):
        ks = slice(h * DK, (h + 1) * DK)
        vs = slice(h * DV, (h + 1) * DV)
        qf = q_ref[:, ks].astype(F32) * scale
        qb = qf.astype(BF16)
        kf = k_ref[:, ks].astype(F32)
        kb = kf.astype(BF16)
        vb = v_ref[:, vs].astype(BF16)
        a_row = a_t[h:h + 1, :]
        a_col = a[:, h:h + 1]
        b_col = bcum[:, h:h + 1]
        m_prev = m_scr[h:h + 1, 0:1]
        amat = jnp.where(causal, a_row, -jnp.inf)
        mx = jnp.max(amat, axis=1, keepdims=True)
        m_inter = b_col + m_prev
        m_t = jnp.maximum(m_inter, b_col + mx)
        dm = jnp.exp(amat + (b_col - m_t))
        s = lax.dot_general(qb, kb, nt, preferred_element_type=F32)
        scores = s * dm
        inter = jnp.exp(m_inter - m_t)
        c_old = c_scr[h]
        n_old = n_scr[h:h + 1, :]
        qc = jnp.dot(qb, c_old.astype(BF16), preferred_element_type=F32)
        num = jnp.dot(scores.astype(BF16), vb, preferred_element_type=F32) + inter * qc
        qn = jnp.sum(qf * n_old, axis=1, keepdims=True)
        den = jnp.sum(scores, axis=1, keepdims=True) + inter * qn
        hh = num / jnp.maximum(jnp.abs(den), jnp.exp(-m_t))
        m_new = m_t[L - 1:L, :]
        b_last = b_col[L - 1:L, :]
        w_col = jnp.exp(b_last + a_col - m_new)
        decay = jnp.exp(b_last + m_prev - m_new)
        kw = kf * w_col
        c_scr[h] = decay * c_old + lax.dot_general(kw.astype(BF16), vb, tn_dims,
                                                   preferred_element_type=F32)
        n_scr[h:h + 1, :] = decay * n_old + jnp.sum(kw, axis=0, keepdims=True)
        m_scr[h:h + 1, :] = jnp.broadcast_to(m_new, (1, LANES))
        y_ref[:, vs] = _head_norm_gate(hh, nw_ref[:, vs], o_ref[:, vs].astype(F32)).astype(y_ref.dtype)

    @pl.when(ci == pl.num_programs(1) - 1)
    def _():
        cout_ref[0] = c_scr[...]
        nout_ref[0] = n_scr[...]
        mout_ref[0] = m_scr[...]


def _mlstm_chunks(proj, gates, gb, nw, c0, n0, m0, batch, seq):
    nc = seq // CHUNK
    qk = N_HEADS * DK
    vd = N_HEADS * DV
    row = lambda b, c: b * nc + c
    st4 = pl.BlockSpec((1, N_HEADS, DK, DV), lambda b, c: (b, 0, 0, 0))
    st3 = pl.BlockSpec((1, N_HEADS, LANES), lambda b, c: (b, 0, 0))
    return pl.pallas_call(
        _mlstm_chunk_kernel,
        grid=(batch, nc),
        in_specs=[pl.BlockSpec((CHUNK, qk), lambda b, c: (row(b, c), 0)),
                  pl.BlockSpec((CHUNK, qk), lambda b, c: (row(b, c), 1)),
                  pl.BlockSpec((CHUNK, vd), lambda b, c: (row(b, c), 1)),
                  pl.BlockSpec((CHUNK, vd), lambda b, c: (row(b, c), 2)),
                  pl.BlockSpec((CHUNK, 2 * LANES), lambda b, c: (row(b, c), 0)),
                  pl.BlockSpec((1, 2 * LANES), lambda b, c: (0, 0)),
                  pl.BlockSpec((1, vd), lambda b, c: (0, 0)),
                  st4, st3, st3],
        out_specs=[pl.BlockSpec((CHUNK, vd), lambda b, c: (row(b, c), 0)), st4, st3, st3],
        out_shape=[jax.ShapeDtypeStruct((batch * seq, vd), BF16),
                   jax.ShapeDtypeStruct((batch, N_HEADS, DK, DV), F32),
                   jax.ShapeDtypeStruct((batch, N_HEADS, LANES), F32),
                   jax.ShapeDtypeStruct((batch, N_HEADS, LANES), F32)],
        scratch_shapes=[pltpu.VMEM((N_HEADS, DK, DV), F32),
                        pltpu.VMEM((N_HEADS, LANES), F32),
                        pltpu.VMEM((N_HEADS, LANES), F32)],
        compiler_params=_cparams(2, 40),
        name="mlstm_chunks",
    )(proj, proj, proj, proj, gates, gb, nw, c0, n0, m0)


def _per_head(x, width):
    return jnp.concatenate(
        [jnp.broadcast_to(x[..., h:h + 1], x.shape[:-1] + (width,)) for h in range(N_HEADS)], axis=-1)


def _head_sums(x, width):
    lane = lax.broadcasted_iota(I32, x.shape[:-1] + (LANES,), x.ndim - 1)
    out = jnp.zeros(x.shape[:-1] + (LANES,), F32)
    for h in range(N_HEADS):
        s = jnp.sum(x[..., h * width:(h + 1) * width], axis=-1, keepdims=True)
        out = jnp.where(lane == h, s, out)
    return out


def _mlstm_short_kernel(seq, q_ref, k_ref, v_ref, o_ref, g_ref, gb_ref, nw_ref, c0_ref, n0_ref, m0_ref,
                        y_ref, cout_ref, nout_ref, mout_ref, qc_scr, kw_scr):
    bt, rows, _ = q_ref.shape
    scale = DK ** -0.5
    tn_dims = (((0,), (0,)), ((), ()))
    row = lax.broadcasted_iota(I32, (bt, rows, LANES), 1)
    g = g_ref[...] + gb_ref[...]
    gi = g[:, :, :LANES]
    lf = _log_sigmoid(g[:, :, LANES:])
    bcum = jnp.zeros_like(lf)
    for s in range(seq):
        bcum = bcum + jnp.where(row >= s, lf[:, s:s + 1, :], 0.0)
    a = gi - bcum
    mx = jnp.full_like(a, -jnp.inf)
    for s in range(seq):
        mx = jnp.maximum(mx, jnp.where(row >= s, a[:, s:s + 1, :], -jnp.inf))
    m_prev = m0_ref[...]
    m_inter = bcum + m_prev
    m_t = jnp.maximum(m_inter, bcum + mx)
    cmt = bcum - m_t
    inter = jnp.exp(m_inter - m_t)
    einv = jnp.exp(-m_t)
    m_new = m_t[:, seq - 1:seq, :]
    b_last = bcum[:, seq - 1:seq, :]
    w = jnp.where(row < seq, jnp.exp(b_last + a - m_new), 0.0)
    decay = jnp.exp(b_last + m_prev - m_new)
    mout_ref[...] = m_new

    q = q_ref[...] * scale
    k = k_ref[...]
    v = v_ref[...]
    n_old = n0_ref[...]
    for b in range(bt):
        for h in range(N_HEADS):
            qc_scr[b, :, h * DV:(h + 1) * DV] = jnp.dot(
                (q_ref[b, :, h * DK:(h + 1) * DK] * scale).astype(BF16), c0_ref[b, h].astype(BF16),
                preferred_element_type=F32)
    den = inter * _head_sums(q * n_old, DK)
    num = _per_head(inter, DV) * qc_scr[...]
    for s in range(seq):
        p = _head_sums(q * k[:, s:s + 1, :], DK) * jnp.where(row >= s, jnp.exp(cmt + a[:, s:s + 1, :]), 0.0)
        den = den + p
        num = num + _per_head(p, DV) * v[:, s:s + 1, :]
    hh = num * _per_head(1.0 / jnp.maximum(jnp.abs(den), einv), DV)
    mu = _head_sums(hh, DV) * (1.0 / DV)
    xc = hh - _per_head(mu, DV)
    var = _head_sums(xc * xc, DV) * (1.0 / DV)
    hn = xc * _per_head(lax.rsqrt(var + LN_EPS), DV) * nw_ref[...]
    y_ref[...] = jax.nn.sigmoid(o_ref[...]) * hn

    kw = k * _per_head(w, DK)
    kw_scr[...] = kw
    nout_ref[...] = _per_head(decay, DK) * n_old + jnp.sum(kw, axis=1, keepdims=True)
    for b in range(bt):
        for h in range(N_HEADS):
            cout_ref[b, h] = (decay[b, :, h:h + 1] * c0_ref[b, h]
                              + lax.dot_general(kw_scr[b, :, h * DK:(h + 1) * DK],
                                                v_ref[b, :, h * DV:(h + 1) * DV],
                                                tn_dims, preferred_element_type=F32))


def _mlstm_short(proj, gates, gb, nw, c0, n0, m0, seq):
    batch, rows, _ = proj.shape
    qk = N_HEADS * DK
    vd = N_HEADS * DV
    bt = 8
    st4 = pl.BlockSpec((bt, N_HEADS, DK, DV), lambda i: (i, 0, 0, 0))
    st3 = pl.BlockSpec((bt, 1, qk), lambda i: (i, 0, 0))
    stm = pl.BlockSpec((bt, 1, LANES), lambda i: (i, 0, 0))
    return pl.pallas_call(
        functools.partial(_mlstm_short_kernel, seq),
        grid=(batch // bt,),
        in_specs=[pl.BlockSpec((bt, rows, qk), lambda i: (i, 0, 0)),
                  pl.BlockSpec((bt, rows, qk), lambda i: (i, 0, 1)),
                  pl.BlockSpec((bt, rows, vd), lambda i: (i, 0, 1)),
                  pl.BlockSpec((bt, rows, vd), lambda i: (i, 0, 2)),
                  pl.BlockSpec((bt, rows, 2 * LANES), lambda i: (i, 0, 0)),
                  pl.BlockSpec((1, 2 * LANES), lambda i: (0, 0)),
                  pl.BlockSpec((1, vd), lambda i: (0, 0)),
                  st4, st3, stm],
        out_specs=[pl.BlockSpec((bt, rows, vd), lambda i: (i, 0, 0)), st4, st3, stm],
        out_shape=[jax.ShapeDtypeStruct((batch, rows, vd), F32),
                   jax.ShapeDtypeStruct((batch, N_HEADS, DK, DV), F32),
                   jax.ShapeDtypeStruct((batch, 1, qk), F32),
                   jax.ShapeDtypeStruct((batch, 1, LANES), F32)],
        scratch_shapes=[pltpu.VMEM((bt, rows, vd), F32), pltpu.VMEM((bt, rows, qk), F32)],
        compiler_params=_cparams(1, 48),
        name="mlstm_short",
    )(proj, proj, proj, proj, gates, gb, nw, c0, n0, m0)


def _residual_ln(zbuf, xres_ref, gate_ref, lng_ref, lnb_ref, xo_ref, nxt):
    nc, tm, tn = zbuf.shape
    inv_d = 1.0 / (nc * tn)
    ssum = jnp.zeros((tm, 1), F32)
    for c in range(nc):
        sl = pl.ds(c * tn, tn)
        z = ALPHA * xres_ref[:, sl] + gate_ref[0, :, sl] * zbuf[c]
        zbuf[c] = z
        ssum = ssum + jnp.sum(z, axis=1, keepdims=True)
    mu = ssum * inv_d
    vsum = jnp.zeros((tm, 1), F32)
    for c in range(nc):
        zc = zbuf[c] - mu
        vsum = vsum + jnp.sum(zc * zc, axis=1, keepdims=True)
    rstd = lax.rsqrt(vsum * inv_d + LN_EPS)
    for c in range(nc):
        sl = pl.ds(c * tn, tn)
        xn = (zbuf[c] - mu) * rstd * lng_ref[:, sl] + lnb_ref[:, sl]
        xo_ref[:, sl] = xn
        if nxt is not None:
            sh_ref, sc_ref, ho_ref = nxt
            ho_ref[:, sl] = (xn * (1.0 + sc_ref[0, :, sl]) + sh_ref[0, :, sl]).astype(ho_ref.dtype)


def _proj_ln_kernel(lhs_ref, w_hbm, xres_ref, gate_ref, lng_ref, lnb_ref, shn_ref, scn_ref,
                    xo_ref, ho_ref, wb, stg, ybuf, sem):
    @pl.when(pl.program_id(0) == 0)
    def _():
        _load_weight_bf16(w_hbm, wb, stg, sem)

    lhs = lhs_ref[...].astype(BF16)
    for c in range(ybuf.shape[0]):
        ybuf[c] = jnp.dot(lhs, wb[c], preferred_element_type=F32)
    _residual_ln(ybuf, xres_ref, gate_ref, lng_ref, lnb_ref, xo_ref, (shn_ref, scn_ref, ho_ref))


def _proj_ln(lhs, w, xres, gate, lng, lnb, shn, scn, mod_idx, tm, tn):
    t, k = lhs.shape
    d = w.shape[1]
    r = gate.shape[1]
    mspec = pl.BlockSpec((1, r, d), lambda i: (mod_idx(i), 0, 0))
    vspec = pl.BlockSpec((1, d), lambda i: (0, 0))
    ospec = pl.BlockSpec((tm, d), lambda i: (i, 0))
    return pl.pallas_call(
        _proj_ln_kernel,
        grid=(t // tm,),
        in_specs=[pl.BlockSpec((tm, k), lambda i: (i, 0)),
                  pl.BlockSpec(memory_space=pl.ANY),
                  ospec, mspec, vspec, vspec, mspec, mspec],
        out_specs=[ospec, ospec],
        out_shape=[jax.ShapeDtypeStruct((t, d), F32), jax.ShapeDtypeStruct((t, d), F32)],
        scratch_shapes=[pltpu.VMEM((d // tn, k, tn), BF16), pltpu.VMEM((2, CAST_ROWS, d), F32),
                        pltpu.VMEM((d // tn, tm, tn), F32), pltpu.SemaphoreType.DMA((2,))],
        compiler_params=_cparams(1, 58),
        name="proj_ln",
    )(lhs, w, xres, gate, lng, lnb, shn, scn)


def _gelu_tanh(x):
    return x * (0.5 * (1.0 + jnp.tanh(0.7978845608028654 * (x + 0.044715 * (x * x * x)))))


def _gmlp_in_kernel(h_ref, w_hbm, b_ref, o_ref, wb, stg, sem):
    j = pl.program_id(1)

    @pl.when((pl.program_id(0) == 0) & (j == 0))
    def _():
        _load_weight_bf16(w_hbm, wb, stg, sem)

    acc = jnp.dot(h_ref[...].astype(BF16), wb[j], preferred_element_type=F32)
    o_ref[...] = _gelu_tanh(acc + b_ref[...]).astype(o_ref.dtype)


def _gmlp_in(h, w, b, tm, out_dtype):
    t, d = h.shape
    n = w.shape[1]
    tn = 1024
    return pl.pallas_call(
        _gmlp_in_kernel,
        grid=(t // tm, n // tn),
        in_specs=[pl.BlockSpec((tm, d), lambda i, j: (i, 0)),
                  pl.BlockSpec(memory_space=pl.ANY),
                  pl.BlockSpec((1, tn), lambda i, j: (0, j))],
        out_specs=pl.BlockSpec((tm, tn), lambda i, j: (i, j)),
        out_shape=jax.ShapeDtypeStruct((t, n), out_dtype),
        scratch_shapes=[pltpu.VMEM((n // tn, d, tn), BF16), pltpu.VMEM((2, STAGE_ROWS, n), F32),
                        pltpu.SemaphoreType.DMA((2,))],
        compiler_params=_cparams(2, 58),
        name="gmlp_in",
    )(h, w, b.reshape(1, n))


def _gmlp_mix_rows(u_ref, v_ref, g_ref, b_ref, mix_ref, bias_ref, o_ref, vn_ref):
    v = v_ref[...].astype(F32)
    mu = jnp.mean(v, axis=1, keepdims=True)
    xc = v - mu
    var = jnp.mean(xc * xc, axis=1, keepdims=True)
    vn = xc * lax.rsqrt(var + LN_EPS) * g_ref[...] + b_ref[...]
    if vn_ref is not None:
        vn_ref[...] = vn
    gd = v.shape[1] // N_GROUPS_B
    for g in range(N_GROUPS_B):
        sl = slice(g * gd, (g + 1) * gd)
        mixed = jnp.dot(mix_ref[g], vn[:, sl].astype(BF16), preferred_element_type=F32) + bias_ref[g]
        o_ref[:, sl] = (u_ref[:, sl].astype(F32) * mixed).astype(o_ref.dtype)


def _mix_proj_ln_kernel(emit_v, u_ref, v_ref, g_ref, b_ref, mix_ref, bias_ref, w_hbm, xres_ref, gate_ref,
                        lng_ref, lnb_ref, shn_ref, scn_ref, *rest):
    if emit_v:
        xo_ref, ho_ref, vn_ref, lhs_scr, wb, stg, ybuf, sem = rest
    else:
        xo_ref, ho_ref, lhs_scr, wb, stg, ybuf, sem = rest
        vn_ref = None

    @pl.when(pl.program_id(0) == 0)
    def _():
        _load_weight_bf16(w_hbm, wb, stg, sem)

    _gmlp_mix_rows(u_ref, v_ref, g_ref, b_ref, mix_ref, bias_ref, lhs_scr, vn_ref)
    lhs = lhs_scr[...]
    for c in range(ybuf.shape[0]):
        ybuf[c] = jnp.dot(lhs, wb[c], preferred_element_type=F32)
    _residual_ln(ybuf, xres_ref, gate_ref, lng_ref, lnb_ref, xo_ref, (shn_ref, scn_ref, ho_ref))


def _mix_proj_ln(uv, nv_g, nv_b, mix, bias, w, xres, gate, lng, lnb, shn, scn, mod_idx, tm, tn, emit_v):
    t, n2 = uv.shape
    di = n2 // 2
    d = w.shape[1]
    r = gate.shape[1]
    mspec = pl.BlockSpec((1, r, d), lambda i: (mod_idx(i), 0, 0))
    vspec = pl.BlockSpec((1, d), lambda i: (0, 0))
    ospec = pl.BlockSpec((tm, d), lambda i: (i, 0))
    nspec = pl.BlockSpec((1, di), lambda i: (0, 0))
    out_specs = [ospec, ospec]
    out_shape = [jax.ShapeDtypeStruct((t, d), F32), jax.ShapeDtypeStruct((t, d), F32)]
    if emit_v:
        out_specs.append(pl.BlockSpec((tm, di), lambda i: (i, 0)))
        out_shape.append(jax.ShapeDtypeStruct((t, di), F32))
    return pl.pallas_call(
        functools.partial(_mix_proj_ln_kernel, emit_v),
        grid=(t // tm,),
        in_specs=[pl.BlockSpec((tm, di), lambda i: (i, 0)),
                  pl.BlockSpec((tm, di), lambda i: (i, 1)),
                  nspec, nspec,
                  pl.BlockSpec(mix.shape, lambda i: (0, 0, 0)),
                  pl.BlockSpec(bias.shape, lambda i: (0, 0, 0)),
                  pl.BlockSpec(memory_space=pl.ANY),
                  ospec, mspec, vspec, vspec, mspec, mspec],
        out_specs=out_specs,
        out_shape=out_shape,
        scratch_shapes=[pltpu.VMEM((tm, di), BF16), pltpu.VMEM((d // tn, di, tn), BF16),
                        pltpu.VMEM((2, CAST_ROWS, d), F32), pltpu.VMEM((d // tn, tm, tn), F32),
                        pltpu.SemaphoreType.DMA((2,))],
        compiler_params=_cparams(1, 58),
        name="gmlp_mix_proj_ln",
    )(uv, uv, nv_g.reshape(1, di), nv_b.reshape(1, di), mix, bias, w, xres, gate, lng, lnb, shn, scn)


def _router_kernel(n_first, ha_ref, hb_ref, wr_ref, br_ref, idx_ref, wts_ref, cnt_ref, carry):
    i = pl.program_id(0)
    tr = ha_ref.shape[0]

    @pl.when(i == 0)
    def _():
        carry[...] = jnp.zeros(carry.shape, carry.dtype)

    nt = (((1,), (1,)), ((), ()))
    h = jnp.where(i < n_first, ha_ref[...], hb_ref[...])
    hb = h.astype(BF16)
    hl = (h - hb.astype(F32)).astype(BF16)
    wr = wr_ref[...]
    wb = wr.astype(BF16)
    wl = (wr - wb.astype(F32)).astype(BF16)
    logits = (lax.dot_general(wb, hb, nt, preferred_element_type=F32)
              + lax.dot_general(wb, hl, nt, preferred_element_type=F32)
              + lax.dot_general(wl, hb, nt, preferred_element_type=F32))
    s = jax.nn.sigmoid(logits)
    sel = s + br_ref[...]
    epg = EXPERTS_PER_GROUP
    r = [sel[e:e + 1, :] for e in range(N_EXPERTS)]
    su = [s[e:e + 1, :] for e in range(N_EXPERTS)]

    def top2sum(v):
        best = v[0] + v[1]
        for x in range(epg):
            for y in range(x + 1, epg):
                if (x, y) != (0, 1):
                    best = jnp.maximum(best, v[x] + v[y])
        return best

    gs = [top2sum(r[g * epg:(g + 1) * epg]) for g in range(N_EXPERT_GROUPS)]
    gbest = gs[0]
    gidx = jnp.zeros((1, tr), I32)
    for g in range(1, N_EXPERT_GROUPS):
        better = gs[g] > gbest
        gidx = jnp.where(better, g, gidx)
        gbest = jnp.where(better, gs[g], gbest)
    v = list(r[:epg])
    sv = list(su[:epg])
    for g in range(1, N_EXPERT_GROUPS):
        pick = gidx == g
        for x in range(epg):
            v[x] = jnp.where(pick, r[g * epg + x], v[x])
            sv[x] = jnp.where(pick, su[g * epg + x], sv[x])
    i1 = jnp.zeros((1, tr), I32)
    b1 = v[0]
    w1 = sv[0]
    for x in range(1, epg):
        better = v[x] > b1
        i1 = jnp.where(better, x, i1)
        b1 = jnp.where(better, v[x], b1)
        w1 = jnp.where(better, sv[x], w1)
    i2 = jnp.zeros((1, tr), I32)
    b2 = jnp.full((1, tr), -jnp.inf, F32)
    w2 = jnp.zeros((1, tr), F32)
    for x in range(epg):
        take = (i1 != x) & (v[x] > b2)
        i2 = jnp.where(take, x, i2)
        b2 = jnp.where(take, v[x], b2)
        w2 = jnp.where(take, sv[x], w2)
    wsum = w1 + w2
    e1 = gidx * epg + i1
    e2 = gidx * epg + i2
    eio = lax.broadcasted_iota(I32, (N_EXPERTS, tr), 0)
    hit1 = eio == e1
    hit2 = eio == e2
    oh = jnp.where(hit1 | hit2, 1.0, 0.0)
    ri = lax.broadcasted_iota(I32, (tr, tr), 0)
    cj = lax.broadcasted_iota(I32, (tr, tr), 1)
    before = jnp.where(ri < cj, 1.0, 0.0).astype(BF16)
    rank = jnp.dot(oh.astype(BF16), before, preferred_element_type=F32) + carry[:, 0:1]
    rank1 = jnp.sum(jnp.where(hit1, rank, 0.0), axis=0, keepdims=True)
    rank2 = jnp.sum(jnp.where(hit2, rank, 0.0), axis=0, keepdims=True)
    idx_ref[0:1, :] = e1
    idx_ref[1:2, :] = e2
    idx_ref[2:3, :] = rank1.astype(I32)
    idx_ref[3:4, :] = rank2.astype(I32)
    idx_ref[4:8, :] = jnp.zeros((4, tr), I32)
    wts_ref[0:1, :] = w1 / wsum
    wts_ref[1:2, :] = w2 / wsum
    wts_ref[2:8, :] = jnp.zeros((6, tr), F32)
    carry[...] = carry[...] + jnp.sum(oh, axis=1, keepdims=True)
    cnt_ref[...] = carry[...]


def _two_set_specs(ha, hb, tile):
    d = ha.shape[1]
    na = ha.shape[0] // tile
    nb = hb.shape[0] // tile
    return na, nb, [pl.BlockSpec((tile, d), lambda i, *_: (jnp.minimum(i, na - 1), 0)),
                    pl.BlockSpec((tile, d), lambda i, *_: (jnp.maximum(i - na, 0), 0))]


def _router(ha, hb, w_router, b_router, tr):
    d = ha.shape[1]
    t = ha.shape[0] + hb.shape[0]
    na, nb, hspecs = _two_set_specs(ha, hb, tr)
    return pl.pallas_call(
        functools.partial(_router_kernel, na),
        grid=(na + nb,),
        in_specs=hspecs + [pl.BlockSpec((N_EXPERTS, d), lambda i: (0, 0)),
                           pl.BlockSpec((N_EXPERTS, 1), lambda i: (0, 0))],
        out_specs=[pl.BlockSpec((SUBLANES, tr), lambda i: (0, i)),
                   pl.BlockSpec((SUBLANES, tr), lambda i: (0, i)),
                   pl.BlockSpec((N_EXPERTS, LANES), lambda i: (0, 0))],
        out_shape=[jax.ShapeDtypeStruct((SUBLANES, t), I32),
                   jax.ShapeDtypeStruct((SUBLANES, t), F32),
                   jax.ShapeDtypeStruct((N_EXPERTS, LANES), F32)],
        scratch_shapes=[pltpu.VMEM((N_EXPERTS, LANES), F32)],
        compiler_params=_cparams(1, 32),
        name="moe_router",
    )(ha, hb, w_router.T, b_router.reshape(N_EXPERTS, 1))


def _scatter_kernel(n_tok, n_first, pos_ref, meta_ref, ha_ref, hb_ref, o_hbm, zero_scr, sem):
    i = pl.program_id(0)
    ts = ha_ref.shape[0]
    base = i * ts

    def row_copy(src, r, p):
        return pltpu.make_async_copy(src.at[pl.ds(r, 1)], o_hbm.at[pl.ds(p, 1)], sem)

    def scatter_tile(h_ref):
        def start_rows(r, c):
            row_copy(h_ref, r, pos_ref[base + r]).start()
            row_copy(h_ref, r, pos_ref[n_tok + base + r]).start()
            return c

        lax.fori_loop(0, ts, start_rows, 0, unroll=8)
        for _ in range(2):
            pltpu.make_async_copy(h_ref, o_hbm.at[pl.ds(0, ts)], sem).wait()

    @pl.when(i < n_first)
    def _():
        scatter_tile(ha_ref)

    @pl.when(i >= n_first)
    def _():
        scatter_tile(hb_ref)

    @pl.when(i == 0)
    def _():
        zero_scr[...] = jnp.zeros(zero_scr.shape, zero_scr.dtype)
        nz = zero_scr.shape[0]

        def block_copy(p):
            return pltpu.make_async_copy(zero_scr, o_hbm.at[pl.ds(pl.multiple_of(p, nz), nz)], sem)

        for e in range(N_EXPERTS + 1):
            lo = meta_ref[META_OFFSET, e] + meta_ref[META_COUNT, e]
            hi = meta_ref[META_OFFSET, e + 1]
            mid = jnp.minimum(((lo + (nz - 1)) // nz) * nz, hi)

            def start_row(p, c):
                row_copy(zero_scr, 0, p).start()
                return c

            def wait_row(p, c):
                row_copy(zero_scr, 0, 0).wait()
                return c

            def start_block(b, c):
                block_copy(mid + b * nz).start()
                return c

            def wait_block(b, c):
                block_copy(0).wait()
                return c

            lax.fori_loop(lo, mid, start_row, 0)
            lax.fori_loop(lo, mid, wait_row, 0)
            nblk = (hi - mid) // nz
            lax.fori_loop(0, nblk, start_block, 0)
            lax.fori_loop(0, nblk, wait_block, 0)


def _scatter_rows(ha, hb, pos, meta, n_rows, ts):
    d = ha.shape[1]
    na, nb, hspecs = _two_set_specs(ha, hb, ts)
    return pl.pallas_call(
        functools.partial(_scatter_kernel, ha.shape[0] + hb.shape[0], na),
        grid_spec=pltpu.PrefetchScalarGridSpec(
            num_scalar_prefetch=2,
            grid=(na + nb,),
            in_specs=hspecs,
            out_specs=pl.BlockSpec(memory_space=pl.ANY),
            scratch_shapes=[pltpu.VMEM((SUBLANES, d), ha.dtype), pltpu.SemaphoreType.DMA]),
        out_shape=jax.ShapeDtypeStruct((n_rows, d), ha.dtype),
        compiler_params=_cparams(1, 32),
        name="moe_scatter",
    )(pos, meta, ha, hb)


def _cast_rows(src, dst):
    def body(c, carry):
        r = pl.multiple_of(c * CAST_ROWS, CAST_ROWS)
        dst[pl.ds(r, CAST_ROWS), :] = src[pl.ds(r, CAST_ROWS), :].astype(BF16)
        return carry

    lax.fori_loop(0, src.shape[0] // CAST_ROWS, body, 0)


def _expert_kernel(layer, meta_ref, x_ref, w1_hbm, w2_hbm, y_ref, stg1, stg2, wb1, wb2, sem):
    i = pl.program_id(0)

    def fetch(e):
        return (pltpu.make_async_copy(w1_hbm.at[layer, e], stg1, sem.at[0]),
                pltpu.make_async_copy(w2_hbm.at[layer, e], stg2, sem.at[1]))

    @pl.when(i == 0)
    def _():
        for cp in fetch(meta_ref[META_TILE_EXPERT, 0]):
            cp.start()

    @pl.when(meta_ref[META_FIRST, i] == 1)
    def _():
        for cp in fetch(0):
            cp.wait()
        _cast_rows(stg1, wb1)
        _cast_rows(stg2, wb2)

        @pl.when(meta_ref[META_NEXT, i] >= 0)
        def _():
            for cp in fetch(meta_ref[META_NEXT, i]):
                cp.start(priority=1)

    @pl.when(i < meta_ref[META_N_USED, 0])
    def _():
        hid = jnp.dot(x_ref[...].astype(BF16), wb1[...], preferred_element_type=F32)
        f = hid.shape[1] // 2
        a = hid[:, :f]
        g = hid[:, f:]
        z = (g * jax.nn.sigmoid(g) * a).astype(BF16)
        y_ref[...] = jnp.dot(z, wb2[...], preferred_element_type=F32)

    @pl.when(i >= meta_ref[META_N_USED, 0])
    def _():
        y_ref[...] = jnp.zeros(y_ref.shape, y_ref.dtype)


def _experts(xs, w1, w2, layer, meta):
    p, d = xs.shape
    tm = MOE_TILE
    ff2 = w1.shape[3]
    ff = ff2 // 2
    row = lambda i, meta: (jnp.minimum(i, meta[META_N_USED, 0] - 1), 0)
    return pl.pallas_call(
        functools.partial(_expert_kernel, layer),
        grid_spec=pltpu.PrefetchScalarGridSpec(
            num_scalar_prefetch=1,
            grid=(p // tm,),
            in_specs=[pl.BlockSpec((tm, d), row),
                      pl.BlockSpec(memory_space=pl.ANY),
                      pl.BlockSpec(memory_space=pl.ANY)],
            out_specs=pl.BlockSpec((tm, d), lambda i, *_: (i, 0)),
            scratch_shapes=[pltpu.VMEM((d, ff2), F32), pltpu.VMEM((ff, d), F32),
                            pltpu.VMEM((d, ff2), BF16), pltpu.VMEM((ff, d), BF16),
                            pltpu.SemaphoreType.DMA((2,))]),
        out_shape=jax.ShapeDtypeStruct((p, d), F32),
        compiler_params=_cparams(1, 56),
        name="moe_experts",
    )(meta, xs, w1, w2)


def _combine_kernel(n_tok, tok_base, has_next, pos_ref, w_ref, y_hbm, xres_ref, gate_ref, lng_ref, lnb_ref,
                    *rest):
    if has_next:
        shn_ref, scn_ref, xo_ref, ho_ref, gbuf, fbuf, sem = rest
        nxt = (shn_ref, scn_ref, ho_ref)
    else:
        xo_ref, gbuf, fbuf, sem = rest
        nxt = None
    tc = xres_ref.shape[0]
    i = pl.program_id(0)
    n_steps = pl.num_programs(0)

    n_slots = gbuf.shape[0]
    ahead = n_slots - 1

    def row_copy(step, slot, r, k):
        p = pos_ref[k * n_tok + tok_base + step * tc + r]
        return pltpu.make_async_copy(y_hbm.at[pl.ds(p, 1)], gbuf.at[slot, k, pl.ds(r, 1)], sem.at[slot])

    def wait_tile(slot):
        for k in range(2):
            pltpu.make_async_copy(y_hbm.at[pl.ds(0, tc)], gbuf.at[slot, k], sem.at[slot]).wait()

    @pl.when(i == 0)
    def _():
        for a in range(ahead):
            def start_rows(r, c, a=a):
                for k in range(2):
                    row_copy(jnp.minimum(a, n_steps - 1), a, r, k).start()
                return c

            lax.fori_loop(0, tc, start_rows, 0, unroll=8)

    slot = i % n_slots
    wait_tile(slot)
    fbuf[0] = w_ref[:, 0:1] * gbuf[slot, 0] + w_ref[:, 1:2] * gbuf[slot, 1]
    _residual_ln(fbuf, xres_ref, gate_ref, lng_ref, lnb_ref, xo_ref, nxt)

    nxt_step = jnp.minimum(i + ahead, n_steps - 1)
    nxt_slot = (i + ahead) % n_slots
    for r in range(tc):
        for k in range(2):
            row_copy(nxt_step, nxt_slot, r, k).start()

    @pl.when(i == n_steps - 1)
    def _():
        for a in range(1, ahead + 1):
            wait_tile((i + a) % n_slots)


def _combine_ln(ys, pos, wts, tok_base, xres, gate, lng, lnb, nxt_mods, mod_idx, tc):
    t, d = xres.shape
    r = gate.shape[1]
    has_next = nxt_mods is not None
    mspec = pl.BlockSpec((1, r, d), lambda i, *_: (mod_idx(i), 0, 0))
    vspec = pl.BlockSpec((1, d), lambda i, *_: (0, 0))
    ospec = pl.BlockSpec((tc, d), lambda i, *_: (i, 0))
    wbase = tok_base // tc
    in_specs = [pl.BlockSpec((tc, 2), lambda i, *_: (wbase + i, 0)),
                pl.BlockSpec(memory_space=pl.ANY), ospec, mspec, vspec, vspec]
    args = [wts, ys, xres, gate, lng, lnb]
    out_specs = [ospec]
    out_shape = [jax.ShapeDtypeStruct((t, d), F32)]
    if has_next:
        in_specs += [mspec, mspec]
        args += list(nxt_mods)
        out_specs.append(ospec)
        out_shape.append(jax.ShapeDtypeStruct((t, d), BF16))
    return pl.pallas_call(
        functools.partial(_combine_kernel, pos.shape[0] // 2, tok_base, has_next),
        grid_spec=pltpu.PrefetchScalarGridSpec(
            num_scalar_prefetch=1,
            grid=(t // tc,),
            in_specs=in_specs,
            out_specs=out_specs,
            scratch_shapes=[pltpu.VMEM((3, 2, tc, d), F32), pltpu.VMEM((1, tc, d), F32),
                            pltpu.SemaphoreType.DMA((3,))]),
        out_shape=out_shape,
        compiler_params=_cparams(1, 40),
        name="moe_combine",
    )(pos, *args)


def _plan_kernel(n_tiles, idx_ref, cnt_ref, pos_ref, meta_ref):
    tm = MOE_TILE
    t = idx_ref.shape[1]
    sub = lax.broadcasted_iota(I32, (N_EXPERTS, LANES), 0)
    lane = lax.broadcasted_iota(I32, (N_EXPERTS, LANES), 1)
    cnt = cnt_ref[...]
    padded = jnp.floor((cnt + (tm - 1)) * (1.0 / tm)) * tm
    ends = padded
    sh = 1
    while sh < N_EXPERTS:
        ends = ends + jnp.where(sub >= sh, pltpu.roll(ends, sh, 0), 0.0)
        sh *= 2
    off = ends - padded
    total = ends[N_EXPERTS - 1:N_EXPERTS, :]
    n_used = jnp.maximum(total * (1.0 / tm), 1.0)

    def as_row(col):
        return jnp.sum(jnp.where(sub == lane, col, 0.0), axis=0, keepdims=True)

    eio = lax.broadcasted_iota(I32, (N_EXPERTS, t), 0)
    off_col = off[:, 0:1]
    for k in range(2):
        start = jnp.sum(jnp.where(eio == idx_ref[k:k + 1, :], off_col, 0.0), axis=0, keepdims=True)
        pos_ref[k:k + 1, :] = start.astype(I32) + idx_ref[2 + k:3 + k, :]
    pos_ref[2:SUBLANES, :] = jnp.zeros((SUBLANES - 2, t), I32)

    tile = lax.broadcasted_iota(I32, (1, LANES), 1).astype(F32)
    tile_start = jnp.minimum(tile, n_used - 1.0) * tm
    te = jnp.minimum(jnp.sum(jnp.where(ends <= tile_start, 1.0, 0.0), axis=0, keepdims=True), N_EXPERTS - 1.0)
    prev = jnp.where(tile == 0.0, -1.0, pltpu.roll(te, 1, 1))
    first = jnp.where((tile < n_used) & (te != prev), 1.0, 0.0)
    present = as_row(jnp.where(cnt > 0.0, 1.0, 0.0))
    later = jnp.where((present > 0.0) & (lane > sub) & (lane < N_EXPERTS), lane.astype(F32), float(N_EXPERTS))
    nxt_e = jnp.min(later, axis=1, keepdims=True)
    nxt_e = jnp.where(nxt_e == float(N_EXPERTS), -1.0, nxt_e)
    nxt = jnp.sum(jnp.where(sub.astype(F32) == te, nxt_e, 0.0), axis=0, keepdims=True)
    lane_row = lax.broadcasted_iota(I32, (1, LANES), 1)
    off_row = as_row(off)
    off_row = jnp.where(lane_row == N_EXPERTS, total, off_row)
    off_row = jnp.where(lane_row == N_EXPERTS + 1, float(n_tiles * tm), off_row)
    rows = {META_TILE_EXPERT: te, META_FIRST: first, META_NEXT: nxt, META_N_USED: n_used,
            META_COUNT: as_row(cnt), META_OFFSET: off_row}
    for r in range(SUBLANES):
        meta_ref[r:r + 1, :] = rows[r].astype(I32) if r in rows else jnp.zeros((1, LANES), I32)


def _moe_plan(idx, cnt, n_tok):
    tm = MOE_TILE
    n_tiles = (2 * n_tok + N_EXPERTS * (tm - 1) + tm - 1) // tm
    assert n_tiles <= LANES
    pos, meta = pl.pallas_call(
        functools.partial(_plan_kernel, n_tiles),
        out_shape=[jax.ShapeDtypeStruct((SUBLANES, n_tok), I32), jax.ShapeDtypeStruct((SUBLANES, LANES), I32)],
        compiler_params=_cparams(0, 32),
        name="moe_plan",
    )(idx, cnt)
    return pos, meta, n_tiles * tm


class _Rows:
    def __init__(self, n_seq, seq_len, tile):
        self.n_seq, self.seq_len, self.tile = n_seq, seq_len, tile
        self.per_row = seq_len < tile

    def mods(self, m):
        if self.per_row:
            return jnp.repeat(m, self.seq_len, axis=0).reshape(-1, self.tile, m.shape[1])
        return m[:, None, :]

    def mod_idx(self, i):
        return i if self.per_row else (i * self.tile) // self.seq_len


def _split_mods(mod_l, n_prompt):
    d = mod_l.shape[1] // 6
    cols = [mod_l[:, k * d:(k + 1) * d] for k in range(6)]
    return [c[:n_prompt] for c in cols], [c[n_prompt:] for c in cols]


def kernel(x_prompt, x_sample, state_mlstm_C, state_mlstm_n, state_mlstm_m, c_prompt, c_sample, w_ada, b_ada, ln_g, ln_b, a_w_in, a_b_gates, a_norm_w, a_w_out, b_w_in, b_b_in, b_norm_g, b_norm_b, b_w_s, b_b_s, b_w_out, w_router, b_router, w_expert_in, w_expert_out):
    bp, sp, d = x_prompt.shape
    bs, ss, _ = x_sample.shape
    tp = bp * sp
    ts = bs * ss
    n_tok = tp + ts
    qk = N_HEADS * DK
    vd = N_HEADS * DV
    n_main = 2 * qk + 2 * vd

    n_seq = bp + bs
    pad = (-n_seq) % SUBLANES
    c_all = jnp.concatenate([c_prompt, c_sample, jnp.zeros((pad, d), F32)])
    mod = _adaln(c_all, w_ada, b_ada)[:, :n_seq]

    sets = {"p": _Rows(bp, sp, 512), "s": _Rows(bs, ss, 256)}
    base = {"p": 0, "s": tp}
    x = {"p": x_prompt.reshape(tp, d), "s": x_sample.reshape(ts, d)}
    hm = {}
    outs = {}
    for layer in range(DEPTH):
        j = layer // 2
        mp, ms = _split_mods(mod[layer], bp)
        md = {"p": mp, "s": ms}
        lng = ln_g[layer]
        lnb = ln_b[layer]
        lhs = {}
        if layer % 2 == 0:
            w_in = a_w_in[j]
            gb = jnp.zeros((1, 2 * LANES), F32)
            gb = gb.at[0, :N_HEADS].set(a_b_gates[j, :N_HEADS])
            gb = gb.at[0, LANES:LANES + N_HEADS].set(a_b_gates[j, N_HEADS:])
            nw = a_norm_w[j].reshape(1, vd)
            proj = {}
            gates = {}
            for k, rs in sets.items():
                proj[k], gates[k] = _inproj(x[k], rs.mods(md[k][0]), rs.mods(md[k][1]), rs.mod_idx,
                                            w_in, n_main, rs.tile, BF16 if k == "p" else F32)
            lhs["p"], c_p, n_p, m_p = _mlstm_chunks(
                proj["p"], gates["p"], gb, nw,
                jnp.zeros((bp, N_HEADS, DK, DV), F32), jnp.zeros((bp, N_HEADS, LANES), F32),
                jnp.zeros((bp, N_HEADS, LANES), F32), bp, sp)
            rpad = SUBLANES - ss
            proj_s = jnp.pad(proj["s"].reshape(bs, ss, n_main), ((0, 0), (0, rpad), (0, 0)))
            gates_s = jnp.pad(gates["s"].reshape(bs, ss, 2 * LANES), ((0, 0), (0, rpad), (0, 0)))
            m0 = jnp.pad(state_mlstm_m[j], ((0, 0), (0, LANES - N_HEADS)))[:, None, :]
            ypre_s, c_s, n_s, m_s = _mlstm_short(proj_s, gates_s, gb, nw, state_mlstm_C[j],
                                                 state_mlstm_n[j].reshape(bs, 1, qk), m0, ss)
            n_s = n_s.reshape(bs, N_HEADS, DK)
            lhs["s"] = ypre_s[:, :ss].reshape(ts, vd)
            outs["C_p"], outs["n_p"], outs["m_p"] = c_p, n_p, m_p[:, :, 0]
            outs["C_s"], outs["n_s"], outs["m_s"] = c_s, n_s, m_s[:, 0, :N_HEADS]
            w_out = a_w_out[j]
        else:
            ws = b_w_s[j]
            bsv = b_b_s[j]
            w_out = b_w_out[j]

            def mixing(l, tmix):
                tri = jnp.tril(jnp.ones((l, l), bool))
                wsl = jnp.where(tri, ws[:, :l, :l], 0.0)
                eye = jnp.eye(tmix // l, dtype=F32)
                mats = jax.vmap(lambda m: jnp.kron(eye, m))(wsl).astype(BF16)
                bias = jnp.tile(bsv[:, :l], (1, tmix // l))[:, :, None]
                return mats, bias

            uv = {k: _gmlp_in(hm[k], b_w_in[j], b_b_in[j], min(1024, rs.n_seq * rs.seq_len),
                              BF16 if k == "p" else F32) for k, rs in sets.items()}
        x1 = {}
        hf = {}
        for k, rs in sets.items():
            if layer % 2 == 0:
                x1[k], hf[k] = _proj_ln(lhs[k], w_out, x[k], rs.mods(md[k][2]), lng[0:1], lnb[0:1],
                                        rs.mods(md[k][3]), rs.mods(md[k][4]), rs.mod_idx, rs.tile, 512)
            else:
                rs = _Rows(rs.n_seq, rs.seq_len, rs.tile // 2)
                mats, bias = mixing(min(CHUNK, rs.seq_len), rs.tile)
                res = _mix_proj_ln(uv[k], b_norm_g[j], b_norm_b[j], mats, bias, w_out, x[k], rs.mods(md[k][2]),
                                   lng[0:1], lnb[0:1], rs.mods(md[k][3]), rs.mods(md[k][4]), rs.mod_idx,
                                   rs.tile, 512, k == "s")
                x1[k], hf[k] = res[0], res[1]
                if k == "s":
                    outs["v_s"] = res[2]

        idx, wts, cnt = _router(hf["p"], hf["s"], w_router, b_router, 512)
        pos, meta, n_rows = _moe_plan(idx, cnt, n_tok)
        pos = pos[:2].reshape(-1)
        xsorted = _scatter_rows(hf["p"], hf["s"], pos, meta, n_rows, 256)
        ysorted = _experts(xsorted, w_expert_in, w_expert_out, layer, meta)
        wts2 = wts[:2].T
        nxt = _split_mods(mod[layer + 1], bp) if layer + 1 < DEPTH else None
        for ki, (k, rs) in enumerate(sets.items()):
            rc = _Rows(rs.n_seq, rs.seq_len, 256)
            nxt_mods = None if nxt is None else (rc.mods(nxt[ki][0]), rc.mods(nxt[ki][1]))
            res = _combine_ln(ysorted, pos, wts2, base[k], x1[k], rc.mods(md[k][5]),
                              lng[1:2], lnb[1:2], nxt_mods, rc.mod_idx, 256)
            x[k] = res[0]
            if nxt is not None:
                hm[k] = res[1]

    return (x["p"].reshape(bp, sp, d), x["s"].reshape(bs, ss, d),
            outs["C_p"][None], outs["n_p"][None], outs["m_p"][None],
            outs["C_s"][None], outs["n_s"][None], outs["m_s"][None],
            outs["v_s"].reshape(bs, ss, -1)[None])
```

```python
import functools

import jax
import jax.numpy as jnp
from jax import lax
from jax.experimental import pallas as pl
from jax.experimental.pallas import tpu as pltpu

F32 = jnp.float32
BF16 = jnp.bfloat16
I32 = jnp.int32

DEPTH = 2
N_HEADS = 8
DK = 128
DV = 256
CHUNK = 128
N_GROUPS_B = 8
N_EXPERTS = 16
N_EXPERT_GROUPS = 4
EXPERTS_PER_GROUP = 4
ALPHA = float((2 * DEPTH) ** 0.25)
LN_EPS = 1e-5

LANES = 128
SUBLANES = 8
MIB = 1024 * 1024
MOE_TILE = 256
CAST_ROWS = 256
STAGE_ROWS = 128
META_TILE_EXPERT, META_FIRST, META_NEXT, META_N_USED, META_COUNT, META_OFFSET = range(6)


def _cparams(n_axes, vmem_mib):
    return pltpu.CompilerParams(
        dimension_semantics=("arbitrary",) * n_axes,
        vmem_limit_bytes=int(vmem_mib * MIB))


def _split3(x):
    hi = x.astype(BF16)
    r1 = x - hi.astype(F32)
    mid = r1.astype(BF16)
    lo = (r1 - mid.astype(F32)).astype(BF16)
    return hi, mid, lo


def _log_sigmoid(x):
    return jnp.minimum(x, 0.0) - jnp.log1p(jnp.exp(-jnp.abs(x)))


def _load_weight_bf16(w_hbm, wb, stg, sem):
    nj, k, tn = wb.shape
    ch = stg.shape[1]
    nch = k // ch

    def chunk(c):
        return pltpu.make_async_copy(w_hbm.at[pl.ds(c * ch, ch), pl.ds(0, nj * tn)], stg.at[c % 2],
                                     sem.at[c % 2])

    chunk(0).start()
    for c in range(nch):
        if c + 1 < nch:
            chunk(c + 1).start()
        chunk(c).wait()
        for jj in range(nj):
            wb[jj, c * ch:(c + 1) * ch, :] = stg[c % 2, :, jj * tn:(jj + 1) * tn].astype(BF16)


def _adaln_kernel(c_ref, w_ref, b_ref, o_ref):
    c = c_ref[...]
    a = (c * jax.nn.sigmoid(c)).astype(BF16)
    o_ref[0] = jnp.dot(a, w_ref[0].astype(BF16), preferred_element_type=F32) + b_ref[0]


def _adaln(c_all, w_ada, b_ada):
    depth, d, n = w_ada.shape
    r = c_all.shape[0]
    tn = 1024
    return pl.pallas_call(
        _adaln_kernel,
        grid=(depth, n // tn),
        in_specs=[pl.BlockSpec((r, d), lambda l, j: (0, 0)),
                  pl.BlockSpec((1, d, tn), lambda l, j: (l, 0, j)),
                  pl.BlockSpec((1, 1, tn), lambda l, j: (l, 0, j))],
        out_specs=pl.BlockSpec((1, r, tn), lambda l, j: (l, 0, j)),
        out_shape=jax.ShapeDtypeStruct((depth, r, n), F32),
        compiler_params=_cparams(2, 32),
        name="adaln",
    )(c_all, w_ada, b_ada.reshape(depth, 1, n))


def _inproj_kernel(x_ref, sh_ref, sc_ref, w_hbm, proj_ref, gates_ref, hin_ref, wb, stg, g16, wgs, sem):
    j = pl.program_id(1)

    @pl.when((pl.program_id(0) == 0) & (j == 0))
    def _():
        _load_weight_bf16(w_hbm, wb, stg, sem)
        n_main = wb.shape[0] * wb.shape[2]
        cp = pltpu.make_async_copy(w_hbm.at[:, pl.ds(n_main, 2 * N_HEADS)], g16, sem.at[0])
        cp.start()
        cp.wait()
        g = g16[...]
        z = jnp.zeros((g.shape[0], LANES - N_HEADS), F32)
        wg = jnp.concatenate([g[:, :N_HEADS], z, g[:, N_HEADS:], z], axis=1)
        hi = wg.astype(BF16)
        wgs[0] = hi
        wgs[1] = (wg - hi.astype(F32)).astype(BF16)

    @pl.when(j == 0)
    def _():
        h = x_ref[...] * (1.0 + sc_ref[0]) + sh_ref[0]
        hb = h.astype(BF16)
        hin_ref[...] = hb
        h_lo = (h - hb.astype(F32)).astype(BF16)
        gates_ref[...] = (jnp.dot(hb, wgs[0], preferred_element_type=F32)
                          + jnp.dot(h_lo, wgs[0], preferred_element_type=F32)
                          + jnp.dot(hb, wgs[1], preferred_element_type=F32))

    proj_ref[...] = jnp.dot(hin_ref[...], wb[j], preferred_element_type=F32).astype(proj_ref.dtype)


def _inproj(x, sh, sc, mod_idx, w_in, n_main, tm, out_dtype):
    t, d = x.shape
    tn = 1024
    r = sh.shape[1]
    mspec = pl.BlockSpec((1, r, d), lambda i, j: (mod_idx(i), 0, 0))
    return pl.pallas_call(
        _inproj_kernel,
        grid=(t // tm, n_main // tn),
        in_specs=[pl.BlockSpec((tm, d), lambda i, j: (i, 0)), mspec, mspec,
                  pl.BlockSpec(memory_space=pl.ANY)],
        out_specs=[pl.BlockSpec((tm, tn), lambda i, j: (i, j)),
                   pl.BlockSpec((tm, 2 * LANES), lambda i, j: (i, 0))],
        out_shape=[jax.ShapeDtypeStruct((t, n_main), out_dtype),
                   jax.ShapeDtypeStruct((t, 2 * LANES), F32)],
        scratch_shapes=[pltpu.VMEM((tm, d), BF16), pltpu.VMEM((n_main // tn, d, tn), BF16),
                        pltpu.VMEM((2, STAGE_ROWS, n_main), F32), pltpu.VMEM((d, 2 * N_HEADS), F32),
                        pltpu.VMEM((2, d, 2 * LANES), BF16), pltpu.SemaphoreType.DMA((2,))],
        compiler_params=_cparams(2, 56),
        name="mlstm_inproj",
    )(x, sh, sc, w_in)


def _head_norm_gate(hh, nw, o):
    mu = jnp.mean(hh, axis=1, keepdims=True)
    xc = hh - mu
    var = jnp.mean(xc * xc, axis=1, keepdims=True)
    return jax.nn.sigmoid(o) * (xc * lax.rsqrt(var + LN_EPS) * nw)


def _mlstm_chunk_kernel(q_ref, k_ref, v_ref, o_ref, g_ref, gb_ref, nw_ref, c0_ref, n0_ref, m0_ref,
                        y_ref, cout_ref, nout_ref, mout_ref, c_scr, n_scr, m_scr):
    ci = pl.program_id(1)
    L = q_ref.shape[0]

    @pl.when(ci == 0)
    def _():
        c_scr[...] = c0_ref[0]
        n_scr[...] = n0_ref[0]
        m_scr[...] = m0_ref[0]

    g = g_ref[...] + gb_ref[...]
    gi = g[:, :LANES]
    lf = _log_sigmoid(g[:, LANES:])
    row = lax.broadcasted_iota(I32, (L, L), 0)
    col = lax.broadcasted_iota(I32, (L, L), 1)
    causal = col <= row
    ltri = jnp.where(causal, 1.0, 0.0).astype(BF16)
    hi, mid, lo = _split3(lf)
    bcum = (jnp.dot(ltri, hi, preferred_element_type=F32)
            + jnp.dot(ltri, mid, preferred_element_type=F32)
            + jnp.dot(ltri, lo, preferred_element_type=F32))
    a = gi - bcum
    a_t = a.T
    scale = DK ** -0.5
    nt = (((1,), (1,)), ((), ()))
    tn_dims = (((0,), (0,)), ((), ()))
    for h in range(N_HEADS):
        ks = slice(h * DK, (h + 1) * DK)
        vs = slice(h * DV, (h + 1) * DV)
        qf = q_ref[:, ks].astype(F32) * scale
        qb = qf.astype(BF16)
        kf = k_ref[:, ks].astype(F32)
        kb = kf.astype(BF16)
        vb = v_ref[:, vs].astype(BF16)
        a_row = a_t[h:h + 1, :]
        a_col = a[:, h:h + 1]
        b_col = bcum[:, h:h + 1]
        m_prev = m_scr[h:h + 1, 0:1]
        amat = jnp.where(causal, a_row, -jnp.inf)
        mx = jnp.max(amat, axis=1, keepdims=True)
        m_inter = b_col + m_prev
        m_t = jnp.maximum(m_inter, b_col + mx)
        dm = jnp.exp(amat + (b_col - m_t))
        s = lax.dot_general(qb, kb, nt, preferred_element_type=F32)
        scores = s * dm
        inter = jnp.exp(m_inter - m_t)
        c_old = c_scr[h]
        n_old = n_scr[h:h + 1, :]
        qc = jnp.dot(qb, c_old.astype(BF16), preferred_element_type=F32)
        num = jnp.dot(scores.astype(BF16), vb, preferred_element_type=F32) + inter * qc
        qn = jnp.sum(qf * n_old, axis=1, keepdims=True)
        den = jnp.sum(scores, axis=1, keepdims=True) + inter * qn
        hh = num / jnp.maximum(jnp.abs(den), jnp.exp(-m_t))
        m_new = m_t[L - 1:L, :]
        b_last = b_col[L - 1:L, :]
        w_col = jnp.exp(b_last + a_col - m_new)
        decay = jnp.exp(b_last + m_prev - m_new)
        kw = kf * w_col
        c_scr[h] = decay * c_old + lax.dot_general(kw.astype(BF16), vb, tn_dims,
                                                   preferred_element_type=F32)
        n_scr[h:h + 1, :] = decay * n_old + jnp.sum(kw, axis=0, keepdims=True)
        m_scr[h:h + 1, :] = jnp.broadcast_to(m_new, (1, LANES))
        y_ref[:, vs] = _head_norm_gate(hh, nw_ref[:, vs], o_ref[:, vs].astype(F32)).astype(y_ref.dtype)

    @pl.when(ci == pl.num_programs(1) - 1)
    def _():
        cout_ref[0] = c_scr[...]
        nout_ref[0] = n_scr[...]
        mout_ref[0] = m_scr[...]


def _mlstm_chunks(proj, gates, gb, nw, c0, n0, m0, batch, seq):
    nc = seq // CHUNK
    qk = N_HEADS * DK
    vd = N_HEADS * DV
    row = lambda b, c: b * nc + c
    st4 = pl.BlockSpec((1, N_HEADS, DK, DV), lambda b, c: (b, 0, 0, 0))
    st3 = pl.BlockSpec((1, N_HEADS, LANES), lambda b, c: (b, 0, 0))
    return pl.pallas_call(
        _mlstm_chunk_kernel,
        grid=(batch, nc),
        in_specs=[pl.BlockSpec((CHUNK, qk), lambda b, c: (row(b, c), 0)),
                  pl.BlockSpec((CHUNK, qk), lambda b, c: (row(b, c), 1)),
                  pl.BlockSpec((CHUNK, vd), lambda b, c: (row(b, c), 1)),
                  pl.BlockSpec((CHUNK, vd), lambda b, c: (row(b, c), 2)),
                  pl.BlockSpec((CHUNK, 2 * LANES), lambda b, c: (row(b, c), 0)),
                  pl.BlockSpec((1, 2 * LANES), lambda b, c: (0, 0)),
                  pl.BlockSpec((1, vd), lambda b, c: (0, 0)),
                  st4, st3, st3],
        out_specs=[pl.BlockSpec((CHUNK, vd), lambda b, c: (row(b, c), 0)), st4, st3, st3],
        out_shape=[jax.ShapeDtypeStruct((batch * seq, vd), BF16),
                   jax.ShapeDtypeStruct((batch, N_HEADS, DK, DV), F32),
                   jax.ShapeDtypeStruct((batch, N_HEADS, LANES), F32),
                   jax.ShapeDtypeStruct((batch, N_HEADS, LANES), F32)],
        scratch_shapes=[pltpu.VMEM((N_HEADS, DK, DV), F32),
                        pltpu.VMEM((N_HEADS, LANES), F32),
                        pltpu.VMEM((N_HEADS, LANES), F32)],
        compiler_params=_cparams(2, 40),
        name="mlstm_chunks",
    )(proj, proj, proj, proj, gates, gb, nw, c0, n0, m0)


def _per_head(x, width):
    return jnp.concatenate(
        [jnp.broadcast_to(x[..., h:h + 1], x.shape[:-1] + (width,)) for h in range(N_HEADS)], axis=-1)


def _head_sums(x, width):
    lane = lax.broadcasted_iota(I32, x.shape[:-1] + (LANES,), x.ndim - 1)
    out = jnp.zeros(x.shape[:-1] + (LANES,), F32)
    for h in range(N_HEADS):
        s = jnp.sum(x[..., h * width:(h + 1) * width], axis=-1, keepdims=True)
        out = jnp.where(lane == h, s, out)
    return out


def _mlstm_short_kernel(seq, q_ref, k_ref, v_ref, o_ref, g_ref, gb_ref, nw_ref, c0_ref, n0_ref, m0_ref,
                        y_ref, cout_ref, nout_ref, mout_ref, qc_scr, kw_scr):
    bt, rows, _ = q_ref.shape
    scale = DK ** -0.5
    tn_dims = (((0,), (0,)), ((), ()))
    row = lax.broadcasted_iota(I32, (bt, rows, LANES), 1)
    g = g_ref[...] + gb_ref[...]
    gi = g[:, :, :LANES]
    lf = _log_sigmoid(g[:, :, LANES:])
    bcum = jnp.zeros_like(lf)
    for s in range(seq):
        bcum = bcum + jnp.where(row >= s, lf[:, s:s + 1, :], 0.0)
    a = gi - bcum
    mx = jnp.full_like(a, -jnp.inf)
    for s in range(seq):
        mx = jnp.maximum(mx, jnp.where(row >= s, a[:, s:s + 1, :], -jnp.inf))
    m_prev = m0_ref[...]
    m_inter = bcum + m_prev
    m_t = jnp.maximum(m_inter, bcum + mx)
    cmt = bcum - m_t
    inter = jnp.exp(m_inter - m_t)
    einv = jnp.exp(-m_t)
    m_new = m_t[:, seq - 1:seq, :]
    b_last = bcum[:, seq - 1:seq, :]
    w = jnp.where(row < seq, jnp.exp(b_last + a - m_new), 0.0)
    decay = jnp.exp(b_last + m_prev - m_new)
    mout_ref[...] = m_new

    q = q_ref[...] * scale
    k = k_ref[...]
    v = v_ref[...]
    n_old = n0_ref[...]
    for b in range(bt):
        for h in range(N_HEADS):
            qc_scr[b, :, h * DV:(h + 1) * DV] = jnp.dot(
                (q_ref[b, :, h * DK:(h + 1) * DK] * scale).astype(BF16), c0_ref[b, h].astype(BF16),
                preferred_element_type=F32)
    den = inter * _head_sums(q * n_old, DK)
    num = _per_head(inter, DV) * qc_scr[...]
    for s in range(seq):
        p = _head_sums(q * k[:, s:s + 1, :], DK) * jnp.where(row >= s, jnp.exp(cmt + a[:, s:s + 1, :]), 0.0)
        den = den + p
        num = num + _per_head(p, DV) * v[:, s:s + 1, :]
    hh = num * _per_head(1.0 / jnp.maximum(jnp.abs(den), einv), DV)
    mu = _head_sums(hh, DV) * (1.0 / DV)
    xc = hh - _per_head(mu, DV)
    var = _head_sums(xc * xc, DV) * (1.0 / DV)
    hn = xc * _per_head(lax.rsqrt(var + LN_EPS), DV) * nw_ref[...]
    y_ref[...] = jax.nn.sigmoid(o_ref[...]) * hn

    kw = k * _per_head(w, DK)
    kw_scr[...] = kw
    nout_ref[...] = _per_head(decay, DK) * n_old + jnp.sum(kw, axis=1, keepdims=True)
    for b in range(bt):
        for h in range(N_HEADS):
            cout_ref[b, h] = (decay[b, :, h:h + 1] * c0_ref[b, h]
                              + lax.dot_general(kw_scr[b, :, h * DK:(h + 1) * DK],
                                                v_ref[b, :, h * DV:(h + 1) * DV],
                                                tn_dims, preferred_element_type=F32))


def _mlstm_short(proj, gates, gb, nw, c0, n0, m0, seq):
    batch, rows, _ = proj.shape
    qk = N_HEADS * DK
    vd = N_HEADS * DV
    bt = 8
    st4 = pl.BlockSpec((bt, N_HEADS, DK, DV), lambda i: (i, 0, 0, 0))
    st3 = pl.BlockSpec((bt, 1, qk), lambda i: (i, 0, 0))
    stm = pl.BlockSpec((bt, 1, LANES), lambda i: (i, 0, 0))
    return pl.pallas_call(
        functools.partial(_mlstm_short_kernel, seq),
        grid=(batch // bt,),
        in_specs=[pl.BlockSpec((bt, rows, qk), lambda i: (i, 0, 0)),
                  pl.BlockSpec((bt, rows, qk), lambda i: (i, 0, 1)),
                  pl.BlockSpec((bt, rows, vd), lambda i: (i, 0, 1)),
                  pl.BlockSpec((bt, rows, vd), lambda i: (i, 0, 2)),
                  pl.BlockSpec((bt, rows, 2 * LANES), lambda i: (i, 0, 0)),
                  pl.BlockSpec((1, 2 * LANES), lambda i: (0, 0)),
                  pl.BlockSpec((1, vd), lambda i: (0, 0)),
                  st4, st3, stm],
        out_specs=[pl.BlockSpec((bt, rows, vd), lambda i: (i, 0, 0)), st4, st3, stm],
        out_shape=[jax.ShapeDtypeStruct((batch, rows, vd), F32),
                   jax.ShapeDtypeStruct((batch, N_HEADS, DK, DV), F32),
                   jax.ShapeDtypeStruct((batch, 1, qk), F32),
                   jax.ShapeDtypeStruct((batch, 1, LANES), F32)],
        scratch_shapes=[pltpu.VMEM((bt, rows, vd), F32), pltpu.VMEM((bt, rows, qk), F32)],
        compiler_params=_cparams(1, 48),
        name="mlstm_short",
    )(proj, proj, proj, proj, gates, gb, nw, c0, n0, m0)


def _residual_ln(zbuf, xres_ref, gate_ref, lng_ref, lnb_ref, xo_ref, nxt):
    nc, tm, tn = zbuf.shape
    inv_d = 1.0 / (nc * tn)
    ssum = jnp.zeros((tm, 1), F32)
    for c in range(nc):
        sl = pl.ds(c * tn, tn)
        z = ALPHA * xres_ref[:, sl] + gate_ref[0, :, sl] * zbuf[c]
        zbuf[c] = z
        ssum = ssum + jnp.sum(z, axis=1, keepdims=True)
    mu = ssum * inv_d
    vsum = jnp.zeros((tm, 1), F32)
    for c in range(nc):
        zc = zbuf[c] - mu
        vsum = vsum + jnp.sum(zc * zc, axis=1, keepdims=True)
    rstd = lax.rsqrt(vsum * inv_d + LN_EPS)
    for c in range(nc):
        sl = pl.ds(c * tn, tn)
        xn = (zbuf[c] - mu) * rstd * lng_ref[:, sl] + lnb_ref[:, sl]
        xo_ref[:, sl] = xn
        if nxt is not None:
            sh_ref, sc_ref, ho_ref = nxt
            ho_ref[:, sl] = (xn * (1.0 + sc_ref[0, :, sl]) + sh_ref[0, :, sl]).astype(ho_ref.dtype)


def _proj_ln_kernel(lhs_ref, w_hbm, xres_ref, gate_ref, lng_ref, lnb_ref, shn_ref, scn_ref,
                    xo_ref, ho_ref, wb, stg, ybuf, sem):
    @pl.when(pl.program_id(0) == 0)
    def _():
        _load_weight_bf16(w_hbm, wb, stg, sem)

    lhs = lhs_ref[...].astype(BF16)
    for c in range(ybuf.shape[0]):
        ybuf[c] = jnp.dot(lhs, wb[c], preferred_element_type=F32)
    _residual_ln(ybuf, xres_ref, gate_ref, lng_ref, lnb_ref, xo_ref, (shn_ref, scn_ref, ho_ref))


def _proj_ln(lhs, w, xres, gate, lng, lnb, shn, scn, mod_idx, tm, tn):
    t, k = lhs.shape
    d = w.shape[1]
    r = gate.shape[1]
    mspec = pl.BlockSpec((1, r, d), lambda i: (mod_idx(i), 0, 0))
    vspec = pl.BlockSpec((1, d), lambda i: (0, 0))
    ospec = pl.BlockSpec((tm, d), lambda i: (i, 0))
    return pl.pallas_call(
        _proj_ln_kernel,
        grid=(t // tm,),
        in_specs=[pl.BlockSpec((tm, k), lambda i: (i, 0)),
                  pl.BlockSpec(memory_space=pl.ANY),
                  ospec, mspec, vspec, vspec, mspec, mspec],
        out_specs=[ospec, ospec],
        out_shape=[jax.ShapeDtypeStruct((t, d), F32), jax.ShapeDtypeStruct((t, d), F32)],
        scratch_shapes=[pltpu.VMEM((d // tn, k, tn), BF16), pltpu.VMEM((2, CAST_ROWS, d), F32),
                        pltpu.VMEM((d // tn, tm, tn), F32), pltpu.SemaphoreType.DMA((2,))],
        compiler_params=_cparams(1, 58),
        name="proj_ln",
    )(lhs, w, xres, gate, lng, lnb, shn, scn)


def _gelu_tanh(x):
    return x * (0.5 * (1.0 + jnp.tanh(0.7978845608028654 * (x + 0.044715 * (x * x * x)))))


def _gmlp_in_kernel(h_ref, w_hbm, b_ref, o_ref, wb, stg, sem):
    j = pl.program_id(1)

    @pl.when((pl.program_id(0) == 0) & (j == 0))
    def _():
        _load_weight_bf16(w_hbm, wb, stg, sem)

    acc = jnp.dot(h_ref[...].astype(BF16), wb[j], preferred_element_type=F32)
    o_ref[...] = _gelu_tanh(acc + b_ref[...]).astype(o_ref.dtype)


def _gmlp_in(h, w, b, tm, out_dtype):
    t, d = h.shape
    n = w.shape[1]
    tn = 1024
    return pl.pallas_call(
        _gmlp_in_kernel,
        grid=(t // tm, n // tn),
        in_specs=[pl.BlockSpec((tm, d), lambda i, j: (i, 0)),
                  pl.BlockSpec(memory_space=pl.ANY),
                  pl.BlockSpec((1, tn), lambda i, j: (0, j))],
        out_specs=pl.BlockSpec((tm, tn), lambda i, j: (i, j)),
        out_shape=jax.ShapeDtypeStruct((t, n), out_dtype),
        scratch_shapes=[pltpu.VMEM((n // tn, d, tn), BF16), pltpu.VMEM((2, STAGE_ROWS, n), F32),
                        pltpu.SemaphoreType.DMA((2,))],
        compiler_params=_cparams(2, 58),
        name="gmlp_in",
    )(h, w, b.reshape(1, n))


def _gmlp_mix_rows(u_ref, v_ref, g_ref, b_ref, mix_ref, bias_ref, o_ref, vn_ref):
    v = v_ref[...].astype(F32)
    mu = jnp.mean(v, axis=1, keepdims=True)
    xc = v - mu
    var = jnp.mean(xc * xc, axis=1, keepdims=True)
    vn = xc * lax.rsqrt(var + LN_EPS) * g_ref[...] + b_ref[...]
    if vn_ref is not None:
        vn_ref[...] = vn
    gd = v.shape[1] // N_GROUPS_B
    for g in range(N_GROUPS_B):
        sl = slice(g * gd, (g + 1) * gd)
        mixed = jnp.dot(mix_ref[g], vn[:, sl].astype(BF16), preferred_element_type=F32) + bias_ref[g]
        o_ref[:, sl] = (u_ref[:, sl].astype(F32) * mixed).astype(o_ref.dtype)


def _mix_proj_ln_kernel(emit_v, u_ref, v_ref, g_ref, b_ref, mix_ref, bias_ref, w_hbm, xres_ref, gate_ref,
                        lng_ref, lnb_ref, shn_ref, scn_ref, *rest):
    if emit_v:
        xo_ref, ho_ref, vn_ref, lhs_scr, wb, stg, ybuf, sem = rest
    else:
        xo_ref, ho_ref, lhs_scr, wb, stg, ybuf, sem = rest
        vn_ref = None

    @pl.when(pl.program_id(0) == 0)
    def _():
        _load_weight_bf16(w_hbm, wb, stg, sem)

    _gmlp_mix_rows(u_ref, v_ref, g_ref, b_ref, mix_ref, bias_ref, lhs_scr, vn_ref)
    lhs = lhs_scr[...]
    for c in range(ybuf.shape[0]):
        ybuf[c] = jnp.dot(lhs, wb[c], preferred_element_type=F32)
    _residual_ln(ybuf, xres_ref, gate_ref, lng_ref, lnb_ref, xo_ref, (shn_ref, scn_ref, ho_ref))


def _mix_proj_ln(uv, nv_g, nv_b, mix, bias, w, xres, gate, lng, lnb, shn, scn, mod_idx, tm, tn, emit_v):
    t, n2 = uv.shape
    di = n2 // 2
    d = w.shape[1]
    r = gate.shape[1]
    mspec = pl.BlockSpec((1, r, d), lambda i: (mod_idx(i), 0, 0))
    vspec = pl.BlockSpec((1, d), lambda i: (0, 0))
    ospec = pl.BlockSpec((tm, d), lambda i: (i, 0))
    nspec = pl.BlockSpec((1, di), lambda i: (0, 0))
    out_specs = [ospec, ospec]
    out_shape = [jax.ShapeDtypeStruct((t, d), F32), jax.ShapeDtypeStruct((t, d), F32)]
    if emit_v:
        out_specs.append(pl.BlockSpec((tm, di), lambda i: (i, 0)))
        out_shape.append(jax.ShapeDtypeStruct((t, di), F32))
    return pl.pallas_call(
        functools.partial(_mix_proj_ln_kernel, emit_v),
        grid=(t // tm,),
        in_specs=[pl.BlockSpec((tm, di), lambda i: (i, 0)),
                  pl.BlockSpec((tm, di), lambda i: (i, 1)),
                  nspec, nspec,
                  pl.BlockSpec(mix.shape, lambda i: (0, 0, 0)),
                  pl.BlockSpec(bias.shape, lambda i: (0, 0, 0)),
                  pl.BlockSpec(memory_space=pl.ANY),
                  ospec, mspec, vspec, vspec, mspec, mspec],
        out_specs=out_specs,
        out_shape=out_shape,
        scratch_shapes=[pltpu.VMEM((tm, di), BF16), pltpu.VMEM((d // tn, di, tn), BF16),
                        pltpu.VMEM((2, CAST_ROWS, d), F32), pltpu.VMEM((d // tn, tm, tn), F32),
                        pltpu.SemaphoreType.DMA((2,))],
        compiler_params=_cparams(1, 58),
        name="gmlp_mix_proj_ln",
    )(uv, uv, nv_g.reshape(1, di), nv_b.reshape(1, di), mix, bias, w, xres, gate, lng, lnb, shn, scn)


def _router_kernel(n_first, ha_ref, hb_ref, wr_ref, br_ref, idx_ref, wts_ref, cnt_ref, hall_ref, carry):
    i = pl.program_id(0)
    tr = ha_ref.shape[0]

    @pl.when(i == 0)
    def _():
        carry[...] = jnp.zeros(carry.shape, carry.dtype)

    nt = (((1,), (1,)), ((), ()))
    h = jnp.where(i < n_first, ha_ref[...], hb_ref[...])
    hall_ref[...] = h
    hb = h.astype(BF16)
    hl = (h - hb.astype(F32)).astype(BF16)
    wr = wr_ref[...]
    wb = wr.astype(BF16)
    wl = (wr - wb.astype(F32)).astype(BF16)
    logits = (lax.dot_general(wb, hb, nt, preferred_element_type=F32)
              + lax.dot_general(wb, hl, nt, preferred_element_type=F32)
              + lax.dot_general(wl, hb, nt, preferred_element_type=F32))
    s = jax.nn.sigmoid(logits)
    sel = s + br_ref[...]
    epg = EXPERTS_PER_GROUP
    r = [sel[e:e + 1, :] for e in range(N_EXPERTS)]
    su = [s[e:e + 1, :] for e in range(N_EXPERTS)]

    def top2sum(v):
        best = v[0] + v[1]
        for x in range(epg):
            for y in range(x + 1, epg):
                if (x, y) != (0, 1):
                    best = jnp.maximum(best, v[x] + v[y])
        return best

    gs = [top2sum(r[g * epg:(g + 1) * epg]) for g in range(N_EXPERT_GROUPS)]
    gbest = gs[0]
    gidx = jnp.zeros((1, tr), I32)
    for g in range(1, N_EXPERT_GROUPS):
        better = gs[g] > gbest
        gidx = jnp.where(better, g, gidx)
        gbest = jnp.where(better, gs[g], gbest)
    v = list(r[:epg])
    sv = list(su[:epg])
    for g in range(1, N_EXPERT_GROUPS):
        pick = gidx == g
        for x in range(epg):
            v[x] = jnp.where(pick, r[g * epg + x], v[x])
            sv[x] = jnp.where(pick, su[g * epg + x], sv[x])
    i1 = jnp.zeros((1, tr), I32)
    b1 = v[0]
    w1 = sv[0]
    for x in range(1, epg):
        better = v[x] > b1
        i1 = jnp.where(better, x, i1)
        b1 = jnp.where(better, v[x], b1)
        w1 = jnp.where(better, sv[x], w1)
    i2 = jnp.zeros((1, tr), I32)
    b2 = jnp.full((1, tr), -jnp.inf, F32)
    w2 = jnp.zeros((1, tr), F32)
    for x in range(epg):
        take = (i1 != x) & (v[x] > b2)
        i2 = jnp.where(take, x, i2)
        b2 = jnp.where(take, v[x], b2)
        w2 = jnp.where(take, sv[x], w2)
    wsum = w1 + w2
    e1 = gidx * epg + i1
    e2 = gidx * epg + i2
    eio = lax.broadcasted_iota(I32, (N_EXPERTS, tr), 0)
    hit1 = eio == e1
    hit2 = eio == e2
    oh = jnp.where(hit1 | hit2, 1.0, 0.0)
    ri = lax.broadcasted_iota(I32, (tr, tr), 0)
    cj = lax.broadcasted_iota(I32, (tr, tr), 1)
    before = jnp.where(ri < cj, 1.0, 0.0).astype(BF16)
    rank = jnp.dot(oh.astype(BF16), before, preferred_element_type=F32) + carry[:, 0:1]
    rank1 = jnp.sum(jnp.where(hit1, rank, 0.0), axis=0, keepdims=True)
    rank2 = jnp.sum(jnp.where(hit2, rank, 0.0), axis=0, keepdims=True)
    idx_ref[0:1, :] = e1
    idx_ref[1:2, :] = e2
    idx_ref[2:3, :] = rank1.astype(I32)
    idx_ref[3:4, :] = rank2.astype(I32)
    idx_ref[4:8, :] = jnp.zeros((4, tr), I32)
    wts_ref[0:1, :] = w1 / wsum
    wts_ref[1:2, :] = w2 / wsum
    wts_ref[2:8, :] = jnp.zeros((6, tr), F32)
    carry[...] = carry[...] + jnp.sum(oh, axis=1, keepdims=True)
    cnt_ref[...] = carry[...]


def _two_set_specs(ha, hb, tile):
    d = ha.shape[1]
    na = ha.shape[0] // tile
    nb = hb.shape[0] // tile
    return na, nb, [pl.BlockSpec((tile, d), lambda i, *_: (jnp.minimum(i, na - 1), 0)),
                    pl.BlockSpec((tile, d), lambda i, *_: (jnp.maximum(i - na, 0), 0))]


def _router(ha, hb, w_router, b_router, tr):
    d = ha.shape[1]
    t = ha.shape[0] + hb.shape[0]
    na, nb, hspecs = _two_set_specs(ha, hb, tr)
    return pl.pallas_call(
        functools.partial(_router_kernel, na),
        grid=(na + nb,),
        in_specs=hspecs + [pl.BlockSpec((N_EXPERTS, d), lambda i: (0, 0)),
                           pl.BlockSpec((N_EXPERTS, 1), lambda i: (0, 0))],
        out_specs=[pl.BlockSpec((SUBLANES, tr), lambda i: (0, i)),
                   pl.BlockSpec((SUBLANES, tr), lambda i: (0, i)),
                   pl.BlockSpec((N_EXPERTS, LANES), lambda i: (0, 0)),
                   pl.BlockSpec((tr, d), lambda i: (i, 0))],
        out_shape=[jax.ShapeDtypeStruct((SUBLANES, t), I32),
                   jax.ShapeDtypeStruct((SUBLANES, t), F32),
                   jax.ShapeDtypeStruct((N_EXPERTS, LANES), F32),
                   jax.ShapeDtypeStruct((t, d), F32)],
        scratch_shapes=[pltpu.VMEM((N_EXPERTS, LANES), F32)],
        compiler_params=_cparams(1, 40),
        name="moe_router",
    )(ha, hb, w_router.T, b_router.reshape(N_EXPERTS, 1))


def _invert_kernel(n_tok, pos_ref, inv_ref):
    def clear(p, c):
        inv_ref[p] = 0
        return c

    lax.fori_loop(0, inv_ref.shape[0], clear, 0, unroll=16)

    def place(t, c):
        inv_ref[pos_ref[t]] = t
        inv_ref[pos_ref[n_tok + t]] = t
        return c

    lax.fori_loop(0, n_tok, place, 0, unroll=8)


def _invert_pos(pos, n_rows):
    n_tok = pos.shape[0] // 2
    return pl.pallas_call(
        functools.partial(_invert_kernel, n_tok),
        grid_spec=pltpu.PrefetchScalarGridSpec(
            num_scalar_prefetch=1, grid=(1,), in_specs=[],
            out_specs=pl.BlockSpec(memory_space=pltpu.SMEM)),
        out_shape=jax.ShapeDtypeStruct((n_rows,), I32),
        compiler_params=_cparams(1, 16),
        name="moe_invert",
    )(pos)


def _cast_rows(src, dst):
    def body(c, carry):
        r = pl.multiple_of(c * CAST_ROWS, CAST_ROWS)
        dst[pl.ds(r, CAST_ROWS), :] = src[pl.ds(r, CAST_ROWS), :].astype(BF16)
        return carry

    lax.fori_loop(0, src.shape[0] // CAST_ROWS, body, 0)


def _expert_kernel(layer, meta_ref, inv_ref, h_hbm, w1_hbm, w2_hbm, y_ref,
                   xring, stg1, stg2, wb1, wb2, sem, xsem):
    i = pl.program_id(0)
    tm = y_ref.shape[0]
    n_used = meta_ref[META_N_USED, 0]
    n_slots = xring.shape[0]
    ahead = n_slots - 1

    def fetch(e):
        return (pltpu.make_async_copy(w1_hbm.at[layer, e], stg1, sem.at[0]),
                pltpu.make_async_copy(w2_hbm.at[layer, e], stg2, sem.at[1]))

    def row_copy(tile, slot, r):
        return pltpu.make_async_copy(h_hbm.at[pl.ds(inv_ref[tile * tm + r], 1)],
                                     xring.at[slot, pl.ds(r, 1)], xsem.at[slot])

    def wait_rows(slot):
        pltpu.make_async_copy(h_hbm.at[pl.ds(0, tm)], xring.at[slot], xsem.at[slot]).wait()

    @pl.when(i == 0)
    def _():
        for cp in fetch(meta_ref[META_TILE_EXPERT, 0]):
            cp.start()
        for a in range(ahead):
            def start_rows(r, c, a=a):
                row_copy(jnp.minimum(a, n_used - 1), a, r).start()
                return c

            lax.fori_loop(0, tm, start_rows, 0, unroll=8)

    @pl.when(meta_ref[META_FIRST, i] == 1)
    def _():
        for cp in fetch(0):
            cp.wait()
        _cast_rows(stg1, wb1)
        _cast_rows(stg2, wb2)

        @pl.when(meta_ref[META_NEXT, i] >= 0)
        def _():
            for cp in fetch(meta_ref[META_NEXT, i]):
                cp.start(priority=1)

    @pl.when(i < n_used)
    def _():
        slot = i % n_slots
        wait_rows(slot)
        hid = jnp.dot(xring[slot].astype(BF16), wb1[...], preferred_element_type=F32)
        f = hid.shape[1] // 2
        a = hid[:, :f]
        g = hid[:, f:]
        z = (g * jax.nn.sigmoid(g) * a).astype(BF16)
        y_ref[...] = jnp.dot(z, wb2[...], preferred_element_type=F32)
        nxt_tile = jnp.minimum(i + ahead, n_used - 1)
        nxt_slot = (i + ahead) % n_slots
        for r in range(tm):
            row_copy(nxt_tile, nxt_slot, r).start()

        @pl.when(i == n_used - 1)
        def _():
            for a in range(1, ahead + 1):
                wait_rows((i + a) % n_slots)

    @pl.when(i >= n_used)
    def _():
        y_ref[...] = jnp.zeros(y_ref.shape, y_ref.dtype)


def _experts(h_all, inv, w1, w2, layer, meta):
    d = h_all.shape[1]
    p = inv.shape[0]
    tm = MOE_TILE
    ff2 = w1.shape[3]
    ff = ff2 // 2
    return pl.pallas_call(
        functools.partial(_expert_kernel, layer),
        grid_spec=pltpu.PrefetchScalarGridSpec(
            num_scalar_prefetch=2,
            grid=(p // tm,),
            in_specs=[pl.BlockSpec(memory_space=pl.ANY),
                      pl.BlockSpec(memory_space=pl.ANY),
                      pl.BlockSpec(memory_space=pl.ANY)],
            out_specs=pl.BlockSpec((tm, d), lambda i, *_: (i, 0)),
            scratch_shapes=[pltpu.VMEM((3, tm, d), F32),
                            pltpu.VMEM((d, ff2), F32), pltpu.VMEM((ff, d), F32),
                            pltpu.VMEM((d, ff2), BF16), pltpu.VMEM((ff, d), BF16),
                            pltpu.SemaphoreType.DMA((2,)), pltpu.SemaphoreType.DMA((3,))]),
        out_shape=jax.ShapeDtypeStruct((p, d), F32),
        compiler_params=_cparams(1, 56),
        name="moe_experts",
    )(meta, inv, h_all, w1, w2)


def _combine_kernel(n_tok, tok_base, has_next, pos_ref, w_ref, y_hbm, xres_ref, gate_ref, lng_ref, lnb_ref,
                    *rest):
    if has_next:
        shn_ref, scn_ref, xo_ref, ho_ref, gbuf, fbuf, sem = rest
        nxt = (shn_ref, scn_ref, ho_ref)
    else:
        xo_ref, gbuf, fbuf, sem = rest
        nxt = None
    tc = xres_ref.shape[0]
    i = pl.program_id(0)
    n_steps = pl.num_programs(0)

    n_slots = gbuf.shape[0]
    ahead = n_slots - 1

    def row_copy(step, slot, r, k):
        p = pos_ref[k * n_tok + tok_base + step * tc + r]
        return pltpu.make_async_copy(y_hbm.at[pl.ds(p, 1)], gbuf.at[slot, k, pl.ds(r, 1)], sem.at[slot])

    def wait_tile(slot):
        for k in range(2):
            pltpu.make_async_copy(y_hbm.at[pl.ds(0, tc)], gbuf.at[slot, k], sem.at[slot]).wait()

    @pl.when(i == 0)
    def _():
        for a in range(ahead):
            def start_rows(r, c, a=a):
                for k in range(2):
                    row_copy(jnp.minimum(a, n_steps - 1), a, r, k).start()
                return c

            lax.fori_loop(0, tc, start_rows, 0, unroll=8)

    slot = i % n_slots
    wait_tile(slot)
    fbuf[0] = w_ref[:, 0:1] * gbuf[slot, 0] + w_ref[:, 1:2] * gbuf[slot, 1]
    _residual_ln(fbuf, xres_ref, gate_ref, lng_ref, lnb_ref, xo_ref, nxt)

    nxt_step = jnp.minimum(i + ahead, n_steps - 1)
    nxt_slot = (i + ahead) % n_slots
    for r in range(tc):
        for k in range(2):
            row_copy(nxt_step, nxt_slot, r, k).start()

    @pl.when(i == n_steps - 1)
    def _():
        for a in range(1, ahead + 1):
            wait_tile((i + a) % n_slots)


def _combine_ln(ys, pos, wts, tok_base, xres, gate, lng, lnb, nxt_mods, mod_idx, tc):
    t, d = xres.shape
    r = gate.shape[1]
    has_next = nxt_mods is not None
    mspec = pl.BlockSpec((1, r, d), lambda i, *_: (mod_idx(i), 0, 0))
    vspec = pl.BlockSpec((1, d), lambda i, *_: (0, 0))
    ospec = pl.BlockSpec((tc, d), lambda i, *_: (i, 0))
    wbase = tok_base // tc
    in_specs = [pl.BlockSpec((tc, 2), lambda i, *_: (wbase + i, 0)),
                pl.BlockSpec(memory_space=pl.ANY), ospec, mspec, vspec, vspec]
    args = [wts, ys, xres, gate, lng, lnb]
    out_specs = [ospec]
    out_shape = [jax.ShapeDtypeStruct((t, d), F32)]
    if has_next:
        in_specs += [mspec, mspec]
        args += list(nxt_mods)
        out_specs.append(ospec)
        out_shape.append(jax.ShapeDtypeStruct((t, d), BF16))
    return pl.pallas_call(
        functools.partial(_combine_kernel, pos.shape[0] // 2, tok_base, has_next),
        grid_spec=pltpu.PrefetchScalarGridSpec(
            num_scalar_prefetch=1,
            grid=(t // tc,),
            in_specs=in_specs,
            out_specs=out_specs,
            scratch_shapes=[pltpu.VMEM((3, 2, tc, d), F32), pltpu.VMEM((1, tc, d), F32),
                            pltpu.SemaphoreType.DMA((3,))]),
        out_shape=out_shape,
        compiler_params=_cparams(1, 40),
        name="moe_combine",
    )(pos, *args)


def _plan_kernel(n_tiles, idx_ref, cnt_ref, pos_ref, meta_ref):
    tm = MOE_TILE
    t = idx_ref.shape[1]
    sub = lax.broadcasted_iota(I32, (N_EXPERTS, LANES), 0)
    lane = lax.broadcasted_iota(I32, (N_EXPERTS, LANES), 1)
    cnt = cnt_ref[...]
    padded = jnp.floor((cnt + (tm - 1)) * (1.0 / tm)) * tm
    ends = padded
    sh = 1
    while sh < N_EXPERTS:
        ends = ends + jnp.where(sub >= sh, pltpu.roll(ends, sh, 0), 0.0)
        sh *= 2
    off = ends - padded
    total = ends[N_EXPERTS - 1:N_EXPERTS, :]
    n_used = jnp.maximum(total * (1.0 / tm), 1.0)

    def as_row(col):
        return jnp.sum(jnp.where(sub == lane, col, 0.0), axis=0, keepdims=True)

    eio = lax.broadcasted_iota(I32, (N_EXPERTS, t), 0)
    off_col = off[:, 0:1]
    for k in range(2):
        start = jnp.sum(jnp.where(eio == idx_ref[k:k + 1, :], off_col, 0.0), axis=0, keepdims=True)
        pos_ref[k:k + 1, :] = start.astype(I32) + idx_ref[2 + k:3 + k, :]
    pos_ref[2:SUBLANES, :] = jnp.zeros((SUBLANES - 2, t), I32)

    tile = lax.broadcasted_iota(I32, (1, LANES), 1).astype(F32)
    tile_start = jnp.minimum(tile, n_used - 1.0) * tm
    te = jnp.minimum(jnp.sum(jnp.where(ends <= tile_start, 1.0, 0.0), axis=0, keepdims=True), N_EXPERTS - 1.0)
    prev = jnp.where(tile == 0.0, -1.0, pltpu.roll(te, 1, 1))
    first = jnp.where((tile < n_used) & (te != prev), 1.0, 0.0)
    present = as_row(jnp.where(cnt > 0.0, 1.0, 0.0))
    later = jnp.where((present > 0.0) & (lane > sub) & (lane < N_EXPERTS), lane.astype(F32), float(N_EXPERTS))
    nxt_e = jnp.min(later, axis=1, keepdims=True)
    nxt_e = jnp.where(nxt_e == float(N_EXPERTS), -1.0, nxt_e)
    nxt = jnp.sum(jnp.where(sub.astype(F32) == te, nxt_e, 0.0), axis=0, keepdims=True)
    lane_row = lax.broadcasted_iota(I32, (1, LANES), 1)
    off_row = as_row(off)
    off_row = jnp.where(lane_row == N_EXPERTS, total, off_row)
    off_row = jnp.where(lane_row == N_EXPERTS + 1, float(n_tiles * tm), off_row)
    rows = {META_TILE_EXPERT: te, META_FIRST: first, META_NEXT: nxt, META_N_USED: n_used,
            META_COUNT: as_row(cnt), META_OFFSET: off_row}
    for r in range(SUBLANES):
        meta_ref[r:r + 1, :] = rows[r].astype(I32) if r in rows else jnp.zeros((1, LANES), I32)


def _moe_plan(idx, cnt, n_tok):
    tm = MOE_TILE
    n_tiles = (2 * n_tok + N_EXPERTS * (tm - 1) + tm - 1) // tm
    assert n_tiles <= LANES
    pos, meta = pl.pallas_call(
        functools.partial(_plan_kernel, n_tiles),
        out_shape=[jax.ShapeDtypeStruct((SUBLANES, n_tok), I32), jax.ShapeDtypeStruct((SUBLANES, LANES), I32)],
        compiler_params=_cparams(0, 32),
        name="moe_plan",
    )(idx, cnt)
    return pos, meta, n_tiles * tm


class _Rows:
    def __init__(self, n_seq, seq_len, tile):
        self.n_seq, self.seq_len, self.tile = n_seq, seq_len, tile
        self.per_row = seq_len < tile

    def mods(self, m):
        if self.per_row:
            return jnp.repeat(m, self.seq_len, axis=0).reshape(-1, self.tile, m.shape[1])
        return m[:, None, :]

    def mod_idx(self, i):
        return i if self.per_row else (i * self.tile) // self.seq_len


def _split_mods(mod_l, n_prompt):
    d = mod_l.shape[1] // 6
    cols = [mod_l[:, k * d:(k + 1) * d] for k in range(6)]
    return [c[:n_prompt] for c in cols], [c[n_prompt:] for c in cols]


def kernel(x_prompt, x_sample, state_mlstm_C, state_mlstm_n, state_mlstm_m, c_prompt, c_sample, w_ada, b_ada, ln_g, ln_b, a_w_in, a_b_gates, a_norm_w, a_w_out, b_w_in, b_b_in, b_norm_g, b_norm_b, b_w_s, b_b_s, b_w_out, w_router, b_router, w_expert_in, w_expert_out):
    bp, sp, d = x_prompt.shape
    bs, ss, _ = x_sample.shape
    tp = bp * sp
    ts = bs * ss
    n_tok = tp + ts
    qk = N_HEADS * DK
    vd = N_HEADS * DV
    n_main = 2 * qk + 2 * vd

    n_seq = bp + bs
    pad = (-n_seq) % SUBLANES
    c_all = jnp.concatenate([c_prompt, c_sample, jnp.zeros((pad, d), F32)])
    mod = _adaln(c_all, w_ada, b_ada)[:, :n_seq]

    sets = {"p": _Rows(bp, sp, 512), "s": _Rows(bs, ss, 256)}
    base = {"p": 0, "s": tp}
    x = {"p": x_prompt.reshape(tp, d), "s": x_sample.reshape(ts, d)}
    hm = {}
    outs = {}
    for layer in range(DEPTH):
        j = layer // 2
        mp, ms = _split_mods(mod[layer], bp)
        md = {"p": mp, "s": ms}
        lng = ln_g[layer]
        lnb = ln_b[layer]
        lhs = {}
        if layer % 2 == 0:
            w_in = a_w_in[j]
            gb = jnp.zeros((1, 2 * LANES), F32)
            gb = gb.at[0, :N_HEADS].set(a_b_gates[j, :N_HEADS])
            gb = gb.at[0, LANES:LANES + N_HEADS].set(a_b_gates[j, N_HEADS:])
            nw = a_norm_w[j].reshape(1, vd)
            proj = {}
            gates = {}
            for k, rs in sets.items():
                proj[k], gates[k] = _inproj(x[k], rs.mods(md[k][0]), rs.mods(md[k][1]), rs.mod_idx,
                                            w_in, n_main, rs.tile, BF16 if k == "p" else F32)
            lhs["p"], c_p, n_p, m_p = _mlstm_chunks(
                proj["p"], gates["p"], gb, nw,
                jnp.zeros((bp, N_HEADS, DK, DV), F32), jnp.zeros((bp, N_HEADS, LANES), F32),
                jnp.zeros((bp, N_HEADS, LANES), F32), bp, sp)
            rpad = SUBLANES - ss
            proj_s = jnp.pad(proj["s"].reshape(bs, ss, n_main), ((0, 0), (0, rpad), (0, 0)))
            gates_s = jnp.pad(gates["s"].reshape(bs, ss, 2 * LANES), ((0, 0), (0, rpad), (0, 0)))
            m0 = jnp.pad(state_mlstm_m[j], ((0, 0), (0, LANES - N_HEADS)))[:, None, :]
            ypre_s, c_s, n_s, m_s = _mlstm_short(proj_s, gates_s, gb, nw, state_mlstm_C[j],
                                                 state_mlstm_n[j].reshape(bs, 1, qk), m0, ss)
            n_s = n_s.reshape(bs, N_HEADS, DK)
            lhs["s"] = ypre_s[:, :ss].reshape(ts, vd)
            outs["C_p"], outs["n_p"], outs["m_p"] = c_p, n_p, m_p[:, :, 0]
            outs["C_s"], outs["n_s"], outs["m_s"] = c_s, n_s, m_s[:, 0, :N_HEADS]
            w_out = a_w_out[j]
        else:
            ws = b_w_s[j]
            bsv = b_b_s[j]
            w_out = b_w_out[j]

            def mixing(l, tmix):
                tri = jnp.tril(jnp.ones((l, l), bool))
                wsl = jnp.where(tri, ws[:, :l, :l], 0.0)
                eye = jnp.eye(tmix // l, dtype=F32)
                mats = jax.vmap(lambda m: jnp.kron(eye, m))(wsl).astype(BF16)
                bias = jnp.tile(bsv[:, :l], (1, tmix // l))[:, :, None]
                return mats, bias

            uv = {k: _gmlp_in(hm[k], b_w_in[j], b_b_in[j], min(1024, rs.n_seq * rs.seq_len),
                              BF16 if k == "p" else F32) for k, rs in sets.items()}
        x1 = {}
        hf = {}
        for k, rs in sets.items():
            if layer % 2 == 0:
                x1[k], hf[k] = _proj_ln(lhs[k], w_out, x[k], rs.mods(md[k][2]), lng[0:1], lnb[0:1],
                                        rs.mods(md[k][3]), rs.mods(md[k][4]), rs.mod_idx, rs.tile, 512)
            else:
                rs = _Rows(rs.n_seq, rs.seq_len, rs.tile // 2)
                mats, bias = mixing(min(CHUNK, rs.seq_len), rs.tile)
                res = _mix_proj_ln(uv[k], b_norm_g[j], b_norm_b[j], mats, bias, w_out, x[k], rs.mods(md[k][2]),
                                   lng[0:1], lnb[0:1], rs.mods(md[k][3]), rs.mods(md[k][4]), rs.mod_idx,
                                   rs.tile, 512, k == "s")
                x1[k], hf[k] = res[0], res[1]
                if k == "s":
                    outs["v_s"] = res[2]

        idx, wts, cnt, hf_all = _router(hf["p"], hf["s"], w_router, b_router, 512)
        pos, meta, n_rows = _moe_plan(idx, cnt, n_tok)
        pos = pos[:2].reshape(-1)
        ysorted = _experts(hf_all, _invert_pos(pos, n_rows), w_expert_in, w_expert_out, layer, meta)
        wts2 = wts[:2].T
        nxt = _split_mods(mod[layer + 1], bp) if layer + 1 < DEPTH else None
        for ki, (k, rs) in enumerate(sets.items()):
            rc = _Rows(rs.n_seq, rs.seq_len, 256)
            nxt_mods = None if nxt is None else (rc.mods(nxt[ki][0]), rc.mods(nxt[ki][1]))
            res = _combine_ln(ysorted, pos, wts2, base[k], x1[k], rc.mods(md[k][5]),
                              lng[1:2], lnb[1:2], nxt_mods, rc.mod_idx, 256)
            x[k] = res[0]
            if nxt is not None:
                hm[k] = res[1]

    return (x["p"].reshape(bp, sp, d), x["s"].reshape(bs, ss, d),
            outs["C_p"][None], outs["n_p"][None], outs["m_p"][None],
            outs["C_s"][None], outs["n_s"][None], outs["m_s"][None],
            outs["v_s"].reshape(bs, ss, -1)[None])
```

```python
import functools

import jax
import jax.numpy as jnp
from jax import lax
from jax.experimental import pallas as pl
from jax.experimental.pallas import tpu as pltpu

F32 = jnp.float32
BF16 = jnp.bfloat16
I32 = jnp.int32

DEPTH = 2
N_HEADS = 8
DK = 128
DV = 256
CHUNK = 128
N_GROUPS_B = 8
N_EXPERTS = 16
N_EXPERT_GROUPS = 4
EXPERTS_PER_GROUP = 4
ALPHA = float((2 * DEPTH) ** 0.25)
LN_EPS = 1e-5

LANES = 128
SUBLANES = 8
MIB = 1024 * 1024
MOE_TILE = 256
CAST_ROWS = 256
STAGE_ROWS = 128
META_TILE_EXPERT, META_FIRST, META_NEXT, META_N_USED, META_COUNT, META_OFFSET = range(6)


def _cparams(n_axes, vmem_mib):
    return pltpu.CompilerParams(
        dimension_semantics=("arbitrary",) * n_axes,
        vmem_limit_bytes=int(vmem_mib * MIB))


def _split3(x):
    hi = x.astype(BF16)
    r1 = x - hi.astype(F32)
    mid = r1.astype(BF16)
    lo = (r1 - mid.astype(F32)).astype(BF16)
    return hi, mid, lo


def _log_sigmoid(x):
    return jnp.minimum(x, 0.0) - jnp.log1p(jnp.exp(-jnp.abs(x)))


def _load_weight_bf16(w_hbm, wb, stg, sem):
    nj, k, tn = wb.shape
    ch = stg.shape[1]
    nch = k // ch

    def chunk(c):
        return pltpu.make_async_copy(w_hbm.at[pl.ds(c * ch, ch), pl.ds(0, nj * tn)], stg.at[c % 2],
                                     sem.at[c % 2])

    chunk(0).start()
    for c in range(nch):
        if c + 1 < nch:
            chunk(c + 1).start()
        chunk(c).wait()
        for jj in range(nj):
            wb[jj, c * ch:(c + 1) * ch, :] = stg[c % 2, :, jj * tn:(jj + 1) * tn].astype(BF16)


def _adaln_kernel(c_ref, w_ref, b_ref, o_ref):
    c = c_ref[...]
    a = (c * jax.nn.sigmoid(c)).astype(BF16)
    o_ref[0] = jnp.dot(a, w_ref[0].astype(BF16), preferred_element_type=F32) + b_ref[0]


def _adaln(c_all, w_ada, b_ada):
    depth, d, n = w_ada.shape
    r = c_all.shape[0]
    tn = 1024
    return pl.pallas_call(
        _adaln_kernel,
        grid=(depth, n // tn),
        in_specs=[pl.BlockSpec((r, d), lambda l, j: (0, 0)),
                  pl.BlockSpec((1, d, tn), lambda l, j: (l, 0, j)),
                  pl.BlockSpec((1, 1, tn), lambda l, j: (l, 0, j))],
        out_specs=pl.BlockSpec((1, r, tn), lambda l, j: (l, 0, j)),
        out_shape=jax.ShapeDtypeStruct((depth, r, n), F32),
        compiler_params=_cparams(2, 32),
        name="adaln",
    )(c_all, w_ada, b_ada.reshape(depth, 1, n))


def _inproj_kernel(x_ref, sh_ref, sc_ref, w_hbm, proj_ref, gates_ref, hin_ref, wb, stg, g16, wgs, sem):
    j = pl.program_id(1)

    @pl.when((pl.program_id(0) == 0) & (j == 0))
    def _():
        _load_weight_bf16(w_hbm, wb, stg, sem)
        n_main = wb.shape[0] * wb.shape[2]
        cp = pltpu.make_async_copy(w_hbm.at[:, pl.ds(n_main, 2 * N_HEADS)], g16, sem.at[0])
        cp.start()
        cp.wait()
        g = g16[...]
        z = jnp.zeros((g.shape[0], LANES - N_HEADS), F32)
        wg = jnp.concatenate([g[:, :N_HEADS], z, g[:, N_HEADS:], z], axis=1)
        hi = wg.astype(BF16)
        wgs[0] = hi
        wgs[1] = (wg - hi.astype(F32)).astype(BF16)

    @pl.when(j == 0)
    def _():
        h = x_ref[...] * (1.0 + sc_ref[0]) + sh_ref[0]
        hb = h.astype(BF16)
        hin_ref[...] = hb
        h_lo = (h - hb.astype(F32)).astype(BF16)
        gates_ref[...] = (jnp.dot(hb, wgs[0], preferred_element_type=F32)
                          + jnp.dot(h_lo, wgs[0], preferred_element_type=F32)
                          + jnp.dot(hb, wgs[1], preferred_element_type=F32))

    proj_ref[...] = jnp.dot(hin_ref[...], wb[j], preferred_element_type=F32).astype(proj_ref.dtype)


def _inproj(x, sh, sc, mod_idx, w_in, n_main, tm, out_dtype):
    t, d = x.shape
    tn = 1024
    r = sh.shape[1]
    mspec = pl.BlockSpec((1, r, d), lambda i, j: (mod_idx(i), 0, 0))
    return pl.pallas_call(
        _inproj_kernel,
        grid=(t // tm, n_main // tn),
        in_specs=[pl.BlockSpec((tm, d), lambda i, j: (i, 0)), mspec, mspec,
                  pl.BlockSpec(memory_space=pl.ANY)],
        out_specs=[pl.BlockSpec((tm, tn), lambda i, j: (i, j)),
                   pl.BlockSpec((tm, 2 * LANES), lambda i, j: (i, 0))],
        out_shape=[jax.ShapeDtypeStruct((t, n_main), out_dtype),
                   jax.ShapeDtypeStruct((t, 2 * LANES), F32)],
        scratch_shapes=[pltpu.VMEM((tm, d), BF16), pltpu.VMEM((n_main // tn, d, tn), BF16),
                        pltpu.VMEM((2, STAGE_ROWS, n_main), F32), pltpu.VMEM((d, 2 * N_HEADS), F32),
                        pltpu.VMEM((2, d, 2 * LANES), BF16), pltpu.SemaphoreType.DMA((2,))],
        compiler_params=_cparams(2, 56),
        name="mlstm_inproj",
    )(x, sh, sc, w_in)


def _head_norm_gate(hh, nw, o):
    mu = jnp.mean(hh, axis=1, keepdims=True)
    xc = hh - mu
    var = jnp.mean(xc * xc, axis=1, keepdims=True)
    return jax.nn.sigmoid(o) * (xc * lax.rsqrt(var + LN_EPS) * nw)


def _mlstm_chunk_kernel(q_ref, k_ref, v_ref, o_ref, g_ref, gb_ref, nw_ref, c0_ref, n0_ref, m0_ref,
                        y_ref, cout_ref, nout_ref, mout_ref, c_scr, n_scr, m_scr):
    ci = pl.program_id(1)
    L = q_ref.shape[0]

    @pl.when(ci == 0)
    def _():
        c_scr[...] = c0_ref[0]
        n_scr[...] = n0_ref[0]
        m_scr[...] = m0_ref[0]

    g = g_ref[...] + gb_ref[...]
    gi = g[:, :LANES]
    lf = _log_sigmoid(g[:, LANES:])
    row = lax.broadcasted_iota(I32, (L, L), 0)
    col = lax.broadcasted_iota(I32, (L, L), 1)
    causal = col <= row
    ltri = jnp.where(causal, 1.0, 0.0).astype(BF16)
    hi, mid, lo = _split3(lf)
    bcum = (jnp.dot(ltri, hi, preferred_element_type=F32)
            + jnp.dot(ltri, mid, preferred_element_type=F32)
            + jnp.dot(ltri, lo, preferred_element_type=F32))
    a = gi - bcum
    a_t = a.T
    scale = DK ** -0.5
    nt = (((1,), (1,)), ((), ()))
    tn_dims = (((0,), (0,)), ((), ()))
    for h in range(N_HEADS):
        ks = slice(h * DK, (h + 1) * DK)
        vs = slice(h * DV, (h + 1) * DV)
        qf = q_ref[:, ks].astype(F32) * scale
        qb = qf.astype(BF16)
        kf = k_ref[:, ks].astype(F32)
        kb = kf.astype(BF16)
        vb = v_ref[:, vs].astype(BF16)
        a_row = a_t[h:h + 1, :]
        a_col = a[:, h:h + 1]
        b_col = bcum[:, h:h + 1]
        m_prev = m_scr[h:h + 1, 0:1]
        amat = jnp.where(causal, a_row, -jnp.inf)
        mx = jnp.max(amat, axis=1, keepdims=True)
        m_inter = b_col + m_prev
        m_t = jnp.maximum(m_inter, b_col + mx)
        dm = jnp.exp(amat + (b_col - m_t))
        s = lax.dot_general(qb, kb, nt, preferred_element_type=F32)
        scores = s * dm
        inter = jnp.exp(m_inter - m_t)
        c_old = c_scr[h]
        n_old = n_scr[h:h + 1, :]
        qc = jnp.dot(qb, c_old.astype(BF16), preferred_element_type=F32)
        num = jnp.dot(scores.astype(BF16), vb, preferred_element_type=F32) + inter * qc
        qn = jnp.sum(qf * n_old, axis=1, keepdims=True)
        den = jnp.sum(scores, axis=1, keepdims=True) + inter * qn
        hh = num / jnp.maximum(jnp.abs(den), jnp.exp(-m_t))
        m_new = m_t[L - 1:L, :]
        b_last = b_col[L - 1:L, :]
        w_col = jnp.exp(b_last + a_col - m_new)
        decay = jnp.exp(b_last + m_prev - m_new)
        kw = kf * w_col
        c_scr[h] = decay * c_old + lax.dot_general(kw.astype(BF16), vb, tn_dims,
                                                   preferred_element_type=F32)
        n_scr[h:h + 1, :] = decay * n_old + jnp.sum(kw, axis=0, keepdims=True)
        m_scr[h:h + 1, :] = jnp.broadcast_to(m_new, (1, LANES))
        y_ref[:, vs] = _head_norm_gate(hh, nw_ref[:, vs], o_ref[:, vs].astype(F32)).astype(y_ref.dtype)

    @pl.when(ci == pl.num_programs(1) - 1)
    def _():
        cout_ref[0] = c_scr[...]
        nout_ref[0] = n_scr[...]
        mout_ref[0] = m_scr[...]


def _mlstm_chunks(proj, gates, gb, nw, c0, n0, m0, batch, seq):
    nc = seq // CHUNK
    qk = N_HEADS * DK
    vd = N_HEADS * DV
    row = lambda b, c: b * nc + c
    st4 = pl.BlockSpec((1, N_HEADS, DK, DV), lambda b, c: (b, 0, 0, 0))
    st3 = pl.BlockSpec((1, N_HEADS, LANES), lambda b, c: (b, 0, 0))
    return pl.pallas_call(
        _mlstm_chunk_kernel,
        grid=(batch, nc),
        in_specs=[pl.BlockSpec((CHUNK, qk), lambda b, c: (row(b, c), 0)),
                  pl.BlockSpec((CHUNK, qk), lambda b, c: (row(b, c), 1)),
                  pl.BlockSpec((CHUNK, vd), lambda b, c: (row(b, c), 1)),
                  pl.BlockSpec((CHUNK, vd), lambda b, c: (row(b, c), 2)),
                  pl.BlockSpec((CHUNK, 2 * LANES), lambda b, c: (row(b, c), 0)),
                  pl.BlockSpec((1, 2 * LANES), lambda b, c: (0, 0)),
                  pl.BlockSpec((1, vd), lambda b, c: (0, 0)),
                  st4, st3, st3],
        out_specs=[pl.BlockSpec((CHUNK, vd), lambda b, c: (row(b, c), 0)), st4, st3, st3],
        out_shape=[jax.ShapeDtypeStruct((batch * seq, vd), BF16),
                   jax.ShapeDtypeStruct((batch, N_HEADS, DK, DV), F32),
                   jax.ShapeDtypeStruct((batch, N_HEADS, LANES), F32),
                   jax.ShapeDtypeStruct((batch, N_HEADS, LANES), F32)],
        scratch_shapes=[pltpu.VMEM((N_HEADS, DK, DV), F32),
                        pltpu.VMEM((N_HEADS, LANES), F32),
                        pltpu.VMEM((N_HEADS, LANES), F32)],
        compiler_params=_cparams(2, 40),
        name="mlstm_chunks",
    )(proj, proj, proj, proj, gates, gb, nw, c0, n0, m0)


def _per_head(x, width):
    return jnp.concatenate(
        [jnp.broadcast_to(x[..., h:h + 1], x.shape[:-1] + (width,)) for h in range(N_HEADS)], axis=-1)


def _head_sums(x, width):
    lane = lax.broadcasted_iota(I32, x.shape[:-1] + (LANES,), x.ndim - 1)
    out = jnp.zeros(x.shape[:-1] + (LANES,), F32)
    for h in range(N_HEADS):
        s = jnp.sum(x[..., h * width:(h + 1) * width], axis=-1, keepdims=True)
        out = jnp.where(lane == h, s, out)
    return out


def _mlstm_short_kernel(seq, q_ref, k_ref, v_ref, o_ref, g_ref, gb_ref, nw_ref, c0_ref, n0_ref, m0_ref,
                        y_ref, cout_ref, nout_ref, mout_ref, qc_scr, kw_scr):
    bt, rows, _ = q_ref.shape
    scale = DK ** -0.5
    tn_dims = (((0,), (0,)), ((), ()))
    row = lax.broadcasted_iota(I32, (bt, rows, LANES), 1)
    g = g_ref[...] + gb_ref[...]
    gi = g[:, :, :LANES]
    lf = _log_sigmoid(g[:, :, LANES:])
    bcum = jnp.zeros_like(lf)
    for s in range(seq):
        bcum = bcum + jnp.where(row >= s, lf[:, s:s + 1, :], 0.0)
    a = gi - bcum
    mx = jnp.full_like(a, -jnp.inf)
    for s in range(seq):
        mx = jnp.maximum(mx, jnp.where(row >= s, a[:, s:s + 1, :], -jnp.inf))
    m_prev = m0_ref[...]
    m_inter = bcum + m_prev
    m_t = jnp.maximum(m_inter, bcum + mx)
    cmt = bcum - m_t
    inter = jnp.exp(m_inter - m_t)
    einv = jnp.exp(-m_t)
    m_new = m_t[:, seq - 1:seq, :]
    b_last = bcum[:, seq - 1:seq, :]
    w = jnp.where(row < seq, jnp.exp(b_last + a - m_new), 0.0)
    decay = jnp.exp(b_last + m_prev - m_new)
    mout_ref[...] = m_new

    q = q_ref[...] * scale
    k = k_ref[...]
    v = v_ref[...]
    n_old = n0_ref[...]
    for b in range(bt):
        for h in range(N_HEADS):
            qc_scr[b, :, h * DV:(h + 1) * DV] = jnp.dot(
                (q_ref[b, :, h * DK:(h + 1) * DK] * scale).astype(BF16), c0_ref[b, h].astype(BF16),
                preferred_element_type=F32)
    den = inter * _head_sums(q * n_old, DK)
    num = _per_head(inter, DV) * qc_scr[...]
    for s in range(seq):
        p = _head_sums(q * k[:, s:s + 1, :], DK) * jnp.where(row >= s, jnp.exp(cmt + a[:, s:s + 1, :]), 0.0)
        den = den + p
        num = num + _per_head(p, DV) * v[:, s:s + 1, :]
    hh = num * _per_head(1.0 / jnp.maximum(jnp.abs(den), einv), DV)
    mu = _head_sums(hh, DV) * (1.0 / DV)
    xc = hh - _per_head(mu, DV)
    var = _head_sums(xc * xc, DV) * (1.0 / DV)
    hn = xc * _per_head(lax.rsqrt(var + LN_EPS), DV) * nw_ref[...]
    y_ref[...] = jax.nn.sigmoid(o_ref[...]) * hn

    kw = k * _per_head(w, DK)
    kw_scr[...] = kw
    nout_ref[...] = _per_head(decay, DK) * n_old + jnp.sum(kw, axis=1, keepdims=True)
    for b in range(bt):
        for h in range(N_HEADS):
            cout_ref[b, h] = (decay[b, :, h:h + 1] * c0_ref[b, h]
                              + lax.dot_general(kw_scr[b, :, h * DK:(h + 1) * DK],
                                                v_ref[b, :, h * DV:(h + 1) * DV],
                                                tn_dims, preferred_element_type=F32))


def _mlstm_short(proj, gates, gb, nw, c0, n0, m0, seq):
    batch, rows, _ = proj.shape
    qk = N_HEADS * DK
    vd = N_HEADS * DV
    bt = 8
    st4 = pl.BlockSpec((bt, N_HEADS, DK, DV), lambda i: (i, 0, 0, 0))
    st3 = pl.BlockSpec((bt, 1, qk), lambda i: (i, 0, 0))
    stm = pl.BlockSpec((bt, 1, LANES), lambda i: (i, 0, 0))
    return pl.pallas_call(
        functools.partial(_mlstm_short_kernel, seq),
        grid=(batch // bt,),
        in_specs=[pl.BlockSpec((bt, rows, qk), lambda i: (i, 0, 0)),
                  pl.BlockSpec((bt, rows, qk), lambda i: (i, 0, 1)),
                  pl.BlockSpec((bt, rows, vd), lambda i: (i, 0, 1)),
                  pl.BlockSpec((bt, rows, vd), lambda i: (i, 0, 2)),
                  pl.BlockSpec((bt, rows, 2 * LANES), lambda i: (i, 0, 0)),
                  pl.BlockSpec((1, 2 * LANES), lambda i: (0, 0)),
                  pl.BlockSpec((1, vd), lambda i: (0, 0)),
                  st4, st3, stm],
        out_specs=[pl.BlockSpec((bt, rows, vd), lambda i: (i, 0, 0)), st4, st3, stm],
        out_shape=[jax.ShapeDtypeStruct((batch, rows, vd), F32),
                   jax.ShapeDtypeStruct((batch, N_HEADS, DK, DV), F32),
                   jax.ShapeDtypeStruct((batch, 1, qk), F32),
                   jax.ShapeDtypeStruct((batch, 1, LANES), F32)],
        scratch_shapes=[pltpu.VMEM((bt, rows, vd), F32), pltpu.VMEM((bt, rows, qk), F32)],
        compiler_params=_cparams(1, 48),
        name="mlstm_short",
    )(proj, proj, proj, proj, gates, gb, nw, c0, n0, m0)


def _residual_ln(zbuf, xres_ref, gate_ref, lng_ref, lnb_ref, xo_ref, nxt):
    nc, tm, tn = zbuf.shape
    inv_d = 1.0 / (nc * tn)
    ssum = jnp.zeros((tm, 1), F32)
    for c in range(nc):
        sl = pl.ds(c * tn, tn)
        z = ALPHA * xres_ref[:, sl] + gate_ref[0, :, sl] * zbuf[c]
        zbuf[c] = z
        ssum = ssum + jnp.sum(z, axis=1, keepdims=True)
    mu = ssum * inv_d
    vsum = jnp.zeros((tm, 1), F32)
    for c in range(nc):
        zc = zbuf[c] - mu
        vsum = vsum + jnp.sum(zc * zc, axis=1, keepdims=True)
    rstd = lax.rsqrt(vsum * inv_d + LN_EPS)
    for c in range(nc):
        sl = pl.ds(c * tn, tn)
        xn = (zbuf[c] - mu) * rstd * lng_ref[:, sl] + lnb_ref[:, sl]
        xo_ref[:, sl] = xn
        if nxt is not None:
            sh_ref, sc_ref, ho_ref = nxt
            ho_ref[:, sl] = (xn * (1.0 + sc_ref[0, :, sl]) + sh_ref[0, :, sl]).astype(ho_ref.dtype)


def _proj_ln_kernel(lhs_ref, w_hbm, xres_ref, gate_ref, lng_ref, lnb_ref, shn_ref, scn_ref,
                    xo_ref, ho_ref, wb, stg, ybuf, sem):
    @pl.when(pl.program_id(0) == 0)
    def _():
        _load_weight_bf16(w_hbm, wb, stg, sem)

    lhs = lhs_ref[...].astype(BF16)
    for c in range(ybuf.shape[0]):
        ybuf[c] = jnp.dot(lhs, wb[c], preferred_element_type=F32)
    _residual_ln(ybuf, xres_ref, gate_ref, lng_ref, lnb_ref, xo_ref, (shn_ref, scn_ref, ho_ref))


def _proj_ln(lhs, w, xres, gate, lng, lnb, shn, scn, mod_idx, tm, tn):
    t, k = lhs.shape
    d = w.shape[1]
    r = gate.shape[1]
    mspec = pl.BlockSpec((1, r, d), lambda i: (mod_idx(i), 0, 0))
    vspec = pl.BlockSpec((1, d), lambda i: (0, 0))
    ospec = pl.BlockSpec((tm, d), lambda i: (i, 0))
    return pl.pallas_call(
        _proj_ln_kernel,
        grid=(t // tm,),
        in_specs=[pl.BlockSpec((tm, k), lambda i: (i, 0)),
                  pl.BlockSpec(memory_space=pl.ANY),
                  ospec, mspec, vspec, vspec, mspec, mspec],
        out_specs=[ospec, ospec],
        out_shape=[jax.ShapeDtypeStruct((t, d), F32), jax.ShapeDtypeStruct((t, d), F32)],
        scratch_shapes=[pltpu.VMEM((d // tn, k, tn), BF16), pltpu.VMEM((2, CAST_ROWS, d), F32),
                        pltpu.VMEM((d // tn, tm, tn), F32), pltpu.SemaphoreType.DMA((2,))],
        compiler_params=_cparams(1, 58),
        name="proj_ln",
    )(lhs, w, xres, gate, lng, lnb, shn, scn)


def _gelu_tanh(x):
    return x * (0.5 * (1.0 + jnp.tanh(0.7978845608028654 * (x + 0.044715 * (x * x * x)))))


def _gmlp_in_kernel(h_ref, w_hbm, b_ref, o_ref, wb, stg, sem):
    j = pl.program_id(1)

    @pl.when((pl.program_id(0) == 0) & (j == 0))
    def _():
        _load_weight_bf16(w_hbm, wb, stg, sem)

    acc = jnp.dot(h_ref[...].astype(BF16), wb[j], preferred_element_type=F32)
    o_ref[...] = _gelu_tanh(acc + b_ref[...]).astype(o_ref.dtype)


def _gmlp_in(h, w, b, tm, out_dtype):
    t, d = h.shape
    n = w.shape[1]
    tn = 1024
    return pl.pallas_call(
        _gmlp_in_kernel,
        grid=(t // tm, n // tn),
        in_specs=[pl.BlockSpec((tm, d), lambda i, j: (i, 0)),
                  pl.BlockSpec(memory_space=pl.ANY),
                  pl.BlockSpec((1, tn), lambda i, j: (0, j))],
        out_specs=pl.BlockSpec((tm, tn), lambda i, j: (i, j)),
        out_shape=jax.ShapeDtypeStruct((t, n), out_dtype),
        scratch_shapes=[pltpu.VMEM((n // tn, d, tn), BF16), pltpu.VMEM((2, STAGE_ROWS, n), F32),
                        pltpu.SemaphoreType.DMA((2,))],
        compiler_params=_cparams(2, 58),
        name="gmlp_in",
    )(h, w, b.reshape(1, n))


def _gmlp_mix_rows(u_ref, v_ref, g_ref, b_ref, mix_ref, bias_ref, o_ref, vn_ref):
    v = v_ref[...].astype(F32)
    mu = jnp.mean(v, axis=1, keepdims=True)
    xc = v - mu
    var = jnp.mean(xc * xc, axis=1, keepdims=True)
    vn = xc * lax.rsqrt(var + LN_EPS) * g_ref[...] + b_ref[...]
    if vn_ref is not None:
        vn_ref[...] = vn
    gd = v.shape[1] // N_GROUPS_B
    for g in range(N_GROUPS_B):
        sl = slice(g * gd, (g + 1) * gd)
        mixed = jnp.dot(mix_ref[g], vn[:, sl].astype(BF16), preferred_element_type=F32) + bias_ref[g]
        o_ref[:, sl] = (u_ref[:, sl].astype(F32) * mixed).astype(o_ref.dtype)


def _mix_proj_ln_kernel(emit_v, u_ref, v_ref, g_ref, b_ref, mix_ref, bias_ref, w_hbm, xres_ref, gate_ref,
                        lng_ref, lnb_ref, shn_ref, scn_ref, *rest):
    if emit_v:
        xo_ref, ho_ref, vn_ref, lhs_scr, wb, stg, ybuf, sem = rest
    else:
        xo_ref, ho_ref, lhs_scr, wb, stg, ybuf, sem = rest
        vn_ref = None

    @pl.when(pl.program_id(0) == 0)
    def _():
        _load_weight_bf16(w_hbm, wb, stg, sem)

    _gmlp_mix_rows(u_ref, v_ref, g_ref, b_ref, mix_ref, bias_ref, lhs_scr, vn_ref)
    lhs = lhs_scr[...]
    for c in range(ybuf.shape[0]):
        ybuf[c] = jnp.dot(lhs, wb[c], preferred_element_type=F32)
    _residual_ln(ybuf, xres_ref, gate_ref, lng_ref, lnb_ref, xo_ref, (shn_ref, scn_ref, ho_ref))


def _mix_proj_ln(uv, nv_g, nv_b, mix, bias, w, xres, gate, lng, lnb, shn, scn, mod_idx, tm, tn, emit_v):
    t, n2 = uv.shape
    di = n2 // 2
    d = w.shape[1]
    r = gate.shape[1]
    mspec = pl.BlockSpec((1, r, d), lambda i: (mod_idx(i), 0, 0))
    vspec = pl.BlockSpec((1, d), lambda i: (0, 0))
    ospec = pl.BlockSpec((tm, d), lambda i: (i, 0))
    nspec = pl.BlockSpec((1, di), lambda i: (0, 0))
    out_specs = [ospec, ospec]
    out_shape = [jax.ShapeDtypeStruct((t, d), F32), jax.ShapeDtypeStruct((t, d), F32)]
    if emit_v:
        out_specs.append(pl.BlockSpec((tm, di), lambda i: (i, 0)))
        out_shape.append(jax.ShapeDtypeStruct((t, di), F32))
    return pl.pallas_call(
        functools.partial(_mix_proj_ln_kernel, emit_v),
        grid=(t // tm,),
        in_specs=[pl.BlockSpec((tm, di), lambda i: (i, 0)),
                  pl.BlockSpec((tm, di), lambda i: (i, 1)),
                  nspec, nspec,
                  pl.BlockSpec(mix.shape, lambda i: (0, 0, 0)),
                  pl.BlockSpec(bias.shape, lambda i: (0, 0, 0)),
                  pl.BlockSpec(memory_space=pl.ANY),
                  ospec, mspec, vspec, vspec, mspec, mspec],
        out_specs=out_specs,
        out_shape=out_shape,
        scratch_shapes=[pltpu.VMEM((tm, di), BF16), pltpu.VMEM((d // tn, di, tn), BF16),
                        pltpu.VMEM((2, CAST_ROWS, d), F32), pltpu.VMEM((d // tn, tm, tn), F32),
                        pltpu.SemaphoreType.DMA((2,))],
        compiler_params=_cparams(1, 58),
        name="gmlp_mix_proj_ln",
    )(uv, uv, nv_g.reshape(1, di), nv_b.reshape(1, di), mix, bias, w, xres, gate, lng, lnb, shn, scn)


def _router_kernel(n_first, ha_ref, hb_ref, wr_ref, br_ref, idx_ref, wts_ref, cnt_ref, hall_ref, carry):
    i = pl.program_id(0)
    tr = ha_ref.shape[0]

    @pl.when(i == 0)
    def _():
        carry[...] = jnp.zeros(carry.shape, carry.dtype)

    nt = (((1,), (1,)), ((), ()))
    h = jnp.where(i < n_first, ha_ref[...], hb_ref[...])
    hall_ref[...] = h
    hb = h.astype(BF16)
    hl = (h - hb.astype(F32)).astype(BF16)
    wr = wr_ref[...]
    wb = wr.astype(BF16)
    wl = (wr - wb.astype(F32)).astype(BF16)
    logits = (lax.dot_general(wb, hb, nt, preferred_element_type=F32)
              + lax.dot_general(wb, hl, nt, preferred_element_type=F32)
              + lax.dot_general(wl, hb, nt, preferred_element_type=F32))
    s = jax.nn.sigmoid(logits)
    sel = s + br_ref[...]
    epg = EXPERTS_PER_GROUP
    r = [sel[e:e + 1, :] for e in range(N_EXPERTS)]
    su = [s[e:e + 1, :] for e in range(N_EXPERTS)]

    def top2sum(v):
        best = v[0] + v[1]
        for x in range(epg):
            for y in range(x + 1, epg):
                if (x, y) != (0, 1):
                    best = jnp.maximum(best, v[x] + v[y])
        return best

    gs = [top2sum(r[g * epg:(g + 1) * epg]) for g in range(N_EXPERT_GROUPS)]
    gbest = gs[0]
    gidx = jnp.zeros((1, tr), I32)
    for g in range(1, N_EXPERT_GROUPS):
        better = gs[g] > gbest
        gidx = jnp.where(better, g, gidx)
        gbest = jnp.where(better, gs[g], gbest)
    v = list(r[:epg])
    sv = list(su[:epg])
    for g in range(1, N_EXPERT_GROUPS):
        pick = gidx == g
        for x in range(epg):
            v[x] = jnp.where(pick, r[g * epg + x], v[x])
            sv[x] = jnp.where(pick, su[g * epg + x], sv[x])
    i1 = jnp.zeros((1, tr), I32)
    b1 = v[0]
    w1 = sv[0]
    for x in range(1, epg):
        better = v[x] > b1
        i1 = jnp.where(better, x, i1)
        b1 = jnp.where(better, v[x], b1)
        w1 = jnp.where(better, sv[x], w1)
    i2 = jnp.zeros((1, tr), I32)
    b2 = jnp.full((1, tr), -jnp.inf, F32)
    w2 = jnp.zeros((1, tr), F32)
    for x in range(epg):
        take = (i1 != x) & (v[x] > b2)
        i2 = jnp.where(take, x, i2)
        b2 = jnp.where(take, v[x], b2)
        w2 = jnp.where(take, sv[x], w2)
    wsum = w1 + w2
    e1 = gidx * epg + i1
    e2 = gidx * epg + i2
    eio = lax.broadcasted_iota(I32, (N_EXPERTS, tr), 0)
    hit1 = eio == e1
    hit2 = eio == e2
    oh = jnp.where(hit1 | hit2, 1.0, 0.0)
    ri = lax.broadcasted_iota(I32, (tr, tr), 0)
    cj = lax.broadcasted_iota(I32, (tr, tr), 1)
    before = jnp.where(ri < cj, 1.0, 0.0).astype(BF16)
    rank = jnp.dot(oh.astype(BF16), before, preferred_element_type=F32) + carry[:, 0:1]
    rank1 = jnp.sum(jnp.where(hit1, rank, 0.0), axis=0, keepdims=True)
    rank2 = jnp.sum(jnp.where(hit2, rank, 0.0), axis=0, keepdims=True)
    idx_ref[0:1, :] = e1
    idx_ref[1:2, :] = e2
    idx_ref[2:3, :] = rank1.astype(I32)
    idx_ref[3:4, :] = rank2.astype(I32)
    idx_ref[4:8, :] = jnp.zeros((4, tr), I32)
    wts_ref[0:1, :] = w1 / wsum
    wts_ref[1:2, :] = w2 / wsum
    wts_ref[2:8, :] = jnp.zeros((6, tr), F32)
    carry[...] = carry[...] + jnp.sum(oh, axis=1, keepdims=True)
    cnt_ref[...] = carry[...]


def _two_set_specs(ha, hb, tile):
    d = ha.shape[1]
    na = ha.shape[0] // tile
    nb = hb.shape[0] // tile
    return na, nb, [pl.BlockSpec((tile, d), lambda i, *_: (jnp.minimum(i, na - 1), 0)),
                    pl.BlockSpec((tile, d), lambda i, *_: (jnp.maximum(i - na, 0), 0))]


def _router(ha, hb, w_router, b_router, tr):
    d = ha.shape[1]
    t = ha.shape[0] + hb.shape[0]
    na, nb, hspecs = _two_set_specs(ha, hb, tr)
    return pl.pallas_call(
        functools.partial(_router_kernel, na),
        grid=(na + nb,),
        in_specs=hspecs + [pl.BlockSpec((N_EXPERTS, d), lambda i: (0, 0)),
                           pl.BlockSpec((N_EXPERTS, 1), lambda i: (0, 0))],
        out_specs=[pl.BlockSpec((SUBLANES, tr), lambda i: (0, i)),
                   pl.BlockSpec((SUBLANES, tr), lambda i: (0, i)),
                   pl.BlockSpec((N_EXPERTS, LANES), lambda i: (0, 0)),
                   pl.BlockSpec((tr, d), lambda i: (i, 0))],
        out_shape=[jax.ShapeDtypeStruct((SUBLANES, t), I32),
                   jax.ShapeDtypeStruct((SUBLANES, t), F32),
                   jax.ShapeDtypeStruct((N_EXPERTS, LANES), F32),
                   jax.ShapeDtypeStruct((t, d), F32)],
        scratch_shapes=[pltpu.VMEM((N_EXPERTS, LANES), F32)],
        compiler_params=_cparams(1, 40),
        name="moe_router",
    )(ha, hb, w_router.T, b_router.reshape(N_EXPERTS, 1))


def _invert_kernel(n_tok, pos_ref, inv_ref):
    def clear(p, c):
        inv_ref[p] = 0
        return c

    lax.fori_loop(0, inv_ref.shape[0], clear, 0, unroll=16)

    def place(t, c):
        inv_ref[pos_ref[t]] = t
        inv_ref[pos_ref[n_tok + t]] = t
        return c

    lax.fori_loop(0, n_tok, place, 0, unroll=8)


def _invert_pos(pos, n_rows):
    n_tok = pos.shape[0] // 2
    return pl.pallas_call(
        functools.partial(_invert_kernel, n_tok),
        grid_spec=pltpu.PrefetchScalarGridSpec(
            num_scalar_prefetch=1, grid=(1,), in_specs=[],
            out_specs=pl.BlockSpec(memory_space=pltpu.SMEM)),
        out_shape=jax.ShapeDtypeStruct((n_rows,), I32),
        compiler_params=_cparams(1, 16),
        name="moe_invert",
    )(pos)


def _cast_rows(src, dst):
    def body(c, carry):
        r = pl.multiple_of(c * CAST_ROWS, CAST_ROWS)
        dst[pl.ds(r, CAST_ROWS), :] = src[pl.ds(r, CAST_ROWS), :].astype(BF16)
        return carry

    lax.fori_loop(0, src.shape[0] // CAST_ROWS, body, 0)


def _expert_kernel(layer, meta_ref, inv_ref, h_hbm, w1_hbm, w2_hbm, y_ref,
                   xring, stg1, stg2, wb1, wb2, sem, xsem):
    i = pl.program_id(0)
    tm = y_ref.shape[0]
    n_used = meta_ref[META_N_USED, 0]
    n_slots = xring.shape[0]
    ahead = n_slots - 1

    def fetch(e):
        return (pltpu.make_async_copy(w1_hbm.at[layer, e], stg1, sem.at[0]),
                pltpu.make_async_copy(w2_hbm.at[layer, e], stg2, sem.at[1]))

    def row_copy(tile, slot, r):
        return pltpu.make_async_copy(h_hbm.at[pl.ds(inv_ref[tile * tm + r], 1)],
                                     xring.at[slot, pl.ds(r, 1)], xsem.at[slot])

    def wait_rows(slot):
        pltpu.make_async_copy(h_hbm.at[pl.ds(0, tm)], xring.at[slot], xsem.at[slot]).wait()

    @pl.when(i == 0)
    def _():
        for cp in fetch(meta_ref[META_TILE_EXPERT, 0]):
            cp.start()
        for a in range(ahead):
            def start_rows(r, c, a=a):
                row_copy(jnp.minimum(a, n_used - 1), a, r).start()
                return c

            lax.fori_loop(0, tm, start_rows, 0, unroll=8)

    @pl.when(meta_ref[META_FIRST, i] == 1)
    def _():
        for cp in fetch(0):
            cp.wait()
        _cast_rows(stg1, wb1)
        _cast_rows(stg2, wb2)

        @pl.when(meta_ref[META_NEXT, i] >= 0)
        def _():
            for cp in fetch(meta_ref[META_NEXT, i]):
                cp.start(priority=1)

    @pl.when(i < n_used)
    def _():
        slot = i % n_slots
        wait_rows(slot)
        hid = jnp.dot(xring[slot].astype(BF16), wb1[...], preferred_element_type=F32)
        f = hid.shape[1] // 2
        a = hid[:, :f]
        g = hid[:, f:]
        z = (g * jax.nn.sigmoid(g) * a).astype(BF16)
        y_ref[...] = jnp.dot(z, wb2[...], preferred_element_type=F32)
        nxt_tile = jnp.minimum(i + ahead, n_used - 1)
        nxt_slot = (i + ahead) % n_slots
        for r in range(tm):
            row_copy(nxt_tile, nxt_slot, r).start(priority=r % 2)

        @pl.when(i == n_used - 1)
        def _():
            for a in range(1, ahead + 1):
                wait_rows((i + a) % n_slots)

    @pl.when(i >= n_used)
    def _():
        y_ref[...] = jnp.zeros(y_ref.shape, y_ref.dtype)


def _experts(h_all, inv, w1, w2, layer, meta):
    d = h_all.shape[1]
    p = inv.shape[0]
    tm = MOE_TILE
    ff2 = w1.shape[3]
    ff = ff2 // 2
    return pl.pallas_call(
        functools.partial(_expert_kernel, layer),
        grid_spec=pltpu.PrefetchScalarGridSpec(
            num_scalar_prefetch=2,
            grid=(p // tm,),
            in_specs=[pl.BlockSpec(memory_space=pl.ANY),
                      pl.BlockSpec(memory_space=pl.ANY),
                      pl.BlockSpec(memory_space=pl.ANY)],
            out_specs=pl.BlockSpec((tm, d), lambda i, *_: (i, 0)),
            scratch_shapes=[pltpu.VMEM((3, tm, d), F32),
                            pltpu.VMEM((d, ff2), F32), pltpu.VMEM((ff, d), F32),
                            pltpu.VMEM((d, ff2), BF16), pltpu.VMEM((ff, d), BF16),
                            pltpu.SemaphoreType.DMA((2,)), pltpu.SemaphoreType.DMA((3,))]),
        out_shape=jax.ShapeDtypeStruct((p, d), F32),
        compiler_params=_cparams(1, 56),
        name="moe_experts",
    )(meta, inv, h_all, w1, w2)


def _combine_kernel(n_tok, tok_base, has_next, pos_ref, w_ref, y_hbm, xres_ref, gate_ref, lng_ref, lnb_ref,
                    *rest):
    if has_next:
        shn_ref, scn_ref, xo_ref, ho_ref, gbuf, fbuf, sem = rest
        nxt = (shn_ref, scn_ref, ho_ref)
    else:
        xo_ref, gbuf, fbuf, sem = rest
        nxt = None
    tc = xres_ref.shape[0]
    i = pl.program_id(0)
    n_steps = pl.num_programs(0)

    n_slots = gbuf.shape[0]
    ahead = n_slots - 1

    def row_copy(step, slot, r, k):
        p = pos_ref[k * n_tok + tok_base + step * tc + r]
        return pltpu.make_async_copy(y_hbm.at[pl.ds(p, 1)], gbuf.at[slot, k, pl.ds(r, 1)], sem.at[slot])

    def wait_tile(slot):
        for k in range(2):
            pltpu.make_async_copy(y_hbm.at[pl.ds(0, tc)], gbuf.at[slot, k], sem.at[slot]).wait()

    @pl.when(i == 0)
    def _():
        for a in range(ahead):
            def start_rows(r, c, a=a):
                for k in range(2):
                    row_copy(jnp.minimum(a, n_steps - 1), a, r, k).start()
                return c

            lax.fori_loop(0, tc, start_rows, 0, unroll=8)

    slot = i % n_slots
    wait_tile(slot)
    fbuf[0] = w_ref[:, 0:1] * gbuf[slot, 0] + w_ref[:, 1:2] * gbuf[slot, 1]
    _residual_ln(fbuf, xres_ref, gate_ref, lng_ref, lnb_ref, xo_ref, nxt)

    nxt_step = jnp.minimum(i + ahead, n_steps - 1)
    nxt_slot = (i + ahead) % n_slots
    for r in range(tc):
        for k in range(2):
            row_copy(nxt_step, nxt_slot, r, k).start()

    @pl.when(i == n_steps - 1)
    def _():
        for a in range(1, ahead + 1):
            wait_tile((i + a) % n_slots)


def _combine_ln(ys, pos, wts, tok_base, xres, gate, lng, lnb, nxt_mods, mod_idx, tc):
    t, d = xres.shape
    r = gate.shape[1]
    has_next = nxt_mods is not None
    mspec = pl.BlockSpec((1, r, d), lambda i, *_: (mod_idx(i), 0, 0))
    vspec = pl.BlockSpec((1, d), lambda i, *_: (0, 0))
    ospec = pl.BlockSpec((tc, d), lambda i, *_: (i, 0))
    wbase = tok_base // tc
    in_specs = [pl.BlockSpec((tc, 2), lambda i, *_: (wbase + i, 0)),
                pl.BlockSpec(memory_space=pl.ANY), ospec, mspec, vspec, vspec]
    args = [wts, ys, xres, gate, lng, lnb]
    out_specs = [ospec]
    out_shape = [jax.ShapeDtypeStruct((t, d), F32)]
    if has_next:
        in_specs += [mspec, mspec]
        args += list(nxt_mods)
        out_specs.append(ospec)
        out_shape.append(jax.ShapeDtypeStruct((t, d), BF16))
    return pl.pallas_call(
        functools.partial(_combine_kernel, pos.shape[0] // 2, tok_base, has_next),
        grid_spec=pltpu.PrefetchScalarGridSpec(
            num_scalar_prefetch=1,
            grid=(t // tc,),
            in_specs=in_specs,
            out_specs=out_specs,
            scratch_shapes=[pltpu.VMEM((3, 2, tc, d), F32), pltpu.VMEM((1, tc, d), F32),
                            pltpu.SemaphoreType.DMA((3,))]),
        out_shape=out_shape,
        compiler_params=_cparams(1, 40),
        name="moe_combine",
    )(pos, *args)


def _plan_kernel(n_tiles, idx_ref, cnt_ref, pos_ref, meta_ref):
    tm = MOE_TILE
    t = idx_ref.shape[1]
    sub = lax.broadcasted_iota(I32, (N_EXPERTS, LANES), 0)
    lane = lax.broadcasted_iota(I32, (N_EXPERTS, LANES), 1)
    cnt = cnt_ref[...]
    padded = jnp.floor((cnt + (tm - 1)) * (1.0 / tm)) * tm
    ends = padded
    sh = 1
    while sh < N_EXPERTS:
        ends = ends + jnp.where(sub >= sh, pltpu.roll(ends, sh, 0), 0.0)
        sh *= 2
    off = ends - padded
    total = ends[N_EXPERTS - 1:N_EXPERTS, :]
    n_used = jnp.maximum(total * (1.0 / tm), 1.0)

    def as_row(col):
        return jnp.sum(jnp.where(sub == lane, col, 0.0), axis=0, keepdims=True)

    eio = lax.broadcasted_iota(I32, (N_EXPERTS, t), 0)
    off_col = off[:, 0:1]
    for k in range(2):
        start = jnp.sum(jnp.where(eio == idx_ref[k:k + 1, :], off_col, 0.0), axis=0, keepdims=True)
        pos_ref[k:k + 1, :] = start.astype(I32) + idx_ref[2 + k:3 + k, :]
    pos_ref[2:SUBLANES, :] = jnp.zeros((SUBLANES - 2, t), I32)

    tile = lax.broadcasted_iota(I32, (1, LANES), 1).astype(F32)
    tile_start = jnp.minimum(tile, n_used - 1.0) * tm
    te = jnp.minimum(jnp.sum(jnp.where(ends <= tile_start, 1.0, 0.0), axis=0, keepdims=True), N_EXPERTS - 1.0)
    prev = jnp.where(tile == 0.0, -1.0, pltpu.roll(te, 1, 1))
    first = jnp.where((tile < n_used) & (te != prev), 1.0, 0.0)
    present = as_row(jnp.where(cnt > 0.0, 1.0, 0.0))
    later = jnp.where((present > 0.0) & (lane > sub) & (lane < N_EXPERTS), lane.astype(F32), float(N_EXPERTS))
    nxt_e = jnp.min(later, axis=1, keepdims=True)
    nxt_e = jnp.where(nxt_e == float(N_EXPERTS), -1.0, nxt_e)
    nxt = jnp.sum(jnp.where(sub.astype(F32) == te, nxt_e, 0.0), axis=0, keepdims=True)
    lane_row = lax.broadcasted_iota(I32, (1, LANES), 1)
    off_row = as_row(off)
    off_row = jnp.where(lane_row == N_EXPERTS, total, off_row)
    off_row = jnp.where(lane_row == N_EXPERTS + 1, float(n_tiles * tm), off_row)
    rows = {META_TILE_EXPERT: te, META_FIRST: first, META_NEXT: nxt, META_N_USED: n_used,
            META_COUNT: as_row(cnt), META_OFFSET: off_row}
    for r in range(SUBLANES):
        meta_ref[r:r + 1, :] = rows[r].astype(I32) if r in rows else jnp.zeros((1, LANES), I32)


def _moe_plan(idx, cnt, n_tok):
    tm = MOE_TILE
    n_tiles = (2 * n_tok + N_EXPERTS * (tm - 1) + tm - 1) // tm
    assert n_tiles <= LANES
    pos, meta = pl.pallas_call(
        functools.partial(_plan_kernel, n_tiles),
        out_shape=[jax.ShapeDtypeStruct((SUBLANES, n_tok), I32), jax.ShapeDtypeStruct((SUBLANES, LANES), I32)],
        compiler_params=_cparams(0, 32),
        name="moe_plan",
    )(idx, cnt)
    return pos, meta, n_tiles * tm


class _Rows:
    def __init__(self, n_seq, seq_len, tile):
        self.n_seq, self.seq_len, self.tile = n_seq, seq_len, tile
        self.per_row = seq_len < tile

    def mods(self, m):
        if self.per_row:
            return jnp.repeat(m, self.seq_len, axis=0).reshape(-1, self.tile, m.shape[1])
        return m[:, None, :]

    def mod_idx(self, i):
        return i if self.per_row else (i * self.tile) // self.seq_len


def _split_mods(mod_l, n_prompt):
    d = mod_l.shape[1] // 6
    cols = [mod_l[:, k * d:(k + 1) * d] for k in range(6)]
    return [c[:n_prompt] for c in cols], [c[n_prompt:] for c in cols]


def kernel(x_prompt, x_sample, state_mlstm_C, state_mlstm_n, state_mlstm_m, c_prompt, c_sample, w_ada, b_ada, ln_g, ln_b, a_w_in, a_b_gates, a_norm_w, a_w_out, b_w_in, b_b_in, b_norm_g, b_norm_b, b_w_s, b_b_s, b_w_out, w_router, b_router, w_expert_in, w_expert_out):
    bp, sp, d = x_prompt.shape
    bs, ss, _ = x_sample.shape
    tp = bp * sp
    ts = bs * ss
    n_tok = tp + ts
    qk = N_HEADS * DK
    vd = N_HEADS * DV
    n_main = 2 * qk + 2 * vd

    n_seq = bp + bs
    pad = (-n_seq) % SUBLANES
    c_all = jnp.concatenate([c_prompt, c_sample, jnp.zeros((pad, d), F32)])
    mod = _adaln(c_all, w_ada, b_ada)[:, :n_seq]

    sets = {"p": _Rows(bp, sp, 512), "s": _Rows(bs, ss, 256)}
    base = {"p": 0, "s": tp}
    x = {"p": x_prompt.reshape(tp, d), "s": x_sample.reshape(ts, d)}
    hm = {}
    outs = {}
    for layer in range(DEPTH):
        j = layer // 2
        mp, ms = _split_mods(mod[layer], bp)
        md = {"p": mp, "s": ms}
        lng = ln_g[layer]
        lnb = ln_b[layer]
        lhs = {}
        if layer % 2 == 0:
            w_in = a_w_in[j]
            gb = jnp.zeros((1, 2 * LANES), F32)
            gb = gb.at[0, :N_HEADS].set(a_b_gates[j, :N_HEADS])
            gb = gb.at[0, LANES:LANES + N_HEADS].set(a_b_gates[j, N_HEADS:])
            nw = a_norm_w[j].reshape(1, vd)
            proj = {}
            gates = {}
            for k, rs in sets.items():
                proj[k], gates[k] = _inproj(x[k], rs.mods(md[k][0]), rs.mods(md[k][1]), rs.mod_idx,
                                            w_in, n_main, rs.tile, BF16 if k == "p" else F32)
            lhs["p"], c_p, n_p, m_p = _mlstm_chunks(
                proj["p"], gates["p"], gb, nw,
                jnp.zeros((bp, N_HEADS, DK, DV), F32), jnp.zeros((bp, N_HEADS, LANES), F32),
                jnp.zeros((bp, N_HEADS, LANES), F32), bp, sp)
            rpad = SUBLANES - ss
            proj_s = jnp.pad(proj["s"].reshape(bs, ss, n_main), ((0, 0), (0, rpad), (0, 0)))
            gates_s = jnp.pad(gates["s"].reshape(bs, ss, 2 * LANES), ((0, 0), (0, rpad), (0, 0)))
            m0 = jnp.pad(state_mlstm_m[j], ((0, 0), (0, LANES - N_HEADS)))[:, None, :]
            ypre_s, c_s, n_s, m_s = _mlstm_short(proj_s, gates_s, gb, nw, state_mlstm_C[j],
                                                 state_mlstm_n[j].reshape(bs, 1, qk), m0, ss)
            n_s = n_s.reshape(bs, N_HEADS, DK)
            lhs["s"] = ypre_s[:, :ss].reshape(ts, vd)
            outs["C_p"], outs["n_p"], outs["m_p"] = c_p, n_p, m_p[:, :, 0]
            outs["C_s"], outs["n_s"], outs["m_s"] = c_s, n_s, m_s[:, 0, :N_HEADS]
            w_out = a_w_out[j]
        else:
            ws = b_w_s[j]
            bsv = b_b_s[j]
            w_out = b_w_out[j]

            def mixing(l, tmix):
                tri = jnp.tril(jnp.ones((l, l), bool))
                wsl = jnp.where(tri, ws[:, :l, :l], 0.0)
                eye = jnp.eye(tmix // l, dtype=F32)
                mats = jax.vmap(lambda m: jnp.kron(eye, m))(wsl).astype(BF16)
                bias = jnp.tile(bsv[:, :l], (1, tmix // l))[:, :, None]
                return mats, bias

            uv = {k: _gmlp_in(hm[k], b_w_in[j], b_b_in[j], min(1024, rs.n_seq * rs.seq_len),
                              BF16 if k == "p" else F32) for k, rs in sets.items()}
        x1 = {}
        hf = {}
        for k, rs in sets.items():
            if layer % 2 == 0:
                x1[k], hf[k] = _proj_ln(lhs[k], w_out, x[k], rs.mods(md[k][2]), lng[0:1], lnb[0:1],
                                        rs.mods(md[k][3]), rs.mods(md[k][4]), rs.mod_idx, rs.tile, 512)
            else:
                rs = _Rows(rs.n_seq, rs.seq_len, rs.tile // 2)
                mats, bias = mixing(min(CHUNK, rs.seq_len), rs.tile)
                res = _mix_proj_ln(uv[k], b_norm_g[j], b_norm_b[j], mats, bias, w_out, x[k], rs.mods(md[k][2]),
                                   lng[0:1], lnb[0:1], rs.mods(md[k][3]), rs.mods(md[k][4]), rs.mod_idx,
                                   rs.tile, 512, k == "s")
                x1[k], hf[k] = res[0], res[1]
                if k == "s":
                    outs["v_s"] = res[2]

        idx, wts, cnt, hf_all = _router(hf["p"], hf["s"], w_router, b_router, 512)
        pos, meta, n_rows = _moe_plan(idx, cnt, n_tok)
        pos = pos[:2].reshape(-1)
        ysorted = _experts(hf_all, _invert_pos(pos, n_rows), w_expert_in, w_expert_out, layer, meta)
        wts2 = wts[:2].T
        nxt = _split_mods(mod[layer + 1], bp) if layer + 1 < DEPTH else None
        for ki, (k, rs) in enumerate(sets.items()):
            rc = _Rows(rs.n_seq, rs.seq_len, 256)
            nxt_mods = None if nxt is None else (rc.mods(nxt[ki][0]), rc.mods(nxt[ki][1]))
            res = _combine_ln(ysorted, pos, wts2, base[k], x1[k], rc.mods(md[k][5]),
                              lng[1:2], lnb[1:2], nxt_mods, rc.mod_idx, 256)
            x[k] = res[0]
            if nxt is not None:
                hm[k] = res[1]

    return (x["p"].reshape(bp, sp, d), x["s"].reshape(bs, ss, d),
            outs["C_p"][None], outs["n_p"][None], outs["m_p"][None],
            outs["C_s"][None], outs["n_s"][None], outs["m_s"][None],
            outs["v_s"].reshape(bs, ss, -1)[None])
```

```python
import functools

import jax
import jax.numpy as jnp
from jax import lax
from jax.experimental import pallas as pl
from jax.experimental.pallas import tpu as pltpu

F32 = jnp.float32
BF16 = jnp.bfloat16
I32 = jnp.int32

DEPTH = 2
N_HEADS = 8
DK = 128
DV = 256
CHUNK = 128
N_GROUPS_B = 8
N_EXPERTS = 16
N_EXPERT_GROUPS = 4
EXPERTS_PER_GROUP = 4
ALPHA = float((2 * DEPTH) ** 0.25)
LN_EPS = 1e-5

LANES = 128
SUBLANES = 8
MIB = 1024 * 1024
MOE_TILE = 256
CAST_ROWS = 256
STAGE_ROWS = 128
META_TILE_EXPERT, META_FIRST, META_NEXT, META_N_USED, META_COUNT, META_OFFSET = range(6)


def _cparams(n_axes, vmem_mib):
    return pltpu.CompilerParams(
        dimension_semantics=("arbitrary",) * n_axes,
        vmem_limit_bytes=int(vmem_mib * MIB))


def _split3(x):
    hi = x.astype(BF16)
    r1 = x - hi.astype(F32)
    mid = r1.astype(BF16)
    lo = (r1 - mid.astype(F32)).astype(BF16)
    return hi, mid, lo


def _log_sigmoid(x):
    return jnp.minimum(x, 0.0) - jnp.log1p(jnp.exp(-jnp.abs(x)))


def _load_weight_bf16(w_hbm, wb, stg, sem):
    nj, k, tn = wb.shape
    ch = stg.shape[1]
    nch = k // ch

    def chunk(c):
        return pltpu.make_async_copy(w_hbm.at[pl.ds(c * ch, ch), pl.ds(0, nj * tn)], stg.at[c % 2],
                                     sem.at[c % 2])

    chunk(0).start()
    for c in range(nch):
        if c + 1 < nch:
            chunk(c + 1).start()
        chunk(c).wait()
        for jj in range(nj):
            wb[jj, c * ch:(c + 1) * ch, :] = stg[c % 2, :, jj * tn:(jj + 1) * tn].astype(BF16)


def _adaln_kernel(c_ref, w_ref, b_ref, o_ref):
    c = c_ref[...]
    a = (c * jax.nn.sigmoid(c)).astype(BF16)
    o_ref[0] = jnp.dot(a, w_ref[0].astype(BF16), preferred_element_type=F32) + b_ref[0]


def _adaln(c_all, w_ada, b_ada):
    depth, d, n = w_ada.shape
    r = c_all.shape[0]
    tn = 1024
    return pl.pallas_call(
        _adaln_kernel,
        grid=(depth, n // tn),
        in_specs=[pl.BlockSpec((r, d), lambda l, j: (0, 0)),
                  pl.BlockSpec((1, d, tn), lambda l, j: (l, 0, j)),
                  pl.BlockSpec((1, 1, tn), lambda l, j: (l, 0, j))],
        out_specs=pl.BlockSpec((1, r, tn), lambda l, j: (l, 0, j)),
        out_shape=jax.ShapeDtypeStruct((depth, r, n), F32),
        compiler_params=_cparams(2, 32),
        name="adaln",
    )(c_all, w_ada, b_ada.reshape(depth, 1, n))


def _inproj_kernel(x_ref, sh_ref, sc_ref, w_hbm, proj_ref, gates_ref, hin_ref, wb, stg, g16, wgs, sem):
    j = pl.program_id(1)

    @pl.when((pl.program_id(0) == 0) & (j == 0))
    def _():
        _load_weight_bf16(w_hbm, wb, stg, sem)
        n_main = wb.shape[0] * wb.shape[2]
        cp = pltpu.make_async_copy(w_hbm.at[:, pl.ds(n_main, 2 * N_HEADS)], g16, sem.at[0])
        cp.start()
        cp.wait()
        g = g16[...]
        z = jnp.zeros((g.shape[0], LANES - N_HEADS), F32)
        wg = jnp.concatenate([g[:, :N_HEADS], z, g[:, N_HEADS:], z], axis=1)
        hi = wg.astype(BF16)
        wgs[0] = hi
        wgs[1] = (wg - hi.astype(F32)).astype(BF16)

    @pl.when(j == 0)
    def _():
        h = x_ref[...] * (1.0 + sc_ref[0]) + sh_ref[0]
        hb = h.astype(BF16)
        hin_ref[...] = hb
        h_lo = (h - hb.astype(F32)).astype(BF16)
        gates_ref[...] = (jnp.dot(hb, wgs[0], preferred_element_type=F32)
                          + jnp.dot(h_lo, wgs[0], preferred_element_type=F32)
                          + jnp.dot(hb, wgs[1], preferred_element_type=F32))

    proj_ref[...] = jnp.dot(hin_ref[...], wb[j], preferred_element_type=F32).astype(proj_ref.dtype)


def _inproj(x, sh, sc, mod_idx, w_in, n_main, tm, out_dtype):
    t, d = x.shape
    tn = 1024
    r = sh.shape[1]
    mspec = pl.BlockSpec((1, r, d), lambda i, j: (mod_idx(i), 0, 0))
    return pl.pallas_call(
        _inproj_kernel,
        grid=(t // tm, n_main // tn),
        in_specs=[pl.BlockSpec((tm, d), lambda i, j: (i, 0)), mspec, mspec,
                  pl.BlockSpec(memory_space=pl.ANY)],
        out_specs=[pl.BlockSpec((tm, tn), lambda i, j: (i, j)),
                   pl.BlockSpec((tm, 2 * LANES), lambda i, j: (i, 0))],
        out_shape=[jax.ShapeDtypeStruct((t, n_main), out_dtype),
                   jax.ShapeDtypeStruct((t, 2 * LANES), F32)],
        scratch_shapes=[pltpu.VMEM((tm, d), BF16), pltpu.VMEM((n_main // tn, d, tn), BF16),
                        pltpu.VMEM((2, STAGE_ROWS, n_main), F32), pltpu.VMEM((d, 2 * N_HEADS), F32),
                        pltpu.VMEM((2, d, 2 * LANES), BF16), pltpu.SemaphoreType.DMA((2,))],
        compiler_params=_cparams(2, 56),
        name="mlstm_inproj",
    )(x, sh, sc, w_in)


def _head_norm_gate(hh, nw, o):
    mu = jnp.mean(hh, axis=1, keepdims=True)
    xc = hh - mu
    var = jnp.mean(xc * xc, axis=1, keepdims=True)
    return jax.nn.sigmoid(o) * (xc * lax.rsqrt(var + LN_EPS) * nw)


def _mlstm_chunk_kernel(q_ref, k_ref, v_ref, o_ref, g_ref, gb_ref, nw_ref, c0_ref, n0_ref, m0_ref,
                        y_ref, cout_ref, nout_ref, mout_ref, c_scr, n_scr, m_scr):
    ci = pl.program_id(1)
    L = q_ref.shape[0]

    @pl.when(ci == 0)
    def _():
        c_scr[...] = c0_ref[0]
        n_scr[...] = n0_ref[0]
        m_scr[...] = m0_ref[0]

    g = g_ref[...] + gb_ref[...]
    gi = g[:, :LANES]
    lf = _log_sigmoid(g[:, LANES:])
    row = lax.broadcasted_iota(I32, (L, L), 0)
    col = lax.broadcasted_iota(I32, (L, L), 1)
    causal = col <= row
    ltri = jnp.where(causal, 1.0, 0.0).astype(BF16)
    hi, mid, lo = _split3(lf)
    bcum = (jnp.dot(ltri, hi, preferred_element_type=F32)
            + jnp.dot(ltri, mid, preferred_element_type=F32)
            + jnp.dot(ltri, lo, preferred_element_type=F32))
    a = gi - bcum
    a_t = a.T
    scale = DK ** -0.5
    nt = (((1,), (1,)), ((), ()))
    tn_dims = (((0,), (0,)), ((), ()))
    for h in range(N_HEADS):
        ks = slice(h * DK, (h + 1) * DK)
        vs = slice(h * DV, (h + 1) * DV)
        qf = q_ref[:, ks].astype(F32) * scale
        qb = qf.astype(BF16)
        kf = k_ref[:, ks].astype(F32)
        kb = kf.astype(BF16)
        vb = v_ref[:, vs].astype(BF16)
        a_row = a_t[h:h + 1, :]
        a_col = a[:, h:h + 1]
        b_col = bcum[:, h:h + 1]
        m_prev = m_scr[h:h + 1, 0:1]
        amat = jnp.where(causal, a_row, -jnp.inf)
        mx = jnp.max(amat, axis=1, keepdims=True)
        m_inter = b_col + m_prev
        m_t = jnp.maximum(m_inter, b_col + mx)
        dm = jnp.exp(amat + (b_col - m_t))
        s = lax.dot_general(qb, kb, nt, preferred_element_type=F32)
        scores = s * dm
        inter = jnp.exp(m_inter - m_t)
        c_old = c_scr[h]
        n_old = n_scr[h:h + 1, :]
        qc = jnp.dot(qb, c_old.astype(BF16), preferred_element_type=F32)
        num = jnp.dot(scores.astype(BF16), vb, preferred_element_type=F32) + inter * qc
        qn = jnp.sum(qf * n_old, axis=1, keepdims=True)
        den = jnp.sum(scores, axis=1, keepdims=True) + inter * qn
        hh = num / jnp.maximum(jnp.abs(den), jnp.exp(-m_t))
        m_new = m_t[L - 1:L, :]
        b_last = b_col[L - 1:L, :]
        w_col = jnp.exp(b_last + a_col - m_new)
        decay = jnp.exp(b_last + m_prev - m_new)
        kw = kf * w_col
        c_scr[h] = decay * c_old + lax.dot_general(kw.astype(BF16), vb, tn_dims,
                                                   preferred_element_type=F32)
        n_scr[h:h + 1, :] = decay * n_old + jnp.sum(kw, axis=0, keepdims=True)
        m_scr[h:h + 1, :] = jnp.broadcast_to(m_new, (1, LANES))
        y_ref[:, vs] = _head_norm_gate(hh, nw_ref[:, vs], o_ref[:, vs].astype(F32)).astype(y_ref.dtype)

    @pl.when(ci == pl.num_programs(1) - 1)
    def _():
        cout_ref[0] = c_scr[...]
        nout_ref[0] = n_scr[...]
        mout_ref[0] = m_scr[...]


def _mlstm_chunks(proj, gates, gb, nw, c0, n0, m0, batch, seq):
    nc = seq // CHUNK
    qk = N_HEADS * DK
    vd = N_HEADS * DV
    row = lambda b, c: b * nc + c
    st4 = pl.BlockSpec((1, N_HEADS, DK, DV), lambda b, c: (b, 0, 0, 0))
    st3 = pl.BlockSpec((1, N_HEADS, LANES), lambda b, c: (b, 0, 0))
    return pl.pallas_call(
        _mlstm_chunk_kernel,
        grid=(batch, nc),
        in_specs=[pl.BlockSpec((CHUNK, qk), lambda b, c: (row(b, c), 0)),
                  pl.BlockSpec((CHUNK, qk), lambda b, c: (row(b, c), 1)),
                  pl.BlockSpec((CHUNK, vd), lambda b, c: (row(b, c), 1)),
                  pl.BlockSpec((CHUNK, vd), lambda b, c: (row(b, c), 2)),
                  pl.BlockSpec((CHUNK, 2 * LANES), lambda b, c: (row(b, c), 0)),
                  pl.BlockSpec((1, 2 * LANES), lambda b, c: (0, 0)),
                  pl.BlockSpec((1, vd), lambda b, c: (0, 0)),
                  st4, st3, st3],
        out_specs=[pl.BlockSpec((CHUNK, vd), lambda b, c: (row(b, c), 0)), st4, st3, st3],
        out_shape=[jax.ShapeDtypeStruct((batch * seq, vd), BF16),
                   jax.ShapeDtypeStruct((batch, N_HEADS, DK, DV), F32),
                   jax.ShapeDtypeStruct((batch, N_HEADS, LANES), F32),
                   jax.ShapeDtypeStruct((batch, N_HEADS, LANES), F32)],
        scratch_shapes=[pltpu.VMEM((N_HEADS, DK, DV), F32),
                        pltpu.VMEM((N_HEADS, LANES), F32),
                        pltpu.VMEM((N_HEADS, LANES), F32)],
        compiler_params=_cparams(2, 40),
        name="mlstm_chunks",
    )(proj, proj, proj, proj, gates, gb, nw, c0, n0, m0)


def _per_head(x, width):
    return jnp.concatenate(
        [jnp.broadcast_to(x[..., h:h + 1], x.shape[:-1] + (width,)) for h in range(N_HEADS)], axis=-1)


def _head_sums(x, width):
    lane = lax.broadcasted_iota(I32, x.shape[:-1] + (LANES,), x.ndim - 1)
    out = jnp.zeros(x.shape[:-1] + (LANES,), F32)
    for h in range(N_HEADS):
        s = jnp.sum(x[..., h * width:(h + 1) * width], axis=-1, keepdims=True)
        out = jnp.where(lane == h, s, out)
    return out


def _mlstm_short_kernel(seq, q_ref, k_ref, v_ref, o_ref, g_ref, gb_ref, nw_ref, c0_ref, n0_ref, m0_ref,
                        y_ref, cout_ref, nout_ref, mout_ref, qc_scr, kw_scr):
    bt, rows, _ = q_ref.shape
    scale = DK ** -0.5
    tn_dims = (((0,), (0,)), ((), ()))
    row = lax.broadcasted_iota(I32, (bt, rows, LANES), 1)
    g = g_ref[...] + gb_ref[...]
    gi = g[:, :, :LANES]
    lf = _log_sigmoid(g[:, :, LANES:])
    bcum = jnp.zeros_like(lf)
    for s in range(seq):
        bcum = bcum + jnp.where(row >= s, lf[:, s:s + 1, :], 0.0)
    a = gi - bcum
    mx = jnp.full_like(a, -jnp.inf)
    for s in range(seq):
        mx = jnp.maximum(mx, jnp.where(row >= s, a[:, s:s + 1, :], -jnp.inf))
    m_prev = m0_ref[...]
    m_inter = bcum + m_prev
    m_t = jnp.maximum(m_inter, bcum + mx)
    cmt = bcum - m_t
    inter = jnp.exp(m_inter - m_t)
    einv = jnp.exp(-m_t)
    m_new = m_t[:, seq - 1:seq, :]
    b_last = bcum[:, seq - 1:seq, :]
    w = jnp.where(row < seq, jnp.exp(b_last + a - m_new), 0.0)
    decay = jnp.exp(b_last + m_prev - m_new)
    mout_ref[...] = m_new

    q = q_ref[...] * scale
    k = k_ref[...]
    v = v_ref[...]
    n_old = n0_ref[...]
    for b in range(bt):
        for h in range(N_HEADS):
            qc_scr[b, :, h * DV:(h + 1) * DV] = jnp.dot(
                (q_ref[b, :, h * DK:(h + 1) * DK] * scale).astype(BF16), c0_ref[b, h].astype(BF16),
                preferred_element_type=F32)
    den = inter * _head_sums(q * n_old, DK)
    num = _per_head(inter, DV) * qc_scr[...]
    for s in range(seq):
        p = _head_sums(q * k[:, s:s + 1, :], DK) * jnp.where(row >= s, jnp.exp(cmt + a[:, s:s + 1, :]), 0.0)
        den = den + p
        num = num + _per_head(p, DV) * v[:, s:s + 1, :]
    hh = num * _per_head(1.0 / jnp.maximum(jnp.abs(den), einv), DV)
    mu = _head_sums(hh, DV) * (1.0 / DV)
    xc = hh - _per_head(mu, DV)
    var = _head_sums(xc * xc, DV) * (1.0 / DV)
    hn = xc * _per_head(lax.rsqrt(var + LN_EPS), DV) * nw_ref[...]
    y_ref[...] = jax.nn.sigmoid(o_ref[...]) * hn

    kw = k * _per_head(w, DK)
    kw_scr[...] = kw
    nout_ref[...] = _per_head(decay, DK) * n_old + jnp.sum(kw, axis=1, keepdims=True)
    for b in range(bt):
        for h in range(N_HEADS):
            cout_ref[b, h] = (decay[b, :, h:h + 1] * c0_ref[b, h]
                              + lax.dot_general(kw_scr[b, :, h * DK:(h + 1) * DK],
                                                v_ref[b, :, h * DV:(h + 1) * DV],
                                                tn_dims, preferred_element_type=F32))


def _mlstm_short(proj, gates, gb, nw, c0, n0, m0, seq):
    batch, rows, _ = proj.shape
    qk = N_HEADS * DK
    vd = N_HEADS * DV
    bt = 8
    st4 = pl.BlockSpec((bt, N_HEADS, DK, DV), lambda i: (i, 0, 0, 0))
    st3 = pl.BlockSpec((bt, 1, qk), lambda i: (i, 0, 0))
    stm = pl.BlockSpec((bt, 1, LANES), lambda i: (i, 0, 0))
    return pl.pallas_call(
        functools.partial(_mlstm_short_kernel, seq),
        grid=(batch // bt,),
        in_specs=[pl.BlockSpec((bt, rows, qk), lambda i: (i, 0, 0)),
                  pl.BlockSpec((bt, rows, qk), lambda i: (i, 0, 1)),
                  pl.BlockSpec((bt, rows, vd), lambda i: (i, 0, 1)),
                  pl.BlockSpec((bt, rows, vd), lambda i: (i, 0, 2)),
                  pl.BlockSpec((bt, rows, 2 * LANES), lambda i: (i, 0, 0)),
                  pl.BlockSpec((1, 2 * LANES), lambda i: (0, 0)),
                  pl.BlockSpec((1, vd), lambda i: (0, 0)),
                  st4, st3, stm],
        out_specs=[pl.BlockSpec((bt, rows, vd), lambda i: (i, 0, 0)), st4, st3, stm],
        out_shape=[jax.ShapeDtypeStruct((batch, rows, vd), F32),
                   jax.ShapeDtypeStruct((batch, N_HEADS, DK, DV), F32),
                   jax.ShapeDtypeStruct((batch, 1, qk), F32),
                   jax.ShapeDtypeStruct((batch, 1, LANES), F32)],
        scratch_shapes=[pltpu.VMEM((bt, rows, vd), F32), pltpu.VMEM((bt, rows, qk), F32)],
        compiler_params=_cparams(1, 48),
        name="mlstm_short",
    )(proj, proj, proj, proj, gates, gb, nw, c0, n0, m0)


def _residual_ln(zbuf, xres_ref, gate_ref, lng_ref, lnb_ref, xo_ref, nxt):
    nc, tm, tn = zbuf.shape
    inv_d = 1.0 / (nc * tn)
    ssum = jnp.zeros((tm, 1), F32)
    for c in range(nc):
        sl = pl.ds(c * tn, tn)
        z = ALPHA * xres_ref[:, sl] + gate_ref[0, :, sl] * zbuf[c]
        zbuf[c] = z
        ssum = ssum + jnp.sum(z, axis=1, keepdims=True)
    mu = ssum * inv_d
    vsum = jnp.zeros((tm, 1), F32)
    for c in range(nc):
        zc = zbuf[c] - mu
        vsum = vsum + jnp.sum(zc * zc, axis=1, keepdims=True)
    rstd = lax.rsqrt(vsum * inv_d + LN_EPS)
    for c in range(nc):
        sl = pl.ds(c * tn, tn)
        xn = (zbuf[c] - mu) * rstd * lng_ref[:, sl] + lnb_ref[:, sl]
        xo_ref[:, sl] = xn
        if nxt is not None:
            sh_ref, sc_ref, ho_ref = nxt
            ho_ref[:, sl] = (xn * (1.0 + sc_ref[0, :, sl]) + sh_ref[0, :, sl]).astype(ho_ref.dtype)


def _proj_ln_kernel(lhs_ref, w_hbm, xres_ref, gate_ref, lng_ref, lnb_ref, shn_ref, scn_ref,
                    xo_ref, ho_ref, wb, stg, ybuf, sem):
    @pl.when(pl.program_id(0) == 0)
    def _():
        _load_weight_bf16(w_hbm, wb, stg, sem)

    lhs = lhs_ref[...].astype(BF16)
    for c in range(ybuf.shape[0]):
        ybuf[c] = jnp.dot(lhs, wb[c], preferred_element_type=F32)
    _residual_ln(ybuf, xres_ref, gate_ref, lng_ref, lnb_ref, xo_ref, (shn_ref, scn_ref, ho_ref))


def _proj_ln(lhs, w, xres, gate, lng, lnb, shn, scn, mod_idx, tm, tn):
    t, k = lhs.shape
    d = w.shape[1]
    r = gate.shape[1]
    mspec = pl.BlockSpec((1, r, d), lambda i: (mod_idx(i), 0, 0))
    vspec = pl.BlockSpec((1, d), lambda i: (0, 0))
    ospec = pl.BlockSpec((tm, d), lambda i: (i, 0))
    return pl.pallas_call(
        _proj_ln_kernel,
        grid=(t // tm,),
        in_specs=[pl.BlockSpec((tm, k), lambda i: (i, 0)),
                  pl.BlockSpec(memory_space=pl.ANY),
                  ospec, mspec, vspec, vspec, mspec, mspec],
        out_specs=[ospec, ospec],
        out_shape=[jax.ShapeDtypeStruct((t, d), F32), jax.ShapeDtypeStruct((t, d), F32)],
        scratch_shapes=[pltpu.VMEM((d // tn, k, tn), BF16), pltpu.VMEM((2, CAST_ROWS, d), F32),
                        pltpu.VMEM((d // tn, tm, tn), F32), pltpu.SemaphoreType.DMA((2,))],
        compiler_params=_cparams(1, 58),
        name="proj_ln",
    )(lhs, w, xres, gate, lng, lnb, shn, scn)


def _gelu_tanh(x):
    return x * (0.5 * (1.0 + jnp.tanh(0.7978845608028654 * (x + 0.044715 * (x * x * x)))))


def _gmlp_in_kernel(h_ref, w_hbm, b_ref, o_ref, wb, stg, sem):
    j = pl.program_id(1)

    @pl.when((pl.program_id(0) == 0) & (j == 0))
    def _():
        _load_weight_bf16(w_hbm, wb, stg, sem)

    acc = jnp.dot(h_ref[...].astype(BF16), wb[j], preferred_element_type=F32)
    o_ref[...] = _gelu_tanh(acc + b_ref[...]).astype(o_ref.dtype)


def _gmlp_in(h, w, b, tm, out_dtype):
    t, d = h.shape
    n = w.shape[1]
    tn = 1024
    return pl.pallas_call(
        _gmlp_in_kernel,
        grid=(t // tm, n // tn),
        in_specs=[pl.BlockSpec((tm, d), lambda i, j: (i, 0)),
                  pl.BlockSpec(memory_space=pl.ANY),
                  pl.BlockSpec((1, tn), lambda i, j: (0, j))],
        out_specs=pl.BlockSpec((tm, tn), lambda i, j: (i, j)),
        out_shape=jax.ShapeDtypeStruct((t, n), out_dtype),
        scratch_shapes=[pltpu.VMEM((n // tn, d, tn), BF16), pltpu.VMEM((2, STAGE_ROWS, n), F32),
                        pltpu.SemaphoreType.DMA((2,))],
        compiler_params=_cparams(2, 58),
        name="gmlp_in",
    )(h, w, b.reshape(1, n))


def _gmlp_mix_rows(u_ref, v_ref, g_ref, b_ref, mix_ref, bias_ref, o_ref, vn_ref):
    v = v_ref[...].astype(F32)
    mu = jnp.mean(v, axis=1, keepdims=True)
    xc = v - mu
    var = jnp.mean(xc * xc, axis=1, keepdims=True)
    vn = xc * lax.rsqrt(var + LN_EPS) * g_ref[...] + b_ref[...]
    if vn_ref is not None:
        vn_ref[...] = vn
    gd = v.shape[1] // N_GROUPS_B
    mb = mix_ref.shape[1]
    for g in range(N_GROUPS_B):
        sl = slice(g * gd, (g + 1) * gd)
        for blk in range(v.shape[0] // mb):
            rows = slice(blk * mb, (blk + 1) * mb)
            mixed = jnp.dot(mix_ref[g], vn[rows, sl].astype(BF16), preferred_element_type=F32) + bias_ref[g]
            o_ref[rows, sl] = (u_ref[rows, sl].astype(F32) * mixed).astype(o_ref.dtype)


def _mix_proj_ln_kernel(emit_v, u_ref, v_ref, g_ref, b_ref, mix_ref, bias_ref, w_hbm, xres_ref, gate_ref,
                        lng_ref, lnb_ref, shn_ref, scn_ref, *rest):
    if emit_v:
        xo_ref, ho_ref, vn_ref, lhs_scr, wb, stg, ybuf, sem = rest
    else:
        xo_ref, ho_ref, lhs_scr, wb, stg, ybuf, sem = rest
        vn_ref = None

    @pl.when(pl.program_id(0) == 0)
    def _():
        _load_weight_bf16(w_hbm, wb, stg, sem)

    _gmlp_mix_rows(u_ref, v_ref, g_ref, b_ref, mix_ref, bias_ref, lhs_scr, vn_ref)
    lhs = lhs_scr[...]
    for c in range(ybuf.shape[0]):
        ybuf[c] = jnp.dot(lhs, wb[c], preferred_element_type=F32)
    _residual_ln(ybuf, xres_ref, gate_ref, lng_ref, lnb_ref, xo_ref, (shn_ref, scn_ref, ho_ref))


def _mix_proj_ln(uv, nv_g, nv_b, mix, bias, w, xres, gate, lng, lnb, shn, scn, mod_idx, tm, tn, emit_v):
    t, n2 = uv.shape
    di = n2 // 2
    d = w.shape[1]
    r = gate.shape[1]
    mspec = pl.BlockSpec((1, r, d), lambda i: (mod_idx(i), 0, 0))
    vspec = pl.BlockSpec((1, d), lambda i: (0, 0))
    ospec = pl.BlockSpec((tm, d), lambda i: (i, 0))
    nspec = pl.BlockSpec((1, di), lambda i: (0, 0))
    out_specs = [ospec, ospec]
    out_shape = [jax.ShapeDtypeStruct((t, d), F32), jax.ShapeDtypeStruct((t, d), F32)]
    if emit_v:
        out_specs.append(pl.BlockSpec((tm, di), lambda i: (i, 0)))
        out_shape.append(jax.ShapeDtypeStruct((t, di), F32))
    return pl.pallas_call(
        functools.partial(_mix_proj_ln_kernel, emit_v),
        grid=(t // tm,),
        in_specs=[pl.BlockSpec((tm, di), lambda i: (i, 0)),
                  pl.BlockSpec((tm, di), lambda i: (i, 1)),
                  nspec, nspec,
                  pl.BlockSpec(mix.shape, lambda i: (0, 0, 0)),
                  pl.BlockSpec(bias.shape, lambda i: (0, 0, 0)),
                  pl.BlockSpec(memory_space=pl.ANY),
                  ospec, mspec, vspec, vspec, mspec, mspec],
        out_specs=out_specs,
        out_shape=out_shape,
        scratch_shapes=[pltpu.VMEM((tm, di), BF16), pltpu.VMEM((d // tn, di, tn), BF16),
                        pltpu.VMEM((2, CAST_ROWS, d), F32), pltpu.VMEM((d // tn, tm, tn), F32),
                        pltpu.SemaphoreType.DMA((2,))],
        compiler_params=_cparams(1, 58),
        name="gmlp_mix_proj_ln",
    )(uv, uv, nv_g.reshape(1, di), nv_b.reshape(1, di), mix, bias, w, xres, gate, lng, lnb, shn, scn)


def _router_kernel(n_first, ha_ref, hb_ref, wr_ref, br_ref, idx_ref, wts_ref, cnt_ref, carry):
    i = pl.program_id(0)
    tr = ha_ref.shape[0]

    @pl.when(i == 0)
    def _():
        carry[...] = jnp.zeros(carry.shape, carry.dtype)

    nt = (((1,), (1,)), ((), ()))
    h = jnp.where(i < n_first, ha_ref[...], hb_ref[...])
    hb = h.astype(BF16)
    hl = (h - hb.astype(F32)).astype(BF16)
    wr = wr_ref[...]
    wb = wr.astype(BF16)
    wl = (wr - wb.astype(F32)).astype(BF16)
    logits = (lax.dot_general(wb, hb, nt, preferred_element_type=F32)
              + lax.dot_general(wb, hl, nt, preferred_element_type=F32)
              + lax.dot_general(wl, hb, nt, preferred_element_type=F32))
    s = jax.nn.sigmoid(logits)
    sel = s + br_ref[...]
    epg = EXPERTS_PER_GROUP
    r = [sel[e:e + 1, :] for e in range(N_EXPERTS)]
    su = [s[e:e + 1, :] for e in range(N_EXPERTS)]

    def top2sum(v):
        best = v[0] + v[1]
        for x in range(epg):
            for y in range(x + 1, epg):
                if (x, y) != (0, 1):
                    best = jnp.maximum(best, v[x] + v[y])
        return best

    gs = [top2sum(r[g * epg:(g + 1) * epg]) for g in range(N_EXPERT_GROUPS)]
    gbest = gs[0]
    gidx = jnp.zeros((1, tr), I32)
    for g in range(1, N_EXPERT_GROUPS):
        better = gs[g] > gbest
        gidx = jnp.where(better, g, gidx)
        gbest = jnp.where(better, gs[g], gbest)
    v = list(r[:epg])
    sv = list(su[:epg])
    for g in range(1, N_EXPERT_GROUPS):
        pick = gidx == g
        for x in range(epg):
            v[x] = jnp.where(pick, r[g * epg + x], v[x])
            sv[x] = jnp.where(pick, su[g * epg + x], sv[x])
    i1 = jnp.zeros((1, tr), I32)
    b1 = v[0]
    w1 = sv[0]
    for x in range(1, epg):
        better = v[x] > b1
        i1 = jnp.where(better, x, i1)
        b1 = jnp.where(better, v[x], b1)
        w1 = jnp.where(better, sv[x], w1)
    i2 = jnp.zeros((1, tr), I32)
    b2 = jnp.full((1, tr), -jnp.inf, F32)
    w2 = jnp.zeros((1, tr), F32)
    for x in range(epg):
        take = (i1 != x) & (v[x] > b2)
        i2 = jnp.where(take, x, i2)
        b2 = jnp.where(take, v[x], b2)
        w2 = jnp.where(take, sv[x], w2)
    wsum = w1 + w2
    e1 = gidx * epg + i1
    e2 = gidx * epg + i2
    eio = lax.broadcasted_iota(I32, (N_EXPERTS, tr), 0)
    hit1 = eio == e1
    hit2 = eio == e2
    oh = jnp.where(hit1 | hit2, 1.0, 0.0)
    ri = lax.broadcasted_iota(I32, (tr, tr), 0)
    cj = lax.broadcasted_iota(I32, (tr, tr), 1)
    before = jnp.where(ri < cj, 1.0, 0.0).astype(BF16)
    rank = jnp.dot(oh.astype(BF16), before, preferred_element_type=F32) + carry[:, 0:1]
    rank1 = jnp.sum(jnp.where(hit1, rank, 0.0), axis=0, keepdims=True)
    rank2 = jnp.sum(jnp.where(hit2, rank, 0.0), axis=0, keepdims=True)
    idx_ref[0:1, :] = e1
    idx_ref[1:2, :] = e2
    idx_ref[2:3, :] = rank1.astype(I32)
    idx_ref[3:4, :] = rank2.astype(I32)
    idx_ref[4:8, :] = jnp.zeros((4, tr), I32)
    wts_ref[0:1, :] = w1 / wsum
    wts_ref[1:2, :] = w2 / wsum
    wts_ref[2:8, :] = jnp.zeros((6, tr), F32)
    carry[...] = carry[...] + jnp.sum(oh, axis=1, keepdims=True)
    cnt_ref[...] = carry[...]


def _two_set_specs(ha, hb, tile):
    d = ha.shape[1]
    na = ha.shape[0] // tile
    nb = hb.shape[0] // tile
    return na, nb, [pl.BlockSpec((tile, d), lambda i, *_: (jnp.minimum(i, na - 1), 0)),
                    pl.BlockSpec((tile, d), lambda i, *_: (jnp.maximum(i - na, 0), 0))]


def _router(ha, hb, w_router, b_router, tr):
    d = ha.shape[1]
    t = ha.shape[0] + hb.shape[0]
    na, nb, hspecs = _two_set_specs(ha, hb, tr)
    return pl.pallas_call(
        functools.partial(_router_kernel, na),
        grid=(na + nb,),
        in_specs=hspecs + [pl.BlockSpec((N_EXPERTS, d), lambda i: (0, 0)),
                           pl.BlockSpec((N_EXPERTS, 1), lambda i: (0, 0))],
        out_specs=[pl.BlockSpec((SUBLANES, tr), lambda i: (0, i)),
                   pl.BlockSpec((SUBLANES, tr), lambda i: (0, i)),
                   pl.BlockSpec((N_EXPERTS, LANES), lambda i: (0, 0))],
        out_shape=[jax.ShapeDtypeStruct((SUBLANES, t), I32),
                   jax.ShapeDtypeStruct((SUBLANES, t), F32),
                   jax.ShapeDtypeStruct((N_EXPERTS, LANES), F32)],
        scratch_shapes=[pltpu.VMEM((N_EXPERTS, LANES), F32)],
        compiler_params=_cparams(1, 32),
        name="moe_router",
    )(ha, hb, w_router.T, b_router.reshape(N_EXPERTS, 1))


def _scatter_kernel(n_tok, n_first, pos_ref, meta_ref, ha_ref, hb_ref, o_hbm, zero_scr, sem):
    i = pl.program_id(0)
    ts = ha_ref.shape[0]
    base = i * ts

    def row_copy(src, r, p):
        return pltpu.make_async_copy(src.at[pl.ds(r, 1)], o_hbm.at[pl.ds(p, 1)], sem)

    def scatter_tile(h_ref):
        def start_rows(r, c):
            row_copy(h_ref, r, pos_ref[base + r]).start(priority=0)
            row_copy(h_ref, r, pos_ref[n_tok + base + r]).start(priority=1)
            return c

        lax.fori_loop(0, ts, start_rows, 0, unroll=8)
        for _ in range(2):
            pltpu.make_async_copy(h_ref, o_hbm.at[pl.ds(0, ts)], sem).wait()

    @pl.when(i < n_first)
    def _():
        scatter_tile(ha_ref)

    @pl.when(i >= n_first)
    def _():
        scatter_tile(hb_ref)

    @pl.when(i == 0)
    def _():
        zero_scr[...] = jnp.zeros(zero_scr.shape, zero_scr.dtype)
        nz = zero_scr.shape[0]

        def block_copy(p):
            return pltpu.make_async_copy(zero_scr, o_hbm.at[pl.ds(pl.multiple_of(p, nz), nz)], sem)

        for e in range(N_EXPERTS + 1):
            lo = meta_ref[META_OFFSET, e] + meta_ref[META_COUNT, e]
            hi = meta_ref[META_OFFSET, e + 1]
            mid = jnp.minimum(((lo + (nz - 1)) // nz) * nz, hi)

            def start_row(p, c):
                row_copy(zero_scr, 0, p).start()
                return c

            def wait_row(p, c):
                row_copy(zero_scr, 0, 0).wait()
                return c

            def start_block(b, c):
                block_copy(mid + b * nz).start()
                return c

            def wait_block(b, c):
                block_copy(0).wait()
                return c

            lax.fori_loop(lo, mid, start_row, 0)
            lax.fori_loop(lo, mid, wait_row, 0)
            nblk = (hi - mid) // nz
            lax.fori_loop(0, nblk, start_block, 0)
            lax.fori_loop(0, nblk, wait_block, 0)


def _scatter_rows(ha, hb, pos, meta, n_rows, ts):
    d = ha.shape[1]
    na, nb, hspecs = _two_set_specs(ha, hb, ts)
    return pl.pallas_call(
        functools.partial(_scatter_kernel, ha.shape[0] + hb.shape[0], na),
        grid_spec=pltpu.PrefetchScalarGridSpec(
            num_scalar_prefetch=2,
            grid=(na + nb,),
            in_specs=hspecs,
            out_specs=pl.BlockSpec(memory_space=pl.ANY),
            scratch_shapes=[pltpu.VMEM((SUBLANES, d), ha.dtype), pltpu.SemaphoreType.DMA]),
        out_shape=jax.ShapeDtypeStruct((n_rows, d), ha.dtype),
        compiler_params=_cparams(1, 32),
        name="moe_scatter",
    )(pos, meta, ha, hb)


def _cast_rows(src, dst):
    def body(c, carry):
        r = pl.multiple_of(c * CAST_ROWS, CAST_ROWS)
        dst[pl.ds(r, CAST_ROWS), :] = src[pl.ds(r, CAST_ROWS), :].astype(BF16)
        return carry

    lax.fori_loop(0, src.shape[0] // CAST_ROWS, body, 0)


def _expert_kernel(layer, meta_ref, x_ref, w1_hbm, w2_hbm, y_ref, stg1, stg2, wb1, wb2, sem):
    i = pl.program_id(0)

    def fetch(e):
        return (pltpu.make_async_copy(w1_hbm.at[layer, e], stg1, sem.at[0]),
                pltpu.make_async_copy(w2_hbm.at[layer, e], stg2, sem.at[1]))

    @pl.when(i == 0)
    def _():
        for cp in fetch(meta_ref[META_TILE_EXPERT, 0]):
            cp.start()

    @pl.when(meta_ref[META_FIRST, i] == 1)
    def _():
        for cp in fetch(0):
            cp.wait()
        _cast_rows(stg1, wb1)
        _cast_rows(stg2, wb2)

        @pl.when(meta_ref[META_NEXT, i] >= 0)
        def _():
            for cp in fetch(meta_ref[META_NEXT, i]):
                cp.start(priority=1)

    @pl.when(i < meta_ref[META_N_USED, 0])
    def _():
        hid = jnp.dot(x_ref[...].astype(BF16), wb1[...], preferred_element_type=F32)
        f = hid.shape[1] // 2
        a = hid[:, :f]
        g = hid[:, f:]
        z = (g * jax.nn.sigmoid(g) * a).astype(BF16)
        y_ref[...] = jnp.dot(z, wb2[...], preferred_element_type=F32)

    @pl.when(i >= meta_ref[META_N_USED, 0])
    def _():
        y_ref[...] = jnp.zeros(y_ref.shape, y_ref.dtype)


def _experts(xs, w1, w2, layer, meta):
    p, d = xs.shape
    tm = MOE_TILE
    ff2 = w1.shape[3]
    ff = ff2 // 2
    row = lambda i, meta: (jnp.minimum(i, meta[META_N_USED, 0] - 1), 0)
    return pl.pallas_call(
        functools.partial(_expert_kernel, layer),
        grid_spec=pltpu.PrefetchScalarGridSpec(
            num_scalar_prefetch=1,
            grid=(p // tm,),
            in_specs=[pl.BlockSpec((tm, d), row),
                      pl.BlockSpec(memory_space=pl.ANY),
                      pl.BlockSpec(memory_space=pl.ANY)],
            out_specs=pl.BlockSpec((tm, d), lambda i, *_: (i, 0)),
            scratch_shapes=[pltpu.VMEM((d, ff2), F32), pltpu.VMEM((ff, d), F32),
                            pltpu.VMEM((d, ff2), BF16), pltpu.VMEM((ff, d), BF16),
                            pltpu.SemaphoreType.DMA((2,))]),
        out_shape=jax.ShapeDtypeStruct((p, d), F32),
        compiler_params=_cparams(1, 56),
        name="moe_experts",
    )(meta, xs, w1, w2)


def _combine_kernel(n_tok, tok_base, has_next, pos_ref, w_ref, y_hbm, xres_ref, gate_ref, lng_ref, lnb_ref,
                    *rest):
    if has_next:
        shn_ref, scn_ref, xo_ref, ho_ref, gbuf, fbuf, sem = rest
        nxt = (shn_ref, scn_ref, ho_ref)
    else:
        xo_ref, gbuf, fbuf, sem = rest
        nxt = None
    tc = xres_ref.shape[0]
    i = pl.program_id(0)
    n_steps = pl.num_programs(0)

    n_slots = gbuf.shape[0]
    ahead = n_slots - 1

    def row_copy(step, slot, r, k):
        p = pos_ref[k * n_tok + tok_base + step * tc + r]
        return pltpu.make_async_copy(y_hbm.at[pl.ds(p, 1)], gbuf.at[slot, k, pl.ds(r, 1)], sem.at[slot])

    def wait_tile(slot):
        for k in range(2):
            pltpu.make_async_copy(y_hbm.at[pl.ds(0, tc)], gbuf.at[slot, k], sem.at[slot]).wait()

    @pl.when(i == 0)
    def _():
        for a in range(ahead):
            def start_rows(r, c, a=a):
                for k in range(2):
                    row_copy(jnp.minimum(a, n_steps - 1), a, r, k).start()
                return c

            lax.fori_loop(0, tc, start_rows, 0, unroll=8)

    slot = i % n_slots
    wait_tile(slot)
    fbuf[0] = w_ref[:, 0:1] * gbuf[slot, 0] + w_ref[:, 1:2] * gbuf[slot, 1]
    _residual_ln(fbuf, xres_ref, gate_ref, lng_ref, lnb_ref, xo_ref, nxt)

    nxt_step = jnp.minimum(i + ahead, n_steps - 1)
    nxt_slot = (i + ahead) % n_slots
    for r in range(tc):
        for k in range(2):
            row_copy(nxt_step, nxt_slot, r, k).start(priority=k)

    @pl.when(i == n_steps - 1)
    def _():
        for a in range(1, ahead + 1):
            wait_tile((i + a) % n_slots)


def _combine_ln(ys, pos, wts, tok_base, xres, gate, lng, lnb, nxt_mods, mod_idx, tc):
    t, d = xres.shape
    r = gate.shape[1]
    has_next = nxt_mods is not None
    mspec = pl.BlockSpec((1, r, d), lambda i, *_: (mod_idx(i), 0, 0))
    vspec = pl.BlockSpec((1, d), lambda i, *_: (0, 0))
    ospec = pl.BlockSpec((tc, d), lambda i, *_: (i, 0))
    wbase = tok_base // tc
    in_specs = [pl.BlockSpec((tc, 2), lambda i, *_: (wbase + i, 0)),
                pl.BlockSpec(memory_space=pl.ANY), ospec, mspec, vspec, vspec]
    args = [wts, ys, xres, gate, lng, lnb]
    out_specs = [ospec]
    out_shape = [jax.ShapeDtypeStruct((t, d), F32)]
    if has_next:
        in_specs += [mspec, mspec]
        args += list(nxt_mods)
        out_specs.append(ospec)
        out_shape.append(jax.ShapeDtypeStruct((t, d), BF16))
    return pl.pallas_call(
        functools.partial(_combine_kernel, pos.shape[0] // 2, tok_base, has_next),
        grid_spec=pltpu.PrefetchScalarGridSpec(
            num_scalar_prefetch=1,
            grid=(t // tc,),
            in_specs=in_specs,
            out_specs=out_specs,
            scratch_shapes=[pltpu.VMEM((3, 2, tc, d), F32), pltpu.VMEM((1, tc, d), F32),
                            pltpu.SemaphoreType.DMA((3,))]),
        out_shape=out_shape,
        compiler_params=_cparams(1, 40),
        name="moe_combine",
    )(pos, *args)


def _plan_kernel(n_tiles, idx_ref, cnt_ref, pos_ref, meta_ref):
    tm = MOE_TILE
    t = idx_ref.shape[1]
    sub = lax.broadcasted_iota(I32, (N_EXPERTS, LANES), 0)
    lane = lax.broadcasted_iota(I32, (N_EXPERTS, LANES), 1)
    cnt = cnt_ref[...]
    padded = jnp.floor((cnt + (tm - 1)) * (1.0 / tm)) * tm
    ends = padded
    sh = 1
    while sh < N_EXPERTS:
        ends = ends + jnp.where(sub >= sh, pltpu.roll(ends, sh, 0), 0.0)
        sh *= 2
    off = ends - padded
    total = ends[N_EXPERTS - 1:N_EXPERTS, :]
    n_used = jnp.maximum(total * (1.0 / tm), 1.0)

    def as_row(col):
        return jnp.sum(jnp.where(sub == lane, col, 0.0), axis=0, keepdims=True)

    eio = lax.broadcasted_iota(I32, (N_EXPERTS, t), 0)
    off_col = off[:, 0:1]
    for k in range(2):
        start = jnp.sum(jnp.where(eio == idx_ref[k:k + 1, :], off_col, 0.0), axis=0, keepdims=True)
        pos_ref[k:k + 1, :] = start.astype(I32) + idx_ref[2 + k:3 + k, :]
    pos_ref[2:SUBLANES, :] = jnp.zeros((SUBLANES - 2, t), I32)

    tile = lax.broadcasted_iota(I32, (1, LANES), 1).astype(F32)
    tile_start = jnp.minimum(tile, n_used - 1.0) * tm
    te = jnp.minimum(jnp.sum(jnp.where(ends <= tile_start, 1.0, 0.0), axis=0, keepdims=True), N_EXPERTS - 1.0)
    prev = jnp.where(tile == 0.0, -1.0, pltpu.roll(te, 1, 1))
    first = jnp.where((tile < n_used) & (te != prev), 1.0, 0.0)
    present = as_row(jnp.where(cnt > 0.0, 1.0, 0.0))
    later = jnp.where((present > 0.0) & (lane > sub) & (lane < N_EXPERTS), lane.astype(F32), float(N_EXPERTS))
    nxt_e = jnp.min(later, axis=1, keepdims=True)
    nxt_e = jnp.where(nxt_e == float(N_EXPERTS), -1.0, nxt_e)
    nxt = jnp.sum(jnp.where(sub.astype(F32) == te, nxt_e, 0.0), axis=0, keepdims=True)
    lane_row = lax.broadcasted_iota(I32, (1, LANES), 1)
    off_row = as_row(off)
    off_row = jnp.where(lane_row == N_EXPERTS, total, off_row)
    off_row = jnp.where(lane_row == N_EXPERTS + 1, float(n_tiles * tm), off_row)
    rows = {META_TILE_EXPERT: te, META_FIRST: first, META_NEXT: nxt, META_N_USED: n_used,
            META_COUNT: as_row(cnt), META_OFFSET: off_row}
    for r in range(SUBLANES):
        meta_ref[r:r + 1, :] = rows[r].astype(I32) if r in rows else jnp.zeros((1, LANES), I32)


def _moe_plan(idx, cnt, n_tok):
    tm = MOE_TILE
    n_tiles = (2 * n_tok + N_EXPERTS * (tm - 1) + tm - 1) // tm
    assert n_tiles <= LANES
    pos, meta = pl.pallas_call(
        functools.partial(_plan_kernel, n_tiles),
        out_shape=[jax.ShapeDtypeStruct((SUBLANES, n_tok), I32), jax.ShapeDtypeStruct((SUBLANES, LANES), I32)],
        compiler_params=_cparams(0, 32),
        name="moe_plan",
    )(idx, cnt)
    return pos, meta, n_tiles * tm


class _Rows:
    def __init__(self, n_seq, seq_len, tile):
        self.n_seq, self.seq_len, self.tile = n_seq, seq_len, tile
        self.per_row = seq_len < tile

    def mods(self, m):
        if self.per_row:
            return jnp.repeat(m, self.seq_len, axis=0).reshape(-1, self.tile, m.shape[1])
        return m[:, None, :]

    def mod_idx(self, i):
        return i if self.per_row else (i * self.tile) // self.seq_len


def _split_mods(mod_l, n_prompt):
    d = mod_l.shape[1] // 6
    cols = [mod_l[:, k * d:(k + 1) * d] for k in range(6)]
    return [c[:n_prompt] for c in cols], [c[n_prompt:] for c in cols]


def kernel(x_prompt, x_sample, state_mlstm_C, state_mlstm_n, state_mlstm_m, c_prompt, c_sample, w_ada, b_ada, ln_g, ln_b, a_w_in, a_b_gates, a_norm_w, a_w_out, b_w_in, b_b_in, b_norm_g, b_norm_b, b_w_s, b_b_s, b_w_out, w_router, b_router, w_expert_in, w_expert_out):
    bp, sp, d = x_prompt.shape
    bs, ss, _ = x_sample.shape
    tp = bp * sp
    ts = bs * ss
    n_tok = tp + ts
    qk = N_HEADS * DK
    vd = N_HEADS * DV
    n_main = 2 * qk + 2 * vd

    n_seq = bp + bs
    pad = (-n_seq) % SUBLANES
    c_all = jnp.concatenate([c_prompt, c_sample, jnp.zeros((pad, d), F32)])
    mod = _adaln(c_all, w_ada, b_ada)[:, :n_seq]

    sets = {"p": _Rows(bp, sp, 512), "s": _Rows(bs, ss, 256)}
    base = {"p": 0, "s": tp}
    x = {"p": x_prompt.reshape(tp, d), "s": x_sample.reshape(ts, d)}
    hm = {}
    outs = {}
    for layer in range(DEPTH):
        j = layer // 2
        mp, ms = _split_mods(mod[layer], bp)
        md = {"p": mp, "s": ms}
        lng = ln_g[layer]
        lnb = ln_b[layer]
        lhs = {}
        if layer % 2 == 0:
            w_in = a_w_in[j]
            gb = jnp.zeros((1, 2 * LANES), F32)
            gb = gb.at[0, :N_HEADS].set(a_b_gates[j, :N_HEADS])
            gb = gb.at[0, LANES:LANES + N_HEADS].set(a_b_gates[j, N_HEADS:])
            nw = a_norm_w[j].reshape(1, vd)
            proj = {}
            gates = {}
            for k, rs in sets.items():
                proj[k], gates[k] = _inproj(x[k], rs.mods(md[k][0]), rs.mods(md[k][1]), rs.mod_idx,
                                            w_in, n_main, rs.tile, BF16 if k == "p" else F32)
            lhs["p"], c_p, n_p, m_p = _mlstm_chunks(
                proj["p"], gates["p"], gb, nw,
                jnp.zeros((bp, N_HEADS, DK, DV), F32), jnp.zeros((bp, N_HEADS, LANES), F32),
                jnp.zeros((bp, N_HEADS, LANES), F32), bp, sp)
            rpad = SUBLANES - ss
            proj_s = jnp.pad(proj["s"].reshape(bs, ss, n_main), ((0, 0), (0, rpad), (0, 0)))
            gates_s = jnp.pad(gates["s"].reshape(bs, ss, 2 * LANES), ((0, 0), (0, rpad), (0, 0)))
            m0 = jnp.pad(state_mlstm_m[j], ((0, 0), (0, LANES - N_HEADS)))[:, None, :]
            ypre_s, c_s, n_s, m_s = _mlstm_short(proj_s, gates_s, gb, nw, state_mlstm_C[j],
                                                 state_mlstm_n[j].reshape(bs, 1, qk), m0, ss)
            n_s = n_s.reshape(bs, N_HEADS, DK)
            lhs["s"] = ypre_s[:, :ss].reshape(ts, vd)
            outs["C_p"], outs["n_p"], outs["m_p"] = c_p, n_p, m_p[:, :, 0]
            outs["C_s"], outs["n_s"], outs["m_s"] = c_s, n_s, m_s[:, 0, :N_HEADS]
            w_out = a_w_out[j]
        else:
            ws = b_w_s[j]
            bsv = b_b_s[j]
            w_out = b_w_out[j]

            def mixing(l):
                tri = jnp.tril(jnp.ones((l, l), bool))
                wsl = jnp.where(tri, ws[:, :l, :l], 0.0)
                eye = jnp.eye(CHUNK // l, dtype=F32)
                mats = jax.vmap(lambda m: jnp.kron(eye, m))(wsl).astype(BF16)
                bias = jnp.tile(bsv[:, :l], (1, CHUNK // l))[:, :, None]
                return mats, bias

            uv = {k: _gmlp_in(hm[k], b_w_in[j], b_b_in[j], min(1024, rs.n_seq * rs.seq_len),
                              BF16 if k == "p" else F32) for k, rs in sets.items()}
        x1 = {}
        hf = {}
        for k, rs in sets.items():
            if layer % 2 == 0:
                x1[k], hf[k] = _proj_ln(lhs[k], w_out, x[k], rs.mods(md[k][2]), lng[0:1], lnb[0:1],
                                        rs.mods(md[k][3]), rs.mods(md[k][4]), rs.mod_idx, rs.tile, 512)
            else:
                rs = _Rows(rs.n_seq, rs.seq_len, rs.tile // 2)
                mats, bias = mixing(min(CHUNK, rs.seq_len))
                res = _mix_proj_ln(uv[k], b_norm_g[j], b_norm_b[j], mats, bias, w_out, x[k], rs.mods(md[k][2]),
                                   lng[0:1], lnb[0:1], rs.mods(md[k][3]), rs.mods(md[k][4]), rs.mod_idx,
                                   rs.tile, 512, k == "s")
                x1[k], hf[k] = res[0], res[1]
                if k == "s":
                    outs["v_s"] = res[2]

        idx, wts, cnt = _router(hf["p"], hf["s"], w_router, b_router, 512)
        pos, meta, n_rows = _moe_plan(idx, cnt, n_tok)
        pos = pos[:2].reshape(-1)
        xsorted = _scatter_rows(hf["p"], hf["s"], pos, meta, n_rows, 256)
        ysorted = _experts(xsorted, w_expert_in, w_expert_out, layer, meta)
        wts2 = wts[:2].T
        nxt = _split_mods(mod[layer + 1], bp) if layer + 1 < DEPTH else None
        for ki, (k, rs) in enumerate(sets.items()):
            rc = _Rows(rs.n_seq, rs.seq_len, 256)
            nxt_mods = None if nxt is None else (rc.mods(nxt[ki][0]), rc.mods(nxt[ki][1]))
            res = _combine_ln(ysorted, pos, wts2, base[k], x1[k], rc.mods(md[k][5]),
                              lng[1:2], lnb[1:2], nxt_mods, rc.mod_idx, 256)
            x[k] = res[0]
            if nxt is not None:
                hm[k] = res[1]

    return (x["p"].reshape(bp, sp, d), x["s"].reshape(bs, ss, d),
            outs["C_p"][None], outs["n_p"][None], outs["m_p"][None],
            outs["C_s"][None], outs["n_s"][None], outs["m_s"][None],
            outs["v_s"].reshape(bs, ss, -1)[None])
```

```python
import functools

import jax
import jax.numpy as jnp
from jax import lax
from jax.experimental import pallas as pl
from jax.experimental.pallas import tpu as pltpu

F32 = jnp.float32
BF16 = jnp.bfloat16
I32 = jnp.int32

DEPTH = 2
N_HEADS = 8
DK = 128
DV = 256
CHUNK = 128
N_GROUPS_B = 8
N_EXPERTS = 16
N_EXPERT_GROUPS = 4
EXPERTS_PER_GROUP = 4
ALPHA = float((2 * DEPTH) ** 0.25)
LN_EPS = 1e-5

LANES = 128
SUBLANES = 8
MIB = 1024 * 1024
MOE_TILE = 256
CAST_ROWS = 256
STAGE_ROWS = 128
META_TILE_EXPERT, META_FIRST, META_NEXT, META_N_USED, META_COUNT, META_OFFSET = range(6)


def _cparams(n_axes, vmem_mib):
    return pltpu.CompilerParams(
        dimension_semantics=("arbitrary",) * n_axes,
        vmem_limit_bytes=int(vmem_mib * MIB))


def _split3(x):
    hi = x.astype(BF16)
    r1 = x - hi.astype(F32)
    mid = r1.astype(BF16)
    lo = (r1 - mid.astype(F32)).astype(BF16)
    return hi, mid, lo


def _log_sigmoid(x):
    return jnp.minimum(x, 0.0) - jnp.log1p(jnp.exp(-jnp.abs(x)))


def _load_weight_bf16(w_hbm, wb, stg, sem):
    nj, k, tn = wb.shape
    ch = stg.shape[1]
    nch = k // ch

    def chunk(c):
        return pltpu.make_async_copy(w_hbm.at[pl.ds(c * ch, ch), pl.ds(0, nj * tn)], stg.at[c % 2],
                                     sem.at[c % 2])

    chunk(0).start()
    for c in range(nch):
        if c + 1 < nch:
            chunk(c + 1).start()
        chunk(c).wait()
        for jj in range(nj):
            wb[jj, c * ch:(c + 1) * ch, :] = stg[c % 2, :, jj * tn:(jj + 1) * tn].astype(BF16)


def _adaln_kernel(c_ref, w_ref, b_ref, o_ref):
    c = c_ref[...]
    a = (c * jax.nn.sigmoid(c)).astype(BF16)
    o_ref[0] = jnp.dot(a, w_ref[0].astype(BF16), preferred_element_type=F32) + b_ref[0]


def _adaln(c_all, w_ada, b_ada):
    depth, d, n = w_ada.shape
    r = c_all.shape[0]
    tn = 1024
    return pl.pallas_call(
        _adaln_kernel,
        grid=(depth, n // tn),
        in_specs=[pl.BlockSpec((r, d), lambda l, j: (0, 0)),
                  pl.BlockSpec((1, d, tn), lambda l, j: (l, 0, j)),
                  pl.BlockSpec((1, 1, tn), lambda l, j: (l, 0, j))],
        out_specs=pl.BlockSpec((1, r, tn), lambda l, j: (l, 0, j)),
        out_shape=jax.ShapeDtypeStruct((depth, r, n), F32),
        compiler_params=_cparams(2, 32),
        name="adaln",
    )(c_all, w_ada, b_ada.reshape(depth, 1, n))


def _inproj_kernel(x_ref, sh_ref, sc_ref, w_hbm, proj_ref, gates_ref, hin_ref, wb, stg, g16, wgs, sem):
    j = pl.program_id(1)

    @pl.when((pl.program_id(0) == 0) & (j == 0))
    def _():
        _load_weight_bf16(w_hbm, wb, stg, sem)
        n_main = wb.shape[0] * wb.shape[2]
        cp = pltpu.make_async_copy(w_hbm.at[:, pl.ds(n_main, 2 * N_HEADS)], g16, sem.at[0])
        cp.start()
        cp.wait()
        g = g16[...]
        z = jnp.zeros((g.shape[0], LANES - N_HEADS), F32)
        wg = jnp.concatenate([g[:, :N_HEADS], z, g[:, N_HEADS:], z], axis=1)
        hi = wg.astype(BF16)
        wgs[0] = hi
        wgs[1] = (wg - hi.astype(F32)).astype(BF16)

    @pl.when(j == 0)
    def _():
        h = x_ref[...] * (1.0 + sc_ref[0]) + sh_ref[0]
        hb = h.astype(BF16)
        hin_ref[...] = hb
        h_lo = (h - hb.astype(F32)).astype(BF16)
        gates_ref[...] = (jnp.dot(hb, wgs[0], preferred_element_type=F32)
                          + jnp.dot(h_lo, wgs[0], preferred_element_type=F32)
                          + jnp.dot(hb, wgs[1], preferred_element_type=F32))

    proj_ref[...] = jnp.dot(hin_ref[...], wb[j], preferred_element_type=F32).astype(proj_ref.dtype)


def _inproj(x, sh, sc, mod_idx, w_in, n_main, tm, out_dtype):
    t, d = x.shape
    tn = 1024
    r = sh.shape[1]
    mspec = pl.BlockSpec((1, r, d), lambda i, j: (mod_idx(i), 0, 0))
    return pl.pallas_call(
        _inproj_kernel,
        grid=(t // tm, n_main // tn),
        in_specs=[pl.BlockSpec((tm, d), lambda i, j: (i, 0)), mspec, mspec,
                  pl.BlockSpec(memory_space=pl.ANY)],
        out_specs=[pl.BlockSpec((tm, tn), lambda i, j: (i, j)),
                   pl.BlockSpec((tm, 2 * LANES), lambda i, j: (i, 0))],
        out_shape=[jax.ShapeDtypeStruct((t, n_main), out_dtype),
                   jax.ShapeDtypeStruct((t, 2 * LANES), F32)],
        scratch_shapes=[pltpu.VMEM((tm, d), BF16), pltpu.VMEM((n_main // tn, d, tn), BF16),
                        pltpu.VMEM((2, STAGE_ROWS, n_main), F32), pltpu.VMEM((d, 2 * N_HEADS), F32),
                        pltpu.VMEM((2, d, 2 * LANES), BF16), pltpu.SemaphoreType.DMA((2,))],
        compiler_params=_cparams(2, 56),
        name="mlstm_inproj",
    )(x, sh, sc, w_in)


def _head_norm_gate(hh, nw, o):
    mu = jnp.mean(hh, axis=1, keepdims=True)
    xc = hh - mu
    var = jnp.mean(xc * xc, axis=1, keepdims=True)
    return jax.nn.sigmoid(o) * (xc * lax.rsqrt(var + LN_EPS) * nw)


def _mlstm_chunk_kernel(q_ref, k_ref, v_ref, o_ref, g_ref, gb_ref, nw_ref, c0_ref, n0_ref, m0_ref,
                        y_ref, cout_ref, nout_ref, mout_ref, c_scr, n_scr, m_scr):
    ci = pl.program_id(1)
    L = q_ref.shape[0]

    @pl.when(ci == 0)
    def _():
        c_scr[...] = c0_ref[0]
        n_scr[...] = n0_ref[0]
        m_scr[...] = m0_ref[0]

    g = g_ref[...] + gb_ref[...]
    gi = g[:, :LANES]
    lf = _log_sigmoid(g[:, LANES:])
    row = lax.broadcasted_iota(I32, (L, L), 0)
    col = lax.broadcasted_iota(I32, (L, L), 1)
    causal = col <= row
    ltri = jnp.where(causal, 1.0, 0.0).astype(BF16)
    hi, mid, lo = _split3(lf)
    bcum = (jnp.dot(ltri, hi, preferred_element_type=F32)
            + jnp.dot(ltri, mid, preferred_element_type=F32)
            + jnp.dot(ltri, lo, preferred_element_type=F32))
    a = gi - bcum
    a_t = a.T
    scale = DK ** -0.5
    nt = (((1,), (1,)), ((), ()))
    tn_dims = (((0,), (0,)), ((), ()))
    for h in range(N_HEADS):
        ks = slice(h * DK, (h + 1) * DK)
        vs = slice(h * DV, (h + 1) * DV)
        qf = q_ref[:, ks].astype(F32) * scale
        qb = qf.astype(BF16)
        kf = k_ref[:, ks].astype(F32)
        kb = kf.astype(BF16)
        vb = v_ref[:, vs].astype(BF16)
        a_row = a_t[h:h + 1, :]
        a_col = a[:, h:h + 1]
        b_col = bcum[:, h:h + 1]
        m_prev = m_scr[h:h + 1, 0:1]
        amat = jnp.where(causal, a_row, -jnp.inf)
        mx = jnp.max(amat, axis=1, keepdims=True)
        m_inter = b_col + m_prev
        m_t = jnp.maximum(m_inter, b_col + mx)
        dm = jnp.exp(amat + (b_col - m_t))
        s = lax.dot_general(qb, kb, nt, preferred_element_type=F32)
        scores = s * dm
        inter = jnp.exp(m_inter - m_t)
        c_old = c_scr[h]
        n_old = n_scr[h:h + 1, :]
        qc = jnp.dot(qb, c_old.astype(BF16), preferred_element_type=F32)
        num = jnp.dot(scores.astype(BF16), vb, preferred_element_type=F32) + inter * qc
        qn = jnp.sum(qf * n_old, axis=1, keepdims=True)
        den = jnp.sum(scores, axis=1, keepdims=True) + inter * qn
        hh = num / jnp.maximum(jnp.abs(den), jnp.exp(-m_t))
        m_new = m_t[L - 1:L, :]
        b_last = b_col[L - 1:L, :]
        w_col = jnp.exp(b_last + a_col - m_new)
        decay = jnp.exp(b_last + m_prev - m_new)
        kw = kf * w_col
        c_scr[h] = decay * c_old + lax.dot_general(kw.astype(BF16), vb, tn_dims,
                                                   preferred_element_type=F32)
        n_scr[h:h + 1, :] = decay * n_old + jnp.sum(kw, axis=0, keepdims=True)
        m_scr[h:h + 1, :] = jnp.broadcast_to(m_new, (1, LANES))
        y_ref[:, vs] = _head_norm_gate(hh, nw_ref[:, vs], o_ref[:, vs].astype(F32)).astype(y_ref.dtype)

    @pl.when(ci == pl.num_programs(1) - 1)
    def _():
        cout_ref[0] = c_scr[...]
        nout_ref[0] = n_scr[...]
        mout_ref[0] = m_scr[...]


def _mlstm_chunks(proj, gates, gb, nw, c0, n0, m0, batch, seq):
    nc = seq // CHUNK
    qk = N_HEADS * DK
    vd = N_HEADS * DV
    row = lambda b, c: b * nc + c
    st4 = pl.BlockSpec((1, N_HEADS, DK, DV), lambda b, c: (b, 0, 0, 0))
    st3 = pl.BlockSpec((1, N_HEADS, LANES), lambda b, c: (b, 0, 0))
    return pl.pallas_call(
        _mlstm_chunk_kernel,
        grid=(batch, nc),
        in_specs=[pl.BlockSpec((CHUNK, qk), lambda b, c: (row(b, c), 0)),
                  pl.BlockSpec((CHUNK, qk), lambda b, c: (row(b, c), 1)),
                  pl.BlockSpec((CHUNK, vd), lambda b, c: (row(b, c), 1)),
                  pl.BlockSpec((CHUNK, vd), lambda b, c: (row(b, c), 2)),
                  pl.BlockSpec((CHUNK, 2 * LANES), lambda b, c: (row(b, c), 0)),
                  pl.BlockSpec((1, 2 * LANES), lambda b, c: (0, 0)),
                  pl.BlockSpec((1, vd), lambda b, c: (0, 0)),
                  st4, st3, st3],
        out_specs=[pl.BlockSpec((CHUNK, vd), lambda b, c: (row(b, c), 0)), st4, st3, st3],
        out_shape=[jax.ShapeDtypeStruct((batch * seq, vd), BF16),
                   jax.ShapeDtypeStruct((batch, N_HEADS, DK, DV), F32),
                   jax.ShapeDtypeStruct((batch, N_HEADS, LANES), F32),
                   jax.ShapeDtypeStruct((batch, N_HEADS, LANES), F32)],
        scratch_shapes=[pltpu.VMEM((N_HEADS, DK, DV), F32),
                        pltpu.VMEM((N_HEADS, LANES), F32),
                        pltpu.VMEM((N_HEADS, LANES), F32)],
        compiler_params=_cparams(2, 40),
        name="mlstm_chunks",
    )(proj, proj, proj, proj, gates, gb, nw, c0, n0, m0)


def _per_head(x, width):
    return jnp.concatenate(
        [jnp.broadcast_to(x[..., h:h + 1], x.shape[:-1] + (width,)) for h in range(N_HEADS)], axis=-1)


def _head_sums(x, width):
    lane = lax.broadcasted_iota(I32, x.shape[:-1] + (LANES,), x.ndim - 1)
    out = jnp.zeros(x.shape[:-1] + (LANES,), F32)
    for h in range(N_HEADS):
        s = jnp.sum(x[..., h * width:(h + 1) * width], axis=-1, keepdims=True)
        out = jnp.where(lane == h, s, out)
    return out


def _mlstm_short_kernel(seq, q_ref, k_ref, v_ref, o_ref, g_ref, gb_ref, nw_ref, c0_ref, n0_ref, m0_ref,
                        y_ref, cout_ref, nout_ref, mout_ref, qc_scr, kw_scr):
    bt, rows, _ = q_ref.shape
    scale = DK ** -0.5
    tn_dims = (((0,), (0,)), ((), ()))
    row = lax.broadcasted_iota(I32, (bt, rows, LANES), 1)
    g = g_ref[...] + gb_ref[...]
    gi = g[:, :, :LANES]
    lf = _log_sigmoid(g[:, :, LANES:])
    bcum = jnp.zeros_like(lf)
    for s in range(seq):
        bcum = bcum + jnp.where(row >= s, lf[:, s:s + 1, :], 0.0)
    a = gi - bcum
    mx = jnp.full_like(a, -jnp.inf)
    for s in range(seq):
        mx = jnp.maximum(mx, jnp.where(row >= s, a[:, s:s + 1, :], -jnp.inf))
    m_prev = m0_ref[...]
    m_inter = bcum + m_prev
    m_t = jnp.maximum(m_inter, bcum + mx)
    cmt = bcum - m_t
    inter = jnp.exp(m_inter - m_t)
    einv = jnp.exp(-m_t)
    m_new = m_t[:, seq - 1:seq, :]
    b_last = bcum[:, seq - 1:seq, :]
    w = jnp.where(row < seq, jnp.exp(b_last + a - m_new), 0.0)
    decay = jnp.exp(b_last + m_prev - m_new)
    mout_ref[...] = m_new

    q = q_ref[...] * scale
    k = k_ref[...]
    v = v_ref[...]
    n_old = n0_ref[...]
    for b in range(bt):
        for h in range(N_HEADS):
            qc_scr[b, :, h * DV:(h + 1) * DV] = jnp.dot(
                (q_ref[b, :, h * DK:(h + 1) * DK] * scale).astype(BF16), c0_ref[b, h].astype(BF16),
                preferred_element_type=F32)
    den = inter * _head_sums(q * n_old, DK)
    num = _per_head(inter, DV) * qc_scr[...]
    for s in range(seq):
        p = _head_sums(q * k[:, s:s + 1, :], DK) * jnp.where(row >= s, jnp.exp(cmt + a[:, s:s + 1, :]), 0.0)
        den = den + p
        num = num + _per_head(p, DV) * v[:, s:s + 1, :]
    hh = num * _per_head(1.0 / jnp.maximum(jnp.abs(den), einv), DV)
    mu = _head_sums(hh, DV) * (1.0 / DV)
    xc = hh - _per_head(mu, DV)
    var = _head_sums(xc * xc, DV) * (1.0 / DV)
    hn = xc * _per_head(lax.rsqrt(var + LN_EPS), DV) * nw_ref[...]
    y_ref[...] = jax.nn.sigmoid(o_ref[...]) * hn

    kw = k * _per_head(w, DK)
    kw_scr[...] = kw
    nout_ref[...] = _per_head(decay, DK) * n_old + jnp.sum(kw, axis=1, keepdims=True)
    for b in range(bt):
        for h in range(N_HEADS):
            cout_ref[b, h] = (decay[b, :, h:h + 1] * c0_ref[b, h]
                              + lax.dot_general(kw_scr[b, :, h * DK:(h + 1) * DK],
                                                v_ref[b, :, h * DV:(h + 1) * DV],
                                                tn_dims, preferred_element_type=F32))


def _mlstm_short(proj, gates, gb, nw, c0, n0, m0, seq):
    batch, rows, _ = proj.shape
    qk = N_HEADS * DK
    vd = N_HEADS * DV
    bt = 8
    st4 = pl.BlockSpec((bt, N_HEADS, DK, DV), lambda i: (i, 0, 0, 0))
    st3 = pl.BlockSpec((bt, 1, qk), lambda i: (i, 0, 0))
    stm = pl.BlockSpec((bt, 1, LANES), lambda i: (i, 0, 0))
    return pl.pallas_call(
        functools.partial(_mlstm_short_kernel, seq),
        grid=(batch // bt,),
        in_specs=[pl.BlockSpec((bt, rows, qk), lambda i: (i, 0, 0)),
                  pl.BlockSpec((bt, rows, qk), lambda i: (i, 0, 1)),
                  pl.BlockSpec((bt, rows, vd), lambda i: (i, 0, 1)),
                  pl.BlockSpec((bt, rows, vd), lambda i: (i, 0, 2)),
                  pl.BlockSpec((bt, rows, 2 * LANES), lambda i: (i, 0, 0)),
                  pl.BlockSpec((1, 2 * LANES), lambda i: (0, 0)),
                  pl.BlockSpec((1, vd), lambda i: (0, 0)),
                  st4, st3, stm],
        out_specs=[pl.BlockSpec((bt, rows, vd), lambda i: (i, 0, 0)), st4, st3, stm],
        out_shape=[jax.ShapeDtypeStruct((batch, rows, vd), F32),
                   jax.ShapeDtypeStruct((batch, N_HEADS, DK, DV), F32),
                   jax.ShapeDtypeStruct((batch, 1, qk), F32),
                   jax.ShapeDtypeStruct((batch, 1, LANES), F32)],
        scratch_shapes=[pltpu.VMEM((bt, rows, vd), F32), pltpu.VMEM((bt, rows, qk), F32)],
        compiler_params=_cparams(1, 48),
        name="mlstm_short",
    )(proj, proj, proj, proj, gates, gb, nw, c0, n0, m0)


def _residual_ln(zbuf, xres_ref, gate_ref, lng_ref, lnb_ref, xo_ref, nxt):
    nc, tm, tn = zbuf.shape
    inv_d = 1.0 / (nc * tn)
    ssum = jnp.zeros((tm, 1), F32)
    for c in range(nc):
        sl = pl.ds(c * tn, tn)
        z = ALPHA * xres_ref[:, sl] + gate_ref[0, :, sl] * zbuf[c]
        zbuf[c] = z
        ssum = ssum + jnp.sum(z, axis=1, keepdims=True)
    mu = ssum * inv_d
    vsum = jnp.zeros((tm, 1), F32)
    for c in range(nc):
        zc = zbuf[c] - mu
        vsum = vsum + jnp.sum(zc * zc, axis=1, keepdims=True)
    rstd = lax.rsqrt(vsum * inv_d + LN_EPS)
    for c in range(nc):
        sl = pl.ds(c * tn, tn)
        xn = (zbuf[c] - mu) * rstd * lng_ref[:, sl] + lnb_ref[:, sl]
        xo_ref[:, sl] = xn
        if nxt is not None:
            sh_ref, sc_ref, ho_ref = nxt
            ho_ref[:, sl] = (xn * (1.0 + sc_ref[0, :, sl]) + sh_ref[0, :, sl]).astype(ho_ref.dtype)


def _proj_ln_kernel(lhs_ref, w_hbm, xres_ref, gate_ref, lng_ref, lnb_ref, shn_ref, scn_ref,
                    xo_ref, ho_ref, wb, stg, ybuf, sem):
    @pl.when(pl.program_id(0) == 0)
    def _():
        _load_weight_bf16(w_hbm, wb, stg, sem)

    lhs = lhs_ref[...].astype(BF16)
    for c in range(ybuf.shape[0]):
        ybuf[c] = jnp.dot(lhs, wb[c], preferred_element_type=F32)
    _residual_ln(ybuf, xres_ref, gate_ref, lng_ref, lnb_ref, xo_ref, (shn_ref, scn_ref, ho_ref))


def _proj_ln(lhs, w, xres, gate, lng, lnb, shn, scn, mod_idx, tm, tn):
    t, k = lhs.shape
    d = w.shape[1]
    r = gate.shape[1]
    mspec = pl.BlockSpec((1, r, d), lambda i: (mod_idx(i), 0, 0))
    vspec = pl.BlockSpec((1, d), lambda i: (0, 0))
    ospec = pl.BlockSpec((tm, d), lambda i: (i, 0))
    return pl.pallas_call(
        _proj_ln_kernel,
        grid=(t // tm,),
        in_specs=[pl.BlockSpec((tm, k), lambda i: (i, 0)),
                  pl.BlockSpec(memory_space=pl.ANY),
                  ospec, mspec, vspec, vspec, mspec, mspec],
        out_specs=[ospec, ospec],
        out_shape=[jax.ShapeDtypeStruct((t, d), F32), jax.ShapeDtypeStruct((t, d), F32)],
        scratch_shapes=[pltpu.VMEM((d // tn, k, tn), BF16), pltpu.VMEM((2, CAST_ROWS, d), F32),
                        pltpu.VMEM((d // tn, tm, tn), F32), pltpu.SemaphoreType.DMA((2,))],
        compiler_params=_cparams(1, 58),
        name="proj_ln",
    )(lhs, w, xres, gate, lng, lnb, shn, scn)


def _gelu_tanh(x):
    return x * (0.5 * (1.0 + jnp.tanh(0.7978845608028654 * (x + 0.044715 * (x * x * x)))))


def _gmlp_in_kernel(h_ref, w_hbm, b_ref, o_ref, wb, stg, sem):
    j = pl.program_id(1)

    @pl.when((pl.program_id(0) == 0) & (j == 0))
    def _():
        _load_weight_bf16(w_hbm, wb, stg, sem)

    acc = jnp.dot(h_ref[...].astype(BF16), wb[j], preferred_element_type=F32)
    o_ref[...] = _gelu_tanh(acc + b_ref[...]).astype(o_ref.dtype)


def _gmlp_in(h, w, b, tm, out_dtype):
    t, d = h.shape
    n = w.shape[1]
    tn = 1024
    return pl.pallas_call(
        _gmlp_in_kernel,
        grid=(t // tm, n // tn),
        in_specs=[pl.BlockSpec((tm, d), lambda i, j: (i, 0)),
                  pl.BlockSpec(memory_space=pl.ANY),
                  pl.BlockSpec((1, tn), lambda i, j: (0, j))],
        out_specs=pl.BlockSpec((tm, tn), lambda i, j: (i, j)),
        out_shape=jax.ShapeDtypeStruct((t, n), out_dtype),
        scratch_shapes=[pltpu.VMEM((n // tn, d, tn), BF16), pltpu.VMEM((2, STAGE_ROWS, n), F32),
                        pltpu.SemaphoreType.DMA((2,))],
        compiler_params=_cparams(2, 58),
        name="gmlp_in",
    )(h, w, b.reshape(1, n))


def _gmlp_mix_rows(u_ref, v_ref, g_ref, b_ref, mix_ref, bias_ref, o_ref, vn_ref):
    v = v_ref[...].astype(F32)
    mu = jnp.mean(v, axis=1, keepdims=True)
    xc = v - mu
    var = jnp.mean(xc * xc, axis=1, keepdims=True)
    vn = xc * lax.rsqrt(var + LN_EPS) * g_ref[...] + b_ref[...]
    if vn_ref is not None:
        vn_ref[...] = vn
    gd = v.shape[1] // N_GROUPS_B
    mb = mix_ref.shape[1]
    for g in range(N_GROUPS_B):
        sl = slice(g * gd, (g + 1) * gd)
        for blk in range(v.shape[0] // mb):
            rows = slice(blk * mb, (blk + 1) * mb)
            mixed = jnp.dot(mix_ref[g], vn[rows, sl].astype(BF16), preferred_element_type=F32) + bias_ref[g]
            o_ref[rows, sl] = (u_ref[rows, sl].astype(F32) * mixed).astype(o_ref.dtype)


def _mix_proj_ln_kernel(emit_v, u_ref, v_ref, g_ref, b_ref, mix_ref, bias_ref, w_hbm, xres_ref, gate_ref,
                        lng_ref, lnb_ref, shn_ref, scn_ref, *rest):
    if emit_v:
        xo_ref, ho_ref, vn_ref, lhs_scr, wb, stg, ybuf, sem = rest
    else:
        xo_ref, ho_ref, lhs_scr, wb, stg, ybuf, sem = rest
        vn_ref = None

    @pl.when(pl.program_id(0) == 0)
    def _():
        _load_weight_bf16(w_hbm, wb, stg, sem)

    _gmlp_mix_rows(u_ref, v_ref, g_ref, b_ref, mix_ref, bias_ref, lhs_scr, vn_ref)
    lhs = lhs_scr[...]
    for c in range(ybuf.shape[0]):
        ybuf[c] = jnp.dot(lhs, wb[c], preferred_element_type=F32)
    _residual_ln(ybuf, xres_ref, gate_ref, lng_ref, lnb_ref, xo_ref, (shn_ref, scn_ref, ho_ref))


def _mix_proj_ln(uv, nv_g, nv_b, mix, bias, w, xres, gate, lng, lnb, shn, scn, mod_idx, tm, tn, emit_v):
    t, n2 = uv.shape
    di = n2 // 2
    d = w.shape[1]
    r = gate.shape[1]
    mspec = pl.BlockSpec((1, r, d), lambda i: (mod_idx(i), 0, 0))
    vspec = pl.BlockSpec((1, d), lambda i: (0, 0))
    ospec = pl.BlockSpec((tm, d), lambda i: (i, 0))
    nspec = pl.BlockSpec((1, di), lambda i: (0, 0))
    out_specs = [ospec, ospec]
    out_shape = [jax.ShapeDtypeStruct((t, d), F32), jax.ShapeDtypeStruct((t, d), F32)]
    if emit_v:
        out_specs.append(pl.BlockSpec((tm, di), lambda i: (i, 0)))
        out_shape.append(jax.ShapeDtypeStruct((t, di), F32))
    return pl.pallas_call(
        functools.partial(_mix_proj_ln_kernel, emit_v),
        grid=(t // tm,),
        in_specs=[pl.BlockSpec((tm, di), lambda i: (i, 0)),
                  pl.BlockSpec((tm, di), lambda i: (i, 1)),
                  nspec, nspec,
                  pl.BlockSpec(mix.shape, lambda i: (0, 0, 0)),
                  pl.BlockSpec(bias.shape, lambda i: (0, 0, 0)),
                  pl.BlockSpec(memory_space=pl.ANY),
                  ospec, mspec, vspec, vspec, mspec, mspec],
        out_specs=out_specs,
        out_shape=out_shape,
        scratch_shapes=[pltpu.VMEM((tm, di), BF16), pltpu.VMEM((d // tn, di, tn), BF16),
                        pltpu.VMEM((2, CAST_ROWS, d), F32), pltpu.VMEM((d // tn, tm, tn), F32),
                        pltpu.SemaphoreType.DMA((2,))],
        compiler_params=_cparams(1, 58),
        name="gmlp_mix_proj_ln",
    )(uv, uv, nv_g.reshape(1, di), nv_b.reshape(1, di), mix, bias, w, xres, gate, lng, lnb, shn, scn)


def _router_kernel(n_first, ha_ref, hb_ref, wr_ref, br_ref, idx_ref, wts_ref, cnt_ref, carry):
    i = pl.program_id(0)
    tr = ha_ref.shape[0]

    @pl.when(i == 0)
    def _():
        carry[...] = jnp.zeros(carry.shape, carry.dtype)

    nt = (((1,), (1,)), ((), ()))
    h = jnp.where(i < n_first, ha_ref[...], hb_ref[...])
    hb = h.astype(BF16)
    hl = (h - hb.astype(F32)).astype(BF16)
    wr = wr_ref[...]
    wb = wr.astype(BF16)
    wl = (wr - wb.astype(F32)).astype(BF16)
    logits = (lax.dot_general(wb, hb, nt, preferred_element_type=F32)
              + lax.dot_general(wb, hl, nt, preferred_element_type=F32)
              + lax.dot_general(wl, hb, nt, preferred_element_type=F32))
    s = jax.nn.sigmoid(logits)
    sel = s + br_ref[...]
    epg = EXPERTS_PER_GROUP
    r = [sel[e:e + 1, :] for e in range(N_EXPERTS)]
    su = [s[e:e + 1, :] for e in range(N_EXPERTS)]

    def top2sum(v):
        best = v[0] + v[1]
        for x in range(epg):
            for y in range(x + 1, epg):
                if (x, y) != (0, 1):
                    best = jnp.maximum(best, v[x] + v[y])
        return best

    gs = [top2sum(r[g * epg:(g + 1) * epg]) for g in range(N_EXPERT_GROUPS)]
    gbest = gs[0]
    gidx = jnp.zeros((1, tr), I32)
    for g in range(1, N_EXPERT_GROUPS):
        better = gs[g] > gbest
        gidx = jnp.where(better, g, gidx)
        gbest = jnp.where(better, gs[g], gbest)
    v = list(r[:epg])
    sv = list(su[:epg])
    for g in range(1, N_EXPERT_GROUPS):
        pick = gidx == g
        for x in range(epg):
            v[x] = jnp.where(pick, r[g * epg + x], v[x])
            sv[x] = jnp.where(pick, su[g * epg + x], sv[x])
    i1 = jnp.zeros((1, tr), I32)
    b1 = v[0]
    w1 = sv[0]
    for x in range(1, epg):
        better = v[x] > b1
        i1 = jnp.where(better, x, i1)
        b1 = jnp.where(better, v[x], b1)
        w1 = jnp.where(better, sv[x], w1)
    i2 = jnp.zeros((1, tr), I32)
    b2 = jnp.full((1, tr), -jnp.inf, F32)
    w2 = jnp.zeros((1, tr), F32)
    for x in range(epg):
        take = (i1 != x) & (v[x] > b2)
        i2 = jnp.where(take, x, i2)
        b2 = jnp.where(take, v[x], b2)
        w2 = jnp.where(take, sv[x], w2)
    wsum = w1 + w2
    e1 = gidx * epg + i1
    e2 = gidx * epg + i2
    eio = lax.broadcasted_iota(I32, (N_EXPERTS, tr), 0)
    hit1 = eio == e1
    hit2 = eio == e2
    oh = jnp.where(hit1 | hit2, 1.0, 0.0)
    ri = lax.broadcasted_iota(I32, (tr, tr), 0)
    cj = lax.broadcasted_iota(I32, (tr, tr), 1)
    before = jnp.where(ri < cj, 1.0, 0.0).astype(BF16)
    rank = jnp.dot(oh.astype(BF16), before, preferred_element_type=F32) + carry[:, 0:1]
    rank1 = jnp.sum(jnp.where(hit1, rank, 0.0), axis=0, keepdims=True)
    rank2 = jnp.sum(jnp.where(hit2, rank, 0.0), axis=0, keepdims=True)
    idx_ref[0:1, :] = e1
    idx_ref[1:2, :] = e2
    idx_ref[2:3, :] = rank1.astype(I32)
    idx_ref[3:4, :] = rank2.astype(I32)
    idx_ref[4:8, :] = jnp.zeros((4, tr), I32)
    wts_ref[0:1, :] = w1 / wsum
    wts_ref[1:2, :] = w2 / wsum
    wts_ref[2:8, :] = jnp.zeros((6, tr), F32)
    carry[...] = carry[...] + jnp.sum(oh, axis=1, keepdims=True)
    cnt_ref[...] = carry[...]


def _two_set_specs(ha, hb, tile):
    d = ha.shape[1]
    na = ha.shape[0] // tile
    nb = hb.shape[0] // tile
    return na, nb, [pl.BlockSpec((tile, d), lambda i, *_: (jnp.minimum(i, na - 1), 0)),
                    pl.BlockSpec((tile, d), lambda i, *_: (jnp.maximum(i - na, 0), 0))]


def _router(ha, hb, w_router, b_router, tr):
    d = ha.shape[1]
    t = ha.shape[0] + hb.shape[0]
    na, nb, hspecs = _two_set_specs(ha, hb, tr)
    return pl.pallas_call(
        functools.partial(_router_kernel, na),
        grid=(na + nb,),
        in_specs=hspecs + [pl.BlockSpec((N_EXPERTS, d), lambda i: (0, 0)),
                           pl.BlockSpec((N_EXPERTS, 1), lambda i: (0, 0))],
        out_specs=[pl.BlockSpec((SUBLANES, tr), lambda i: (0, i)),
                   pl.BlockSpec((SUBLANES, tr), lambda i: (0, i)),
                   pl.BlockSpec((N_EXPERTS, LANES), lambda i: (0, 0))],
        out_shape=[jax.ShapeDtypeStruct((SUBLANES, t), I32),
                   jax.ShapeDtypeStruct((SUBLANES, t), F32),
                   jax.ShapeDtypeStruct((N_EXPERTS, LANES), F32)],
        scratch_shapes=[pltpu.VMEM((N_EXPERTS, LANES), F32)],
        compiler_params=_cparams(1, 32),
        name="moe_router",
    )(ha, hb, w_router.T, b_router.reshape(N_EXPERTS, 1))


def _scatter_kernel(n_tok, n_first, pos_ref, meta_ref, ha_ref, hb_ref, o_hbm, zero_scr, sem):
    i = pl.program_id(0)
    ts = ha_ref.shape[0]
    base = i * ts

    def row_copy(src, r, p):
        return pltpu.make_async_copy(src.at[pl.ds(r, 1)], o_hbm.at[pl.ds(p, 1)], sem)

    def scatter_tile(h_ref):
        def start_rows(r, c):
            row_copy(h_ref, r, pos_ref[base + r]).start(priority=0)
            row_copy(h_ref, r, pos_ref[n_tok + base + r]).start(priority=1)
            return c

        lax.fori_loop(0, ts, start_rows, 0, unroll=8)
        for _ in range(2):
            pltpu.make_async_copy(h_ref, o_hbm.at[pl.ds(0, ts)], sem).wait()

    @pl.when(i < n_first)
    def _():
        scatter_tile(ha_ref)

    @pl.when(i >= n_first)
    def _():
        scatter_tile(hb_ref)

    @pl.when(i == 0)
    def _():
        zero_scr[...] = jnp.zeros(zero_scr.shape, zero_scr.dtype)
        nz = zero_scr.shape[0]

        def block_copy(p):
            return pltpu.make_async_copy(zero_scr, o_hbm.at[pl.ds(pl.multiple_of(p, nz), nz)], sem)

        for e in range(N_EXPERTS + 1):
            lo = meta_ref[META_OFFSET, e] + meta_ref[META_COUNT, e]
            hi = meta_ref[META_OFFSET, e + 1]
            mid = jnp.minimum(((lo + (nz - 1)) // nz) * nz, hi)

            def start_row(p, c):
                row_copy(zero_scr, 0, p).start()
                return c

            def wait_row(p, c):
                row_copy(zero_scr, 0, 0).wait()
                return c

            def start_block(b, c):
                block_copy(mid + b * nz).start()
                return c

            def wait_block(b, c):
                block_copy(0).wait()
                return c

            lax.fori_loop(lo, mid, start_row, 0)
            lax.fori_loop(lo, mid, wait_row, 0)
            nblk = (hi - mid) // nz
            lax.fori_loop(0, nblk, start_block, 0)
            lax.fori_loop(0, nblk, wait_block, 0)


def _scatter_rows(ha, hb, pos, meta, n_rows, ts):
    d = ha.shape[1]
    na, nb, hspecs = _two_set_specs(ha, hb, ts)
    return pl.pallas_call(
        functools.partial(_scatter_kernel, ha.shape[0] + hb.shape[0], na),
        grid_spec=pltpu.PrefetchScalarGridSpec(
            num_scalar_prefetch=2,
            grid=(na + nb,),
            in_specs=hspecs,
            out_specs=pl.BlockSpec(memory_space=pl.ANY),
            scratch_shapes=[pltpu.VMEM((SUBLANES, d), ha.dtype), pltpu.SemaphoreType.DMA]),
        out_shape=jax.ShapeDtypeStruct((n_rows, d), ha.dtype),
        compiler_params=_cparams(1, 32),
        name="moe_scatter",
    )(pos, meta, ha, hb)


def _cast_rows(src, dst):
    def body(c, carry):
        r = pl.multiple_of(c * CAST_ROWS, CAST_ROWS)
        dst[pl.ds(r, CAST_ROWS), :] = src[pl.ds(r, CAST_ROWS), :].astype(BF16)
        return carry

    lax.fori_loop(0, src.shape[0] // CAST_ROWS, body, 0)


def _expert_kernel(layer, meta_ref, x_ref, w1_hbm, w2_hbm, y_ref, stg1, stg2, wb1, wb2, sem):
    i = pl.program_id(0)

    def fetch(e):
        return (pltpu.make_async_copy(w1_hbm.at[layer, e], stg1, sem.at[0]),
                pltpu.make_async_copy(w2_hbm.at[layer, e], stg2, sem.at[1]))

    @pl.when(i == 0)
    def _():
        for cp in fetch(meta_ref[META_TILE_EXPERT, 0]):
            cp.start()

    @pl.when(meta_ref[META_FIRST, i] == 1)
    def _():
        for cp in fetch(0):
            cp.wait()
        _cast_rows(stg1, wb1)
        _cast_rows(stg2, wb2)

        @pl.when(meta_ref[META_NEXT, i] >= 0)
        def _():
            for cp in fetch(meta_ref[META_NEXT, i]):
                cp.start(priority=1)

    @pl.when(i < meta_ref[META_N_USED, 0])
    def _():
        hid = jnp.dot(x_ref[...].astype(BF16), wb1[...], preferred_element_type=F32)
        f = hid.shape[1] // 2
        a = hid[:, :f]
        g = hid[:, f:]
        z = (g * jax.nn.sigmoid(g) * a).astype(BF16)
        y_ref[...] = jnp.dot(z, wb2[...], preferred_element_type=F32)

    @pl.when(i >= meta_ref[META_N_USED, 0])
    def _():
        y_ref[...] = jnp.zeros(y_ref.shape, y_ref.dtype)


def _experts(xs, w1, w2, layer, meta):
    p, d = xs.shape
    tm = MOE_TILE
    ff2 = w1.shape[3]
    ff = ff2 // 2
    row = lambda i, meta: (jnp.minimum(i, meta[META_N_USED, 0] - 1), 0)
    return pl.pallas_call(
        functools.partial(_expert_kernel, layer),
        grid_spec=pltpu.PrefetchScalarGridSpec(
            num_scalar_prefetch=1,
            grid=(p // tm,),
            in_specs=[pl.BlockSpec((tm, d), row),
                      pl.BlockSpec(memory_space=pl.ANY),
                      pl.BlockSpec(memory_space=pl.ANY)],
            out_specs=pl.BlockSpec((tm, d), lambda i, *_: (i, 0)),
            scratch_shapes=[pltpu.VMEM((d, ff2), F32), pltpu.VMEM((ff, d), F32),
                            pltpu.VMEM((d, ff2), BF16), pltpu.VMEM((ff, d), BF16),
                            pltpu.SemaphoreType.DMA((2,))]),
        out_shape=jax.ShapeDtypeStruct((p, d), F32),
        compiler_params=_cparams(1, 56),
        name="moe_experts",
    )(meta, xs, w1, w2)


def _combine_kernel(n_tok, tok_base, has_next, pos_ref, w_ref, y_hbm, xres_ref, gate_ref, lng_ref, lnb_ref,
                    *rest):
    if has_next:
        shn_ref, scn_ref, xo_ref, ho_ref, gbuf, fbuf, sem = rest
        nxt = (shn_ref, scn_ref, ho_ref)
    else:
        xo_ref, gbuf, fbuf, sem = rest
        nxt = None
    tc = xres_ref.shape[0]
    i = pl.program_id(0)
    n_steps = pl.num_programs(0)

    n_slots = gbuf.shape[0]
    ahead = n_slots - 1

    def row_copy(step, slot, r, k):
        p = pos_ref[k * n_tok + tok_base + step * tc + r]
        return pltpu.make_async_copy(y_hbm.at[pl.ds(p, 1)], gbuf.at[slot, k, pl.ds(r, 1)], sem.at[slot])

    def wait_tile(slot):
        for k in range(2):
            pltpu.make_async_copy(y_hbm.at[pl.ds(0, tc)], gbuf.at[slot, k], sem.at[slot]).wait()

    @pl.when(i == 0)
    def _():
        for a in range(ahead):
            def start_rows(r, c, a=a):
                for k in range(2):
                    row_copy(jnp.minimum(a, n_steps - 1), a, r, k).start()
                return c

            lax.fori_loop(0, tc, start_rows, 0, unroll=8)

    slot = i % n_slots
    wait_tile(slot)
    fbuf[0] = w_ref[:, 0:1] * gbuf[slot, 0] + w_ref[:, 1:2] * gbuf[slot, 1]
    _residual_ln(fbuf, xres_ref, gate_ref, lng_ref, lnb_ref, xo_ref, nxt)

    nxt_step = jnp.minimum(i + ahead, n_steps - 1)
    nxt_slot = (i + ahead) % n_slots
    for r in range(tc):
        for k in range(2):
            row_copy(nxt_step, nxt_slot, r, k).start(priority=k)

    @pl.when(i == n_steps - 1)
    def _():
        for a in range(1, ahead + 1):
            wait_tile((i + a) % n_slots)


def _combine_ln(ys, pos, wts, tok_base, xres, gate, lng, lnb, nxt_mods, mod_idx, tc):
    t, d = xres.shape
    r = gate.shape[1]
    has_next = nxt_mods is not None
    mspec = pl.BlockSpec((1, r, d), lambda i, *_: (mod_idx(i), 0, 0))
    vspec = pl.BlockSpec((1, d), lambda i, *_: (0, 0))
    ospec = pl.BlockSpec((tc, d), lambda i, *_: (i, 0))
    wbase = tok_base // tc
    in_specs = [pl.BlockSpec((tc, 2), lambda i, *_: (wbase + i, 0)),
                pl.BlockSpec(memory_space=pl.ANY), ospec, mspec, vspec, vspec]
    args = [wts, ys, xres, gate, lng, lnb]
    out_specs = [ospec]
    out_shape = [jax.ShapeDtypeStruct((t, d), F32)]
    if has_next:
        in_specs += [mspec, mspec]
        args += list(nxt_mods)
        out_specs.append(ospec)
        out_shape.append(jax.ShapeDtypeStruct((t, d), BF16))
    return pl.pallas_call(
        functools.partial(_combine_kernel, pos.shape[0] // 2, tok_base, has_next),
        grid_spec=pltpu.PrefetchScalarGridSpec(
            num_scalar_prefetch=1,
            grid=(t // tc,),
            in_specs=in_specs,
            out_specs=out_specs,
            scratch_shapes=[pltpu.VMEM((3, 2, tc, d), F32), pltpu.VMEM((1, tc, d), F32),
                            pltpu.SemaphoreType.DMA((3,))]),
        out_shape=out_shape,
        compiler_params=_cparams(1, 40),
        name="moe_combine",
    )(pos, *args)


def _plan_kernel(n_tiles, idx_ref, cnt_ref, pos_ref, meta_ref):
    tm = MOE_TILE
    t = idx_ref.shape[1]
    sub = lax.broadcasted_iota(I32, (N_EXPERTS, LANES), 0)
    lane = lax.broadcasted_iota(I32, (N_EXPERTS, LANES), 1)
    cnt = cnt_ref[...]
    padded = jnp.floor((cnt + (tm - 1)) * (1.0 / tm)) * tm
    ends = padded
    sh = 1
    while sh < N_EXPERTS:
        ends = ends + jnp.where(sub >= sh, pltpu.roll(ends, sh, 0), 0.0)
        sh *= 2
    off = ends - padded
    total = ends[N_EXPERTS - 1:N_EXPERTS, :]
    n_used = jnp.maximum(total * (1.0 / tm), 1.0)

    def as_row(col):
        return jnp.sum(jnp.where(sub == lane, col, 0.0), axis=0, keepdims=True)

    eio = lax.broadcasted_iota(I32, (N_EXPERTS, t), 0)
    off_col = off[:, 0:1]
    for k in range(2):
        start = jnp.sum(jnp.where(eio == idx_ref[k:k + 1, :], off_col, 0.0), axis=0, keepdims=True)
        pos_ref[k:k + 1, :] = start.astype(I32) + idx_ref[2 + k:3 + k, :]
    pos_ref[2:SUBLANES, :] = jnp.zeros((SUBLANES - 2, t), I32)

    tile = lax.broadcasted_iota(I32, (1, LANES), 1).astype(F32)
    tile_start = jnp.minimum(tile, n_used - 1.0) * tm
    te = jnp.minimum(jnp.sum(jnp.where(ends <= tile_start, 1.0, 0.0), axis=0, keepdims=True), N_EXPERTS - 1.0)
    prev = jnp.where(tile == 0.0, -1.0, pltpu.roll(te, 1, 1))
    first = jnp.where((tile < n_used) & (te != prev), 1.0, 0.0)
    present = as_row(jnp.where(cnt > 0.0, 1.0, 0.0))
    later = jnp.where((present > 0.0) & (lane > sub) & (lane < N_EXPERTS), lane.astype(F32), float(N_EXPERTS))
    nxt_e = jnp.min(later, axis=1, keepdims=True)
    nxt_e = jnp.where(nxt_e == float(N_EXPERTS), -1.0, nxt_e)
    nxt = jnp.sum(jnp.where(sub.astype(F32) == te, nxt_e, 0.0), axis=0, keepdims=True)
    lane_row = lax.broadcasted_iota(I32, (1, LANES), 1)
    off_row = as_row(off)
    off_row = jnp.where(lane_row == N_EXPERTS, total, off_row)
    off_row = jnp.where(lane_row == N_EXPERTS + 1, float(n_tiles * tm), off_row)
    rows = {META_TILE_EXPERT: te, META_FIRST: first, META_NEXT: nxt, META_N_USED: n_used,
            META_COUNT: as_row(cnt), META_OFFSET: off_row}
    for r in range(SUBLANES):
        meta_ref[r:r + 1, :] = rows[r].astype(I32) if r in rows else jnp.zeros((1, LANES), I32)


def _moe_plan(idx, cnt, n_tok):
    tm = MOE_TILE
    n_tiles = (2 * n_tok + N_EXPERTS * (tm - 1) + tm - 1) // tm
    assert n_tiles <= LANES
    pos, meta = pl.pallas_call(
        functools.partial(_plan_kernel, n_tiles),
        out_shape=[jax.ShapeDtypeStruct((SUBLANES, n_tok), I32), jax.ShapeDtypeStruct((SUBLANES, LANES), I32)],
        compiler_params=_cparams(0, 32),
        name="moe_plan",
    )(idx, cnt)
    return pos, meta, n_tiles * tm


class _Rows:
    def __init__(self, n_seq, seq_len, tile):
        self.n_seq, self.seq_len, self.tile = n_seq, seq_len, tile
        self.per_row = seq_len < tile

    def mods(self, m):
        if self.per_row:
            return jnp.repeat(m, self.seq_len, axis=0).reshape(-1, self.tile, m.shape[1])
        return m[:, None, :]

    def mod_idx(self, i):
        return i if self.per_row else (i * self.tile) // self.seq_len


def _split_mods(mod_l, n_prompt):
    d = mod_l.shape[1] // 6
    cols = [mod_l[:, k * d:(k + 1) * d] for k in range(6)]
    return [c[:n_prompt] for c in cols], [c[n_prompt:] for c in cols]


def kernel(x_prompt, x_sample, state_mlstm_C, state_mlstm_n, state_mlstm_m, c_prompt, c_sample, w_ada, b_ada, ln_g, ln_b, a_w_in, a_b_gates, a_norm_w, a_w_out, b_w_in, b_b_in, b_norm_g, b_norm_b, b_w_s, b_b_s, b_w_out, w_router, b_router, w_expert_in, w_expert_out):
    bp, sp, d = x_prompt.shape
    bs, ss, _ = x_sample.shape
    tp = bp * sp
    ts = bs * ss
    n_tok = tp + ts
    qk = N_HEADS * DK
    vd = N_HEADS * DV
    n_main = 2 * qk + 2 * vd

    n_seq = bp + bs
    pad = (-n_seq) % SUBLANES
    c_all = jnp.concatenate([c_prompt, c_sample, jnp.zeros((pad, d), F32)])
    mod = _adaln(c_all, w_ada, b_ada)[:, :n_seq]

    sets = {"p": _Rows(bp, sp, 512), "s": _Rows(bs, ss, 256)}
    base = {"p": 0, "s": tp}
    x = {"p": x_prompt.reshape(tp, d), "s": x_sample.reshape(ts, d)}
    hm = {}
    outs = {}
    for layer in range(DEPTH):
        j = layer // 2
        mp, ms = _split_mods(mod[layer], bp)
        md = {"p": mp, "s": ms}
        lng = ln_g[layer]
        lnb = ln_b[layer]
        lhs = {}
        if layer % 2 == 0:
            w_in = a_w_in[j]
            gb = jnp.zeros((1, 2 * LANES), F32)
            gb = gb.at[0, :N_HEADS].set(a_b_gates[j, :N_HEADS])
            gb = gb.at[0, LANES:LANES + N_HEADS].set(a_b_gates[j, N_HEADS:])
            nw = a_norm_w[j].reshape(1, vd)
            proj = {}
            gates = {}
            for k, rs in sets.items():
                proj[k], gates[k] = _inproj(x[k], rs.mods(md[k][0]), rs.mods(md[k][1]), rs.mod_idx,
                                            w_in, n_main, rs.tile, BF16 if k == "p" else F32)
            lhs["p"], c_p, n_p, m_p = _mlstm_chunks(
                proj["p"], gates["p"], gb, nw,
                jnp.zeros((bp, N_HEADS, DK, DV), F32), jnp.zeros((bp, N_HEADS, LANES), F32),
                jnp.zeros((bp, N_HEADS, LANES), F32), bp, sp)
            rpad = SUBLANES - ss
            proj_s = jnp.pad(proj["s"].reshape(bs, ss, n_main), ((0, 0), (0, rpad), (0, 0)))
            gates_s = jnp.pad(gates["s"].reshape(bs, ss, 2 * LANES), ((0, 0), (0, rpad), (0, 0)))
            m0 = jnp.pad(state_mlstm_m[j], ((0, 0), (0, LANES - N_HEADS)))[:, None, :]
            ypre_s, c_s, n_s, m_s = _mlstm_short(proj_s, gates_s, gb, nw, state_mlstm_C[j],
                                                 state_mlstm_n[j].reshape(bs, 1, qk), m0, ss)
            n_s = n_s.reshape(bs, N_HEADS, DK)
            lhs["s"] = ypre_s[:, :ss].reshape(ts, vd)
            outs["C_p"], outs["n_p"], outs["m_p"] = c_p, n_p, m_p[:, :, 0]
            outs["C_s"], outs["n_s"], outs["m_s"] = c_s, n_s, m_s[:, 0, :N_HEADS]
            w_out = a_w_out[j]
        else:
            ws = b_w_s[j]
            bsv = b_b_s[j]
            w_out = b_w_out[j]

            def mixing(l):
                tri = jnp.tril(jnp.ones((l, l), bool))
                wsl = jnp.where(tri, ws[:, :l, :l], 0.0)
                eye = jnp.eye(CHUNK // l, dtype=F32)
                mats = jax.vmap(lambda m: jnp.kron(eye, m))(wsl).astype(BF16)
                bias = jnp.tile(bsv[:, :l], (1, CHUNK // l))[:, :, None]
                return mats, bias

            uv = {k: _gmlp_in(hm[k], b_w_in[j], b_b_in[j], min(1024, rs.n_seq * rs.seq_len),
                              BF16 if k == "p" else F32) for k, rs in sets.items()}
        x1 = {}
        hf = {}
        for k, rs in sets.items():
            if layer % 2 == 0:
                x1[k], hf[k] = _proj_ln(lhs[k], w_out, x[k], rs.mods(md[k][2]), lng[0:1], lnb[0:1],
                                        rs.mods(md[k][3]), rs.mods(md[k][4]), rs.mod_idx, rs.tile, 512)
            else:
                rs = _Rows(rs.n_seq, rs.seq_len, rs.tile // 2)
                mats, bias = mixing(min(CHUNK, rs.seq_len))
                res = _mix_proj_ln(uv[k], b_norm_g[j], b_norm_b[j], mats, bias, w_out, x[k], rs.mods(md[k][2]),
                                   lng[0:1], lnb[0:1], rs.mods(md[k][3]), rs.mods(md[k][4]), rs.mod_idx,
                                   rs.tile, 512, k == "s")
                x1[k], hf[k] = res[0], res[1]
                if k == "s":
                    outs["v_s"] = res[2]

        idx, wts, cnt = _router(hf["p"], hf["s"], w_router, b_router, 512)
        pos, meta, n_rows = _moe_plan(idx, cnt, n_tok)
        pos = pos[:2].reshape(-1)
        xsorted = _scatter_rows(hf["p"], hf["s"], pos, meta, n_rows, 512)
        ysorted = _experts(xsorted, w_expert_in, w_expert_out, layer, meta)
        wts2 = wts[:2].T
        nxt = _split_mods(mod[layer + 1], bp) if layer + 1 < DEPTH else None
        for ki, (k, rs) in enumerate(sets.items()):
            rc = _Rows(rs.n_seq, rs.seq_len, 256)
            nxt_mods = None if nxt is None else (rc.mods(nxt[ki][0]), rc.mods(nxt[ki][1]))
            res = _combine_ln(ysorted, pos, wts2, base[k], x1[k], rc.mods(md[k][5]),
                              lng[1:2], lnb[1:2], nxt_mods, rc.mod_idx, 256)
            x[k] = res[0]
            if nxt is not None:
                hm[k] = res[1]

    return (x["p"].reshape(bp, sp, d), x["s"].reshape(bs, ss, d),
            outs["C_p"][None], outs["n_p"][None], outs["m_p"][None],
            outs["C_s"][None], outs["n_s"][None], outs["m_s"][None],
            outs["v_s"].reshape(bs, ss, -1)[None])
```

```python
import functools

import jax
import jax.numpy as jnp
from jax import lax
from jax.experimental import pallas as pl
from jax.experimental.pallas import tpu as pltpu

F32 = jnp.float32
BF16 = jnp.bfloat16
I32 = jnp.int32

DEPTH = 2
N_HEADS = 8
DK = 128
DV = 256
CHUNK = 128
N_GROUPS_B = 8
N_EXPERTS = 16
N_EXPERT_GROUPS = 4
EXPERTS_PER_GROUP = 4
ALPHA = float((2 * DEPTH) ** 0.25)
LN_EPS = 1e-5

LANES = 128
SUBLANES = 8
MIB = 1024 * 1024
MOE_TILE = 256
CAST_ROWS = 256
STAGE_ROWS = 128
META_TILE_EXPERT, META_FIRST, META_NEXT, META_N_USED, META_COUNT, META_OFFSET = range(6)


def _cparams(n_axes, vmem_mib):
    return pltpu.CompilerParams(
        dimension_semantics=("arbitrary",) * n_axes,
        vmem_limit_bytes=int(vmem_mib * MIB))


def _split3(x):
    hi = x.astype(BF16)
    r1 = x - hi.astype(F32)
    mid = r1.astype(BF16)
    lo = (r1 - mid.astype(F32)).astype(BF16)
    return hi, mid, lo


def _log_sigmoid(x):
    return jnp.minimum(x, 0.0) - jnp.log1p(jnp.exp(-jnp.abs(x)))


def _load_weight_bf16(w_hbm, wb, stg, sem):
    nj, k, tn = wb.shape
    ch = stg.shape[1]
    nch = k // ch

    def chunk(c):
        return pltpu.make_async_copy(w_hbm.at[pl.ds(c * ch, ch), pl.ds(0, nj * tn)], stg.at[c % 2],
                                     sem.at[c % 2])

    chunk(0).start()
    for c in range(nch):
        if c + 1 < nch:
            chunk(c + 1).start()
        chunk(c).wait()
        for jj in range(nj):
            wb[jj, c * ch:(c + 1) * ch, :] = stg[c % 2, :, jj * tn:(jj + 1) * tn].astype(BF16)


def _adaln_kernel(c_ref, w_ref, b_ref, o_ref):
    c = c_ref[...]
    a = (c * jax.nn.sigmoid(c)).astype(BF16)
    o_ref[0] = jnp.dot(a, w_ref[0].astype(BF16), preferred_element_type=F32) + b_ref[0]


def _adaln(c_all, w_ada, b_ada):
    depth, d, n = w_ada.shape
    r = c_all.shape[0]
    tn = 1024
    return pl.pallas_call(
        _adaln_kernel,
        grid=(depth, n // tn),
        in_specs=[pl.BlockSpec((r, d), lambda l, j: (0, 0)),
                  pl.BlockSpec((1, d, tn), lambda l, j: (l, 0, j)),
                  pl.BlockSpec((1, 1, tn), lambda l, j: (l, 0, j))],
        out_specs=pl.BlockSpec((1, r, tn), lambda l, j: (l, 0, j)),
        out_shape=jax.ShapeDtypeStruct((depth, r, n), F32),
        compiler_params=_cparams(2, 32),
        name="adaln",
    )(c_all, w_ada, b_ada.reshape(depth, 1, n))


def _inproj_kernel(x_ref, sh_ref, sc_ref, w_hbm, proj_ref, gates_ref, hin_ref, wb, stg, g16, wgs, sem):
    j = pl.program_id(1)

    @pl.when((pl.program_id(0) == 0) & (j == 0))
    def _():
        _load_weight_bf16(w_hbm, wb, stg, sem)
        n_main = wb.shape[0] * wb.shape[2]
        cp = pltpu.make_async_copy(w_hbm.at[:, pl.ds(n_main, 2 * N_HEADS)], g16, sem.at[0])
        cp.start()
        cp.wait()
        g = g16[...]
        z = jnp.zeros((g.shape[0], LANES - N_HEADS), F32)
        wg = jnp.concatenate([g[:, :N_HEADS], z, g[:, N_HEADS:], z], axis=1)
        hi = wg.astype(BF16)
        wgs[0] = hi
        wgs[1] = (wg - hi.astype(F32)).astype(BF16)

    @pl.when(j == 0)
    def _():
        h = x_ref[...] * (1.0 + sc_ref[0]) + sh_ref[0]
        hb = h.astype(BF16)
        hin_ref[...] = hb
        h_lo = (h - hb.astype(F32)).astype(BF16)
        gates_ref[...] = (jnp.dot(hb, wgs[0], preferred_element_type=F32)
                          + jnp.dot(h_lo, wgs[0], preferred_element_type=F32)
                          + jnp.dot(hb, wgs[1], preferred_element_type=F32))

    proj_ref[...] = jnp.dot(hin_ref[...], wb[j], preferred_element_type=F32).astype(proj_ref.dtype)


def _inproj(x, sh, sc, mod_idx, w_in, n_main, tm, out_dtype):
    t, d = x.shape
    tn = 1024
    r = sh.shape[1]
    mspec = pl.BlockSpec((1, r, d), lambda i, j: (mod_idx(i), 0, 0))
    return pl.pallas_call(
        _inproj_kernel,
        grid=(t // tm, n_main // tn),
        in_specs=[pl.BlockSpec((tm, d), lambda i, j: (i, 0)), mspec, mspec,
                  pl.BlockSpec(memory_space=pl.ANY)],
        out_specs=[pl.BlockSpec((tm, tn), lambda i, j: (i, j)),
                   pl.BlockSpec((tm, 2 * LANES), lambda i, j: (i, 0))],
        out_shape=[jax.ShapeDtypeStruct((t, n_main), out_dtype),
                   jax.ShapeDtypeStruct((t, 2 * LANES), F32)],
        scratch_shapes=[pltpu.VMEM((tm, d), BF16), pltpu.VMEM((n_main // tn, d, tn), BF16),
                        pltpu.VMEM((2, STAGE_ROWS, n_main), F32), pltpu.VMEM((d, 2 * N_HEADS), F32),
                        pltpu.VMEM((2, d, 2 * LANES), BF16), pltpu.SemaphoreType.DMA((2,))],
        compiler_params=_cparams(2, 56),
        name="mlstm_inproj",
    )(x, sh, sc, w_in)


def _head_norm_gate(hh, nw, o):
    mu = jnp.mean(hh, axis=1, keepdims=True)
    xc = hh - mu
    var = jnp.mean(xc * xc, axis=1, keepdims=True)
    return jax.nn.sigmoid(o) * (xc * lax.rsqrt(var + LN_EPS) * nw)


def _mlstm_chunk_kernel(q_ref, k_ref, v_ref, o_ref, g_ref, gb_ref, nw_ref, c0_ref, n0_ref, m0_ref,
                        y_ref, cout_ref, nout_ref, mout_ref, c_scr, n_scr, m_scr):
    ci = pl.program_id(1)
    L = q_ref.shape[0]

    @pl.when(ci == 0)
    def _():
        c_scr[...] = c0_ref[0]
        n_scr[...] = n0_ref[0]
        m_scr[...] = m0_ref[0]

    g = g_ref[...] + gb_ref[...]
    gi = g[:, :LANES]
    lf = _log_sigmoid(g[:, LANES:])
    row = lax.broadcasted_iota(I32, (L, L), 0)
    col = lax.broadcasted_iota(I32, (L, L), 1)
    causal = col <= row
    ltri = jnp.where(causal, 1.0, 0.0).astype(BF16)
    hi, mid, lo = _split3(lf)
    bcum = (jnp.dot(ltri, hi, preferred_element_type=F32)
            + jnp.dot(ltri, mid, preferred_element_type=F32)
            + jnp.dot(ltri, lo, preferred_element_type=F32))
    a = gi - bcum
    a_t = a.T
    scale = DK ** -0.5
    nt = (((1,), (1,)), ((), ()))
    tn_dims = (((0,), (0,)), ((), ()))
    for h in range(N_HEADS):
        ks = slice(h * DK, (h + 1) * DK)
        vs = slice(h * DV, (h + 1) * DV)
        qf = q_ref[:, ks].astype(F32) * scale
        qb = qf.astype(BF16)
        kf = k_ref[:, ks].astype(F32)
        kb = kf.astype(BF16)
        vb = v_ref[:, vs].astype(BF16)
        a_row = a_t[h:h + 1, :]
        a_col = a[:, h:h + 1]
        b_col = bcum[:, h:h + 1]
        m_prev = m_scr[h:h + 1, 0:1]
        amat = jnp.where(causal, a_row, -jnp.inf)
        mx = jnp.max(amat, axis=1, keepdims=True)
        m_inter = b_col + m_prev
        m_t = jnp.maximum(m_inter, b_col + mx)
        dm = jnp.exp(amat + (b_col - m_t))
        s = lax.dot_general(qb, kb, nt, preferred_element_type=F32)
        scores = s * dm
        inter = jnp.exp(m_inter - m_t)
        c_old = c_scr[h]
        n_old = n_scr[h:h + 1, :]
        qc = jnp.dot(qb, c_old.astype(BF16), preferred_element_type=F32)
        num = jnp.dot(scores.astype(BF16), vb, preferred_element_type=F32) + inter * qc
        qn = jnp.sum(qf * n_old, axis=1, keepdims=True)
        den = jnp.sum(scores, axis=1, keepdims=True) + inter * qn
        hh = num / jnp.maximum(jnp.abs(den), jnp.exp(-m_t))
        m_new = m_t[L - 1:L, :]
        b_last = b_col[L - 1:L, :]
        w_col = jnp.exp(b_last + a_col - m_new)
        decay = jnp.exp(b_last + m_prev - m_new)
        kw = kf * w_col
        c_scr[h] = decay * c_old + lax.dot_general(kw.astype(BF16), vb, tn_dims,
                                                   preferred_element_type=F32)
        n_scr[h:h + 1, :] = decay * n_old + jnp.sum(kw, axis=0, keepdims=True)
        m_scr[h:h + 1, :] = jnp.broadcast_to(m_new, (1, LANES))
        y_ref[:, vs] = _head_norm_gate(hh, nw_ref[:, vs], o_ref[:, vs].astype(F32)).astype(y_ref.dtype)

    @pl.when(ci == pl.num_programs(1) - 1)
    def _():
        cout_ref[0] = c_scr[...]
        nout_ref[0] = n_scr[...]
        mout_ref[0] = m_scr[...]


def _mlstm_chunks(proj, gates, gb, nw, c0, n0, m0, batch, seq):
    nc = seq // CHUNK
    qk = N_HEADS * DK
    vd = N_HEADS * DV
    row = lambda b, c: b * nc + c
    st4 = pl.BlockSpec((1, N_HEADS, DK, DV), lambda b, c: (b, 0, 0, 0))
    st3 = pl.BlockSpec((1, N_HEADS, LANES), lambda b, c: (b, 0, 0))
    return pl.pallas_call(
        _mlstm_chunk_kernel,
        grid=(batch, nc),
        in_specs=[pl.BlockSpec((CHUNK, qk), lambda b, c: (row(b, c), 0)),
                  pl.BlockSpec((CHUNK, qk), lambda b, c: (row(b, c), 1)),
                  pl.BlockSpec((CHUNK, vd), lambda b, c: (row(b, c), 1)),
                  pl.BlockSpec((CHUNK, vd), lambda b, c: (row(b, c), 2)),
                  pl.BlockSpec((CHUNK, 2 * LANES), lambda b, c: (row(b, c), 0)),
                  pl.BlockSpec((1, 2 * LANES), lambda b, c: (0, 0)),
                  pl.BlockSpec((1, vd), lambda b, c: (0, 0)),
                  st4, st3, st3],
        out_specs=[pl.BlockSpec((CHUNK, vd), lambda b, c: (row(b, c), 0)), st4, st3, st3],
        out_shape=[jax.ShapeDtypeStruct((batch * seq, vd), BF16),
                   jax.ShapeDtypeStruct((batch, N_HEADS, DK, DV), F32),
                   jax.ShapeDtypeStruct((batch, N_HEADS, LANES), F32),
                   jax.ShapeDtypeStruct((batch, N_HEADS, LANES), F32)],
        scratch_shapes=[pltpu.VMEM((N_HEADS, DK, DV), F32),
                        pltpu.VMEM((N_HEADS, LANES), F32),
                        pltpu.VMEM((N_HEADS, LANES), F32)],
        compiler_params=_cparams(2, 40),
        name="mlstm_chunks",
    )(proj, proj, proj, proj, gates, gb, nw, c0, n0, m0)


def _per_head(x, width):
    return jnp.concatenate(
        [jnp.broadcast_to(x[..., h:h + 1], x.shape[:-1] + (width,)) for h in range(N_HEADS)], axis=-1)


def _head_sums(x, width):
    lane = lax.broadcasted_iota(I32, x.shape[:-1] + (LANES,), x.ndim - 1)
    out = jnp.zeros(x.shape[:-1] + (LANES,), F32)
    for h in range(N_HEADS):
        s = jnp.sum(x[..., h * width:(h + 1) * width], axis=-1, keepdims=True)
        out = jnp.where(lane == h, s, out)
    return out


def _mlstm_short_kernel(seq, q_ref, k_ref, v_ref, o_ref, g_ref, gb_ref, nw_ref, c0_ref, n0_ref, m0_ref,
                        y_ref, cout_ref, nout_ref, mout_ref, qc_scr, kw_scr):
    bt, rows, _ = q_ref.shape
    scale = DK ** -0.5
    tn_dims = (((0,), (0,)), ((), ()))
    row = lax.broadcasted_iota(I32, (bt, rows, LANES), 1)
    g = g_ref[...] + gb_ref[...]
    gi = g[:, :, :LANES]
    lf = _log_sigmoid(g[:, :, LANES:])
    bcum = jnp.zeros_like(lf)
    for s in range(seq):
        bcum = bcum + jnp.where(row >= s, lf[:, s:s + 1, :], 0.0)
    a = gi - bcum
    mx = jnp.full_like(a, -jnp.inf)
    for s in range(seq):
        mx = jnp.maximum(mx, jnp.where(row >= s, a[:, s:s + 1, :], -jnp.inf))
    m_prev = m0_ref[...]
    m_inter = bcum + m_prev
    m_t = jnp.maximum(m_inter, bcum + mx)
    cmt = bcum - m_t
    inter = jnp.exp(m_inter - m_t)
    einv = jnp.exp(-m_t)
    m_new = m_t[:, seq - 1:seq, :]
    b_last = bcum[:, seq - 1:seq, :]
    w = jnp.where(row < seq, jnp.exp(b_last + a - m_new), 0.0)
    decay = jnp.exp(b_last + m_prev - m_new)
    mout_ref[...] = m_new

    q = q_ref[...] * scale
    k = k_ref[...]
    v = v_ref[...]
    n_old = n0_ref[...]
    for b in range(bt):
        for h in range(N_HEADS):
            qc_scr[b, :, h * DV:(h + 1) * DV] = jnp.dot(
                (q_ref[b, :, h * DK:(h + 1) * DK] * scale).astype(BF16), c0_ref[b, h].astype(BF16),
                preferred_element_type=F32)
    den = inter * _head_sums(q * n_old, DK)
    num = _per_head(inter, DV) * qc_scr[...]
    for s in range(seq):
        p = _head_sums(q * k[:, s:s + 1, :], DK) * jnp.where(row >= s, jnp.exp(cmt + a[:, s:s + 1, :]), 0.0)
        den = den + p
        num = num + _per_head(p, DV) * v[:, s:s + 1, :]
    hh = num * _per_head(1.0 / jnp.maximum(jnp.abs(den), einv), DV)
    mu = _head_sums(hh, DV) * (1.0 / DV)
    xc = hh - _per_head(mu, DV)
    var = _head_sums(xc * xc, DV) * (1.0 / DV)
    hn = xc * _per_head(lax.rsqrt(var + LN_EPS), DV) * nw_ref[...]
    y_ref[...] = jax.nn.sigmoid(o_ref[...]) * hn

    kw = k * _per_head(w, DK)
    kw_scr[...] = kw
    nout_ref[...] = _per_head(decay, DK) * n_old + jnp.sum(kw, axis=1, keepdims=True)
    for b in range(bt):
        for h in range(N_HEADS):
            cout_ref[b, h] = (decay[b, :, h:h + 1] * c0_ref[b, h]
                              + lax.dot_general(kw_scr[b, :, h * DK:(h + 1) * DK],
                                                v_ref[b, :, h * DV:(h + 1) * DV],
                                                tn_dims, preferred_element_type=F32))


def _mlstm_short(proj, gates, gb, nw, c0, n0, m0, seq):
    batch, rows, _ = proj.shape
    qk = N_HEADS * DK
    vd = N_HEADS * DV
    bt = 8
    st4 = pl.BlockSpec((bt, N_HEADS, DK, DV), lambda i: (i, 0, 0, 0))
    st3 = pl.BlockSpec((bt, 1, qk), lambda i: (i, 0, 0))
    stm = pl.BlockSpec((bt, 1, LANES), lambda i: (i, 0, 0))
    return pl.pallas_call(
        functools.partial(_mlstm_short_kernel, seq),
        grid=(batch // bt,),
        in_specs=[pl.BlockSpec((bt, rows, qk), lambda i: (i, 0, 0)),
                  pl.BlockSpec((bt, rows, qk), lambda i: (i, 0, 1)),
                  pl.BlockSpec((bt, rows, vd), lambda i: (i, 0, 1)),
                  pl.BlockSpec((bt, rows, vd), lambda i: (i, 0, 2)),
                  pl.BlockSpec((bt, rows, 2 * LANES), lambda i: (i, 0, 0)),
                  pl.BlockSpec((1, 2 * LANES), lambda i: (0, 0)),
                  pl.BlockSpec((1, vd), lambda i: (0, 0)),
                  st4, st3, stm],
        out_specs=[pl.BlockSpec((bt, rows, vd), lambda i: (i, 0, 0)), st4, st3, stm],
        out_shape=[jax.ShapeDtypeStruct((batch, rows, vd), F32),
                   jax.ShapeDtypeStruct((batch, N_HEADS, DK, DV), F32),
                   jax.ShapeDtypeStruct((batch, 1, qk), F32),
                   jax.ShapeDtypeStruct((batch, 1, LANES), F32)],
        scratch_shapes=[pltpu.VMEM((bt, rows, vd), F32), pltpu.VMEM((bt, rows, qk), F32)],
        compiler_params=_cparams(1, 48),
        name="mlstm_short",
    )(proj, proj, proj, proj, gates, gb, nw, c0, n0, m0)


def _residual_ln(zbuf, xres_ref, gate_ref, lng_ref, lnb_ref, xo_ref, nxt):
    nc, tm, tn = zbuf.shape
    inv_d = 1.0 / (nc * tn)
    ssum = jnp.zeros((tm, 1), F32)
    for c in range(nc):
        sl = pl.ds(c * tn, tn)
        z = ALPHA * xres_ref[:, sl] + gate_ref[0, :, sl] * zbuf[c]
        zbuf[c] = z
        ssum = ssum + jnp.sum(z, axis=1, keepdims=True)
    mu = ssum * inv_d
    vsum = jnp.zeros((tm, 1), F32)
    for c in range(nc):
        zc = zbuf[c] - mu
        vsum = vsum + jnp.sum(zc * zc, axis=1, keepdims=True)
    rstd = lax.rsqrt(vsum * inv_d + LN_EPS)
    for c in range(nc):
        sl = pl.ds(c * tn, tn)
        xn = (zbuf[c] - mu) * rstd * lng_ref[:, sl] + lnb_ref[:, sl]
        xo_ref[:, sl] = xn
        if nxt is not None:
            sh_ref, sc_ref, ho_ref = nxt
            ho_ref[:, sl] = (xn * (1.0 + sc_ref[0, :, sl]) + sh_ref[0, :, sl]).astype(ho_ref.dtype)


def _proj_ln_kernel(lhs_ref, w_hbm, xres_ref, gate_ref, lng_ref, lnb_ref, shn_ref, scn_ref,
                    xo_ref, ho_ref, wb, stg, ybuf, sem):
    @pl.when(pl.program_id(0) == 0)
    def _():
        _load_weight_bf16(w_hbm, wb, stg, sem)

    lhs = lhs_ref[...].astype(BF16)
    for c in range(ybuf.shape[0]):
        ybuf[c] = jnp.dot(lhs, wb[c], preferred_element_type=F32)
    _residual_ln(ybuf, xres_ref, gate_ref, lng_ref, lnb_ref, xo_ref, (shn_ref, scn_ref, ho_ref))


def _proj_ln(lhs, w, xres, gate, lng, lnb, shn, scn, mod_idx, tm, tn):
    t, k = lhs.shape
    d = w.shape[1]
    r = gate.shape[1]
    mspec = pl.BlockSpec((1, r, d), lambda i: (mod_idx(i), 0, 0))
    vspec = pl.BlockSpec((1, d), lambda i: (0, 0))
    ospec = pl.BlockSpec((tm, d), lambda i: (i, 0))
    return pl.pallas_call(
        _proj_ln_kernel,
        grid=(t // tm,),
        in_specs=[pl.BlockSpec((tm, k), lambda i: (i, 0)),
                  pl.BlockSpec(memory_space=pl.ANY),
                  ospec, mspec, vspec, vspec, mspec, mspec],
        out_specs=[ospec, ospec],
        out_shape=[jax.ShapeDtypeStruct((t, d), F32), jax.ShapeDtypeStruct((t, d), F32)],
        scratch_shapes=[pltpu.VMEM((d // tn, k, tn), BF16), pltpu.VMEM((2, CAST_ROWS, d), F32),
                        pltpu.VMEM((d // tn, tm, tn), F32), pltpu.SemaphoreType.DMA((2,))],
        compiler_params=_cparams(1, 58),
        name="proj_ln",
    )(lhs, w, xres, gate, lng, lnb, shn, scn)


def _gelu_tanh(x):
    return x * (0.5 * (1.0 + jnp.tanh(0.7978845608028654 * (x + 0.044715 * (x * x * x)))))


def _gmlp_in_kernel(h_ref, w_hbm, b_ref, o_ref, wb, stg, sem):
    j = pl.program_id(1)

    @pl.when((pl.program_id(0) == 0) & (j == 0))
    def _():
        _load_weight_bf16(w_hbm, wb, stg, sem)

    acc = jnp.dot(h_ref[...].astype(BF16), wb[j], preferred_element_type=F32)
    o_ref[...] = _gelu_tanh(acc + b_ref[...]).astype(o_ref.dtype)


def _gmlp_in(h, w, b, tm, out_dtype):
    t, d = h.shape
    n = w.shape[1]
    tn = 1024
    return pl.pallas_call(
        _gmlp_in_kernel,
        grid=(t // tm, n // tn),
        in_specs=[pl.BlockSpec((tm, d), lambda i, j: (i, 0)),
                  pl.BlockSpec(memory_space=pl.ANY),
                  pl.BlockSpec((1, tn), lambda i, j: (0, j))],
        out_specs=pl.BlockSpec((tm, tn), lambda i, j: (i, j)),
        out_shape=jax.ShapeDtypeStruct((t, n), out_dtype),
        scratch_shapes=[pltpu.VMEM((n // tn, d, tn), BF16), pltpu.VMEM((2, STAGE_ROWS, n), F32),
                        pltpu.SemaphoreType.DMA((2,))],
        compiler_params=_cparams(2, 58),
        name="gmlp_in",
    )(h, w, b.reshape(1, n))


def _gmlp_mix_rows(u_ref, v_ref, g_ref, b_ref, mix_ref, bias_ref, o_ref, vn_ref):
    v = v_ref[...].astype(F32)
    mu = jnp.mean(v, axis=1, keepdims=True)
    xc = v - mu
    var = jnp.mean(xc * xc, axis=1, keepdims=True)
    vn = xc * lax.rsqrt(var + LN_EPS) * g_ref[...] + b_ref[...]
    if vn_ref is not None:
        vn_ref[...] = vn
    gd = v.shape[1] // N_GROUPS_B
    mb = mix_ref.shape[1]
    for g in range(N_GROUPS_B):
        sl = slice(g * gd, (g + 1) * gd)
        for blk in range(v.shape[0] // mb):
            rows = slice(blk * mb, (blk + 1) * mb)
            mixed = jnp.dot(mix_ref[g], vn[rows, sl].astype(BF16), preferred_element_type=F32) + bias_ref[g]
            o_ref[rows, sl] = (u_ref[rows, sl].astype(F32) * mixed).astype(o_ref.dtype)


def _mix_proj_ln_kernel(emit_v, u_ref, v_ref, g_ref, b_ref, mix_ref, bias_ref, w_hbm, xres_ref, gate_ref,
                        lng_ref, lnb_ref, shn_ref, scn_ref, *rest):
    if emit_v:
        xo_ref, ho_ref, vn_ref, lhs_scr, wb, stg, ybuf, sem = rest
    else:
        xo_ref, ho_ref, lhs_scr, wb, stg, ybuf, sem = rest
        vn_ref = None

    @pl.when(pl.program_id(0) == 0)
    def _():
        _load_weight_bf16(w_hbm, wb, stg, sem)

    _gmlp_mix_rows(u_ref, v_ref, g_ref, b_ref, mix_ref, bias_ref, lhs_scr, vn_ref)
    lhs = lhs_scr[...]
    for c in range(ybuf.shape[0]):
        ybuf[c] = jnp.dot(lhs, wb[c], preferred_element_type=F32)
    _residual_ln(ybuf, xres_ref, gate_ref, lng_ref, lnb_ref, xo_ref, (shn_ref, scn_ref, ho_ref))


def _mix_proj_ln(uv, nv_g, nv_b, mix, bias, w, xres, gate, lng, lnb, shn, scn, mod_idx, tm, tn, emit_v):
    t, n2 = uv.shape
    di = n2 // 2
    d = w.shape[1]
    r = gate.shape[1]
    mspec = pl.BlockSpec((1, r, d), lambda i: (mod_idx(i), 0, 0))
    vspec = pl.BlockSpec((1, d), lambda i: (0, 0))
    ospec = pl.BlockSpec((tm, d), lambda i: (i, 0))
    nspec = pl.BlockSpec((1, di), lambda i: (0, 0))
    out_specs = [ospec, ospec]
    out_shape = [jax.ShapeDtypeStruct((t, d), F32), jax.ShapeDtypeStruct((t, d), F32)]
    if emit_v:
        out_specs.append(pl.BlockSpec((tm, di), lambda i: (i, 0)))
        out_shape.append(jax.ShapeDtypeStruct((t, di), F32))
    return pl.pallas_call(
        functools.partial(_mix_proj_ln_kernel, emit_v),
        grid=(t // tm,),
        in_specs=[pl.BlockSpec((tm, di), lambda i: (i, 0)),
                  pl.BlockSpec((tm, di), lambda i: (i, 1)),
                  nspec, nspec,
                  pl.BlockSpec(mix.shape, lambda i: (0, 0, 0)),
                  pl.BlockSpec(bias.shape, lambda i: (0, 0, 0)),
                  pl.BlockSpec(memory_space=pl.ANY),
                  ospec, mspec, vspec, vspec, mspec, mspec],
        out_specs=out_specs,
        out_shape=out_shape,
        scratch_shapes=[pltpu.VMEM((tm, di), BF16), pltpu.VMEM((d // tn, di, tn), BF16),
                        pltpu.VMEM((2, CAST_ROWS, d), F32), pltpu.VMEM((d // tn, tm, tn), F32),
                        pltpu.SemaphoreType.DMA((2,))],
        compiler_params=_cparams(1, 58),
        name="gmlp_mix_proj_ln",
    )(uv, uv, nv_g.reshape(1, di), nv_b.reshape(1, di), mix, bias, w, xres, gate, lng, lnb, shn, scn)


def _router_kernel(n_first, ha_ref, hb_ref, wr_ref, br_ref, idx_ref, wts_ref, cnt_ref, carry):
    i = pl.program_id(0)
    tr = ha_ref.shape[0]

    @pl.when(i == 0)
    def _():
        carry[...] = jnp.zeros(carry.shape, carry.dtype)

    nt = (((1,), (1,)), ((), ()))
    h = jnp.where(i < n_first, ha_ref[...], hb_ref[...])
    hb = h.astype(BF16)
    hl = (h - hb.astype(F32)).astype(BF16)
    wr = wr_ref[...]
    wb = wr.astype(BF16)
    wl = (wr - wb.astype(F32)).astype(BF16)
    logits = (lax.dot_general(wb, hb, nt, preferred_element_type=F32)
              + lax.dot_general(wb, hl, nt, preferred_element_type=F32)
              + lax.dot_general(wl, hb, nt, preferred_element_type=F32))
    s = jax.nn.sigmoid(logits)
    sel = s + br_ref[...]
    epg = EXPERTS_PER_GROUP
    r = [sel[e:e + 1, :] for e in range(N_EXPERTS)]
    su = [s[e:e + 1, :] for e in range(N_EXPERTS)]

    def top2sum(v):
        best = v[0] + v[1]
        for x in range(epg):
            for y in range(x + 1, epg):
                if (x, y) != (0, 1):
                    best = jnp.maximum(best, v[x] + v[y])
        return best

    gs = [top2sum(r[g * epg:(g + 1) * epg]) for g in range(N_EXPERT_GROUPS)]
    gbest = gs[0]
    gidx = jnp.zeros((1, tr), I32)
    for g in range(1, N_EXPERT_GROUPS):
        better = gs[g] > gbest
        gidx = jnp.where(better, g, gidx)
        gbest = jnp.where(better, gs[g], gbest)
    v = list(r[:epg])
    sv = list(su[:epg])
    for g in range(1, N_EXPERT_GROUPS):
        pick = gidx == g
        for x in range(epg):
            v[x] = jnp.where(pick, r[g * epg + x], v[x])
            sv[x] = jnp.where(pick, su[g * epg + x], sv[x])
    i1 = jnp.zeros((1, tr), I32)
    b1 = v[0]
    w1 = sv[0]
    for x in range(1, epg):
        better = v[x] > b1
        i1 = jnp.where(better, x, i1)
        b1 = jnp.where(better, v[x], b1)
        w1 = jnp.where(better, sv[x], w1)
    i2 = jnp.zeros((1, tr), I32)
    b2 = jnp.full((1, tr), -jnp.inf, F32)
    w2 = jnp.zeros((1, tr), F32)
    for x in range(epg):
        take = (i1 != x) & (v[x] > b2)
        i2 = jnp.where(take, x, i2)
        b2 = jnp.where(take, v[x], b2)
        w2 = jnp.where(take, sv[x], w2)
    wsum = w1 + w2
    e1 = gidx * epg + i1
    e2 = gidx * epg + i2
    eio = lax.broadcasted_iota(I32, (N_EXPERTS, tr), 0)
    hit1 = eio == e1
    hit2 = eio == e2
    oh = jnp.where(hit1 | hit2, 1.0, 0.0)
    ri = lax.broadcasted_iota(I32, (tr, tr), 0)
    cj = lax.broadcasted_iota(I32, (tr, tr), 1)
    before = jnp.where(ri < cj, 1.0, 0.0).astype(BF16)
    rank = jnp.dot(oh.astype(BF16), before, preferred_element_type=F32) + carry[:, 0:1]
    rank1 = jnp.sum(jnp.where(hit1, rank, 0.0), axis=0, keepdims=True)
    rank2 = jnp.sum(jnp.where(hit2, rank, 0.0), axis=0, keepdims=True)
    idx_ref[0:1, :] = e1
    idx_ref[1:2, :] = e2
    idx_ref[2:3, :] = rank1.astype(I32)
    idx_ref[3:4, :] = rank2.astype(I32)
    idx_ref[4:8, :] = jnp.zeros((4, tr), I32)
    wts_ref[0:1, :] = w1 / wsum
    wts_ref[1:2, :] = w2 / wsum
    wts_ref[2:8, :] = jnp.zeros((6, tr), F32)
    carry[...] = carry[...] + jnp.sum(oh, axis=1, keepdims=True)
    cnt_ref[...] = carry[...]


def _two_set_specs(ha, hb, tile):
    d = ha.shape[1]
    na = ha.shape[0] // tile
    nb = hb.shape[0] // tile
    return na, nb, [pl.BlockSpec((tile, d), lambda i, *_: (jnp.minimum(i, na - 1), 0)),
                    pl.BlockSpec((tile, d), lambda i, *_: (jnp.maximum(i - na, 0), 0))]


def _router(ha, hb, w_router, b_router, tr):
    d = ha.shape[1]
    t = ha.shape[0] + hb.shape[0]
    na, nb, hspecs = _two_set_specs(ha, hb, tr)
    return pl.pallas_call(
        functools.partial(_router_kernel, na),
        grid=(na + nb,),
        in_specs=hspecs + [pl.BlockSpec((N_EXPERTS, d), lambda i: (0, 0)),
                           pl.BlockSpec((N_EXPERTS, 1), lambda i: (0, 0))],
        out_specs=[pl.BlockSpec((SUBLANES, tr), lambda i: (0, i)),
                   pl.BlockSpec((SUBLANES, tr), lambda i: (0, i)),
                   pl.BlockSpec((N_EXPERTS, LANES), lambda i: (0, 0))],
        out_shape=[jax.ShapeDtypeStruct((SUBLANES, t), I32),
                   jax.ShapeDtypeStruct((SUBLANES, t), F32),
                   jax.ShapeDtypeStruct((N_EXPERTS, LANES), F32)],
        scratch_shapes=[pltpu.VMEM((N_EXPERTS, LANES), F32)],
        compiler_params=_cparams(1, 32),
        name="moe_router",
    )(ha, hb, w_router.T, b_router.reshape(N_EXPERTS, 1))


def _scatter_kernel(n_tok, n_first, pos_ref, meta_ref, ha_ref, hb_ref, o_hbm, zero_scr, sem):
    i = pl.program_id(0)
    ts = ha_ref.shape[0]
    base = i * ts

    def row_copy(src, r, p):
        return pltpu.make_async_copy(src.at[pl.ds(r, 1)], o_hbm.at[pl.ds(p, 1)], sem)

    def scatter_tile(h_ref):
        def start_rows(r, c):
            row_copy(h_ref, r, pos_ref[base + r]).start(priority=0)
            row_copy(h_ref, r, pos_ref[n_tok + base + r]).start(priority=1)
            return c

        lax.fori_loop(0, ts, start_rows, 0, unroll=8)
        for _ in range(2):
            pltpu.make_async_copy(h_ref, o_hbm.at[pl.ds(0, ts)], sem).wait()

    @pl.when(i < n_first)
    def _():
        scatter_tile(ha_ref)

    @pl.when(i >= n_first)
    def _():
        scatter_tile(hb_ref)

    @pl.when(i == 0)
    def _():
        zero_scr[...] = jnp.zeros(zero_scr.shape, zero_scr.dtype)
        nz = zero_scr.shape[0]

        def block_copy(p):
            return pltpu.make_async_copy(zero_scr, o_hbm.at[pl.ds(pl.multiple_of(p, nz), nz)], sem)

        for e in range(N_EXPERTS + 1):
            lo = meta_ref[META_OFFSET, e] + meta_ref[META_COUNT, e]
            hi = meta_ref[META_OFFSET, e + 1]
            mid = jnp.minimum(((lo + (nz - 1)) // nz) * nz, hi)

            def start_row(p, c):
                row_copy(zero_scr, 0, p).start()
                return c

            def wait_row(p, c):
                row_copy(zero_scr, 0, 0).wait()
                return c

            def start_block(b, c):
                block_copy(mid + b * nz).start()
                return c

            def wait_block(b, c):
                block_copy(0).wait()
                return c

            lax.fori_loop(lo, mid, start_row, 0)
            lax.fori_loop(lo, mid, wait_row, 0)
            nblk = (hi - mid) // nz
            lax.fori_loop(0, nblk, start_block, 0)
            lax.fori_loop(0, nblk, wait_block, 0)


def _scatter_rows(ha, hb, pos, meta, n_rows, ts):
    d = ha.shape[1]
    na, nb, hspecs = _two_set_specs(ha, hb, ts)
    return pl.pallas_call(
        functools.partial(_scatter_kernel, ha.shape[0] + hb.shape[0], na),
        grid_spec=pltpu.PrefetchScalarGridSpec(
            num_scalar_prefetch=2,
            grid=(na + nb,),
            in_specs=hspecs,
            out_specs=pl.BlockSpec(memory_space=pl.ANY),
            scratch_shapes=[pltpu.VMEM((SUBLANES, d), ha.dtype), pltpu.SemaphoreType.DMA]),
        out_shape=jax.ShapeDtypeStruct((n_rows, d), ha.dtype),
        compiler_params=_cparams(1, 32),
        name="moe_scatter",
    )(pos, meta, ha, hb)


def _cast_rows(src, dst):
    def body(c, carry):
        r = pl.multiple_of(c * CAST_ROWS, CAST_ROWS)
        dst[pl.ds(r, CAST_ROWS), :] = src[pl.ds(r, CAST_ROWS), :].astype(BF16)
        return carry

    lax.fori_loop(0, src.shape[0] // CAST_ROWS, body, 0)


def _expert_kernel(layer, meta_ref, x_ref, w1_hbm, w2_hbm, y_ref, stg1, stg2, wb1, wb2, sem):
    i = pl.program_id(0)

    def fetch(e):
        return (pltpu.make_async_copy(w1_hbm.at[layer, e], stg1, sem.at[0]),
                pltpu.make_async_copy(w2_hbm.at[layer, e], stg2, sem.at[1]))

    @pl.when(i == 0)
    def _():
        for cp in fetch(meta_ref[META_TILE_EXPERT, 0]):
            cp.start()

    @pl.when(meta_ref[META_FIRST, i] == 1)
    def _():
        for cp in fetch(0):
            cp.wait()
        _cast_rows(stg1, wb1)
        _cast_rows(stg2, wb2)

        @pl.when(meta_ref[META_NEXT, i] >= 0)
        def _():
            for cp in fetch(meta_ref[META_NEXT, i]):
                cp.start(priority=1)

    @pl.when(i < meta_ref[META_N_USED, 0])
    def _():
        hid = jnp.dot(x_ref[...].astype(BF16), wb1[...], preferred_element_type=F32)
        f = hid.shape[1] // 2
        a = hid[:, :f]
        g = hid[:, f:]
        z = (g * jax.nn.sigmoid(g) * a).astype(BF16)
        y_ref[...] = jnp.dot(z, wb2[...], preferred_element_type=F32)

    @pl.when(i >= meta_ref[META_N_USED, 0])
    def _():
        y_ref[...] = jnp.zeros(y_ref.shape, y_ref.dtype)


def _experts(xs, w1, w2, layer, meta):
    p, d = xs.shape
    tm = MOE_TILE
    ff2 = w1.shape[3]
    ff = ff2 // 2
    row = lambda i, meta: (jnp.minimum(i, meta[META_N_USED, 0] - 1), 0)
    return pl.pallas_call(
        functools.partial(_expert_kernel, layer),
        grid_spec=pltpu.PrefetchScalarGridSpec(
            num_scalar_prefetch=1,
            grid=(p // tm,),
            in_specs=[pl.BlockSpec((tm, d), row),
                      pl.BlockSpec(memory_space=pl.ANY),
                      pl.BlockSpec(memory_space=pl.ANY)],
            out_specs=pl.BlockSpec((tm, d), lambda i, *_: (i, 0)),
            scratch_shapes=[pltpu.VMEM((d, ff2), F32), pltpu.VMEM((ff, d), F32),
                            pltpu.VMEM((d, ff2), BF16), pltpu.VMEM((ff, d), BF16),
                            pltpu.SemaphoreType.DMA((2,))]),
        out_shape=jax.ShapeDtypeStruct((p, d), F32),
        compiler_params=_cparams(1, 56),
        name="moe_experts",
    )(meta, xs, w1, w2)


def _combine_kernel(n_tok, tok_base, has_next, pos_ref, w_ref, y_hbm, xres_ref, gate_ref, lng_ref, lnb_ref,
                    *rest):
    if has_next:
        shn_ref, scn_ref, xo_ref, ho_ref, gbuf, fbuf, sem = rest
        nxt = (shn_ref, scn_ref, ho_ref)
    else:
        xo_ref, gbuf, fbuf, sem = rest
        nxt = None
    tc = xres_ref.shape[0]
    i = pl.program_id(0)
    n_steps = pl.num_programs(0)

    n_slots = gbuf.shape[0]
    ahead = n_slots - 1

    def row_copy(step, slot, r, k):
        p = pos_ref[k * n_tok + tok_base + step * tc + r]
        return pltpu.make_async_copy(y_hbm.at[pl.ds(p, 1)], gbuf.at[slot, k, pl.ds(r, 1)], sem.at[slot])

    def wait_tile(slot):
        for k in range(2):
            pltpu.make_async_copy(y_hbm.at[pl.ds(0, tc)], gbuf.at[slot, k], sem.at[slot]).wait()

    @pl.when(i == 0)
    def _():
        for a in range(ahead):
            def start_rows(r, c, a=a):
                for k in range(2):
                    row_copy(jnp.minimum(a, n_steps - 1), a, r, k).start()
                return c

            lax.fori_loop(0, tc, start_rows, 0, unroll=8)

    slot = i % n_slots
    wait_tile(slot)
    wt = jnp.concatenate([w_ref[...]] * (LANES // SUBLANES), axis=0).T
    fbuf[0] = wt[:, 0:1] * gbuf[slot, 0] + wt[:, 1:2] * gbuf[slot, 1]
    _residual_ln(fbuf, xres_ref, gate_ref, lng_ref, lnb_ref, xo_ref, nxt)

    nxt_step = jnp.minimum(i + ahead, n_steps - 1)
    nxt_slot = (i + ahead) % n_slots
    for r in range(tc):
        for k in range(2):
            row_copy(nxt_step, nxt_slot, r, k).start(priority=k)

    @pl.when(i == n_steps - 1)
    def _():
        for a in range(1, ahead + 1):
            wait_tile((i + a) % n_slots)


def _combine_ln(ys, pos, wts, tok_base, xres, gate, lng, lnb, nxt_mods, mod_idx, tc):
    t, d = xres.shape
    r = gate.shape[1]
    has_next = nxt_mods is not None
    mspec = pl.BlockSpec((1, r, d), lambda i, *_: (mod_idx(i), 0, 0))
    vspec = pl.BlockSpec((1, d), lambda i, *_: (0, 0))
    ospec = pl.BlockSpec((tc, d), lambda i, *_: (i, 0))
    wbase = tok_base // tc
    in_specs = [pl.BlockSpec((SUBLANES, tc), lambda i, *_: (0, wbase + i)),
                pl.BlockSpec(memory_space=pl.ANY), ospec, mspec, vspec, vspec]
    args = [wts, ys, xres, gate, lng, lnb]
    out_specs = [ospec]
    out_shape = [jax.ShapeDtypeStruct((t, d), F32)]
    if has_next:
        in_specs += [mspec, mspec]
        args += list(nxt_mods)
        out_specs.append(ospec)
        out_shape.append(jax.ShapeDtypeStruct((t, d), BF16))
    return pl.pallas_call(
        functools.partial(_combine_kernel, pos.shape[0] // 2, tok_base, has_next),
        grid_spec=pltpu.PrefetchScalarGridSpec(
            num_scalar_prefetch=1,
            grid=(t // tc,),
            in_specs=in_specs,
            out_specs=out_specs,
            scratch_shapes=[pltpu.VMEM((3, 2, tc, d), F32), pltpu.VMEM((1, tc, d), F32),
                            pltpu.SemaphoreType.DMA((3,))]),
        out_shape=out_shape,
        compiler_params=_cparams(1, 40),
        name="moe_combine",
    )(pos, *args)


def _plan_kernel(n_tiles, idx_ref, cnt_ref, pos_ref, meta_ref):
    tm = MOE_TILE
    t = idx_ref.shape[1]
    sub = lax.broadcasted_iota(I32, (N_EXPERTS, LANES), 0)
    lane = lax.broadcasted_iota(I32, (N_EXPERTS, LANES), 1)
    cnt = cnt_ref[...]
    padded = jnp.floor((cnt + (tm - 1)) * (1.0 / tm)) * tm
    ends = padded
    sh = 1
    while sh < N_EXPERTS:
        ends = ends + jnp.where(sub >= sh, pltpu.roll(ends, sh, 0), 0.0)
        sh *= 2
    off = ends - padded
    total = ends[N_EXPERTS - 1:N_EXPERTS, :]
    n_used = jnp.maximum(total * (1.0 / tm), 1.0)

    def as_row(col):
        return jnp.sum(jnp.where(sub == lane, col, 0.0), axis=0, keepdims=True)

    eio = lax.broadcasted_iota(I32, (N_EXPERTS, t), 0)
    off_col = off[:, 0:1]
    for k in range(2):
        start = jnp.sum(jnp.where(eio == idx_ref[k:k + 1, :], off_col, 0.0), axis=0, keepdims=True)
        pos_ref[k:k + 1, :] = start.astype(I32) + idx_ref[2 + k:3 + k, :]
    pos_ref[2:SUBLANES, :] = jnp.zeros((SUBLANES - 2, t), I32)

    tile = lax.broadcasted_iota(I32, (1, LANES), 1).astype(F32)
    tile_start = jnp.minimum(tile, n_used - 1.0) * tm
    te = jnp.minimum(jnp.sum(jnp.where(ends <= tile_start, 1.0, 0.0), axis=0, keepdims=True), N_EXPERTS - 1.0)
    prev = jnp.where(tile == 0.0, -1.0, pltpu.roll(te, 1, 1))
    first = jnp.where((tile < n_used) & (te != prev), 1.0, 0.0)
    present = as_row(jnp.where(cnt > 0.0, 1.0, 0.0))
    later = jnp.where((present > 0.0) & (lane > sub) & (lane < N_EXPERTS), lane.astype(F32), float(N_EXPERTS))
    nxt_e = jnp.min(later, axis=1, keepdims=True)
    nxt_e = jnp.where(nxt_e == float(N_EXPERTS), -1.0, nxt_e)
    nxt = jnp.sum(jnp.where(sub.astype(F32) == te, nxt_e, 0.0), axis=0, keepdims=True)
    lane_row = lax.broadcasted_iota(I32, (1, LANES), 1)
    off_row = as_row(off)
    off_row = jnp.where(lane_row == N_EXPERTS, total, off_row)
    off_row = jnp.where(lane_row == N_EXPERTS + 1, float(n_tiles * tm), off_row)
    rows = {META_TILE_EXPERT: te, META_FIRST: first, META_NEXT: nxt, META_N_USED: n_used,
            META_COUNT: as_row(cnt), META_OFFSET: off_row}
    for r in range(SUBLANES):
        meta_ref[r:r + 1, :] = rows[r].astype(I32) if r in rows else jnp.zeros((1, LANES), I32)


def _moe_plan(idx, cnt, n_tok):
    tm = MOE_TILE
    n_tiles = (2 * n_tok + N_EXPERTS * (tm - 1) + tm - 1) // tm
    assert n_tiles <= LANES
    pos, meta = pl.pallas_call(
        functools.partial(_plan_kernel, n_tiles),
        out_shape=[jax.ShapeDtypeStruct((SUBLANES, n_tok), I32), jax.ShapeDtypeStruct((SUBLANES, LANES), I32)],
        compiler_params=_cparams(0, 32),
        name="moe_plan",
    )(idx, cnt)
    return pos, meta, n_tiles * tm


class _Rows:
    def __init__(self, n_seq, seq_len, tile):
        self.n_seq, self.seq_len, self.tile = n_seq, seq_len, tile
        self.per_row = seq_len < tile

    def mods(self, m):
        if self.per_row:
            return jnp.repeat(m, self.seq_len, axis=0).reshape(-1, self.tile, m.shape[1])
        return m[:, None, :]

    def mod_idx(self, i):
        return i if self.per_row else (i * self.tile) // self.seq_len


def _split_mods(mod_l, n_prompt):
    d = mod_l.shape[1] // 6
    cols = [mod_l[:, k * d:(k + 1) * d] for k in range(6)]
    return [c[:n_prompt] for c in cols], [c[n_prompt:] for c in cols]


def kernel(x_prompt, x_sample, state_mlstm_C, state_mlstm_n, state_mlstm_m, c_prompt, c_sample, w_ada, b_ada, ln_g, ln_b, a_w_in, a_b_gates, a_norm_w, a_w_out, b_w_in, b_b_in, b_norm_g, b_norm_b, b_w_s, b_b_s, b_w_out, w_router, b_router, w_expert_in, w_expert_out):
    bp, sp, d = x_prompt.shape
    bs, ss, _ = x_sample.shape
    tp = bp * sp
    ts = bs * ss
    n_tok = tp + ts
    qk = N_HEADS * DK
    vd = N_HEADS * DV
    n_main = 2 * qk + 2 * vd

    n_seq = bp + bs
    pad = (-n_seq) % SUBLANES
    c_all = jnp.concatenate([c_prompt, c_sample, jnp.zeros((pad, d), F32)])
    mod = _adaln(c_all, w_ada, b_ada)[:, :n_seq]

    sets = {"p": _Rows(bp, sp, 512), "s": _Rows(bs, ss, 256)}
    base = {"p": 0, "s": tp}
    x = {"p": x_prompt.reshape(tp, d), "s": x_sample.reshape(ts, d)}
    hm = {}
    outs = {}
    for layer in range(DEPTH):
        j = layer // 2
        mp, ms = _split_mods(mod[layer], bp)
        md = {"p": mp, "s": ms}
        lng = ln_g[layer]
        lnb = ln_b[layer]
        lhs = {}
        if layer % 2 == 0:
            w_in = a_w_in[j]
            gb = jnp.zeros((1, 2 * LANES), F32)
            gb = gb.at[0, :N_HEADS].set(a_b_gates[j, :N_HEADS])
            gb = gb.at[0, LANES:LANES + N_HEADS].set(a_b_gates[j, N_HEADS:])
            nw = a_norm_w[j].reshape(1, vd)
            proj = {}
            gates = {}
            for k, rs in sets.items():
                proj[k], gates[k] = _inproj(x[k], rs.mods(md[k][0]), rs.mods(md[k][1]), rs.mod_idx,
                                            w_in, n_main, rs.tile, BF16 if k == "p" else F32)
            lhs["p"], c_p, n_p, m_p = _mlstm_chunks(
                proj["p"], gates["p"], gb, nw,
                jnp.zeros((bp, N_HEADS, DK, DV), F32), jnp.zeros((bp, N_HEADS, LANES), F32),
                jnp.zeros((bp, N_HEADS, LANES), F32), bp, sp)
            rpad = SUBLANES - ss
            proj_s = jnp.pad(proj["s"].reshape(bs, ss, n_main), ((0, 0), (0, rpad), (0, 0)))
            gates_s = jnp.pad(gates["s"].reshape(bs, ss, 2 * LANES), ((0, 0), (0, rpad), (0, 0)))
            m0 = jnp.pad(state_mlstm_m[j], ((0, 0), (0, LANES - N_HEADS)))[:, None, :]
            ypre_s, c_s, n_s, m_s = _mlstm_short(proj_s, gates_s, gb, nw, state_mlstm_C[j],
                                                 state_mlstm_n[j].reshape(bs, 1, qk), m0, ss)
            n_s = n_s.reshape(bs, N_HEADS, DK)
            lhs["s"] = ypre_s[:, :ss].reshape(ts, vd)
            outs["C_p"], outs["n_p"], outs["m_p"] = c_p, n_p, m_p[:, :, 0]
            outs["C_s"], outs["n_s"], outs["m_s"] = c_s, n_s, m_s[:, 0, :N_HEADS]
            w_out = a_w_out[j]
        else:
            ws = b_w_s[j]
            bsv = b_b_s[j]
            w_out = b_w_out[j]

            def mixing(l):
                tri = jnp.tril(jnp.ones((l, l), bool))
                wsl = jnp.where(tri, ws[:, :l, :l], 0.0)
                eye = jnp.eye(CHUNK // l, dtype=F32)
                mats = jax.vmap(lambda m: jnp.kron(eye, m))(wsl).astype(BF16)
                bias = jnp.tile(bsv[:, :l], (1, CHUNK // l))[:, :, None]
                return mats, bias

            uv = {k: _gmlp_in(hm[k], b_w_in[j], b_b_in[j], min(1024, rs.n_seq * rs.seq_len),
                              BF16 if k == "p" else F32) for k, rs in sets.items()}
        x1 = {}
        hf = {}
        for k, rs in sets.items():
            if layer % 2 == 0:
                x1[k], hf[k] = _proj_ln(lhs[k], w_out, x[k], rs.mods(md[k][2]), lng[0:1], lnb[0:1],
                                        rs.mods(md[k][3]), rs.mods(md[k][4]), rs.mod_idx, rs.tile, 512)
            else:
                rs = _Rows(rs.n_seq, rs.seq_len, rs.tile // 2)
                mats, bias = mixing(min(CHUNK, rs.seq_len))
                res = _mix_proj_ln(uv[k], b_norm_g[j], b_norm_b[j], mats, bias, w_out, x[k], rs.mods(md[k][2]),
                                   lng[0:1], lnb[0:1], rs.mods(md[k][3]), rs.mods(md[k][4]), rs.mod_idx,
                                   rs.tile, 512, k == "s")
                x1[k], hf[k] = res[0], res[1]
                if k == "s":
                    outs["v_s"] = res[2]

        idx, wts, cnt = _router(hf["p"], hf["s"], w_router, b_router, 512)
        pos, meta, n_rows = _moe_plan(idx, cnt, n_tok)
        pos = pos[:2].reshape(-1)
        xsorted = _scatter_rows(hf["p"], hf["s"], pos, meta, n_rows, 512)
        ysorted = _experts(xsorted, w_expert_in, w_expert_out, layer, meta)
        nxt = _split_mods(mod[layer + 1], bp) if layer + 1 < DEPTH else None
        for ki, (k, rs) in enumerate(sets.items()):
            rc = _Rows(rs.n_seq, rs.seq_len, 256)
            nxt_mods = None if nxt is None else (rc.mods(nxt[ki][0]), rc.mods(nxt[ki][1]))
            res = _combine_ln(ysorted, pos, wts, base[k], x1[k], rc.mods(md[k][5]),
                              lng[1:2], lnb[1:2], nxt_mods, rc.mod_idx, 256)
            x[k] = res[0]
            if nxt is not None:
                hm[k] = res[1]

    return (x["p"].reshape(bp, sp, d), x["s"].reshape(bs, ss, d),
            outs["C_p"][None], outs["n_p"][None], outs["m_p"][None],
            outs["C_s"][None], outs["n_s"][None], outs["m_s"][None],
            outs["v_s"].reshape(bs, ss, -1)[None])
```
